```python
import math
import jax
import jax.numpy as jnp
from jax import lax
import numpy as np

D_MODEL = 1024
BATCH = 32
SEQ = 2048
DEPTH = 1

N_META = 16
BLOCK = 128
PAD = BLOCK - N_META
WINDOW = 128
A_HEADS = 8
A_KV_HEADS = 2
A_HEAD_DIM = 64
B_HEADS = 8
B_HEAD_DIM = 64
B_Q_RANK = 256
B_KV_RANK = 128
IDX_HEADS = 4
IDX_DIM = 64
TOPK_MAX = 256
N_BUCKETS = 32
MAX_DISTANCE = 128
TOTAL_HEADS = A_HEADS + B_HEADS
D_FF = -(-8 * D_MODEL // (3 * 256)) * 256
EPS = 1e-6
NEG = -1e30

IN_WIDTHS = (A_HEADS * A_HEAD_DIM, A_KV_HEADS * A_HEAD_DIM, A_KV_HEADS * A_HEAD_DIM,
             B_Q_RANK, B_KV_RANK, IDX_HEADS * IDX_DIM, IDX_DIM, IDX_HEADS, 2 * D_MODEL)
IN_SPLITS = tuple(sum(IN_WIDTHS[:i + 1]) for i in range(len(IN_WIDTHS) - 1))
D_IN = sum(IN_WIDTHS)

kernel_name = "hybrid_gated_swa_sink_dsa_mla_block"


def rms_norm(x, g):
    xf = x.astype(jnp.float32)
    y = xf * lax.rsqrt(jnp.mean(xf * xf, axis=-1, keepdims=True) + EPS)
    return (y * g.astype(jnp.float32)).astype(x.dtype)


def layer_norm(x, g, b):
    xf = x.astype(jnp.float32)
    mu = jnp.mean(xf, axis=-1, keepdims=True)
    xc = xf - mu
    var = jnp.mean(xc * xc, axis=-1, keepdims=True)
    return (xc * lax.rsqrt(var + EPS) * g.astype(jnp.float32) + b.astype(jnp.float32)).astype(x.dtype)


def t5_bucket(dist):
    max_exact = N_BUCKETS // 2
    d = jnp.maximum(dist, 1).astype(jnp.float32)
    large = max_exact + (jnp.log(d / max_exact) / math.log(MAX_DISTANCE / max_exact)
                         * (N_BUCKETS - max_exact)).astype(jnp.int32)
    large = jnp.minimum(large, N_BUCKETS - 1)
    return jnp.where(dist < max_exact, dist, large)


def sliding_window_sink_attention(q, k, v, sinks, bias_table):
    b, p, _ = q.shape
    nb = p // BLOCK
    grp = A_HEADS // A_KV_HEADS
    qb = q.reshape(b, nb, BLOCK, A_KV_HEADS, grp, A_HEAD_DIM)
    kb = k.reshape(b, nb, BLOCK, A_KV_HEADS, A_HEAD_DIM)
    vb = v.reshape(b, nb, BLOCK, A_KV_HEADS, A_HEAD_DIM)

    def with_prev(t):
        prev = jnp.concatenate([jnp.zeros_like(t[:, :1]), t[:, :-1]], axis=1)
        return jnp.concatenate([prev, t], axis=2)

    kw, vw = with_prev(kb), with_prev(vb)
    s = jnp.einsum('bnqhgd,bnkhd->bnhgqk', qb, kw).astype(jnp.float32) * (A_HEAD_DIM ** -0.5)
    blk = jnp.arange(nb)[:, None] * BLOCK
    qpos = blk + jnp.arange(BLOCK)[None]
    kpos = blk - BLOCK + jnp.arange(2 * BLOCK)[None]
    dist = qpos[:, :, None] - kpos[:, None, :]
    allowed = (dist >= 0) & (dist < WINDOW) & (kpos[:, None, :] >= PAD)
    bias = bias_table[t5_bucket(jnp.maximum(dist, 0))]
    bias = bias.reshape(nb, BLOCK, 2 * BLOCK, A_KV_HEADS, grp).transpose(0, 3, 4, 1, 2)
    s = jnp.where(allowed[None, :, None, None], s + bias.astype(jnp.float32)[None], NEG)
    sink = sinks.astype(jnp.float32).reshape(A_KV_HEADS, grp)[None, None, :, :, None]
    m = jnp.maximum(s.max(axis=-1), sink)
    e = jnp.exp(s - m[..., None])
    pr = e / (e.sum(axis=-1, keepdims=True) + jnp.exp(sink - m)[..., None])
    o = jnp.einsum('bnhgqk,bnkhd->bnqhgd', pr.astype(v.dtype), vw)
    return o.reshape(b, p, A_HEADS * A_HEAD_DIM)


def indexed_sparse_mla(q_abs, c_kv, iq, ik, iw, w_uv, bias_table):
    b, p, _, _ = q_abs.shape
    nb = p // BLOCK
    topk = min(TOPK_MAX, (p - BLOCK) // 4)
    kpos = jnp.arange(p)
    gather = jax.vmap(lambda rows, idx: rows[idx])
    scale = B_HEAD_DIM ** -0.5

    def one_block(i):
        start = i * BLOCK
        qa = lax.dynamic_slice_in_dim(q_abs, start, BLOCK, axis=1)
        qi = lax.dynamic_slice_in_dim(iq, start, BLOCK, axis=1)
        wi = lax.dynamic_slice_in_dim(iw, start, BLOCK, axis=1)
        qpos = start + jnp.arange(BLOCK)
        logits = jnp.einsum('bqhd,bsd->bqhs', qi, ik)
        score = jnp.einsum('bqhs,bqh->bqs', jax.nn.relu(logits).astype(jnp.float32),
                           wi.astype(jnp.float32))
        admissible = (kpos[None, :] <= qpos[:, None]) & (kpos[None, :] >= PAD)
        score = jnp.where(admissible[None], score, -jnp.inf)
        _, sel = lax.top_k(score, topk)
        valid = (sel <= qpos[None, :, None]) & (sel >= PAD)
        c_sel = gather(c_kv, sel)
        s = jnp.einsum('bqhc,bqkc->bqhk', qa, c_sel).astype(jnp.float32) * scale
        bias = bias_table[t5_bucket(jnp.maximum(qpos[None, :, None] - sel, 0))]
        s = jnp.where(valid[:, :, None, :], s + jnp.moveaxis(bias, -1, 2).astype(jnp.float32), NEG)
        pr = jax.nn.softmax(s, axis=-1)
        return jnp.einsum('bqhk,bqkc->bqhc', pr.astype(c_kv.dtype), c_sel)

    lat = lax.map(one_block, jnp.arange(nb))
    lat = lat.transpose(1, 0, 2, 3, 4).reshape(b, p, B_HEADS, B_KV_RANK)
    o = jnp.einsum('bphc,chd->bphd', lat, w_uv)
    return o.reshape(b, p, B_HEADS * B_HEAD_DIM)


def setup_inputs(seed: int = 0) -> dict:
    key = jax.random.key(seed)
    ks = jax.random.split(key, 24)
    f32 = jnp.float32

    def dense(k, shape, fan_in):
        return jax.random.normal(k, shape, f32) * (fan_in ** -0.5)

    def gain(k, shape):
        return 1.0 + 0.05 * jax.random.normal(k, shape, f32)

    return {
        "x": jax.random.normal(ks[0], (BATCH, SEQ, D_MODEL), f32),
        "meta_tokens": jax.random.normal(ks[1], (N_META, D_MODEL), f32),
        "attn_norm_g": gain(ks[2], (DEPTH, D_MODEL)),
        "w_in": dense(ks[3], (DEPTH, D_MODEL, D_IN), D_MODEL),
        "b_gates": 0.02 * jax.random.normal(ks[4], (DEPTH, 2 * D_MODEL), f32),
        "q_norm_g": gain(ks[5], (DEPTH, B_Q_RANK)),
        "kv_norm_g": gain(ks[6], (DEPTH, B_KV_RANK)),
        "w_uq": dense(ks[7], (DEPTH, B_Q_RANK, B_HEADS * B_HEAD_DIM), B_Q_RANK),
        "w_uk": dense(ks[8], (DEPTH, B_KV_RANK, B_HEADS, B_HEAD_DIM), B_KV_RANK),
        "w_uv": dense(ks[9], (DEPTH, B_KV_RANK, B_HEADS, B_HEAD_DIM), B_KV_RANK),
        "idx_k_ln_g": gain(ks[10], (DEPTH, IDX_DIM)),
        "idx_k_ln_b": 0.02 * jax.random.normal(ks[11], (DEPTH, IDX_DIM), f32),
        "sinks": 0.5 * jax.random.normal(ks[12], (DEPTH, A_HEADS), f32),
        "rel_bias": 0.2 * jax.random.normal(ks[13], (N_BUCKETS, TOTAL_HEADS), f32),
        "w_branch_a": dense(ks[14], (DEPTH, A_HEADS * A_HEAD_DIM, D_MODEL), A_HEADS * A_HEAD_DIM),
        "w_branch_b": dense(ks[15], (DEPTH, B_HEADS * B_HEAD_DIM, D_MODEL), B_HEADS * B_HEAD_DIM),
        "w_out": dense(ks[16], (DEPTH, D_MODEL, D_MODEL), D_MODEL),
        "ffn_norm_g": gain(ks[17], (DEPTH, D_MODEL)),
        "w_ffn_gate": dense(ks[18], (DEPTH, D_MODEL, D_FF), D_MODEL),
        "w_ffn_up": dense(ks[19], (DEPTH, D_MODEL, D_FF), D_MODEL),
        "w_ffn_down": dense(ks[20], (DEPTH, D_FF, D_MODEL), D_FF),
        "final_norm_g": gain(ks[21], (D_MODEL,)),
    }


def reference(x, meta_tokens, attn_norm_g, w_in, b_gates, q_norm_g, kv_norm_g, w_uq, w_uk, w_uv,
              idx_k_ln_g, idx_k_ln_b, sinks, rel_bias, w_branch_a, w_branch_b, w_out,
              ffn_norm_g, w_ffn_gate, w_ffn_up, w_ffn_down, final_norm_g):
    b, seq, d = x.shape
    meta = jnp.broadcast_to(meta_tokens.astype(x.dtype)[None], (b, N_META, d))
    h = jnp.concatenate([jnp.zeros((b, PAD, d), x.dtype), meta, x], axis=1)
    p = h.shape[1]
    bias_a = rel_bias[:, :A_HEADS]
    bias_b = rel_bias[:, A_HEADS:]
    for l in range(DEPTH):
        hn = rms_norm(h, attn_norm_g[l])
        proj = hn @ w_in[l]
        aq, ak, av, bq, bkv, iq, ik, iw, gates = jnp.split(proj, IN_SPLITS, axis=-1)
        gates = jax.nn.sigmoid((gates + b_gates[l]).astype(jnp.float32)).astype(h.dtype)
        gate_a, gate_b = jnp.split(gates, 2, axis=-1)
        o_a = sliding_window_sink_attention(aq, ak, av, sinks[l], bias_a)
        q = (rms_norm(bq, q_norm_g[l]) @ w_uq[l]).reshape(b, p, B_HEADS, B_HEAD_DIM)
        c_kv = rms_norm(bkv, kv_norm_g[l])
        q_abs = jnp.einsum('bphd,chd->bphc', q, w_uk[l])
        iq = iq.reshape(b, p, IDX_HEADS, IDX_DIM)
        ik = layer_norm(ik, idx_k_ln_g[l], idx_k_ln_b[l])
        iw = iw * ((IDX_HEADS * IDX_DIM) ** -0.5)
        o_b = indexed_sparse_mla(q_abs, c_kv, iq, ik, iw, w_uv[l], bias_b)
        mixed = gate_a * (o_a @ w_branch_a[l]) + gate_b * (o_b @ w_branch_b[l])
        h = h + mixed @ w_out[l]
        hn = rms_norm(h, ffn_norm_g[l])
        h = h + (jax.nn.silu(hn @ w_ffn_gate[l]) * (hn @ w_ffn_up[l])) @ w_ffn_down[l]
    y = rms_norm(h, final_norm_g)
    return y[:, PAD + N_META:]
```

```python
import functools
import math

import numpy as np
import jax
import jax.numpy as jnp
from jax import lax
from jax.experimental import pallas as pl
from jax.experimental.pallas import tpu as pltpu

N_META = 16
BLOCK = 128
PAD = BLOCK - N_META
WINDOW = 128
A_HEADS = 8
A_KV_HEADS = 2
A_HEAD_DIM = 64
B_HEADS = 8
B_HEAD_DIM = 64
B_Q_RANK = 256
B_KV_RANK = 128
IDX_HEADS = 4
IDX_DIM = 64
TOPK_MAX = 256
N_BUCKETS = 32
MAX_DISTANCE = 128
EPS = 1e-6
NEG = -1e30
INT_MIN = -(2 ** 31)

LANES = 128
VMEM_LIMIT_BYTES = 56 * 1024 * 1024

TOK_TILE = 512
TQ = 256
FFN_CHUNKS = 2

C_AQ = 0
C_AKV = C_AQ + A_HEADS * A_HEAD_DIM
C_BQ = C_AKV + 2 * A_KV_HEADS * A_HEAD_DIM
C_BKV = C_BQ + B_Q_RANK
C_IQ = C_BKV + B_KV_RANK
C_IK = C_IQ + IDX_HEADS * IDX_DIM
C_IW = C_IK + LANES
C_END = C_IW + LANES

F32 = jnp.float32
BF16 = jnp.bfloat16


def _t5_bucket_np(dist):
    dist = np.asarray(dist, np.int64)
    max_exact = N_BUCKETS // 2
    d = np.maximum(dist, 1).astype(np.float32)
    large = max_exact + (np.log(d / np.float32(max_exact)) / np.float32(math.log(MAX_DISTANCE / max_exact))
                         * np.float32(N_BUCKETS - max_exact)).astype(np.int32)
    large = np.minimum(large, N_BUCKETS - 1)
    return np.where(dist < max_exact, dist, large).astype(np.int32)


def _rms(x, g):
    return x * lax.rsqrt(jnp.mean(x * x, axis=-1, keepdims=True) + EPS) * g


def _full_spec(shape):
    nd = len(shape)
    return pl.BlockSpec(shape, lambda *_: (0,) * nd)


def _bias_body(tab_ref, bkt_a_ref, bkt_near_ref, bkt_meta_ref, ba_ref, bnear_ref, bmeta_ref):
    def lookup(bkt, col):
        acc = jnp.zeros(bkt.shape, F32)
        for b in range(N_BUCKETS):
            acc = jnp.where(bkt == b, tab_ref[b, col], acc)
        return acc

    bkt_a = bkt_a_ref[...]
    bkt_near = bkt_near_ref[...]
    bkt_meta = bkt_meta_ref[...]
    for h in range(A_HEADS):
        ba_ref[h] = lookup(bkt_a, h)
    for h in range(B_HEADS):
        bnear_ref[h * TQ:(h + 1) * TQ, :] = lookup(bkt_near, A_HEADS + h)
        bmeta_ref[h * TQ:(h + 1) * TQ, :] = lookup(bkt_meta, A_HEADS + h)


def _bias_call(rel_bias):
    q = np.arange(BLOCK)[:, None]
    k = np.arange(2 * BLOCK)[None, :]
    bkt_a = _t5_bucket_np(np.maximum(q + BLOCK - k, 0))
    q = np.arange(TQ)[:, None]
    k = np.arange(2 * TQ)[None, :]
    bkt_near = _t5_bucket_np(np.maximum(q + TQ - k, 0))
    k = np.arange(BLOCK)[None, :]
    bkt_meta = _t5_bucket_np(np.maximum(q + BLOCK - k, 0))
    return pl.pallas_call(
        _bias_body,
        out_shape=(jax.ShapeDtypeStruct((A_HEADS, BLOCK, 2 * BLOCK), F32),
                   jax.ShapeDtypeStruct((B_HEADS * TQ, 2 * TQ), F32),
                   jax.ShapeDtypeStruct((B_HEADS * TQ, BLOCK), F32)),
        in_specs=[pl.BlockSpec(memory_space=pltpu.SMEM),
                  pl.BlockSpec(memory_space=pltpu.VMEM),
                  pl.BlockSpec(memory_space=pltpu.VMEM),
                  pl.BlockSpec(memory_space=pltpu.VMEM)],
        out_specs=(pl.BlockSpec(memory_space=pltpu.VMEM),
                   pl.BlockSpec(memory_space=pltpu.VMEM),
                   pl.BlockSpec(memory_space=pltpu.VMEM)),
        name="bias_tables",
    )(rel_bias, jnp.asarray(bkt_a), jnp.asarray(bkt_near), jnp.asarray(bkt_meta))


def _proj_body(x_ref, g_ref, w1_ref, qg_ref, wuq_ref, wukt_ref, kvg_ref, ikg_ref, ikb_ref,
               aq_ref, akv_ref, qabs_ref, ckv_ref, iqs_ref, ik_ref, iw_ref, *, tq):
    tm = x_ref.shape[0]
    nblk = tm // tq
    hn = _rms(x_ref[...], g_ref[...]).astype(BF16)

    def proj(lo, hi):
        return jnp.dot(hn, w1_ref[:, lo:hi], preferred_element_type=F32)

    aq_ref[...] = (proj(C_AQ, C_AKV) * (A_HEAD_DIM ** -0.5)).astype(BF16)
    akv_ref[...] = proj(C_AKV, C_BQ).astype(BF16)

    qn = _rms(proj(C_BQ, C_BKV), qg_ref[...]).astype(BF16)
    q = jnp.dot(qn, wuq_ref[...], preferred_element_type=F32).astype(BF16)
    for h in range(B_HEADS):
        qa = jnp.dot(q[:, h * B_HEAD_DIM:(h + 1) * B_HEAD_DIM], wukt_ref[h], preferred_element_type=F32)
        qabs_ref[:, h] = (qa * (B_HEAD_DIM ** -0.5)).astype(BF16).reshape(nblk, tq, B_KV_RANK)

    ckv_ref[...] = _rms(proj(C_BKV, C_IQ), kvg_ref[...]).astype(BF16)

    iq = proj(C_IQ, C_IK).astype(BF16)
    for h in range(IDX_HEADS):
        iqs_ref[:, h] = iq[:, h * IDX_DIM:(h + 1) * IDX_DIM].reshape(nblk, tq, IDX_DIM)

    ikw = proj(C_IK, C_END)
    ik = ikw[:, :IDX_DIM]
    mu = jnp.mean(ik, axis=-1, keepdims=True)
    xc = ik - mu
    var = jnp.mean(xc * xc, axis=-1, keepdims=True)
    ik_ref[...] = (xc * lax.rsqrt(var + EPS) * ikg_ref[...] + ikb_ref[...]).astype(BF16)
    iw_ref[...] = ikw[:, LANES:] * ((IDX_HEADS * IDX_DIM) ** -0.5)


def _proj_call(x2, tm, tq, g, w1, qg, wuq, wukt, kvg, ikg, ikb):
    n, d = x2.shape
    grid = (n // tm,)
    row = lambda i: (i, 0)
    blk4 = lambda i: (i, 0, 0, 0)
    out_shape = (
        jax.ShapeDtypeStruct((n, A_HEADS * A_HEAD_DIM), BF16),
        jax.ShapeDtypeStruct((n, 2 * A_KV_HEADS * A_HEAD_DIM), BF16),
        jax.ShapeDtypeStruct((n // tq, B_HEADS, tq, B_KV_RANK), BF16),
        jax.ShapeDtypeStruct((n, B_KV_RANK), BF16),
        jax.ShapeDtypeStruct((n // tq, IDX_HEADS, tq, IDX_DIM), BF16),
        jax.ShapeDtypeStruct((n, IDX_DIM), BF16),
        jax.ShapeDtypeStruct((n, LANES), F32),
    )
    out_specs = (
        pl.BlockSpec((tm, A_HEADS * A_HEAD_DIM), row),
        pl.BlockSpec((tm, 2 * A_KV_HEADS * A_HEAD_DIM), row),
        pl.BlockSpec((tm // tq, B_HEADS, tq, B_KV_RANK), blk4),
        pl.BlockSpec((tm, B_KV_RANK), row),
        pl.BlockSpec((tm // tq, IDX_HEADS, tq, IDX_DIM), blk4),
        pl.BlockSpec((tm, IDX_DIM), row),
        pl.BlockSpec((tm, LANES), row),
    )
    in_specs = [pl.BlockSpec((tm, d), row), _full_spec(g.shape), _full_spec(w1.shape), _full_spec(qg.shape),
                _full_spec(wuq.shape), _full_spec(wukt.shape), _full_spec(kvg.shape), _full_spec(ikg.shape),
                _full_spec(ikb.shape)]
    return pl.pallas_call(
        functools.partial(_proj_body, tq=tq),
        grid=grid, in_specs=in_specs, out_specs=out_specs, out_shape=out_shape,
        compiler_params=pltpu.CompilerParams(dimension_semantics=("arbitrary",),
                                             vmem_limit_bytes=VMEM_LIMIT_BYTES),
        name="in_proj",
    )(x2, g, w1, qg, wuq, wukt, kvg, ikg, ikb)


def _swa_body(sinks_ref, aq_ref, kcur_ref, kprev_ref, kmeta_ref, bias_ref, o_ref):
    n = pl.program_id(1)
    kprev = jnp.where(n == 0, kmeta_ref[...], kprev_ref[0])
    kwin = jnp.concatenate([kprev, kcur_ref[0]], axis=0)
    aq = aq_ref[0]
    qi = lax.broadcasted_iota(jnp.int32, (BLOCK, 2 * BLOCK), 0)
    ki = lax.broadcasted_iota(jnp.int32, (BLOCK, 2 * BLOCK), 1)
    dist = qi + BLOCK - ki
    allowed = (dist >= 0) & (dist < WINDOW) & ((n > 0) | (ki >= PAD))
    grp = A_HEADS // A_KV_HEADS
    outs = []
    for h in range(A_HEADS):
        kvh = h // grp
        k = kwin[:, kvh * A_HEAD_DIM:(kvh + 1) * A_HEAD_DIM]
        v = kwin[:, (A_KV_HEADS + kvh) * A_HEAD_DIM:(A_KV_HEADS + kvh + 1) * A_HEAD_DIM]
        q = aq[:, h * A_HEAD_DIM:(h + 1) * A_HEAD_DIM]
        s = lax.dot_general(q, k, (((1,), (1,)), ((), ())), preferred_element_type=F32)
        s = jnp.where(allowed, s + bias_ref[h], NEG)
        sink = sinks_ref[h]
        m = jnp.maximum(jnp.max(s, axis=-1, keepdims=True), sink)
        e = jnp.exp(s - m)
        den = jnp.sum(e, axis=-1, keepdims=True) + jnp.exp(sink - m)
        p = (e / den).astype(BF16)
        outs.append(jnp.dot(p, v, preferred_element_type=F32))
    o_ref[0] = jnp.concatenate(outs, axis=1).astype(BF16)


def _swa_call(sinks, aq, akv, makv, bias_a):
    b, s, _ = aq.shape
    nb = s // BLOCK
    return pl.pallas_call(
        _swa_body,
        grid=(b, nb),
        in_specs=[pl.BlockSpec(memory_space=pltpu.SMEM),
                  pl.BlockSpec((1, BLOCK, aq.shape[2]), lambda i, n: (i, n, 0)),
                  pl.BlockSpec((1, BLOCK, akv.shape[2]), lambda i, n: (i, n, 0)),
                  pl.BlockSpec((1, BLOCK, akv.shape[2]), lambda i, n: (i, jnp.maximum(n - 1, 0), 0)),
                  _full_spec(makv.shape), _full_spec(bias_a.shape)],
        out_specs=pl.BlockSpec((1, BLOCK, aq.shape[2]), lambda i, n: (i, n, 0)),
        out_shape=jax.ShapeDtypeStruct(aq.shape, BF16),
        compiler_params=pltpu.CompilerParams(dimension_semantics=("arbitrary", "arbitrary"),
                                             vmem_limit_bytes=VMEM_LIMIT_BYTES),
        name="swa_sink_attention",
    )(sinks, aq, akv, akv, makv, bias_a)


def _mla_body(far_ref, iqs_ref, iw_ref, qabs_ref, ik_ref, ckv_ref, mik_ref, mckv_ref, bnear_ref, bmeta_ref,
              wuv_ref, o_ref, key_sc, e_sc, m_sc, acc_sc, j_sc, *, topk):
    m_blk = pl.program_id(1)
    sub = TQ // LANES
    iq_all = iqs_ref[0].reshape(IDX_HEADS * TQ, IDX_DIM)
    qabs_all = qabs_ref[0].reshape(B_HEADS * TQ, B_KV_RANK)
    iw = iw_ref[0]
    wcol = [iw[:, h:h + 1] for h in range(IDX_HEADS)]

    row_i = lax.broadcasted_iota(jnp.int32, (TQ, LANES), 0)
    lane_i = lax.broadcasted_iota(jnp.int32, (TQ, LANES), 1)

    def sort_key(sc):
        bits = lax.bitcast_convert_type(sc, jnp.int32)
        bits = jnp.where(sc == 0.0, 0, bits)
        return jnp.where(bits < 0, bits ^ 0x7FFFFFFF, bits)

    def score(ik_c):
        logits = lax.dot_general(iq_all, ik_c, (((1,), (1,)), ((), ())), preferred_element_type=F32)
        sc = jnp.maximum(logits[0:TQ], 0.0) * wcol[0]
        for h in range(1, IDX_HEADS):
            sc = sc + jnp.maximum(logits[h * TQ:(h + 1) * TQ], 0.0) * wcol[h]
        return sc

    key_sc[0] = jnp.where(lane_i >= PAD, sort_key(score(mik_ref[...])), INT_MIN)

    def score_chunk(c, causal):
        start = pl.multiple_of(c * TQ, TQ)
        keys = sort_key(score(ik_ref[0, pl.ds(start, TQ), :]))
        for u in range(sub):
            piece = keys[:, u * LANES:(u + 1) * LANES]
            if causal:
                piece = jnp.where(lane_i + u * LANES <= row_i, piece, INT_MIN)
            key_sc[1 + c * sub + u] = piece

    def score_loop(c, carry):
        score_chunk(c, False)
        return carry

    lax.fori_loop(0, m_blk, score_loop, 0)
    score_chunk(m_blk, True)
    npieces = 1 + (m_blk + 1) * sub

    def count(pred):
        def body(j, part):
            return part + jnp.where(pred(key_sc[j], j), 1.0, 0.0)
        part = lax.fori_loop(0, npieces, body, jnp.zeros((TQ, LANES), F32))
        return jnp.broadcast_to(jnp.sum(part, axis=1, keepdims=True), (TQ, LANES))

    kf = float(topk)

    def bisect(i, t):
        cand = t + jnp.left_shift(jnp.int32(1), 31 - i)
        cnt = count(lambda kp, j: kp >= cand)
        return jnp.where(cnt >= kf, cand, t)

    thr = lax.fori_loop(0, 32, bisect, jnp.full((TQ, LANES), INT_MIN, jnp.int32))
    cnt_gt = count(lambda kp, j: kp > thr)
    cnt_ge = count(lambda kp, j: kp >= thr)
    need = kf - cnt_gt
    ambiguous = (cnt_ge - cnt_gt != need) & (thr != INT_MIN)

    j_sc[...] = jnp.full((TQ, LANES), 2 ** 30, jnp.int32)

    @pl.when(jnp.max(ambiguous.astype(F32)) > 0.0)
    def _():
        def tie(i, jt):
            cand = jt + jnp.left_shift(jnp.int32(1), 11 - i)
            cnt = count(lambda kp, j: (kp == thr) & (j * LANES + lane_i < cand))
            return jnp.where(cnt < need, cand, jt)
        j_sc[...] = lax.fori_loop(0, 12, tie, jnp.zeros((TQ, LANES), jnp.int32))

    jt = j_sc[...]

    m_sc[...] = jnp.full(m_sc.shape, NEG, F32)
    acc_sc[...] = jnp.zeros(acc_sc.shape, F32)

    def selected(j):
        kp = key_sc[j]
        return ((kp > thr) | ((kp == thr) & (j * LANES + lane_i <= jt))) & (kp != INT_MIN)

    def mla_chunk(ckv_c, sel, bias_fn):
        ck = ckv_c.shape[0]
        s_all = lax.dot_general(qabs_all, ckv_c, (((1,), (1,)), ((), ())), preferred_element_type=F32)
        alphas = []
        for h in range(B_HEADS):
            rows = slice(h * TQ, (h + 1) * TQ)
            sh = jnp.where(sel, s_all[rows] + bias_fn(h), NEG)
            m_prev = m_sc[rows]
            m_new = jnp.maximum(m_prev, jnp.max(sh, axis=1, keepdims=True))
            alphas.append(jnp.exp(m_prev - m_new))
            e = jnp.where(sel, jnp.exp(sh - m_new[:, 0:1]), 0.0)
            e_sc[rows, 0:ck] = e.astype(BF16)
            m_sc[rows] = m_new
        vext = jnp.concatenate([ckv_c, jnp.ones((ck, LANES), BF16)], axis=1)
        pv = jnp.dot(e_sc[:, 0:ck], vext, preferred_element_type=F32)
        for h in range(B_HEADS):
            rows = slice(h * TQ, (h + 1) * TQ)
            a2 = jnp.concatenate([alphas[h], alphas[h]], axis=1)
            acc_sc[rows] = a2 * acc_sc[rows] + pv[rows]

    def real_sel(c):
        return jnp.concatenate([selected(1 + c * sub + u) for u in range(sub)], axis=1)

    def real_ckv(c):
        return ckv_ref[0, pl.ds(pl.multiple_of(c * TQ, TQ), TQ), :]

    first = m_blk == 0
    mla_chunk(mckv_ref[...], selected(0),
              lambda h: jnp.where(first, bmeta_ref[h * TQ:(h + 1) * TQ, :], far_ref[h]))

    def far_loop(c, carry):
        mla_chunk(real_ckv(c), real_sel(c), lambda h: far_ref[h])
        return carry

    lax.fori_loop(0, jnp.maximum(m_blk - 1, 0), far_loop, 0)

    @pl.when(m_blk >= 1)
    def _():
        c = m_blk - 1
        mla_chunk(real_ckv(c), real_sel(c), lambda h: bnear_ref[h * TQ:(h + 1) * TQ, 0:TQ])

    mla_chunk(real_ckv(m_blk), real_sel(m_blk), lambda h: bnear_ref[h * TQ:(h + 1) * TQ, TQ:2 * TQ])

    outs = []
    for h in range(B_HEADS):
        acc = acc_sc[h * TQ:(h + 1) * TQ]
        lat = (acc[:, 0:B_KV_RANK] / acc[:, B_KV_RANK:]).astype(BF16)
        outs.append(jnp.dot(lat, wuv_ref[h], preferred_element_type=F32))
    o_ref[0] = jnp.concatenate(outs, axis=1).astype(BF16)


def _mla_call(far, iqs, iw, qabs, ik, ckv, mik, mckv, bnear, bmeta, wuv, topk):
    b, s, _ = ik.shape
    nq = s // TQ
    npieces_max = 1 + s // LANES
    return pl.pallas_call(
        functools.partial(_mla_body, topk=topk),
        grid=(b, nq),
        in_specs=[pl.BlockSpec(memory_space=pltpu.SMEM),
                  pl.BlockSpec((1, IDX_HEADS, TQ, IDX_DIM), lambda i, m: (i * nq + m, 0, 0, 0)),
                  pl.BlockSpec((1, TQ, LANES), lambda i, m: (i, m, 0)),
                  pl.BlockSpec((1, B_HEADS, TQ, B_KV_RANK), lambda i, m: (i * nq + m, 0, 0, 0)),
                  pl.BlockSpec((1, s, IDX_DIM), lambda i, m: (i, 0, 0)),
                  pl.BlockSpec((1, s, B_KV_RANK), lambda i, m: (i, 0, 0)),
                  _full_spec(mik.shape), _full_spec(mckv.shape), _full_spec(bnear.shape),
                  _full_spec(bmeta.shape), _full_spec(wuv.shape)],
        out_specs=pl.BlockSpec((1, TQ, B_HEADS * B_HEAD_DIM), lambda i, m: (i, m, 0)),
        out_shape=jax.ShapeDtypeStruct((b, s, B_HEADS * B_HEAD_DIM), BF16),
        scratch_shapes=[pltpu.VMEM((npieces_max, TQ, LANES), jnp.int32),
                        pltpu.VMEM((B_HEADS * TQ, TQ), BF16),
                        pltpu.VMEM((B_HEADS * TQ, LANES), F32),
                        pltpu.VMEM((B_HEADS * TQ, 2 * LANES), F32),
                        pltpu.VMEM((TQ, LANES), jnp.int32)],
        compiler_params=pltpu.CompilerParams(dimension_semantics=("arbitrary", "arbitrary"),
                                             vmem_limit_bytes=VMEM_LIMIT_BYTES),
        name="indexer_topk_mla",
    )(far, iqs, iw, qabs, ik, ckv, mik, mckv, bnear, bmeta, wuv)


def _out_body(x_ref, oa_ref, ob_ref, ag_ref, wg_ref, bg_ref, wa_ref, wb_ref, wo_ref, fg_ref,
              wfg_ref, wfu_ref, wfd_ref, ng_ref, y_ref):
    d = x_ref.shape[1]
    x = x_ref[...]
    hn = _rms(x, ag_ref[...]).astype(BF16)

    def gated(o_ref, w_ref, lo):
        gate = jax.nn.sigmoid(jnp.dot(hn, wg_ref[:, lo:lo + d], preferred_element_type=F32) + bg_ref[:, lo:lo + d])
        return gate * jnp.dot(o_ref[...], w_ref[...], preferred_element_type=F32)

    mixed = (gated(oa_ref, wa_ref, 0) + gated(ob_ref, wb_ref, d)).astype(BF16)
    h = x + jnp.dot(mixed, wo_ref[...], preferred_element_type=F32)
    hn2 = _rms(h, fg_ref[...]).astype(BF16)
    dff = wfg_ref.shape[1]
    step = dff // FFN_CHUNKS
    for c in range(0, dff, step):
        g = jnp.dot(hn2, wfg_ref[:, c:c + step], preferred_element_type=F32)
        u = jnp.dot(hn2, wfu_ref[:, c:c + step], preferred_element_type=F32)
        act = (g * jax.nn.sigmoid(g) * u).astype(BF16)
        h = h + jnp.dot(act, wfd_ref[c:c + step, :], preferred_element_type=F32)
    y_ref[...] = _rms(h, ng_ref[...])


def _out_call(x2, oa, ob, ag, wg, bg, wa, wb, wo, fg, wfg, wfu, wfd, ng):
    n, d = x2.shape
    tm = TOK_TILE
    row = lambda i: (i, 0)

    def const_spec(a):
        return pl.BlockSpec(a.shape, lambda i: (0,) * a.ndim, pipeline_mode=pl.Buffered(1))

    consts = [ag, wg, bg, wa, wb, wo, fg, wfg, wfu, wfd, ng]
    return pl.pallas_call(
        _out_body,
        grid=(n // tm,),
        in_specs=[pl.BlockSpec((tm, d), row), pl.BlockSpec((tm, oa.shape[1]), row),
                  pl.BlockSpec((tm, ob.shape[1]), row)] + [const_spec(a) for a in consts],
        out_specs=pl.BlockSpec((tm, d), row),
        out_shape=jax.ShapeDtypeStruct((n, d), F32),
        compiler_params=pltpu.CompilerParams(dimension_semantics=("arbitrary",),
                                             vmem_limit_bytes=VMEM_LIMIT_BYTES),
        name="merge_ffn_norm",
    )(x2, oa, ob, *consts)


def kernel(x, meta_tokens, attn_norm_g, w_in, b_gates, q_norm_g, kv_norm_g, w_uq, w_uk, w_uv, idx_k_ln_g,
           idx_k_ln_b, sinks, rel_bias, w_branch_a, w_branch_b, w_out, ffn_norm_g, w_ffn_gate, w_ffn_up,
           w_ffn_down, final_norm_g):
    b, s, d = x.shape
    assert attn_norm_g.shape[0] == 1, "single-layer block"
    assert s % TQ == 0 and (b * s) % TOK_TILE == 0 and TOK_TILE % TQ == 0
    assert w_ffn_gate.shape[2] % (FFN_CHUNKS * LANES) == 0
    topk = min(TOPK_MAX, s // 4)
    far_bkts = np.unique(_t5_bucket_np(np.arange(BLOCK + 1, s + BLOCK + 1)))
    assert far_bkts.size == 1
    far_bkt = int(far_bkts[0])

    wi = w_in[0]
    n_in = C_BQ + B_Q_RANK + B_KV_RANK + IDX_HEADS * IDX_DIM + IDX_DIM + IDX_HEADS
    c0 = C_IK
    zpad = lambda k: jnp.zeros((d, k), wi.dtype)
    w1 = jnp.concatenate([wi[:, :c0], wi[:, c0:c0 + IDX_DIM], zpad(LANES - IDX_DIM),
                          wi[:, c0 + IDX_DIM:n_in], zpad(LANES - IDX_HEADS)], axis=1).astype(BF16)
    wg = wi[:, n_in:].astype(BF16)
    row2 = lambda v: v.reshape(1, -1).astype(F32)
    wukt = jnp.transpose(w_uk[0], (1, 2, 0)).astype(BF16)
    wuv = jnp.transpose(w_uv[0], (1, 0, 2)).astype(BF16)
    proj_w = (row2(attn_norm_g[0]), w1, row2(q_norm_g[0]), w_uq[0].astype(BF16), wukt, row2(kv_norm_g[0]),
              row2(idx_k_ln_g[0]), row2(idx_k_ln_b[0]))

    bias_a, bnear, bmeta = _bias_call(rel_bias)
    far = rel_bias[far_bkt, A_HEADS:]

    x2 = x.reshape(b * s, d)
    aq, akv, qabs, ckv, iqs, ik, iw = _proj_call(x2, TOK_TILE, TQ, *proj_w)
    meta_blk = jnp.concatenate([jnp.zeros((PAD, d), x.dtype), meta_tokens.astype(x.dtype)], axis=0)
    _, makv, _, mckv, _, mik, _ = _proj_call(meta_blk, BLOCK, BLOCK, *proj_w)

    o_a = _swa_call(sinks[0], aq.reshape(b, s, -1), akv.reshape(b, s, -1), makv, bias_a)
    o_b = _mla_call(far, iqs, iw.reshape(b, s, -1), qabs, ik.reshape(b, s, -1), ckv.reshape(b, s, -1),
                    mik, mckv, bnear, bmeta, wuv, topk)

    y = _out_call(x2, o_a.reshape(b * s, -1), o_b.reshape(b * s, -1), row2(attn_norm_g[0]), wg,
                  row2(b_gates[0]), w_branch_a[0].astype(BF16), w_branch_b[0].astype(BF16),
                  w_out[0].astype(BF16), row2(ffn_norm_g[0]), w_ffn_gate[0].astype(BF16),
                  w_ffn_up[0].astype(BF16), w_ffn_down[0].astype(BF16), row2(final_norm_g))
    return y.reshape(b, s, d)
```

```python
import functools
import math

import numpy as np
import jax
import jax.numpy as jnp
from jax import lax
from jax.experimental import pallas as pl
from jax.experimental.pallas import tpu as pltpu

N_META = 16
BLOCK = 128
PAD = BLOCK - N_META
WINDOW = 128
A_HEADS = 8
A_KV_HEADS = 2
A_HEAD_DIM = 64
B_HEADS = 8
B_HEAD_DIM = 64
B_Q_RANK = 256
B_KV_RANK = 128
IDX_HEADS = 4
IDX_DIM = 64
TOPK_MAX = 256
N_BUCKETS = 32
MAX_DISTANCE = 128
EPS = 1e-6
NEG = -1e30
INT_MIN = -(2 ** 31)

LANES = 128
SUBLANES = 8
BF16_ROWS = 16
VMEM_LIMIT_BYTES = 56 * 1024 * 1024

TOK_TILE = 512
TQ = 256
FFN_CHUNKS = 2

C_AQ = 0
C_AKV = C_AQ + A_HEADS * A_HEAD_DIM
C_BQ = C_AKV + 2 * A_KV_HEADS * A_HEAD_DIM
C_BKV = C_BQ + B_Q_RANK
C_IQ = C_BKV + B_KV_RANK
C_IK = C_IQ + IDX_HEADS * IDX_DIM
C_IW = C_IK + LANES
C_END = C_IW + LANES

F32 = jnp.float32
BF16 = jnp.bfloat16
NT_DIMS = (((1,), (1,)), ((), ()))


def _t5_bucket_np(dist):
    dist = np.asarray(dist, np.int64)
    max_exact = N_BUCKETS // 2
    d = np.maximum(dist, 1).astype(np.float32)
    large = max_exact + (np.log(d / np.float32(max_exact)) / np.float32(math.log(MAX_DISTANCE / max_exact))
                         * np.float32(N_BUCKETS - max_exact)).astype(np.int32)
    large = np.minimum(large, N_BUCKETS - 1)
    return np.where(dist < max_exact, dist, large).astype(np.int32)


def _rms(x, g):
    return x * lax.rsqrt(jnp.mean(x * x, axis=-1, keepdims=True) + EPS) * g


def _full_spec(shape):
    nd = len(shape)
    return pl.BlockSpec(shape, lambda *_: (0,) * nd)


def _bias_body(tab_ref, bkt_a_ref, bkt_near_ref, ba_ref, bnear_ref, *, far_bkt):
    def lookup(bkt, col):
        acc = jnp.zeros(bkt.shape, F32)
        for b in range(N_BUCKETS):
            acc = jnp.where(bkt == b, tab_ref[b, col], acc)
        return acc

    bkt_a = bkt_a_ref[...]
    bkt_near = bkt_near_ref[...]
    for h in range(A_HEADS):
        ba_ref[h] = lookup(bkt_a, h)
    for h in range(B_HEADS):
        col = A_HEADS + h
        bnear_ref[:, h * TQ:(h + 1) * TQ] = lookup(bkt_near, col) - tab_ref[far_bkt, col]


def _bias_call(rel_bias, far_bkt):
    q = np.arange(BLOCK)[:, None]
    k = np.arange(2 * BLOCK)[None, :]
    bkt_a = _t5_bucket_np(np.maximum(q + BLOCK - k, 0))
    k = np.arange(2 * TQ)[:, None]
    q = np.arange(TQ)[None, :]
    bkt_near = _t5_bucket_np(np.maximum(q + BLOCK - k, 0))
    vmem = pl.BlockSpec(memory_space=pltpu.VMEM)
    return pl.pallas_call(
        functools.partial(_bias_body, far_bkt=far_bkt),
        out_shape=(jax.ShapeDtypeStruct((A_HEADS, BLOCK, 2 * BLOCK), F32),
                   jax.ShapeDtypeStruct((2 * TQ, B_HEADS * TQ), F32)),
        in_specs=[pl.BlockSpec(memory_space=pltpu.SMEM), vmem, vmem],
        out_specs=(vmem, vmem),
        name="bias_tables",
    )(rel_bias, jnp.asarray(bkt_a), jnp.asarray(bkt_near))


def _proj_body(x_ref, g_ref, w1_ref, qg_ref, wuq_ref, wukt_ref, kvg_ref, ikg_ref, ikb_ref,
               aq_ref, akv_ref, qabs_ref, ckv_ref, iqs_ref, ik_ref, iw_ref, *, tq):
    tm = x_ref.shape[0]
    nblk = tm // tq
    hn = _rms(x_ref[...], g_ref[...]).astype(BF16)

    def proj(lo, hi):
        return jnp.dot(hn, w1_ref[:, lo:hi], preferred_element_type=F32)

    aq_ref[...] = (proj(C_AQ, C_AKV) * (A_HEAD_DIM ** -0.5)).astype(BF16)
    akv_ref[...] = proj(C_AKV, C_BQ).astype(BF16)

    qn = _rms(proj(C_BQ, C_BKV), qg_ref[...]).astype(BF16)
    q = jnp.dot(qn, wuq_ref[...], preferred_element_type=F32).astype(BF16)
    for h in range(B_HEADS):
        qa = jnp.dot(q[:, h * B_HEAD_DIM:(h + 1) * B_HEAD_DIM], wukt_ref[h], preferred_element_type=F32)
        qabs_ref[:, h] = (qa * (B_HEAD_DIM ** -0.5)).astype(BF16).reshape(nblk, tq, B_KV_RANK)

    ckv_ref[...] = _rms(proj(C_BKV, C_IQ), kvg_ref[...]).astype(BF16)

    iq = proj(C_IQ, C_IK).astype(BF16)
    for h in range(IDX_HEADS):
        iqs_ref[:, h] = iq[:, h * IDX_DIM:(h + 1) * IDX_DIM].reshape(nblk, tq, IDX_DIM)

    ikw = proj(C_IK, C_END)
    ik = ikw[:, :IDX_DIM]
    mu = jnp.mean(ik, axis=-1, keepdims=True)
    xc = ik - mu
    var = jnp.mean(xc * xc, axis=-1, keepdims=True)
    ik_ref[...] = (xc * lax.rsqrt(var + EPS) * ikg_ref[...] + ikb_ref[...]).astype(BF16)
    iw_ref[...] = ikw[:, LANES:] * ((IDX_HEADS * IDX_DIM) ** -0.5)


def _proj_call(x2, tm, tq, g, w1, qg, wuq, wukt, kvg, ikg, ikb):
    n, d = x2.shape
    grid = (n // tm,)
    row = lambda i: (i, 0)
    blk4 = lambda i: (i, 0, 0, 0)
    out_shape = (
        jax.ShapeDtypeStruct((n, A_HEADS * A_HEAD_DIM), BF16),
        jax.ShapeDtypeStruct((n, 2 * A_KV_HEADS * A_HEAD_DIM), BF16),
        jax.ShapeDtypeStruct((n // tq, B_HEADS, tq, B_KV_RANK), BF16),
        jax.ShapeDtypeStruct((n, B_KV_RANK), BF16),
        jax.ShapeDtypeStruct((n // tq, IDX_HEADS, tq, IDX_DIM), BF16),
        jax.ShapeDtypeStruct((n, IDX_DIM), BF16),
        jax.ShapeDtypeStruct((n, LANES), F32),
    )
    out_specs = (
        pl.BlockSpec((tm, A_HEADS * A_HEAD_DIM), row),
        pl.BlockSpec((tm, 2 * A_KV_HEADS * A_HEAD_DIM), row),
        pl.BlockSpec((tm // tq, B_HEADS, tq, B_KV_RANK), blk4),
        pl.BlockSpec((tm, B_KV_RANK), row),
        pl.BlockSpec((tm // tq, IDX_HEADS, tq, IDX_DIM), blk4),
        pl.BlockSpec((tm, IDX_DIM), row),
        pl.BlockSpec((tm, LANES), row),
    )
    in_specs = [pl.BlockSpec((tm, d), row), _full_spec(g.shape), _full_spec(w1.shape), _full_spec(qg.shape),
                _full_spec(wuq.shape), _full_spec(wukt.shape), _full_spec(kvg.shape), _full_spec(ikg.shape),
                _full_spec(ikb.shape)]
    return pl.pallas_call(
        functools.partial(_proj_body, tq=tq),
        grid=grid, in_specs=in_specs, out_specs=out_specs, out_shape=out_shape,
        compiler_params=pltpu.CompilerParams(dimension_semantics=("arbitrary",),
                                             vmem_limit_bytes=VMEM_LIMIT_BYTES),
        name="in_proj",
    )(x2, g, w1, qg, wuq, wukt, kvg, ikg, ikb)


def _swa_body(sinks_ref, aq_ref, kcur_ref, kprev_ref, kmeta_ref, bias_ref, o_ref):
    n = pl.program_id(1)
    kprev = jnp.where(n == 0, kmeta_ref[...], kprev_ref[0])
    kwin = jnp.concatenate([kprev, kcur_ref[0]], axis=0)
    aq = aq_ref[0]
    qi = lax.broadcasted_iota(jnp.int32, (BLOCK, 2 * BLOCK), 0)
    ki = lax.broadcasted_iota(jnp.int32, (BLOCK, 2 * BLOCK), 1)
    dist = qi + BLOCK - ki
    allowed = (dist >= 0) & (dist < WINDOW) & ((n > 0) | (ki >= PAD))
    grp = A_HEADS // A_KV_HEADS
    outs = []
    for h in range(A_HEADS):
        kvh = h // grp
        k = kwin[:, kvh * A_HEAD_DIM:(kvh + 1) * A_HEAD_DIM]
        v = kwin[:, (A_KV_HEADS + kvh) * A_HEAD_DIM:(A_KV_HEADS + kvh + 1) * A_HEAD_DIM]
        q = aq[:, h * A_HEAD_DIM:(h + 1) * A_HEAD_DIM]
        s = lax.dot_general(q, k, NT_DIMS, preferred_element_type=F32)
        s = jnp.where(allowed, s + bias_ref[h], NEG)
        sink = sinks_ref[h]
        m = jnp.maximum(jnp.max(s, axis=-1, keepdims=True), sink)
        e = jnp.exp(s - m)
        den = jnp.sum(e, axis=-1, keepdims=True) + jnp.exp(sink - m)
        p = (e / den).astype(BF16)
        outs.append(jnp.dot(p, v, preferred_element_type=F32))
    o_ref[0] = jnp.concatenate(outs, axis=1).astype(BF16)


def _swa_call(sinks, aq, akv, makv, bias_a):
    b, s, _ = aq.shape
    nb = s // BLOCK
    return pl.pallas_call(
        _swa_body,
        grid=(b, nb),
        in_specs=[pl.BlockSpec(memory_space=pltpu.SMEM),
                  pl.BlockSpec((1, BLOCK, aq.shape[2]), lambda i, n: (i, n, 0)),
                  pl.BlockSpec((1, BLOCK, akv.shape[2]), lambda i, n: (i, n, 0)),
                  pl.BlockSpec((1, BLOCK, akv.shape[2]), lambda i, n: (i, jnp.maximum(n - 1, 0), 0)),
                  _full_spec(makv.shape), _full_spec(bias_a.shape)],
        out_specs=pl.BlockSpec((1, BLOCK, aq.shape[2]), lambda i, n: (i, n, 0)),
        out_shape=jax.ShapeDtypeStruct(aq.shape, BF16),
        compiler_params=pltpu.CompilerParams(dimension_semantics=("arbitrary", "arbitrary"),
                                             vmem_limit_bytes=VMEM_LIMIT_BYTES),
        name="swa_sink_attention",
    )(sinks, aq, akv, akv, makv, bias_a)


def _mla_body(iqs_ref, iw_ref, qabs_ref, ikp_ref, ckvp_ref, bnear_ref, wuvt_ref, o_ref,
              key_sc, e_sc, m_sc, acc_sc, j_sc, *, topk):
    m_blk = pl.program_id(1)
    nchunks = m_blk + 2
    groups = TQ // SUBLANES
    acc_rows = B_KV_RANK + BF16_ROWS
    iq_all = iqs_ref[0].reshape(IDX_HEADS * TQ, IDX_DIM)
    qabs_all = qabs_ref[0].reshape(B_HEADS * TQ, B_KV_RANK)
    iw_t = iw_ref[0].T
    wrow = [iw_t[h:h + 1, :] for h in range(IDX_HEADS)]

    sub_pos = (lax.broadcasted_iota(jnp.int32, (groups, SUBLANES, TQ), 0) * SUBLANES
               + lax.broadcasted_iota(jnp.int32, (groups, SUBLANES, TQ), 1))
    qpos = BLOCK + m_blk * TQ + lax.broadcasted_iota(jnp.int32, (groups, SUBLANES, TQ), 2)

    def rows_of(c):
        return pl.ds(pl.multiple_of(c * TQ, TQ), TQ)

    def tiles(x):
        return x.reshape(groups, SUBLANES, x.shape[-1])

    def sort_key(sc):
        bits = lax.bitcast_convert_type(sc, jnp.int32)
        bits = jnp.where(sc == 0.0, 0, bits)
        return jnp.where(bits < 0, bits ^ 0x7FFFFFFF, bits)

    def score_chunk(c, carry):
        logits = lax.dot_general(ikp_ref[0, rows_of(c), :], iq_all, NT_DIMS, preferred_element_type=F32)
        sc = jnp.maximum(logits[:, 0:TQ], 0.0) * wrow[0]
        for h in range(1, IDX_HEADS):
            sc = sc + jnp.maximum(logits[:, h * TQ:(h + 1) * TQ], 0.0) * wrow[h]
        pos = c * TQ + sub_pos
        adm = (pos >= PAD) & (pos <= qpos)
        key_sc[rows_of(c), :] = jnp.where(adm, tiles(sort_key(sc)), INT_MIN).reshape(TQ, TQ)
        return carry

    lax.fori_loop(0, nchunks, score_chunk, 0)

    def count(pred):
        def body(c, part):
            hit = pred(tiles(key_sc[rows_of(c), :]), c * TQ + sub_pos)
            return part + jnp.sum(jnp.where(hit, 1.0, 0.0), axis=0)
        part = lax.fori_loop(0, nchunks, body, jnp.zeros((SUBLANES, TQ), F32))
        return jnp.broadcast_to(jnp.sum(part, axis=0, keepdims=True), (SUBLANES, TQ))

    kf = float(topk)

    def bisect(i, t):
        cand = t + jnp.left_shift(jnp.int32(1), 31 - i)
        cnt = count(lambda kp, pos: kp >= cand[None])
        return jnp.where(cnt >= kf, cand, t)

    thr = lax.fori_loop(0, 32, bisect, jnp.full((SUBLANES, TQ), INT_MIN, jnp.int32))
    cnt_gt = count(lambda kp, pos: kp > thr[None])
    cnt_ge = count(lambda kp, pos: kp >= thr[None])
    need = kf - cnt_gt
    ambiguous = (cnt_ge - cnt_gt != need) & (thr != INT_MIN)

    j_sc[...] = jnp.full((SUBLANES, TQ), 2 ** 30, jnp.int32)

    @pl.when(jnp.max(ambiguous.astype(F32)) > 0.0)
    def _():
        def tie(i, jt):
            cand = jt + jnp.left_shift(jnp.int32(1), 11 - i)
            cnt = count(lambda kp, pos: (kp == thr[None]) & (pos < cand[None]))
            return jnp.where(cnt < need, cand, jt)
        j_sc[...] = lax.fori_loop(0, 12, tie, jnp.zeros((SUBLANES, TQ), jnp.int32))

    jt = j_sc[...]

    m_sc[...] = jnp.full(m_sc.shape, NEG, F32)
    acc_sc[...] = jnp.zeros(acc_sc.shape, F32)

    def mla_chunk(c, bias_row0):
        ckv_c = ckvp_ref[0, rows_of(c), :]
        s_all = lax.dot_general(ckv_c, qabs_all, NT_DIMS, preferred_element_type=F32)
        kp = tiles(key_sc[rows_of(c), :])
        pos = c * TQ + sub_pos
        sel = ((kp > thr[None]) | ((kp == thr[None]) & (pos <= jt[None]))) & (kp != INT_MIN)
        alphas = []
        for h in range(B_HEADS):
            lanes = slice(h * TQ, (h + 1) * TQ)
            sh = s_all[:, lanes]
            if bias_row0 is not None:
                sh = sh + bnear_ref[bias_row0:bias_row0 + TQ, lanes]
            sh = jnp.where(sel, tiles(sh), NEG)
            m_prev = m_sc[:, lanes]
            mx = jnp.max(jnp.max(sh, axis=0), axis=0, keepdims=True)
            m_new = jnp.maximum(m_prev, mx)
            alphas.append(jnp.exp(m_prev - m_new))
            e = jnp.where(sel, jnp.exp(sh - m_new[None]), 0.0)
            e_sc[:, lanes] = e.reshape(TQ, TQ).astype(BF16)
            m_sc[:, lanes] = m_new
        vt = ckv_c.astype(F32).T.astype(BF16)
        vext = jnp.concatenate([vt, jnp.ones((BF16_ROWS, TQ), BF16)], axis=0)
        pv = jnp.dot(vext, e_sc[...], preferred_element_type=F32)
        alpha = jnp.concatenate(alphas, axis=1)
        acc = acc_sc[...].reshape(acc_rows // SUBLANES, SUBLANES, B_HEADS * TQ)
        acc = acc * alpha[None] + pv.reshape(acc.shape)
        acc_sc[...] = acc.reshape(acc_rows, B_HEADS * TQ)

    def far_loop(c, carry):
        mla_chunk(c, None)
        return carry

    lax.fori_loop(0, m_blk, far_loop, 0)
    mla_chunk(m_blk, 0)
    mla_chunk(m_blk + 1, TQ)

    acc = acc_sc[...]
    lat = (acc[0:B_KV_RANK] / acc[B_KV_RANK:B_KV_RANK + 1]).astype(BF16)
    outs = [jnp.dot(wuvt_ref[h], lat[:, h * TQ:(h + 1) * TQ], preferred_element_type=F32)
            for h in range(B_HEADS)]
    o_ref[0] = jnp.concatenate(outs, axis=0).T.astype(BF16)


def _mla_call(iqs, iw, qabs, ikp, ckvp, bnear, wuvt, topk):
    b, sp, _ = ikp.shape
    s = iw.shape[1]
    nq = s // TQ
    acc_rows = B_KV_RANK + BF16_ROWS
    return pl.pallas_call(
        functools.partial(_mla_body, topk=topk),
        grid=(b, nq),
        in_specs=[pl.BlockSpec((1, IDX_HEADS, TQ, IDX_DIM), lambda i, m: (i * nq + m, 0, 0, 0)),
                  pl.BlockSpec((1, TQ, LANES), lambda i, m: (i, m, 0)),
                  pl.BlockSpec((1, B_HEADS, TQ, B_KV_RANK), lambda i, m: (i * nq + m, 0, 0, 0)),
                  pl.BlockSpec((1, sp, IDX_DIM), lambda i, m: (i, 0, 0)),
                  pl.BlockSpec((1, sp, B_KV_RANK), lambda i, m: (i, 0, 0)),
                  _full_spec(bnear.shape), _full_spec(wuvt.shape)],
        out_specs=pl.BlockSpec((1, TQ, B_HEADS * B_HEAD_DIM), lambda i, m: (i, m, 0)),
        out_shape=jax.ShapeDtypeStruct((b, s, B_HEADS * B_HEAD_DIM), BF16),
        scratch_shapes=[pltpu.VMEM((sp, TQ), jnp.int32),
                        pltpu.VMEM((TQ, B_HEADS * TQ), BF16),
                        pltpu.VMEM((SUBLANES, B_HEADS * TQ), F32),
                        pltpu.VMEM((acc_rows, B_HEADS * TQ), F32),
                        pltpu.VMEM((SUBLANES, TQ), jnp.int32)],
        compiler_params=pltpu.CompilerParams(dimension_semantics=("arbitrary", "arbitrary"),
                                             vmem_limit_bytes=VMEM_LIMIT_BYTES),
        name="indexer_topk_mla",
    )(iqs, iw, qabs, ikp, ckvp, bnear, wuvt)


def _out_body(x_ref, oa_ref, ob_ref, ag_ref, wg_ref, bg_ref, wa_ref, wb_ref, wo_ref, fg_ref,
              wfg_ref, wfu_ref, wfd_ref, ng_ref, y_ref):
    d = x_ref.shape[1]
    x = x_ref[...]
    hn = _rms(x, ag_ref[...]).astype(BF16)

    def gated(o_ref, w_ref, lo):
        gate = jax.nn.sigmoid(jnp.dot(hn, wg_ref[:, lo:lo + d], preferred_element_type=F32) + bg_ref[:, lo:lo + d])
        return gate * jnp.dot(o_ref[...], w_ref[...], preferred_element_type=F32)

    mixed = (gated(oa_ref, wa_ref, 0) + gated(ob_ref, wb_ref, d)).astype(BF16)
    h = x + jnp.dot(mixed, wo_ref[...], preferred_element_type=F32)
    hn2 = _rms(h, fg_ref[...]).astype(BF16)
    dff = wfg_ref.shape[1]
    step = dff // FFN_CHUNKS
    for c in range(0, dff, step):
        g = jnp.dot(hn2, wfg_ref[:, c:c + step], preferred_element_type=F32)
        u = jnp.dot(hn2, wfu_ref[:, c:c + step], preferred_element_type=F32)
        act = (g * jax.nn.sigmoid(g) * u).astype(BF16)
        h = h + jnp.dot(act, wfd_ref[c:c + step, :], preferred_element_type=F32)
    y_ref[...] = _rms(h, ng_ref[...])


def _out_call(x2, oa, ob, ag, wg, bg, wa, wb, wo, fg, wfg, wfu, wfd, ng):
    n, d = x2.shape
    tm = TOK_TILE
    row = lambda i: (i, 0)

    def const_spec(a):
        return pl.BlockSpec(a.shape, lambda i: (0,) * a.ndim, pipeline_mode=pl.Buffered(1))

    consts = [ag, wg, bg, wa, wb, wo, fg, wfg, wfu, wfd, ng]
    return pl.pallas_call(
        _out_body,
        grid=(n // tm,),
        in_specs=[pl.BlockSpec((tm, d), row), pl.BlockSpec((tm, oa.shape[1]), row),
                  pl.BlockSpec((tm, ob.shape[1]), row)] + [const_spec(a) for a in consts],
        out_specs=pl.BlockSpec((tm, d), row),
        out_shape=jax.ShapeDtypeStruct((n, d), F32),
        compiler_params=pltpu.CompilerParams(dimension_semantics=("arbitrary",),
                                             vmem_limit_bytes=VMEM_LIMIT_BYTES),
        name="merge_ffn_norm",
    )(x2, oa, ob, *consts)


def kernel(x, meta_tokens, attn_norm_g, w_in, b_gates, q_norm_g, kv_norm_g, w_uq, w_uk, w_uv, idx_k_ln_g,
           idx_k_ln_b, sinks, rel_bias, w_branch_a, w_branch_b, w_out, ffn_norm_g, w_ffn_gate, w_ffn_up,
           w_ffn_down, final_norm_g):
    b, s, d = x.shape
    assert attn_norm_g.shape[0] == 1, "single-layer block"
    assert s % TQ == 0 and (b * s) % TOK_TILE == 0 and TOK_TILE % TQ == 0
    assert w_ffn_gate.shape[2] % (FFN_CHUNKS * LANES) == 0
    topk = min(TOPK_MAX, s // 4)
    far_bkts = np.unique(_t5_bucket_np(np.arange(BLOCK + 1, s + BLOCK + 1)))
    assert far_bkts.size == 1
    far_bkt = int(far_bkts[0])

    wi = w_in[0]
    n_in = C_BQ + B_Q_RANK + B_KV_RANK + IDX_HEADS * IDX_DIM + IDX_DIM + IDX_HEADS
    c0 = C_IK
    zpad = lambda k: jnp.zeros((d, k), wi.dtype)
    w1 = jnp.concatenate([wi[:, :c0], wi[:, c0:c0 + IDX_DIM], zpad(LANES - IDX_DIM),
                          wi[:, c0 + IDX_DIM:n_in], zpad(LANES - IDX_HEADS)], axis=1).astype(BF16)
    wg = wi[:, n_in:].astype(BF16)
    row2 = lambda v: v.reshape(1, -1).astype(F32)
    wukt = jnp.transpose(w_uk[0], (1, 2, 0)).astype(BF16)
    wuvt = jnp.transpose(w_uv[0], (1, 2, 0)).astype(BF16)
    proj_w = (row2(attn_norm_g[0]), w1, row2(q_norm_g[0]), w_uq[0].astype(BF16), wukt, row2(kv_norm_g[0]),
              row2(idx_k_ln_g[0]), row2(idx_k_ln_b[0]))

    bias_a, bnear = _bias_call(rel_bias, far_bkt)

    x2 = x.reshape(b * s, d)
    aq, akv, qabs, ckv, iqs, ik, iw = _proj_call(x2, TOK_TILE, TQ, *proj_w)
    meta_blk = jnp.concatenate([jnp.zeros((PAD, d), x.dtype), meta_tokens.astype(x.dtype)], axis=0)
    _, makv, _, mckv, _, mik, _ = _proj_call(meta_blk, BLOCK, BLOCK, *proj_w)

    def padded_keys(real, meta):
        c = real.shape[-1]
        return jnp.concatenate([jnp.broadcast_to(meta[None], (b, BLOCK, c)), real.reshape(b, s, c),
                                jnp.zeros((b, TQ - BLOCK, c), real.dtype)], axis=1)

    o_a = _swa_call(sinks[0], aq.reshape(b, s, -1), akv.reshape(b, s, -1), makv, bias_a)
    o_b = _mla_call(iqs, iw.reshape(b, s, -1), qabs, padded_keys(ik, mik), padded_keys(ckv, mckv), bnear, wuvt,
                    topk)

    y = _out_call(x2, o_a.reshape(b * s, -1), o_b.reshape(b * s, -1), row2(attn_norm_g[0]), wg,
                  row2(b_gates[0]), w_branch_a[0].astype(BF16), w_branch_b[0].astype(BF16),
                  w_out[0].astype(BF16), row2(ffn_norm_g[0]), w_ffn_gate[0].astype(BF16),
                  w_ffn_up[0].astype(BF16), w_ffn_down[0].astype(BF16), row2(final_norm_g))
    return y.reshape(b, s, d)
```

```python
import functools
import math

import numpy as np
import jax
import jax.numpy as jnp
from jax import lax
from jax.experimental import pallas as pl
from jax.experimental.pallas import tpu as pltpu

N_META = 16
BLOCK = 128
PAD = BLOCK - N_META
WINDOW = 128
A_HEADS = 8
A_KV_HEADS = 2
A_HEAD_DIM = 64
B_HEADS = 8
B_HEAD_DIM = 64
B_Q_RANK = 256
B_KV_RANK = 128
IDX_HEADS = 4
IDX_DIM = 64
TOPK_MAX = 256
N_BUCKETS = 32
MAX_DISTANCE = 128
EPS = 1e-6
NEG = -1e30
MAX_FLOOR = -3.0e38
INT_MIN = -(2 ** 31)

LANES = 128
SUBLANES = 8
BF16_ROWS = 16
VMEM_LIMIT_BYTES = 56 * 1024 * 1024

TOK_TILE = 512
TQ = 256
FFN_CHUNKS = 2

C_AQ = 0
C_AKV = C_AQ + A_HEADS * A_HEAD_DIM
C_BQ = C_AKV + 2 * A_KV_HEADS * A_HEAD_DIM
C_BKV = C_BQ + B_Q_RANK
C_IQ = C_BKV + B_KV_RANK
C_IK = C_IQ + IDX_HEADS * IDX_DIM
C_IW = C_IK + LANES
C_END = C_IW + LANES

F32 = jnp.float32
BF16 = jnp.bfloat16
NT_DIMS = (((1,), (1,)), ((), ()))


def _t5_bucket_np(dist):
    dist = np.asarray(dist, np.int64)
    max_exact = N_BUCKETS // 2
    d = np.maximum(dist, 1).astype(np.float32)
    large = max_exact + (np.log(d / np.float32(max_exact)) / np.float32(math.log(MAX_DISTANCE / max_exact))
                         * np.float32(N_BUCKETS - max_exact)).astype(np.int32)
    large = np.minimum(large, N_BUCKETS - 1)
    return np.where(dist < max_exact, dist, large).astype(np.int32)


def _rms(x, g):
    return x * lax.rsqrt(jnp.mean(x * x, axis=-1, keepdims=True) + EPS) * g


def _tree_reduce(op, x):
    while x.shape[0] > 1:
        half = x.shape[0] // 2
        x = op(x[:half], x[half:])
    return x[0]


def _full_spec(shape):
    nd = len(shape)
    return pl.BlockSpec(shape, lambda *_: (0,) * nd)


def _bias_body(tab_ref, bkt_a_ref, bkt_near_ref, ba_ref, bnear_ref, *, far_bkt):
    def lookup(bkt, col):
        acc = jnp.zeros(bkt.shape, F32)
        for b in range(N_BUCKETS):
            acc = jnp.where(bkt == b, tab_ref[b, col], acc)
        return acc

    bkt_a = bkt_a_ref[...]
    bkt_near = bkt_near_ref[...]
    for h in range(A_HEADS):
        ba_ref[h] = lookup(bkt_a, h)
    for h in range(B_HEADS):
        col = A_HEADS + h
        bnear_ref[:, h * TQ:(h + 1) * TQ] = lookup(bkt_near, col) - tab_ref[far_bkt, col]


def _bias_call(rel_bias, far_bkt):
    q = np.arange(BLOCK)[:, None]
    k = np.arange(2 * BLOCK)[None, :]
    bkt_a = _t5_bucket_np(np.maximum(q + BLOCK - k, 0))
    k = np.arange(2 * TQ)[:, None]
    q = np.arange(TQ)[None, :]
    bkt_near = _t5_bucket_np(np.maximum(q + BLOCK - k, 0))
    vmem = pl.BlockSpec(memory_space=pltpu.VMEM)
    return pl.pallas_call(
        functools.partial(_bias_body, far_bkt=far_bkt),
        out_shape=(jax.ShapeDtypeStruct((A_HEADS, BLOCK, 2 * BLOCK), F32),
                   jax.ShapeDtypeStruct((2 * TQ, B_HEADS * TQ), F32)),
        in_specs=[pl.BlockSpec(memory_space=pltpu.SMEM), vmem, vmem],
        out_specs=(vmem, vmem),
        name="bias_tables",
    )(rel_bias, jnp.asarray(bkt_a), jnp.asarray(bkt_near))


def _proj_body(x_ref, g_ref, w1_ref, qg_ref, wuq_ref, wukt_ref, kvg_ref, ikg_ref, ikb_ref,
               aq_ref, akv_ref, qabs_ref, ckv_ref, iqs_ref, ik_ref, iw_ref, *, tq):
    tm = x_ref.shape[0]
    nblk = tm // tq
    hn = _rms(x_ref[...], g_ref[...]).astype(BF16)

    def proj(lo, hi):
        return jnp.dot(hn, w1_ref[:, lo:hi], preferred_element_type=F32)

    aq_ref[...] = (proj(C_AQ, C_AKV) * (A_HEAD_DIM ** -0.5)).astype(BF16)
    akv_ref[...] = proj(C_AKV, C_BQ).astype(BF16)

    qn = _rms(proj(C_BQ, C_BKV), qg_ref[...]).astype(BF16)
    q = jnp.dot(qn, wuq_ref[...], preferred_element_type=F32).astype(BF16)
    for h in range(B_HEADS):
        qa = jnp.dot(q[:, h * B_HEAD_DIM:(h + 1) * B_HEAD_DIM], wukt_ref[h], preferred_element_type=F32)
        qabs_ref[:, h] = (qa * (B_HEAD_DIM ** -0.5)).astype(BF16).reshape(nblk, tq, B_KV_RANK)

    ckv_ref[...] = _rms(proj(C_BKV, C_IQ), kvg_ref[...]).astype(BF16)

    iq = proj(C_IQ, C_IK).astype(BF16)
    for h in range(IDX_HEADS):
        iqs_ref[:, h] = iq[:, h * IDX_DIM:(h + 1) * IDX_DIM].reshape(nblk, tq, IDX_DIM)

    ikw = proj(C_IK, C_END)
    ik = ikw[:, :IDX_DIM]
    mu = jnp.mean(ik, axis=-1, keepdims=True)
    xc = ik - mu
    var = jnp.mean(xc * xc, axis=-1, keepdims=True)
    ik_ref[...] = (xc * lax.rsqrt(var + EPS) * ikg_ref[...] + ikb_ref[...]).astype(BF16)
    iw_ref[...] = ikw[:, LANES:] * ((IDX_HEADS * IDX_DIM) ** -0.5)


def _proj_call(x2, tm, tq, g, w1, qg, wuq, wukt, kvg, ikg, ikb):
    n, d = x2.shape
    grid = (n // tm,)
    row = lambda i: (i, 0)
    blk4 = lambda i: (i, 0, 0, 0)
    out_shape = (
        jax.ShapeDtypeStruct((n, A_HEADS * A_HEAD_DIM), BF16),
        jax.ShapeDtypeStruct((n, 2 * A_KV_HEADS * A_HEAD_DIM), BF16),
        jax.ShapeDtypeStruct((n // tq, B_HEADS, tq, B_KV_RANK), BF16),
        jax.ShapeDtypeStruct((n, B_KV_RANK), BF16),
        jax.ShapeDtypeStruct((n // tq, IDX_HEADS, tq, IDX_DIM), BF16),
        jax.ShapeDtypeStruct((n, IDX_DIM), BF16),
        jax.ShapeDtypeStruct((n, LANES), F32),
    )
    out_specs = (
        pl.BlockSpec((tm, A_HEADS * A_HEAD_DIM), row),
        pl.BlockSpec((tm, 2 * A_KV_HEADS * A_HEAD_DIM), row),
        pl.BlockSpec((tm // tq, B_HEADS, tq, B_KV_RANK), blk4),
        pl.BlockSpec((tm, B_KV_RANK), row),
        pl.BlockSpec((tm // tq, IDX_HEADS, tq, IDX_DIM), blk4),
        pl.BlockSpec((tm, IDX_DIM), row),
        pl.BlockSpec((tm, LANES), row),
    )
    in_specs = [pl.BlockSpec((tm, d), row), _full_spec(g.shape), _full_spec(w1.shape), _full_spec(qg.shape),
                _full_spec(wuq.shape), _full_spec(wukt.shape), _full_spec(kvg.shape), _full_spec(ikg.shape),
                _full_spec(ikb.shape)]
    return pl.pallas_call(
        functools.partial(_proj_body, tq=tq),
        grid=grid, in_specs=in_specs, out_specs=out_specs, out_shape=out_shape,
        compiler_params=pltpu.CompilerParams(dimension_semantics=("arbitrary",),
                                             vmem_limit_bytes=VMEM_LIMIT_BYTES),
        name="in_proj",
    )(x2, g, w1, qg, wuq, wukt, kvg, ikg, ikb)


def _swa_body(sinks_ref, aq_ref, kcur_ref, kprev_ref, kmeta_ref, bias_ref, o_ref):
    n = pl.program_id(1)
    kprev = jnp.where(n == 0, kmeta_ref[...], kprev_ref[0])
    kwin = jnp.concatenate([kprev, kcur_ref[0]], axis=0)
    aq = aq_ref[0]
    qi = lax.broadcasted_iota(jnp.int32, (BLOCK, 2 * BLOCK), 0)
    ki = lax.broadcasted_iota(jnp.int32, (BLOCK, 2 * BLOCK), 1)
    dist = qi + BLOCK - ki
    allowed = (dist >= 0) & (dist < WINDOW) & ((n > 0) | (ki >= PAD))
    grp = A_HEADS // A_KV_HEADS
    outs = []
    for h in range(A_HEADS):
        kvh = h // grp
        k = kwin[:, kvh * A_HEAD_DIM:(kvh + 1) * A_HEAD_DIM]
        v = kwin[:, (A_KV_HEADS + kvh) * A_HEAD_DIM:(A_KV_HEADS + kvh + 1) * A_HEAD_DIM]
        q = aq[:, h * A_HEAD_DIM:(h + 1) * A_HEAD_DIM]
        s = lax.dot_general(q, k, NT_DIMS, preferred_element_type=F32)
        s = jnp.where(allowed, s + bias_ref[h], NEG)
        sink = sinks_ref[h]
        m = jnp.maximum(jnp.max(s, axis=-1, keepdims=True), sink)
        e = jnp.exp(s - m)
        den = jnp.sum(e, axis=-1, keepdims=True) + jnp.exp(sink - m)
        p = (e / den).astype(BF16)
        outs.append(jnp.dot(p, v, preferred_element_type=F32))
    o_ref[0] = jnp.concatenate(outs, axis=1).astype(BF16)


def _swa_call(sinks, aq, akv, makv, bias_a):
    b, s, _ = aq.shape
    nb = s // BLOCK
    return pl.pallas_call(
        _swa_body,
        grid=(b, nb),
        in_specs=[pl.BlockSpec(memory_space=pltpu.SMEM),
                  pl.BlockSpec((1, BLOCK, aq.shape[2]), lambda i, n: (i, n, 0)),
                  pl.BlockSpec((1, BLOCK, akv.shape[2]), lambda i, n: (i, n, 0)),
                  pl.BlockSpec((1, BLOCK, akv.shape[2]), lambda i, n: (i, jnp.maximum(n - 1, 0), 0)),
                  _full_spec(makv.shape), _full_spec(bias_a.shape)],
        out_specs=pl.BlockSpec((1, BLOCK, aq.shape[2]), lambda i, n: (i, n, 0)),
        out_shape=jax.ShapeDtypeStruct(aq.shape, BF16),
        compiler_params=pltpu.CompilerParams(dimension_semantics=("arbitrary", "arbitrary"),
                                             vmem_limit_bytes=VMEM_LIMIT_BYTES),
        name="swa_sink_attention",
    )(sinks, aq, akv, akv, makv, bias_a)


def _mla_body(iqs_ref, iw_ref, qabs_ref, ikp_ref, ckvp_ref, bnear_ref, wuvt_ref, o_ref,
              key_sc, e_sc, m_sc, acc_sc, j_sc, *, topk):
    m_blk = pl.program_id(1)
    nchunks = m_blk + 2
    groups = TQ // SUBLANES
    acc_rows = B_KV_RANK + BF16_ROWS
    iq_all = iqs_ref[0].reshape(IDX_HEADS * TQ, IDX_DIM)
    qabs_all = qabs_ref[0].reshape(B_HEADS * TQ, B_KV_RANK)
    iw_t = iw_ref[0].T
    wrow = [iw_t[h:h + 1, :] for h in range(IDX_HEADS)]

    sub_pos = (lax.broadcasted_iota(jnp.int32, (groups, SUBLANES, TQ), 0) * SUBLANES
               + lax.broadcasted_iota(jnp.int32, (groups, SUBLANES, TQ), 1))
    qpos = BLOCK + m_blk * TQ + lax.broadcasted_iota(jnp.int32, (groups, SUBLANES, TQ), 2)

    def rows_of(c):
        return pl.ds(pl.multiple_of(c * TQ, TQ), TQ)

    def tiles(x):
        return x.reshape(groups, SUBLANES, x.shape[-1])

    def sort_key(sc):
        bits = lax.bitcast_convert_type(sc, jnp.int32)
        bits = jnp.where(sc == 0.0, 0, bits)
        return jnp.where(bits < 0, bits ^ 0x7FFFFFFF, bits)

    def score_chunk(c, carry):
        logits = lax.dot_general(ikp_ref[0, rows_of(c), :], iq_all, NT_DIMS, preferred_element_type=F32)
        sc = jnp.maximum(logits[:, 0:TQ], 0.0) * wrow[0]
        for h in range(1, IDX_HEADS):
            sc = sc + jnp.maximum(logits[:, h * TQ:(h + 1) * TQ], 0.0) * wrow[h]
        pos = c * TQ + sub_pos
        adm = (pos >= PAD) & (pos <= qpos)
        key_sc[rows_of(c), :] = jnp.where(adm, tiles(sort_key(sc)), INT_MIN).reshape(TQ, TQ)
        return carry

    lax.fori_loop(0, nchunks, score_chunk, 0)

    def count(pred):
        def body(c, part):
            hit = pred(tiles(key_sc[rows_of(c), :]), c * TQ + sub_pos)
            return part + _tree_reduce(jnp.add, jnp.where(hit, 1.0, 0.0))
        part = lax.fori_loop(0, nchunks, body, jnp.zeros((SUBLANES, TQ), F32))
        return jnp.broadcast_to(jnp.sum(part, axis=0, keepdims=True), (SUBLANES, TQ))

    kf = float(topk)

    def bisect(i, t):
        cand = t + jnp.left_shift(jnp.int32(1), 31 - i)
        cnt = count(lambda kp, pos: kp >= cand[None])
        return jnp.where(cnt >= kf, cand, t)

    thr = lax.fori_loop(0, 32, bisect, jnp.full((SUBLANES, TQ), INT_MIN, jnp.int32))
    cnt_gt = count(lambda kp, pos: kp > thr[None])
    cnt_ge = count(lambda kp, pos: kp >= thr[None])
    need = kf - cnt_gt
    ambiguous = (cnt_ge - cnt_gt != need) & (thr != INT_MIN)

    j_sc[...] = jnp.full((SUBLANES, TQ), 2 ** 30, jnp.int32)

    @pl.when(jnp.max(ambiguous.astype(F32)) > 0.0)
    def _():
        def tie(i, jt):
            cand = jt + jnp.left_shift(jnp.int32(1), 11 - i)
            cnt = count(lambda kp, pos: (kp == thr[None]) & (pos < cand[None]))
            return jnp.where(cnt < need, cand, jt)
        j_sc[...] = lax.fori_loop(0, 12, tie, jnp.zeros((SUBLANES, TQ), jnp.int32))

    jt = j_sc[...]

    m_sc[...] = jnp.full(m_sc.shape, MAX_FLOOR, F32)
    acc_sc[...] = jnp.zeros(acc_sc.shape, F32)

    def mla_chunk(c, bias_row0):
        ckv_c = ckvp_ref[0, rows_of(c), :]
        s_all = lax.dot_general(ckv_c, qabs_all, NT_DIMS, preferred_element_type=F32)
        kp = tiles(key_sc[rows_of(c), :])
        pos = c * TQ + sub_pos
        sel = ((kp > thr[None]) | ((kp == thr[None]) & (pos <= jt[None]))) & (kp != INT_MIN)
        mask = jnp.where(sel, 0.0, -jnp.inf)
        alphas = []
        for h in range(B_HEADS):
            lanes = slice(h * TQ, (h + 1) * TQ)
            sh = s_all[:, lanes]
            if bias_row0 is not None:
                sh = sh + bnear_ref[bias_row0:bias_row0 + TQ, lanes]
            sh = tiles(sh) + mask
            m_prev = m_sc[:, lanes]
            mx = jnp.max(_tree_reduce(jnp.maximum, sh), axis=0, keepdims=True)
            m_new = jnp.maximum(m_prev, mx)
            alphas.append(jnp.exp(m_prev - m_new))
            e_sc[:, lanes] = jnp.exp(sh - m_new[None]).reshape(TQ, TQ).astype(BF16)
            m_sc[:, lanes] = m_new
        vt = ckv_c.astype(F32).T.astype(BF16)
        vext = jnp.concatenate([vt, jnp.ones((BF16_ROWS, TQ), BF16)], axis=0)
        pv = jnp.dot(vext, e_sc[...], preferred_element_type=F32)
        alpha = jnp.concatenate(alphas, axis=1)
        acc = acc_sc[...].reshape(acc_rows // SUBLANES, SUBLANES, B_HEADS * TQ)
        acc = acc * alpha[None] + pv.reshape(acc.shape)
        acc_sc[...] = acc.reshape(acc_rows, B_HEADS * TQ)

    def far_loop(c, carry):
        mla_chunk(c, None)
        return carry

    lax.fori_loop(0, m_blk, far_loop, 0)
    mla_chunk(m_blk, 0)
    mla_chunk(m_blk + 1, TQ)

    acc = acc_sc[...]
    lat = (acc[0:B_KV_RANK] / acc[B_KV_RANK:B_KV_RANK + 1]).astype(BF16)
    outs = [jnp.dot(wuvt_ref[h], lat[:, h * TQ:(h + 1) * TQ], preferred_element_type=F32)
            for h in range(B_HEADS)]
    o_ref[0] = jnp.concatenate(outs, axis=0).T.astype(BF16)


def _mla_call(iqs, iw, qabs, ikp, ckvp, bnear, wuvt, topk):
    b, sp, _ = ikp.shape
    s = iw.shape[1]
    nq = s // TQ
    acc_rows = B_KV_RANK + BF16_ROWS
    return pl.pallas_call(
        functools.partial(_mla_body, topk=topk),
        grid=(b, nq),
        in_specs=[pl.BlockSpec((1, IDX_HEADS, TQ, IDX_DIM), lambda i, m: (i * nq + m, 0, 0, 0)),
                  pl.BlockSpec((1, TQ, LANES), lambda i, m: (i, m, 0)),
                  pl.BlockSpec((1, B_HEADS, TQ, B_KV_RANK), lambda i, m: (i * nq + m, 0, 0, 0)),
                  pl.BlockSpec((1, sp, IDX_DIM), lambda i, m: (i, 0, 0)),
                  pl.BlockSpec((1, sp, B_KV_RANK), lambda i, m: (i, 0, 0)),
                  _full_spec(bnear.shape), _full_spec(wuvt.shape)],
        out_specs=pl.BlockSpec((1, TQ, B_HEADS * B_HEAD_DIM), lambda i, m: (i, m, 0)),
        out_shape=jax.ShapeDtypeStruct((b, s, B_HEADS * B_HEAD_DIM), BF16),
        scratch_shapes=[pltpu.VMEM((sp, TQ), jnp.int32),
                        pltpu.VMEM((TQ, B_HEADS * TQ), BF16),
                        pltpu.VMEM((SUBLANES, B_HEADS * TQ), F32),
                        pltpu.VMEM((acc_rows, B_HEADS * TQ), F32),
                        pltpu.VMEM((SUBLANES, TQ), jnp.int32)],
        compiler_params=pltpu.CompilerParams(dimension_semantics=("arbitrary", "arbitrary"),
                                             vmem_limit_bytes=VMEM_LIMIT_BYTES),
        name="indexer_topk_mla",
    )(iqs, iw, qabs, ikp, ckvp, bnear, wuvt)


def _out_body(x_ref, oa_ref, ob_ref, ag_ref, wg_ref, bg_ref, wa_ref, wb_ref, wo_ref, fg_ref,
              wfg_ref, wfu_ref, wfd_ref, ng_ref, y_ref):
    d = x_ref.shape[1]
    x = x_ref[...]
    hn = _rms(x, ag_ref[...]).astype(BF16)

    def gated(o_ref, w_ref, lo):
        gate = jax.nn.sigmoid(jnp.dot(hn, wg_ref[:, lo:lo + d], preferred_element_type=F32) + bg_ref[:, lo:lo + d])
        return gate * jnp.dot(o_ref[...], w_ref[...], preferred_element_type=F32)

    mixed = (gated(oa_ref, wa_ref, 0) + gated(ob_ref, wb_ref, d)).astype(BF16)
    h = x + jnp.dot(mixed, wo_ref[...], preferred_element_type=F32)
    hn2 = _rms(h, fg_ref[...]).astype(BF16)
    dff = wfg_ref.shape[1]
    step = dff // FFN_CHUNKS
    for c in range(0, dff, step):
        g = jnp.dot(hn2, wfg_ref[:, c:c + step], preferred_element_type=F32)
        u = jnp.dot(hn2, wfu_ref[:, c:c + step], preferred_element_type=F32)
        act = (g * jax.nn.sigmoid(g) * u).astype(BF16)
        h = h + jnp.dot(act, wfd_ref[c:c + step, :], preferred_element_type=F32)
    y_ref[...] = _rms(h, ng_ref[...])


def _out_call(x2, oa, ob, ag, wg, bg, wa, wb, wo, fg, wfg, wfu, wfd, ng):
    n, d = x2.shape
    tm = TOK_TILE
    row = lambda i: (i, 0)

    def const_spec(a):
        return pl.BlockSpec(a.shape, lambda i: (0,) * a.ndim, pipeline_mode=pl.Buffered(1))

    consts = [ag, wg, bg, wa, wb, wo, fg, wfg, wfu, wfd, ng]
    return pl.pallas_call(
        _out_body,
        grid=(n // tm,),
        in_specs=[pl.BlockSpec((tm, d), row), pl.BlockSpec((tm, oa.shape[1]), row),
                  pl.BlockSpec((tm, ob.shape[1]), row)] + [const_spec(a) for a in consts],
        out_specs=pl.BlockSpec((tm, d), row),
        out_shape=jax.ShapeDtypeStruct((n, d), F32),
        compiler_params=pltpu.CompilerParams(dimension_semantics=("arbitrary",),
                                             vmem_limit_bytes=VMEM_LIMIT_BYTES),
        name="merge_ffn_norm",
    )(x2, oa, ob, *consts)


def kernel(x, meta_tokens, attn_norm_g, w_in, b_gates, q_norm_g, kv_norm_g, w_uq, w_uk, w_uv, idx_k_ln_g,
           idx_k_ln_b, sinks, rel_bias, w_branch_a, w_branch_b, w_out, ffn_norm_g, w_ffn_gate, w_ffn_up,
           w_ffn_down, final_norm_g):
    b, s, d = x.shape
    assert attn_norm_g.shape[0] == 1, "single-layer block"
    assert s % TQ == 0 and (b * s) % TOK_TILE == 0 and TOK_TILE % TQ == 0
    assert w_ffn_gate.shape[2] % (FFN_CHUNKS * LANES) == 0
    topk = min(TOPK_MAX, s // 4)
    far_bkts = np.unique(_t5_bucket_np(np.arange(BLOCK + 1, s + BLOCK + 1)))
    assert far_bkts.size == 1
    far_bkt = int(far_bkts[0])

    wi = w_in[0]
    n_in = C_BQ + B_Q_RANK + B_KV_RANK + IDX_HEADS * IDX_DIM + IDX_DIM + IDX_HEADS
    c0 = C_IK
    zpad = lambda k: jnp.zeros((d, k), wi.dtype)
    w1 = jnp.concatenate([wi[:, :c0], wi[:, c0:c0 + IDX_DIM], zpad(LANES - IDX_DIM),
                          wi[:, c0 + IDX_DIM:n_in], zpad(LANES - IDX_HEADS)], axis=1).astype(BF16)
    wg = wi[:, n_in:].astype(BF16)
    row2 = lambda v: v.reshape(1, -1).astype(F32)
    wukt = jnp.transpose(w_uk[0], (1, 2, 0)).astype(BF16)
    wuvt = jnp.transpose(w_uv[0], (1, 2, 0)).astype(BF16)
    proj_w = (row2(attn_norm_g[0]), w1, row2(q_norm_g[0]), w_uq[0].astype(BF16), wukt, row2(kv_norm_g[0]),
              row2(idx_k_ln_g[0]), row2(idx_k_ln_b[0]))

    bias_a, bnear = _bias_call(rel_bias, far_bkt)

    x2 = x.reshape(b * s, d)
    aq, akv, qabs, ckv, iqs, ik, iw = _proj_call(x2, TOK_TILE, TQ, *proj_w)
    meta_blk = jnp.concatenate([jnp.zeros((PAD, d), x.dtype), meta_tokens.astype(x.dtype)], axis=0)
    _, makv, _, mckv, _, mik, _ = _proj_call(meta_blk, BLOCK, BLOCK, *proj_w)

    def padded_keys(real, meta):
        c = real.shape[-1]
        return jnp.concatenate([jnp.broadcast_to(meta[None], (b, BLOCK, c)), real.reshape(b, s, c),
                                jnp.zeros((b, TQ - BLOCK, c), real.dtype)], axis=1)

    o_a = _swa_call(sinks[0], aq.reshape(b, s, -1), akv.reshape(b, s, -1), makv, bias_a)
    o_b = _mla_call(iqs, iw.reshape(b, s, -1), qabs, padded_keys(ik, mik), padded_keys(ckv, mckv), bnear, wuvt,
                    topk)

    y = _out_call(x2, o_a.reshape(b * s, -1), o_b.reshape(b * s, -1), row2(attn_norm_g[0]), wg,
                  row2(b_gates[0]), w_branch_a[0].astype(BF16), w_branch_b[0].astype(BF16),
                  w_out[0].astype(BF16), row2(ffn_norm_g[0]), w_ffn_gate[0].astype(BF16),
                  w_ffn_up[0].astype(BF16), w_ffn_down[0].astype(BF16), row2(final_norm_g))
    return y.reshape(b, s, d)
```

```python
import functools
import math

import numpy as np
import jax
import jax.numpy as jnp
from jax import lax
from jax.experimental import pallas as pl
from jax.experimental.pallas import tpu as pltpu

N_META = 16
BLOCK = 128
PAD = BLOCK - N_META
WINDOW = 128
A_HEADS = 8
A_KV_HEADS = 2
A_HEAD_DIM = 64
B_HEADS = 8
B_HEAD_DIM = 64
B_Q_RANK = 256
B_KV_RANK = 128
IDX_HEADS = 4
IDX_DIM = 64
TOPK_MAX = 256
N_BUCKETS = 32
MAX_DISTANCE = 128
EPS = 1e-6
NEG = -1e30
MAX_FLOOR = -3.0e38
INT_MIN = -(2 ** 31)

LANES = 128
SUBLANES = 8
BF16_ROWS = 16
VMEM_LIMIT_BYTES = 56 * 1024 * 1024

TOK_TILE = 512
TQ = 256
FFN_CHUNKS = 2

C_AQ = 0
C_AKV = C_AQ + A_HEADS * A_HEAD_DIM
C_BQ = C_AKV + 2 * A_KV_HEADS * A_HEAD_DIM
C_BKV = C_BQ + B_Q_RANK
C_IQ = C_BKV + B_KV_RANK
C_IK = C_IQ + IDX_HEADS * IDX_DIM
C_IW = C_IK + LANES
C_END = C_IW + LANES

F32 = jnp.float32
BF16 = jnp.bfloat16
NT_DIMS = (((1,), (1,)), ((), ()))


def _t5_bucket_np(dist):
    dist = np.asarray(dist, np.int64)
    max_exact = N_BUCKETS // 2
    d = np.maximum(dist, 1).astype(np.float32)
    large = max_exact + (np.log(d / np.float32(max_exact)) / np.float32(math.log(MAX_DISTANCE / max_exact))
                         * np.float32(N_BUCKETS - max_exact)).astype(np.int32)
    large = np.minimum(large, N_BUCKETS - 1)
    return np.where(dist < max_exact, dist, large).astype(np.int32)


def _rms(x, g):
    return x * lax.rsqrt(jnp.mean(x * x, axis=-1, keepdims=True) + EPS) * g


def _tree_reduce(op, x):
    while x.shape[0] > 1:
        half = x.shape[0] // 2
        x = op(x[:half], x[half:])
    return x[0]


def _full_spec(shape):
    nd = len(shape)
    return pl.BlockSpec(shape, lambda *_: (0,) * nd)


def _bias_body(tab_ref, bkt_a_ref, bkt_near_ref, ba_ref, bnear_ref, *, far_bkt):
    def lookup(bkt, col):
        acc = jnp.zeros(bkt.shape, F32)
        for b in range(N_BUCKETS):
            acc = jnp.where(bkt == b, tab_ref[b, col], acc)
        return acc

    bkt_a = bkt_a_ref[...]
    bkt_near = bkt_near_ref[...]
    for h in range(A_HEADS):
        ba_ref[h] = lookup(bkt_a, h)
    for h in range(B_HEADS):
        col = A_HEADS + h
        bnear_ref[h] = lookup(bkt_near, col) - tab_ref[far_bkt, col]


def _bias_call(rel_bias, far_bkt):
    q = np.arange(BLOCK)[:, None]
    k = np.arange(2 * BLOCK)[None, :]
    bkt_a = _t5_bucket_np(np.maximum(q + BLOCK - k, 0))
    k = np.arange(2 * TQ)[:, None]
    q = np.arange(TQ)[None, :]
    bkt_near = _t5_bucket_np(np.maximum(q + BLOCK - k, 0))
    vmem = pl.BlockSpec(memory_space=pltpu.VMEM)
    return pl.pallas_call(
        functools.partial(_bias_body, far_bkt=far_bkt),
        out_shape=(jax.ShapeDtypeStruct((A_HEADS, BLOCK, 2 * BLOCK), F32),
                   jax.ShapeDtypeStruct((B_HEADS, 2 * TQ, TQ), F32)),
        in_specs=[pl.BlockSpec(memory_space=pltpu.SMEM), vmem, vmem],
        out_specs=(vmem, vmem),
        name="bias_tables",
    )(rel_bias, jnp.asarray(bkt_a), jnp.asarray(bkt_near))


def _proj_body(x_ref, g_ref, w1_ref, qg_ref, wuq_ref, wukt_ref, kvg_ref, ikg_ref, ikb_ref,
               aq_ref, akv_ref, qabs_ref, ckv_ref, iqs_ref, ik_ref, iw_ref, *, tq):
    tm = x_ref.shape[0]
    nblk = tm // tq
    hn = _rms(x_ref[...], g_ref[...]).astype(BF16)

    def proj(lo, hi):
        return jnp.dot(hn, w1_ref[:, lo:hi], preferred_element_type=F32)

    aq_ref[...] = (proj(C_AQ, C_AKV) * (A_HEAD_DIM ** -0.5)).astype(BF16)
    akv_ref[...] = proj(C_AKV, C_BQ).astype(BF16)

    qn = _rms(proj(C_BQ, C_BKV), qg_ref[...]).astype(BF16)
    q = jnp.dot(qn, wuq_ref[...], preferred_element_type=F32).astype(BF16)
    for h in range(B_HEADS):
        qa = jnp.dot(q[:, h * B_HEAD_DIM:(h + 1) * B_HEAD_DIM], wukt_ref[h], preferred_element_type=F32)
        qabs_ref[:, h] = (qa * (B_HEAD_DIM ** -0.5)).astype(BF16).reshape(nblk, tq, B_KV_RANK)

    ckv_ref[...] = _rms(proj(C_BKV, C_IQ), kvg_ref[...]).astype(BF16)

    iq = proj(C_IQ, C_IK).astype(BF16)
    for h in range(IDX_HEADS):
        iqs_ref[:, h] = iq[:, h * IDX_DIM:(h + 1) * IDX_DIM].reshape(nblk, tq, IDX_DIM)

    ikw = proj(C_IK, C_END)
    ik = ikw[:, :IDX_DIM]
    mu = jnp.mean(ik, axis=-1, keepdims=True)
    xc = ik - mu
    var = jnp.mean(xc * xc, axis=-1, keepdims=True)
    ik_ref[...] = (xc * lax.rsqrt(var + EPS) * ikg_ref[...] + ikb_ref[...]).astype(BF16)
    iw_ref[...] = ikw[:, LANES:] * ((IDX_HEADS * IDX_DIM) ** -0.5)


def _proj_call(x2, tm, tq, g, w1, qg, wuq, wukt, kvg, ikg, ikb):
    n, d = x2.shape
    grid = (n // tm,)
    row = lambda i: (i, 0)
    blk4 = lambda i: (i, 0, 0, 0)
    out_shape = (
        jax.ShapeDtypeStruct((n, A_HEADS * A_HEAD_DIM), BF16),
        jax.ShapeDtypeStruct((n, 2 * A_KV_HEADS * A_HEAD_DIM), BF16),
        jax.ShapeDtypeStruct((n // tq, B_HEADS, tq, B_KV_RANK), BF16),
        jax.ShapeDtypeStruct((n, B_KV_RANK), BF16),
        jax.ShapeDtypeStruct((n // tq, IDX_HEADS, tq, IDX_DIM), BF16),
        jax.ShapeDtypeStruct((n, IDX_DIM), BF16),
        jax.ShapeDtypeStruct((n, LANES), F32),
    )
    out_specs = (
        pl.BlockSpec((tm, A_HEADS * A_HEAD_DIM), row),
        pl.BlockSpec((tm, 2 * A_KV_HEADS * A_HEAD_DIM), row),
        pl.BlockSpec((tm // tq, B_HEADS, tq, B_KV_RANK), blk4),
        pl.BlockSpec((tm, B_KV_RANK), row),
        pl.BlockSpec((tm // tq, IDX_HEADS, tq, IDX_DIM), blk4),
        pl.BlockSpec((tm, IDX_DIM), row),
        pl.BlockSpec((tm, LANES), row),
    )
    in_specs = [pl.BlockSpec((tm, d), row), _full_spec(g.shape), _full_spec(w1.shape), _full_spec(qg.shape),
                _full_spec(wuq.shape), _full_spec(wukt.shape), _full_spec(kvg.shape), _full_spec(ikg.shape),
                _full_spec(ikb.shape)]
    return pl.pallas_call(
        functools.partial(_proj_body, tq=tq),
        grid=grid, in_specs=in_specs, out_specs=out_specs, out_shape=out_shape,
        compiler_params=pltpu.CompilerParams(dimension_semantics=("arbitrary",),
                                             vmem_limit_bytes=VMEM_LIMIT_BYTES),
        name="in_proj",
    )(x2, g, w1, qg, wuq, wukt, kvg, ikg, ikb)


def _swa_body(sinks_ref, aq_ref, kcur_ref, kprev_ref, kmeta_ref, bias_ref, o_ref):
    n = pl.program_id(1)
    kprev = jnp.where(n == 0, kmeta_ref[...], kprev_ref[0])
    kwin = jnp.concatenate([kprev, kcur_ref[0]], axis=0)
    aq = aq_ref[0]
    qi = lax.broadcasted_iota(jnp.int32, (BLOCK, 2 * BLOCK), 0)
    ki = lax.broadcasted_iota(jnp.int32, (BLOCK, 2 * BLOCK), 1)
    dist = qi + BLOCK - ki
    allowed = (dist >= 0) & (dist < WINDOW) & ((n > 0) | (ki >= PAD))
    grp = A_HEADS // A_KV_HEADS
    outs = []
    for h in range(A_HEADS):
        kvh = h // grp
        k = kwin[:, kvh * A_HEAD_DIM:(kvh + 1) * A_HEAD_DIM]
        v = kwin[:, (A_KV_HEADS + kvh) * A_HEAD_DIM:(A_KV_HEADS + kvh + 1) * A_HEAD_DIM]
        q = aq[:, h * A_HEAD_DIM:(h + 1) * A_HEAD_DIM]
        s = lax.dot_general(q, k, NT_DIMS, preferred_element_type=F32)
        s = jnp.where(allowed, s + bias_ref[h], NEG)
        sink = sinks_ref[h]
        m = jnp.maximum(jnp.max(s, axis=-1, keepdims=True), sink)
        e = jnp.exp(s - m)
        den = jnp.sum(e, axis=-1, keepdims=True) + jnp.exp(sink - m)
        p = (e / den).astype(BF16)
        outs.append(jnp.dot(p, v, preferred_element_type=F32))
    o_ref[0] = jnp.concatenate(outs, axis=1).astype(BF16)


def _swa_call(sinks, aq, akv, makv, bias_a):
    b, s, _ = aq.shape
    nb = s // BLOCK
    return pl.pallas_call(
        _swa_body,
        grid=(b, nb),
        in_specs=[pl.BlockSpec(memory_space=pltpu.SMEM),
                  pl.BlockSpec((1, BLOCK, aq.shape[2]), lambda i, n: (i, n, 0)),
                  pl.BlockSpec((1, BLOCK, akv.shape[2]), lambda i, n: (i, n, 0)),
                  pl.BlockSpec((1, BLOCK, akv.shape[2]), lambda i, n: (i, jnp.maximum(n - 1, 0), 0)),
                  _full_spec(makv.shape), _full_spec(bias_a.shape)],
        out_specs=pl.BlockSpec((1, BLOCK, aq.shape[2]), lambda i, n: (i, n, 0)),
        out_shape=jax.ShapeDtypeStruct(aq.shape, BF16),
        compiler_params=pltpu.CompilerParams(dimension_semantics=("arbitrary", "arbitrary"),
                                             vmem_limit_bytes=VMEM_LIMIT_BYTES),
        name="swa_sink_attention",
    )(sinks, aq, akv, akv, makv, bias_a)


def _mla_body(iqs_ref, iw_ref, qabs_ref, ikp_ref, ckvp_ref, bnear_ref, wuvt_ref, o_ref,
              key_sc, s_sc, m_sc, acc_sc, j_sc, *, topk):
    m_blk = pl.program_id(1)
    nchunks = m_blk + 2
    groups = TQ // SUBLANES
    acc_rows = B_KV_RANK + BF16_ROWS
    iw_t = iw_ref[0].T
    wrow = [iw_t[h:h + 1, :] for h in range(IDX_HEADS)]

    sub_pos = (lax.broadcasted_iota(jnp.int32, (groups, SUBLANES, TQ), 0) * SUBLANES
               + lax.broadcasted_iota(jnp.int32, (groups, SUBLANES, TQ), 1))
    qpos = BLOCK + m_blk * TQ + lax.broadcasted_iota(jnp.int32, (groups, SUBLANES, TQ), 2)

    def rows_of(c):
        return pl.ds(pl.multiple_of(c * TQ, TQ), TQ)

    def tiles(x):
        return x.reshape(groups, SUBLANES, x.shape[-1])

    def sort_key(sc):
        bits = lax.bitcast_convert_type(sc, jnp.int32)
        bits = jnp.where(sc == 0.0, 0, bits)
        return jnp.where(bits < 0, bits ^ 0x7FFFFFFF, bits)

    def score_chunk(c, carry):
        ik_c = ikp_ref[0, rows_of(c), :]
        sc = None
        for h in range(IDX_HEADS):
            logits = lax.dot_general(ik_c, iqs_ref[0, h], NT_DIMS, preferred_element_type=F32)
            term = jnp.maximum(logits, 0.0) * wrow[h]
            sc = term if sc is None else sc + term
        pos = c * TQ + sub_pos
        adm = (pos >= PAD) & (pos <= qpos)
        key_sc[rows_of(c), :] = jnp.where(adm, tiles(sort_key(sc)), INT_MIN).reshape(TQ, TQ)
        return carry

    lax.fori_loop(0, nchunks, score_chunk, 0)

    def count(pred):
        def body(c, part):
            hit = pred(tiles(key_sc[rows_of(c), :]), c * TQ + sub_pos)
            return part + _tree_reduce(jnp.add, jnp.where(hit, 1.0, 0.0))
        part = lax.fori_loop(0, nchunks, body, jnp.zeros((SUBLANES, TQ), F32))
        return jnp.broadcast_to(jnp.sum(part, axis=0, keepdims=True), (SUBLANES, TQ))

    kf = float(topk)

    def bisect(i, t):
        cand = t + jnp.left_shift(jnp.int32(1), 31 - i)
        cnt = count(lambda kp, pos: kp >= cand[None])
        return jnp.where(cnt >= kf, cand, t)

    thr = lax.fori_loop(0, 32, bisect, jnp.full((SUBLANES, TQ), INT_MIN, jnp.int32))
    cnt_gt = count(lambda kp, pos: kp > thr[None])
    cnt_ge = count(lambda kp, pos: kp >= thr[None])
    need = kf - cnt_gt
    ambiguous = (cnt_ge - cnt_gt != need) & (thr != INT_MIN)

    j_sc[...] = jnp.full((SUBLANES, TQ), 2 ** 30, jnp.int32)

    @pl.when(jnp.max(ambiguous.astype(F32)) > 0.0)
    def _():
        def tie(i, jt):
            cand = jt + jnp.left_shift(jnp.int32(1), 11 - i)
            cnt = count(lambda kp, pos: (kp == thr[None]) & (pos < cand[None]))
            return jnp.where(cnt < need, cand, jt)
        j_sc[...] = lax.fori_loop(0, 12, tie, jnp.zeros((SUBLANES, TQ), jnp.int32))

    jt = j_sc[...]

    m_sc[...] = jnp.full(m_sc.shape, MAX_FLOOR, F32)
    acc_sc[...] = jnp.zeros(acc_sc.shape, F32)

    def mla_chunk(c, bias_row0):
        ckv_c = ckvp_ref[0, rows_of(c), :]
        kp = tiles(key_sc[rows_of(c), :])
        pos = c * TQ + sub_pos
        sel = ((kp > thr[None]) | ((kp == thr[None]) & (pos <= jt[None]))) & (kp != INT_MIN)
        mask = jnp.where(sel, 0.0, -jnp.inf)
        vt = ckv_c.astype(F32).T.astype(BF16)
        vext = jnp.concatenate([vt, jnp.ones((BF16_ROWS, TQ), BF16)], axis=0)
        alphas = []
        for h in range(B_HEADS):
            sh = lax.dot_general(ckv_c, qabs_ref[0, h], NT_DIMS, preferred_element_type=F32)
            if bias_row0 is not None:
                sh = sh + bnear_ref[h, pl.ds(bias_row0, TQ), :]
            sh = tiles(sh) + mask
            s_sc[h] = sh.reshape(TQ, TQ)
            m_prev = m_sc[h]
            mx = jnp.max(_tree_reduce(jnp.maximum, sh), axis=0, keepdims=True)
            m_new = jnp.maximum(m_prev, mx)
            alphas.append(jnp.exp(m_prev - m_new))
            m_sc[h] = m_new
        for h in range(B_HEADS):
            e = jnp.exp(tiles(s_sc[h]) - m_sc[h][None]).reshape(TQ, TQ).astype(BF16)
            pv = jnp.dot(vext, e, preferred_element_type=F32)
            acc = acc_sc[h].reshape(acc_rows // SUBLANES, SUBLANES, TQ)
            acc_sc[h] = (acc * alphas[h][None] + pv.reshape(acc.shape)).reshape(acc_rows, TQ)

    def far_loop(c, carry):
        mla_chunk(c, None)
        return carry

    def near_loop(i, carry):
        mla_chunk(m_blk + i, pl.multiple_of(i * TQ, TQ))
        return carry

    lax.fori_loop(0, m_blk, far_loop, 0)
    lax.fori_loop(0, 2, near_loop, 0)

    outs = []
    for h in range(B_HEADS):
        acc = acc_sc[h]
        lat = (acc[0:B_KV_RANK] / acc[B_KV_RANK:B_KV_RANK + 1]).astype(BF16)
        outs.append(jnp.dot(wuvt_ref[h], lat, preferred_element_type=F32))
    o_ref[0] = jnp.concatenate(outs, axis=0).T.astype(BF16)


def _mla_call(iqs, iw, qabs, ikp, ckvp, bnear, wuvt, topk):
    b, sp, _ = ikp.shape
    s = iw.shape[1]
    nq = s // TQ
    acc_rows = B_KV_RANK + BF16_ROWS
    return pl.pallas_call(
        functools.partial(_mla_body, topk=topk),
        grid=(b, nq),
        in_specs=[pl.BlockSpec((1, IDX_HEADS, TQ, IDX_DIM), lambda i, m: (i * nq + m, 0, 0, 0)),
                  pl.BlockSpec((1, TQ, LANES), lambda i, m: (i, m, 0)),
                  pl.BlockSpec((1, B_HEADS, TQ, B_KV_RANK), lambda i, m: (i * nq + m, 0, 0, 0)),
                  pl.BlockSpec((1, sp, IDX_DIM), lambda i, m: (i, 0, 0)),
                  pl.BlockSpec((1, sp, B_KV_RANK), lambda i, m: (i, 0, 0)),
                  _full_spec(bnear.shape), _full_spec(wuvt.shape)],
        out_specs=pl.BlockSpec((1, TQ, B_HEADS * B_HEAD_DIM), lambda i, m: (i, m, 0)),
        out_shape=jax.ShapeDtypeStruct((b, s, B_HEADS * B_HEAD_DIM), BF16),
        scratch_shapes=[pltpu.VMEM((sp, TQ), jnp.int32),
                        pltpu.VMEM((B_HEADS, TQ, TQ), F32),
                        pltpu.VMEM((B_HEADS, SUBLANES, TQ), F32),
                        pltpu.VMEM((B_HEADS, acc_rows, TQ), F32),
                        pltpu.VMEM((SUBLANES, TQ), jnp.int32)],
        compiler_params=pltpu.CompilerParams(dimension_semantics=("arbitrary", "arbitrary"),
                                             vmem_limit_bytes=VMEM_LIMIT_BYTES),
        name="indexer_topk_mla",
    )(iqs, iw, qabs, ikp, ckvp, bnear, wuvt)


def _out_body(x_ref, oa_ref, ob_ref, ag_ref, wg_ref, bg_ref, wa_ref, wb_ref, wo_ref, fg_ref,
              wfg_ref, wfu_ref, wfd_ref, ng_ref, y_ref):
    d = x_ref.shape[1]
    x = x_ref[...]
    hn = _rms(x, ag_ref[...]).astype(BF16)

    def gated(o_ref, w_ref, lo):
        gate = jax.nn.sigmoid(jnp.dot(hn, wg_ref[:, lo:lo + d], preferred_element_type=F32) + bg_ref[:, lo:lo + d])
        return gate * jnp.dot(o_ref[...], w_ref[...], preferred_element_type=F32)

    mixed = (gated(oa_ref, wa_ref, 0) + gated(ob_ref, wb_ref, d)).astype(BF16)
    h = x + jnp.dot(mixed, wo_ref[...], preferred_element_type=F32)
    hn2 = _rms(h, fg_ref[...]).astype(BF16)
    dff = wfg_ref.shape[1]
    step = dff // FFN_CHUNKS
    for c in range(0, dff, step):
        g = jnp.dot(hn2, wfg_ref[:, c:c + step], preferred_element_type=F32)
        u = jnp.dot(hn2, wfu_ref[:, c:c + step], preferred_element_type=F32)
        act = (g * jax.nn.sigmoid(g) * u).astype(BF16)
        h = h + jnp.dot(act, wfd_ref[c:c + step, :], preferred_element_type=F32)
    y_ref[...] = _rms(h, ng_ref[...])


def _out_call(x2, oa, ob, ag, wg, bg, wa, wb, wo, fg, wfg, wfu, wfd, ng):
    n, d = x2.shape
    tm = TOK_TILE
    row = lambda i: (i, 0)

    def const_spec(a):
        return pl.BlockSpec(a.shape, lambda i: (0,) * a.ndim, pipeline_mode=pl.Buffered(1))

    consts = [ag, wg, bg, wa, wb, wo, fg, wfg, wfu, wfd, ng]
    return pl.pallas_call(
        _out_body,
        grid=(n // tm,),
        in_specs=[pl.BlockSpec((tm, d), row), pl.BlockSpec((tm, oa.shape[1]), row),
                  pl.BlockSpec((tm, ob.shape[1]), row)] + [const_spec(a) for a in consts],
        out_specs=pl.BlockSpec((tm, d), row),
        out_shape=jax.ShapeDtypeStruct((n, d), F32),
        compiler_params=pltpu.CompilerParams(dimension_semantics=("arbitrary",),
                                             vmem_limit_bytes=VMEM_LIMIT_BYTES),
        name="merge_ffn_norm",
    )(x2, oa, ob, *consts)


def kernel(x, meta_tokens, attn_norm_g, w_in, b_gates, q_norm_g, kv_norm_g, w_uq, w_uk, w_uv, idx_k_ln_g,
           idx_k_ln_b, sinks, rel_bias, w_branch_a, w_branch_b, w_out, ffn_norm_g, w_ffn_gate, w_ffn_up,
           w_ffn_down, final_norm_g):
    b, s, d = x.shape
    assert attn_norm_g.shape[0] == 1, "single-layer block"
    assert s % TQ == 0 and (b * s) % TOK_TILE == 0 and TOK_TILE % TQ == 0
    assert w_ffn_gate.shape[2] % (FFN_CHUNKS * LANES) == 0
    topk = min(TOPK_MAX, s // 4)
    far_bkts = np.unique(_t5_bucket_np(np.arange(BLOCK + 1, s + BLOCK + 1)))
    assert far_bkts.size == 1
    far_bkt = int(far_bkts[0])

    wi = w_in[0]
    n_in = C_BQ + B_Q_RANK + B_KV_RANK + IDX_HEADS * IDX_DIM + IDX_DIM + IDX_HEADS
    c0 = C_IK
    zpad = lambda k: jnp.zeros((d, k), wi.dtype)
    w1 = jnp.concatenate([wi[:, :c0], wi[:, c0:c0 + IDX_DIM], zpad(LANES - IDX_DIM),
                          wi[:, c0 + IDX_DIM:n_in], zpad(LANES - IDX_HEADS)], axis=1).astype(BF16)
    wg = wi[:, n_in:].astype(BF16)
    row2 = lambda v: v.reshape(1, -1).astype(F32)
    wukt = jnp.transpose(w_uk[0], (1, 2, 0)).astype(BF16)
    wuvt = jnp.transpose(w_uv[0], (1, 2, 0)).astype(BF16)
    proj_w = (row2(attn_norm_g[0]), w1, row2(q_norm_g[0]), w_uq[0].astype(BF16), wukt, row2(kv_norm_g[0]),
              row2(idx_k_ln_g[0]), row2(idx_k_ln_b[0]))

    bias_a, bnear = _bias_call(rel_bias, far_bkt)

    x2 = x.reshape(b * s, d)
    aq, akv, qabs, ckv, iqs, ik, iw = _proj_call(x2, TOK_TILE, TQ, *proj_w)
    meta_blk = jnp.concatenate([jnp.zeros((PAD, d), x.dtype), meta_tokens.astype(x.dtype)], axis=0)
    _, makv, _, mckv, _, mik, _ = _proj_call(meta_blk, BLOCK, BLOCK, *proj_w)

    def padded_keys(real, meta):
        c = real.shape[-1]
        return jnp.concatenate([jnp.broadcast_to(meta[None], (b, BLOCK, c)), real.reshape(b, s, c),
                                jnp.zeros((b, TQ - BLOCK, c), real.dtype)], axis=1)

    o_a = _swa_call(sinks[0], aq.reshape(b, s, -1), akv.reshape(b, s, -1), makv, bias_a)
    o_b = _mla_call(iqs, iw.reshape(b, s, -1), qabs, padded_keys(ik, mik), padded_keys(ckv, mckv), bnear, wuvt,
                    topk)

    y = _out_call(x2, o_a.reshape(b * s, -1), o_b.reshape(b * s, -1), row2(attn_norm_g[0]), wg,
                  row2(b_gates[0]), w_branch_a[0].astype(BF16), w_branch_b[0].astype(BF16),
                  w_out[0].astype(BF16), row2(ffn_norm_g[0]), w_ffn_gate[0].astype(BF16),
                  w_ffn_up[0].astype(BF16), w_ffn_down[0].astype(BF16), row2(final_norm_g))
    return y.reshape(b, s, d)
```

```python
import functools
import math

import numpy as np
import jax
import jax.numpy as jnp
from jax import lax
from jax.experimental import pallas as pl
from jax.experimental.pallas import tpu as pltpu

N_META = 16
BLOCK = 128
PAD = BLOCK - N_META
WINDOW = 128
A_HEADS = 8
A_KV_HEADS = 2
A_HEAD_DIM = 64
B_HEADS = 8
B_HEAD_DIM = 64
B_Q_RANK = 256
B_KV_RANK = 128
IDX_HEADS = 4
IDX_DIM = 64
TOPK_MAX = 256
N_BUCKETS = 32
MAX_DISTANCE = 128
EPS = 1e-6
NEG = -1e30
MAX_FLOOR = -3.0e38
INT_MIN = -(2 ** 31)

LANES = 128
SUBLANES = 8
BF16_ROWS = 16
VMEM_LIMIT_BYTES = 56 * 1024 * 1024

TOK_TILE = 512
TQ = 256
FFN_CHUNKS = 2

C_AQ = 0
C_AKV = C_AQ + A_HEADS * A_HEAD_DIM
C_BKV = C_AKV + 2 * A_KV_HEADS * A_HEAD_DIM
C_IK = C_BKV + B_KV_RANK
C_IW = C_IK + LANES
C_END = C_IW + LANES

F32 = jnp.float32
BF16 = jnp.bfloat16
NT_DIMS = (((1,), (1,)), ((), ()))


def _t5_bucket_np(dist):
    dist = np.asarray(dist, np.int64)
    max_exact = N_BUCKETS // 2
    d = np.maximum(dist, 1).astype(np.float32)
    large = max_exact + (np.log(d / np.float32(max_exact)) / np.float32(math.log(MAX_DISTANCE / max_exact))
                         * np.float32(N_BUCKETS - max_exact)).astype(np.int32)
    large = np.minimum(large, N_BUCKETS - 1)
    return np.where(dist < max_exact, dist, large).astype(np.int32)


def _rms(x, g):
    return x * lax.rsqrt(jnp.mean(x * x, axis=-1, keepdims=True) + EPS) * g


def _tree_reduce(op, x):
    while x.shape[0] > 1:
        half = x.shape[0] // 2
        x = op(x[:half], x[half:])
    return x[0]


def _full_spec(shape):
    nd = len(shape)
    return pl.BlockSpec(shape, lambda *_: (0,) * nd)


def _bias_body(tab_ref, bkt_a_ref, bkt_near_ref, ba_ref, bnear_ref, *, far_bkt):
    def lookup(bkt, col):
        acc = jnp.zeros(bkt.shape, F32)
        for b in range(N_BUCKETS):
            acc = jnp.where(bkt == b, tab_ref[b, col], acc)
        return acc

    bkt_a = bkt_a_ref[...]
    bkt_near = bkt_near_ref[...]
    for h in range(A_HEADS):
        ba_ref[h] = lookup(bkt_a, h)
    for h in range(B_HEADS):
        col = A_HEADS + h
        bnear_ref[h] = lookup(bkt_near, col) - tab_ref[far_bkt, col]


def _bias_call(rel_bias, far_bkt):
    q = np.arange(BLOCK)[:, None]
    k = np.arange(2 * BLOCK)[None, :]
    bkt_a = _t5_bucket_np(np.maximum(q + BLOCK - k, 0))
    k = np.arange(2 * TQ)[:, None]
    q = np.arange(TQ)[None, :]
    bkt_near = _t5_bucket_np(np.maximum(q + BLOCK - k, 0))
    vmem = pl.BlockSpec(memory_space=pltpu.VMEM)
    return pl.pallas_call(
        functools.partial(_bias_body, far_bkt=far_bkt),
        out_shape=(jax.ShapeDtypeStruct((A_HEADS, BLOCK, 2 * BLOCK), F32),
                   jax.ShapeDtypeStruct((B_HEADS, 2 * TQ, TQ), F32)),
        in_specs=[pl.BlockSpec(memory_space=pltpu.SMEM), vmem, vmem],
        out_specs=(vmem, vmem),
        name="bias_tables",
    )(rel_bias, jnp.asarray(bkt_a), jnp.asarray(bkt_near))


def _proj_body(x_ref, g_ref, w1_ref, wqi_ref, qg_ref, wuq_ref, wuk_ref, kvg_ref, ikg_ref, ikb_ref,
               aq_ref, akv_ref, qabs_ref, ckv_ref, iqs_ref, ik_ref, iw_ref, *, tq):
    tm = x_ref.shape[0]
    nblk = tm // tq
    hn = _rms(x_ref[...], g_ref[...]).astype(BF16)

    def proj(lo, hi):
        return jnp.dot(hn, w1_ref[:, lo:hi], preferred_element_type=F32)

    aq_ref[...] = (proj(C_AQ, C_AKV) * (A_HEAD_DIM ** -0.5)).astype(BF16)
    akv_ref[...] = proj(C_AKV, C_BKV).astype(BF16)

    qi_t = lax.dot_general(wqi_ref[...], hn, NT_DIMS, preferred_element_type=F32)
    bq_t = qi_t[0:B_Q_RANK]
    qn_t = bq_t * lax.rsqrt(jnp.mean(bq_t * bq_t, axis=0, keepdims=True) + EPS) * qg_ref[...]
    q_t = jnp.dot(wuq_ref[...], qn_t.astype(BF16), preferred_element_type=F32).astype(BF16)
    for h in range(B_HEADS):
        qa_t = jnp.dot(wuk_ref[h], q_t[h * B_HEAD_DIM:(h + 1) * B_HEAD_DIM], preferred_element_type=F32)
        qa_t = (qa_t * (B_HEAD_DIM ** -0.5)).astype(BF16)
        for j in range(nblk):
            qabs_ref[j, h] = qa_t[:, j * tq:(j + 1) * tq]
    iq_t = qi_t[B_Q_RANK:].astype(BF16)
    for h in range(IDX_HEADS):
        for j in range(nblk):
            iqs_ref[j, h] = iq_t[h * IDX_DIM:(h + 1) * IDX_DIM, j * tq:(j + 1) * tq]

    ckv_ref[...] = _rms(proj(C_BKV, C_IK), kvg_ref[...]).astype(BF16)

    ikw = proj(C_IK, C_END)
    ik = ikw[:, :IDX_DIM]
    mu = jnp.mean(ik, axis=-1, keepdims=True)
    xc = ik - mu
    var = jnp.mean(xc * xc, axis=-1, keepdims=True)
    ik_ref[...] = (xc * lax.rsqrt(var + EPS) * ikg_ref[...] + ikb_ref[...]).astype(BF16)
    iw_ref[...] = ikw[:, LANES:] * ((IDX_HEADS * IDX_DIM) ** -0.5)


def _proj_call(x2, tm, tq, *weights):
    n, d = x2.shape
    grid = (n // tm,)
    row = lambda i: (i, 0)
    blk4 = lambda i: (i, 0, 0, 0)
    out_shape = (
        jax.ShapeDtypeStruct((n, A_HEADS * A_HEAD_DIM), BF16),
        jax.ShapeDtypeStruct((n, 2 * A_KV_HEADS * A_HEAD_DIM), BF16),
        jax.ShapeDtypeStruct((n // tq, B_HEADS, B_KV_RANK, tq), BF16),
        jax.ShapeDtypeStruct((n, B_KV_RANK), BF16),
        jax.ShapeDtypeStruct((n // tq, IDX_HEADS, IDX_DIM, tq), BF16),
        jax.ShapeDtypeStruct((n, IDX_DIM), BF16),
        jax.ShapeDtypeStruct((n, LANES), F32),
    )
    out_specs = (
        pl.BlockSpec((tm, A_HEADS * A_HEAD_DIM), row),
        pl.BlockSpec((tm, 2 * A_KV_HEADS * A_HEAD_DIM), row),
        pl.BlockSpec((tm // tq, B_HEADS, B_KV_RANK, tq), blk4),
        pl.BlockSpec((tm, B_KV_RANK), row),
        pl.BlockSpec((tm // tq, IDX_HEADS, IDX_DIM, tq), blk4),
        pl.BlockSpec((tm, IDX_DIM), row),
        pl.BlockSpec((tm, LANES), row),
    )
    in_specs = [pl.BlockSpec((tm, d), row)] + [_full_spec(w.shape) for w in weights]
    return pl.pallas_call(
        functools.partial(_proj_body, tq=tq),
        grid=grid, in_specs=in_specs, out_specs=out_specs, out_shape=out_shape,
        compiler_params=pltpu.CompilerParams(dimension_semantics=("arbitrary",),
                                             vmem_limit_bytes=VMEM_LIMIT_BYTES),
        name="in_proj",
    )(x2, *weights)


def _swa_body(sinks_ref, aq_ref, kcur_ref, kprev_ref, kmeta_ref, bias_ref, o_ref):
    n = pl.program_id(1)
    kprev = jnp.where(n == 0, kmeta_ref[...], kprev_ref[0])
    kwin = jnp.concatenate([kprev, kcur_ref[0]], axis=0)
    aq = aq_ref[0]
    qi = lax.broadcasted_iota(jnp.int32, (BLOCK, 2 * BLOCK), 0)
    ki = lax.broadcasted_iota(jnp.int32, (BLOCK, 2 * BLOCK), 1)
    dist = qi + BLOCK - ki
    allowed = (dist >= 0) & (dist < WINDOW) & ((n > 0) | (ki >= PAD))
    grp = A_HEADS // A_KV_HEADS
    outs = []
    for h in range(A_HEADS):
        kvh = h // grp
        k = kwin[:, kvh * A_HEAD_DIM:(kvh + 1) * A_HEAD_DIM]
        v = kwin[:, (A_KV_HEADS + kvh) * A_HEAD_DIM:(A_KV_HEADS + kvh + 1) * A_HEAD_DIM]
        q = aq[:, h * A_HEAD_DIM:(h + 1) * A_HEAD_DIM]
        s = lax.dot_general(q, k, NT_DIMS, preferred_element_type=F32)
        s = jnp.where(allowed, s + bias_ref[h], NEG)
        sink = sinks_ref[h]
        m = jnp.maximum(jnp.max(s, axis=-1, keepdims=True), sink)
        e = jnp.exp(s - m)
        den = jnp.sum(e, axis=-1, keepdims=True) + jnp.exp(sink - m)
        p = (e / den).astype(BF16)
        outs.append(jnp.dot(p, v, preferred_element_type=F32))
    o_ref[0] = jnp.concatenate(outs, axis=1).astype(BF16)


def _swa_call(sinks, aq, akv, makv, bias_a):
    b, s, _ = aq.shape
    nb = s // BLOCK
    return pl.pallas_call(
        _swa_body,
        grid=(b, nb),
        in_specs=[pl.BlockSpec(memory_space=pltpu.SMEM),
                  pl.BlockSpec((1, BLOCK, aq.shape[2]), lambda i, n: (i, n, 0)),
                  pl.BlockSpec((1, BLOCK, akv.shape[2]), lambda i, n: (i, n, 0)),
                  pl.BlockSpec((1, BLOCK, akv.shape[2]), lambda i, n: (i, jnp.maximum(n - 1, 0), 0)),
                  _full_spec(makv.shape), _full_spec(bias_a.shape)],
        out_specs=pl.BlockSpec((1, BLOCK, aq.shape[2]), lambda i, n: (i, n, 0)),
        out_shape=jax.ShapeDtypeStruct(aq.shape, BF16),
        compiler_params=pltpu.CompilerParams(dimension_semantics=("arbitrary", "arbitrary"),
                                             vmem_limit_bytes=VMEM_LIMIT_BYTES),
        name="swa_sink_attention",
    )(sinks, aq, akv, akv, makv, bias_a)


def _mla_body(iqs_ref, iw_ref, qabs_ref, ikp_ref, ckvp_ref, bnear_ref, wuvt_ref, o_ref,
              key_sc, s_sc, m_sc, acc_sc, j_sc, *, topk):
    m_blk = pl.program_id(1)
    nchunks = m_blk + 2
    groups = TQ // SUBLANES
    acc_rows = B_KV_RANK + BF16_ROWS
    iw_t = iw_ref[0].T
    wrow = [iw_t[h:h + 1, :] for h in range(IDX_HEADS)]

    sub_pos = (lax.broadcasted_iota(jnp.int32, (groups, SUBLANES, TQ), 0) * SUBLANES
               + lax.broadcasted_iota(jnp.int32, (groups, SUBLANES, TQ), 1))
    qpos = BLOCK + m_blk * TQ + lax.broadcasted_iota(jnp.int32, (groups, SUBLANES, TQ), 2)

    def rows_of(c):
        return pl.ds(pl.multiple_of(c * TQ, TQ), TQ)

    def tiles(x):
        return x.reshape(groups, SUBLANES, x.shape[-1])

    def sort_key(sc):
        bits = lax.bitcast_convert_type(sc, jnp.int32)
        bits = jnp.where(sc == 0.0, 0, bits)
        return jnp.where(bits < 0, bits ^ 0x7FFFFFFF, bits)

    def score_chunk(c, carry):
        ik_c = ikp_ref[0, rows_of(c), :]
        sc = None
        for h in range(IDX_HEADS):
            logits = jnp.dot(ik_c, iqs_ref[0, h], preferred_element_type=F32)
            term = jnp.maximum(logits, 0.0) * wrow[h]
            sc = term if sc is None else sc + term
        pos = c * TQ + sub_pos
        adm = (pos >= PAD) & (pos <= qpos)
        key_sc[rows_of(c), :] = jnp.where(adm, tiles(sort_key(sc)), INT_MIN).reshape(TQ, TQ)
        return carry

    lax.fori_loop(0, nchunks, score_chunk, 0)

    def count(pred):
        def body(c, part):
            hit = pred(tiles(key_sc[rows_of(c), :]), c * TQ + sub_pos)
            return part + _tree_reduce(jnp.add, jnp.where(hit, 1.0, 0.0))
        part = lax.fori_loop(0, nchunks, body, jnp.zeros((SUBLANES, TQ), F32))
        return jnp.broadcast_to(jnp.sum(part, axis=0, keepdims=True), (SUBLANES, TQ))

    kf = float(topk)

    def bisect(i, t):
        cand = t + jnp.left_shift(jnp.int32(1), 31 - i)
        cnt = count(lambda kp, pos: kp >= cand[None])
        return jnp.where(cnt >= kf, cand, t)

    thr = lax.fori_loop(0, 32, bisect, jnp.full((SUBLANES, TQ), INT_MIN, jnp.int32))
    cnt_gt = count(lambda kp, pos: kp > thr[None])
    cnt_ge = count(lambda kp, pos: kp >= thr[None])
    need = kf - cnt_gt
    ambiguous = (cnt_ge - cnt_gt != need) & (thr != INT_MIN)

    j_sc[...] = jnp.full((SUBLANES, TQ), 2 ** 30, jnp.int32)

    @pl.when(jnp.max(ambiguous.astype(F32)) > 0.0)
    def _():
        def tie(i, jt):
            cand = jt + jnp.left_shift(jnp.int32(1), 11 - i)
            cnt = count(lambda kp, pos: (kp == thr[None]) & (pos < cand[None]))
            return jnp.where(cnt < need, cand, jt)
        j_sc[...] = lax.fori_loop(0, 12, tie, jnp.zeros((SUBLANES, TQ), jnp.int32))

    jt = j_sc[...]

    m_sc[...] = jnp.full(m_sc.shape, MAX_FLOOR, F32)
    acc_sc[...] = jnp.zeros(acc_sc.shape, F32)

    def mla_chunk(c, bias_row0):
        ckv_c = ckvp_ref[0, rows_of(c), :]
        kp = tiles(key_sc[rows_of(c), :])
        pos = c * TQ + sub_pos
        sel = ((kp > thr[None]) | ((kp == thr[None]) & (pos <= jt[None]))) & (kp != INT_MIN)
        mask = jnp.where(sel, 0.0, -jnp.inf)
        vt = ckv_c.astype(F32).T.astype(BF16)
        vext = jnp.concatenate([vt, jnp.ones((BF16_ROWS, TQ), BF16)], axis=0)
        alphas = []
        for h in range(B_HEADS):
            sh = jnp.dot(ckv_c, qabs_ref[0, h], preferred_element_type=F32)
            if bias_row0 is not None:
                sh = sh + bnear_ref[h, pl.ds(bias_row0, TQ), :]
            sh = tiles(sh) + mask
            s_sc[h] = sh.reshape(TQ, TQ)
            m_prev = m_sc[h]
            mx = jnp.max(_tree_reduce(jnp.maximum, sh), axis=0, keepdims=True)
            m_new = jnp.maximum(m_prev, mx)
            alphas.append(jnp.exp(m_prev - m_new))
            m_sc[h] = m_new
        for h in range(B_HEADS):
            e = jnp.exp(tiles(s_sc[h]) - m_sc[h][None]).reshape(TQ, TQ).astype(BF16)
            pv = jnp.dot(vext, e, preferred_element_type=F32)
            acc = acc_sc[h].reshape(acc_rows // SUBLANES, SUBLANES, TQ)
            acc_sc[h] = (acc * alphas[h][None] + pv.reshape(acc.shape)).reshape(acc_rows, TQ)

    def far_loop(c, carry):
        mla_chunk(c, None)
        return carry

    def near_loop(i, carry):
        mla_chunk(m_blk + i, pl.multiple_of(i * TQ, TQ))
        return carry

    lax.fori_loop(0, m_blk, far_loop, 0)
    lax.fori_loop(0, 2, near_loop, 0)

    outs = []
    for h in range(B_HEADS):
        acc = acc_sc[h]
        lat = (acc[0:B_KV_RANK] / acc[B_KV_RANK:B_KV_RANK + 1]).astype(BF16)
        outs.append(jnp.dot(wuvt_ref[h], lat, preferred_element_type=F32))
    o_ref[0] = jnp.concatenate(outs, axis=0).T.astype(BF16)


def _mla_call(iqs, iw, qabs, ikp, ckvp, bnear, wuvt, topk):
    b, sp, _ = ikp.shape
    s = iw.shape[1]
    nq = s // TQ
    acc_rows = B_KV_RANK + BF16_ROWS
    return pl.pallas_call(
        functools.partial(_mla_body, topk=topk),
        grid=(b, nq),
        in_specs=[pl.BlockSpec((1, IDX_HEADS, IDX_DIM, TQ), lambda i, m: (i * nq + m, 0, 0, 0)),
                  pl.BlockSpec((1, TQ, LANES), lambda i, m: (i, m, 0)),
                  pl.BlockSpec((1, B_HEADS, B_KV_RANK, TQ), lambda i, m: (i * nq + m, 0, 0, 0)),
                  pl.BlockSpec((1, sp, IDX_DIM), lambda i, m: (i, 0, 0)),
                  pl.BlockSpec((1, sp, B_KV_RANK), lambda i, m: (i, 0, 0)),
                  _full_spec(bnear.shape), _full_spec(wuvt.shape)],
        out_specs=pl.BlockSpec((1, TQ, B_HEADS * B_HEAD_DIM), lambda i, m: (i, m, 0)),
        out_shape=jax.ShapeDtypeStruct((b, s, B_HEADS * B_HEAD_DIM), BF16),
        scratch_shapes=[pltpu.VMEM((sp, TQ), jnp.int32),
                        pltpu.VMEM((B_HEADS, TQ, TQ), F32),
                        pltpu.VMEM((B_HEADS, SUBLANES, TQ), F32),
                        pltpu.VMEM((B_HEADS, acc_rows, TQ), F32),
                        pltpu.VMEM((SUBLANES, TQ), jnp.int32)],
        compiler_params=pltpu.CompilerParams(dimension_semantics=("arbitrary", "arbitrary"),
                                             vmem_limit_bytes=VMEM_LIMIT_BYTES),
        name="indexer_topk_mla",
    )(iqs, iw, qabs, ikp, ckvp, bnear, wuvt)


def _out_body(x_ref, oa_ref, ob_ref, ag_ref, wg_ref, bg_ref, wa_ref, wb_ref, wo_ref, fg_ref,
              wfg_ref, wfu_ref, wfd_ref, ng_ref, y_ref):
    d = x_ref.shape[1]
    x = x_ref[...]
    hn = _rms(x, ag_ref[...]).astype(BF16)

    def gated(o_ref, w_ref, lo):
        gate = jax.nn.sigmoid(jnp.dot(hn, wg_ref[:, lo:lo + d], preferred_element_type=F32) + bg_ref[:, lo:lo + d])
        return gate * jnp.dot(o_ref[...], w_ref[...], preferred_element_type=F32)

    mixed = (gated(oa_ref, wa_ref, 0) + gated(ob_ref, wb_ref, d)).astype(BF16)
    h = x + jnp.dot(mixed, wo_ref[...], preferred_element_type=F32)
    hn2 = _rms(h, fg_ref[...]).astype(BF16)
    dff = wfg_ref.shape[1]
    step = dff // FFN_CHUNKS
    for c in range(0, dff, step):
        g = jnp.dot(hn2, wfg_ref[:, c:c + step], preferred_element_type=F32)
        u = jnp.dot(hn2, wfu_ref[:, c:c + step], preferred_element_type=F32)
        act = (g * jax.nn.sigmoid(g) * u).astype(BF16)
        h = h + jnp.dot(act, wfd_ref[c:c + step, :], preferred_element_type=F32)
    y_ref[...] = _rms(h, ng_ref[...])


def _out_call(x2, oa, ob, ag, wg, bg, wa, wb, wo, fg, wfg, wfu, wfd, ng):
    n, d = x2.shape
    tm = TOK_TILE
    row = lambda i: (i, 0)

    def const_spec(a):
        return pl.BlockSpec(a.shape, lambda i: (0,) * a.ndim, pipeline_mode=pl.Buffered(1))

    consts = [ag, wg, bg, wa, wb, wo, fg, wfg, wfu, wfd, ng]
    return pl.pallas_call(
        _out_body,
        grid=(n // tm,),
        in_specs=[pl.BlockSpec((tm, d), row), pl.BlockSpec((tm, oa.shape[1]), row),
                  pl.BlockSpec((tm, ob.shape[1]), row)] + [const_spec(a) for a in consts],
        out_specs=pl.BlockSpec((tm, d), row),
        out_shape=jax.ShapeDtypeStruct((n, d), F32),
        compiler_params=pltpu.CompilerParams(dimension_semantics=("arbitrary",),
                                             vmem_limit_bytes=VMEM_LIMIT_BYTES),
        name="merge_ffn_norm",
    )(x2, oa, ob, *consts)


def kernel(x, meta_tokens, attn_norm_g, w_in, b_gates, q_norm_g, kv_norm_g, w_uq, w_uk, w_uv, idx_k_ln_g,
           idx_k_ln_b, sinks, rel_bias, w_branch_a, w_branch_b, w_out, ffn_norm_g, w_ffn_gate, w_ffn_up,
           w_ffn_down, final_norm_g):
    b, s, d = x.shape
    assert attn_norm_g.shape[0] == 1, "single-layer block"
    assert s % TQ == 0 and (b * s) % TOK_TILE == 0 and TOK_TILE % TQ == 0
    assert w_ffn_gate.shape[2] % (FFN_CHUNKS * LANES) == 0
    topk = min(TOPK_MAX, s // 4)
    far_bkts = np.unique(_t5_bucket_np(np.arange(BLOCK + 1, s + BLOCK + 1)))
    assert far_bkts.size == 1
    far_bkt = int(far_bkts[0])

    wi = w_in[0]
    widths = (A_HEADS * A_HEAD_DIM, 2 * A_KV_HEADS * A_HEAD_DIM, B_Q_RANK, B_KV_RANK, IDX_HEADS * IDX_DIM,
              IDX_DIM, IDX_HEADS, 2 * d)
    starts = np.concatenate([[0], np.cumsum(widths)])
    w_aq, w_akv, w_bq, w_bkv, w_iq, w_ik, w_iw, w_gates = (wi[:, int(a):int(b_)]
                                                            for a, b_ in zip(starts[:-1], starts[1:]))
    zpad = lambda k: jnp.zeros((d, k), wi.dtype)
    w1 = jnp.concatenate([w_aq, w_akv, w_bkv, w_ik, zpad(LANES - IDX_DIM), w_iw, zpad(LANES - IDX_HEADS)],
                         axis=1).astype(BF16)
    wqi = jnp.concatenate([w_bq, w_iq], axis=1).T.astype(BF16)
    wg = w_gates.astype(BF16)
    row2 = lambda v: v.reshape(1, -1).astype(F32)
    col2 = lambda v: v.reshape(-1, 1).astype(F32)
    wuk = jnp.transpose(w_uk[0], (1, 0, 2)).astype(BF16)
    wuvt = jnp.transpose(w_uv[0], (1, 2, 0)).astype(BF16)
    proj_w = (row2(attn_norm_g[0]), w1, wqi, col2(q_norm_g[0]), w_uq[0].T.astype(BF16), wuk,
              row2(kv_norm_g[0]), row2(idx_k_ln_g[0]), row2(idx_k_ln_b[0]))

    bias_a, bnear = _bias_call(rel_bias, far_bkt)

    x2 = x.reshape(b * s, d)
    aq, akv, qabs, ckv, iqs, ik, iw = _proj_call(x2, TOK_TILE, TQ, *proj_w)
    meta_blk = jnp.concatenate([jnp.zeros((PAD, d), x.dtype), meta_tokens.astype(x.dtype)], axis=0)
    _, makv, _, mckv, _, mik, _ = _proj_call(meta_blk, BLOCK, BLOCK, *proj_w)

    def padded_keys(real, meta):
        c = real.shape[-1]
        return jnp.concatenate([jnp.broadcast_to(meta[None], (b, BLOCK, c)), real.reshape(b, s, c),
                                jnp.zeros((b, TQ - BLOCK, c), real.dtype)], axis=1)

    o_a = _swa_call(sinks[0], aq.reshape(b, s, -1), akv.reshape(b, s, -1), makv, bias_a)
    o_b = _mla_call(iqs, iw.reshape(b, s, -1), qabs, padded_keys(ik, mik), padded_keys(ckv, mckv), bnear, wuvt,
                    topk)

    y = _out_call(x2, o_a.reshape(b * s, -1), o_b.reshape(b * s, -1), row2(attn_norm_g[0]), wg,
                  row2(b_gates[0]), w_branch_a[0].astype(BF16), w_branch_b[0].astype(BF16),
                  w_out[0].astype(BF16), row2(ffn_norm_g[0]), w_ffn_gate[0].astype(BF16),
                  w_ffn_up[0].astype(BF16), w_ffn_down[0].astype(BF16), row2(final_norm_g))
    return y.reshape(b, s, d)
```

```python
import functools
import math

import numpy as np
import jax
import jax.numpy as jnp
from jax import lax
from jax.experimental import pallas as pl
from jax.experimental.pallas import tpu as pltpu

N_META = 16
BLOCK = 128
PAD = BLOCK - N_META
WINDOW = 128
A_HEADS = 8
A_KV_HEADS = 2
A_HEAD_DIM = 64
B_HEADS = 8
B_HEAD_DIM = 64
B_Q_RANK = 256
B_KV_RANK = 128
IDX_HEADS = 4
IDX_DIM = 64
TOPK_MAX = 256
N_BUCKETS = 32
MAX_DISTANCE = 128
EPS = 1e-6
NEG = -1e30
MAX_FLOOR = -3.0e38
INT_MIN = -(2 ** 31)
I16_MIN = -(2 ** 15)
I16_MAX = 2 ** 15 - 1

LANES = 128
SUBLANES = 8
BF16_ROWS = 16
VMEM_LIMIT_BYTES = 56 * 1024 * 1024

TOK_TILE = 512
TQ = 256
FFN_CHUNKS = 2

C_AQ = 0
C_AKV = C_AQ + A_HEADS * A_HEAD_DIM
C_BKV = C_AKV + 2 * A_KV_HEADS * A_HEAD_DIM
C_IK = C_BKV + B_KV_RANK
C_IW = C_IK + LANES
C_END = C_IW + LANES

F32 = jnp.float32
BF16 = jnp.bfloat16
NT_DIMS = (((1,), (1,)), ((), ()))


def _t5_bucket_np(dist):
    dist = np.asarray(dist, np.int64)
    max_exact = N_BUCKETS // 2
    d = np.maximum(dist, 1).astype(np.float32)
    large = max_exact + (np.log(d / np.float32(max_exact)) / np.float32(math.log(MAX_DISTANCE / max_exact))
                         * np.float32(N_BUCKETS - max_exact)).astype(np.int32)
    large = np.minimum(large, N_BUCKETS - 1)
    return np.where(dist < max_exact, dist, large).astype(np.int32)


def _rms(x, g):
    return x * lax.rsqrt(jnp.mean(x * x, axis=-1, keepdims=True) + EPS) * g


def _tree_reduce(op, x):
    while x.shape[0] > 1:
        half = x.shape[0] // 2
        x = op(x[:half], x[half:])
    return x[0]


def _full_spec(shape):
    nd = len(shape)
    return pl.BlockSpec(shape, lambda *_: (0,) * nd)


def _bias_body(tab_ref, bkt_a_ref, bkt_near_ref, ba_ref, bnear_ref, *, far_bkt):
    def lookup(bkt, col):
        acc = jnp.zeros(bkt.shape, F32)
        for b in range(N_BUCKETS):
            acc = jnp.where(bkt == b, tab_ref[b, col], acc)
        return acc

    bkt_a = bkt_a_ref[...]
    bkt_near = bkt_near_ref[...]
    for h in range(A_HEADS):
        ba_ref[h] = lookup(bkt_a, h)
    for h in range(B_HEADS):
        col = A_HEADS + h
        bnear_ref[h] = lookup(bkt_near, col) - tab_ref[far_bkt, col]


def _bias_call(rel_bias, far_bkt):
    q = np.arange(BLOCK)[:, None]
    k = np.arange(2 * BLOCK)[None, :]
    bkt_a = _t5_bucket_np(np.maximum(q + BLOCK - k, 0))
    k = np.arange(2 * TQ)[:, None]
    q = np.arange(TQ)[None, :]
    bkt_near = _t5_bucket_np(np.maximum(q + BLOCK - k, 0))
    vmem = pl.BlockSpec(memory_space=pltpu.VMEM)
    return pl.pallas_call(
        functools.partial(_bias_body, far_bkt=far_bkt),
        out_shape=(jax.ShapeDtypeStruct((A_HEADS, BLOCK, 2 * BLOCK), F32),
                   jax.ShapeDtypeStruct((B_HEADS, 2 * TQ, TQ), F32)),
        in_specs=[pl.BlockSpec(memory_space=pltpu.SMEM), vmem, vmem],
        out_specs=(vmem, vmem),
        name="bias_tables",
    )(rel_bias, jnp.asarray(bkt_a), jnp.asarray(bkt_near))


def _proj_body(x_ref, g_ref, w1_ref, wqi_ref, qg_ref, wuq_ref, wuk_ref, kvg_ref, ikg_ref, ikb_ref,
               aq_ref, akv_ref, qabs_ref, ckv_ref, iqs_ref, ik_ref, iw_ref, *, tq):
    tm = x_ref.shape[0]
    nblk = tm // tq
    hn = _rms(x_ref[...], g_ref[...]).astype(BF16)

    def proj(lo, hi):
        return jnp.dot(hn, w1_ref[:, lo:hi], preferred_element_type=F32)

    aq_ref[...] = (proj(C_AQ, C_AKV) * (A_HEAD_DIM ** -0.5)).astype(BF16)
    akv_ref[...] = proj(C_AKV, C_BKV).astype(BF16)

    qi_t = lax.dot_general(wqi_ref[...], hn, NT_DIMS, preferred_element_type=F32)
    bq_t = qi_t[0:B_Q_RANK]
    qn_t = bq_t * lax.rsqrt(jnp.mean(bq_t * bq_t, axis=0, keepdims=True) + EPS) * qg_ref[...]
    q_t = jnp.dot(wuq_ref[...], qn_t.astype(BF16), preferred_element_type=F32).astype(BF16)
    for h in range(B_HEADS):
        qa_t = jnp.dot(wuk_ref[h], q_t[h * B_HEAD_DIM:(h + 1) * B_HEAD_DIM], preferred_element_type=F32)
        qa_t = (qa_t * (B_HEAD_DIM ** -0.5)).astype(BF16)
        for j in range(nblk):
            qabs_ref[j, h] = qa_t[:, j * tq:(j + 1) * tq]
    iq_t = qi_t[B_Q_RANK:].astype(BF16)
    for h in range(IDX_HEADS):
        for j in range(nblk):
            iqs_ref[j, h] = iq_t[h * IDX_DIM:(h + 1) * IDX_DIM, j * tq:(j + 1) * tq]

    ckv_ref[...] = _rms(proj(C_BKV, C_IK), kvg_ref[...]).astype(BF16)

    ikw = proj(C_IK, C_END)
    ik = ikw[:, :IDX_DIM]
    mu = jnp.mean(ik, axis=-1, keepdims=True)
    xc = ik - mu
    var = jnp.mean(xc * xc, axis=-1, keepdims=True)
    ik_ref[...] = (xc * lax.rsqrt(var + EPS) * ikg_ref[...] + ikb_ref[...]).astype(BF16)
    iw_ref[...] = ikw[:, LANES:] * ((IDX_HEADS * IDX_DIM) ** -0.5)


def _proj_call(x2, tm, tq, *weights):
    n, d = x2.shape
    grid = (n // tm,)
    row = lambda i: (i, 0)
    blk4 = lambda i: (i, 0, 0, 0)
    out_shape = (
        jax.ShapeDtypeStruct((n, A_HEADS * A_HEAD_DIM), BF16),
        jax.ShapeDtypeStruct((n, 2 * A_KV_HEADS * A_HEAD_DIM), BF16),
        jax.ShapeDtypeStruct((n // tq, B_HEADS, B_KV_RANK, tq), BF16),
        jax.ShapeDtypeStruct((n, B_KV_RANK), BF16),
        jax.ShapeDtypeStruct((n // tq, IDX_HEADS, IDX_DIM, tq), BF16),
        jax.ShapeDtypeStruct((n, IDX_DIM), BF16),
        jax.ShapeDtypeStruct((n, LANES), F32),
    )
    out_specs = (
        pl.BlockSpec((tm, A_HEADS * A_HEAD_DIM), row),
        pl.BlockSpec((tm, 2 * A_KV_HEADS * A_HEAD_DIM), row),
        pl.BlockSpec((tm // tq, B_HEADS, B_KV_RANK, tq), blk4),
        pl.BlockSpec((tm, B_KV_RANK), row),
        pl.BlockSpec((tm // tq, IDX_HEADS, IDX_DIM, tq), blk4),
        pl.BlockSpec((tm, IDX_DIM), row),
        pl.BlockSpec((tm, LANES), row),
    )
    in_specs = [pl.BlockSpec((tm, d), row)] + [_full_spec(w.shape) for w in weights]
    return pl.pallas_call(
        functools.partial(_proj_body, tq=tq),
        grid=grid, in_specs=in_specs, out_specs=out_specs, out_shape=out_shape,
        compiler_params=pltpu.CompilerParams(dimension_semantics=("arbitrary",),
                                             vmem_limit_bytes=VMEM_LIMIT_BYTES),
        name="in_proj",
    )(x2, *weights)


def _swa_body(sinks_ref, aq_ref, kcur_ref, kprev_ref, kmeta_ref, bias_ref, o_ref):
    n = pl.program_id(1)
    kprev = jnp.where(n == 0, kmeta_ref[...], kprev_ref[0])
    kwin = jnp.concatenate([kprev, kcur_ref[0]], axis=0)
    aq = aq_ref[0]
    qi = lax.broadcasted_iota(jnp.int32, (BLOCK, 2 * BLOCK), 0)
    ki = lax.broadcasted_iota(jnp.int32, (BLOCK, 2 * BLOCK), 1)
    dist = qi + BLOCK - ki
    allowed = (dist >= 0) & (dist < WINDOW) & ((n > 0) | (ki >= PAD))
    grp = A_HEADS // A_KV_HEADS
    outs = []
    for h in range(A_HEADS):
        kvh = h // grp
        k = kwin[:, kvh * A_HEAD_DIM:(kvh + 1) * A_HEAD_DIM]
        v = kwin[:, (A_KV_HEADS + kvh) * A_HEAD_DIM:(A_KV_HEADS + kvh + 1) * A_HEAD_DIM]
        q = aq[:, h * A_HEAD_DIM:(h + 1) * A_HEAD_DIM]
        s = lax.dot_general(q, k, NT_DIMS, preferred_element_type=F32)
        s = jnp.where(allowed, s + bias_ref[h], NEG)
        sink = sinks_ref[h]
        m = jnp.maximum(jnp.max(s, axis=-1, keepdims=True), sink)
        e = jnp.exp(s - m)
        den = jnp.sum(e, axis=-1, keepdims=True) + jnp.exp(sink - m)
        p = (e / den).astype(BF16)
        outs.append(jnp.dot(p, v, preferred_element_type=F32))
    o_ref[0] = jnp.concatenate(outs, axis=1).astype(BF16)


def _swa_call(sinks, aq, akv, makv, bias_a):
    b, s, _ = aq.shape
    nb = s // BLOCK
    return pl.pallas_call(
        _swa_body,
        grid=(b, nb),
        in_specs=[pl.BlockSpec(memory_space=pltpu.SMEM),
                  pl.BlockSpec((1, BLOCK, aq.shape[2]), lambda i, n: (i, n, 0)),
                  pl.BlockSpec((1, BLOCK, akv.shape[2]), lambda i, n: (i, n, 0)),
                  pl.BlockSpec((1, BLOCK, akv.shape[2]), lambda i, n: (i, jnp.maximum(n - 1, 0), 0)),
                  _full_spec(makv.shape), _full_spec(bias_a.shape)],
        out_specs=pl.BlockSpec((1, BLOCK, aq.shape[2]), lambda i, n: (i, n, 0)),
        out_shape=jax.ShapeDtypeStruct(aq.shape, BF16),
        compiler_params=pltpu.CompilerParams(dimension_semantics=("arbitrary", "arbitrary"),
                                             vmem_limit_bytes=VMEM_LIMIT_BYTES),
        name="swa_sink_attention",
    )(sinks, aq, akv, akv, makv, bias_a)


def _mla_body(iqs_ref, iw_ref, qabs_ref, ikp_ref, ckvp_ref, bnear_ref, wuvt_ref, o_ref,
              key_sc, khi_sc, klo_sc, s_sc, m_sc, acc_sc, j_sc, *, topk):
    m_blk = pl.program_id(1)
    nchunks = m_blk + 2
    groups = TQ // SUBLANES
    acc_rows = B_KV_RANK + BF16_ROWS
    iw_t = iw_ref[0].T
    wrow = [iw_t[h:h + 1, :] for h in range(IDX_HEADS)]

    sub_pos = (lax.broadcasted_iota(jnp.int32, (groups, SUBLANES, TQ), 0) * SUBLANES
               + lax.broadcasted_iota(jnp.int32, (groups, SUBLANES, TQ), 1))
    qpos = BLOCK + m_blk * TQ + lax.broadcasted_iota(jnp.int32, (groups, SUBLANES, TQ), 2)

    def rows_of(c):
        return pl.ds(pl.multiple_of(c * TQ, TQ), TQ)

    def tiles(x):
        return x.reshape(groups, SUBLANES, x.shape[-1])

    def sort_key(sc):
        bits = lax.bitcast_convert_type(sc, jnp.int32)
        bits = jnp.where(sc == 0.0, 0, bits)
        return jnp.where(bits < 0, bits ^ 0x7FFFFFFF, bits)

    def score_chunk(c, carry):
        ik_c = ikp_ref[0, rows_of(c), :]
        sc = None
        for h in range(IDX_HEADS):
            logits = jnp.dot(ik_c, iqs_ref[0, h], preferred_element_type=F32)
            term = jnp.maximum(logits, 0.0) * wrow[h]
            sc = term if sc is None else sc + term
        pos = c * TQ + sub_pos
        adm = (pos >= PAD) & (pos <= qpos)
        key = jnp.where(adm, tiles(sort_key(sc)), INT_MIN).reshape(TQ, TQ)
        key_sc[rows_of(c), :] = key
        khi_sc[rows_of(c), :] = jnp.right_shift(key, 16).astype(jnp.int16)
        klo_sc[rows_of(c), :] = (key ^ 0x8000).astype(jnp.int16)
        return carry

    lax.fori_loop(0, nchunks, score_chunk, 0)

    def count(pred):
        def body(c, part):
            hit = pred(tiles(key_sc[rows_of(c), :]), c * TQ + sub_pos)
            return part + _tree_reduce(jnp.add, jnp.where(hit, 1.0, 0.0))
        part = lax.fori_loop(0, nchunks, body, jnp.zeros((SUBLANES, TQ), F32))
        return jnp.broadcast_to(jnp.sum(part, axis=0, keepdims=True), (SUBLANES, TQ))

    kf = float(topk)

    def tiles16(x):
        return x.reshape(TQ // BF16_ROWS, BF16_ROWS, TQ)

    def count16_ge(ref, cand):
        cand16 = jnp.broadcast_to(cand[0:1], (BF16_ROWS, TQ)).astype(jnp.int16)
        one, zero = jnp.int16(1), jnp.int16(0)

        def body(c, part):
            hit = tiles16(ref[rows_of(c), :]) >= cand16[None]
            return part + _tree_reduce(jnp.add, jnp.where(hit, one, zero))
        part = lax.fori_loop(0, nchunks, body, jnp.zeros((BF16_ROWS, TQ), jnp.int16))
        total = jnp.sum(part.astype(jnp.int32), axis=0, keepdims=True)
        return jnp.broadcast_to(total, (SUBLANES, TQ))

    def digit_search(ref):
        def step(i, t):
            cand = t + jnp.left_shift(jnp.int32(1), 15 - i)
            return jnp.where(count16_ge(ref, cand) >= topk, cand, t)
        return lax.fori_loop(0, 16, step, jnp.full((SUBLANES, TQ), I16_MIN, jnp.int32))

    t_hi = digit_search(khi_sc)
    t_hi16 = jnp.broadcast_to(t_hi[0:1], (BF16_ROWS, TQ)).astype(jnp.int16)

    def pin_low(c, carry):
        hi = tiles16(khi_sc[rows_of(c), :])
        lo = tiles16(klo_sc[rows_of(c), :])
        lo = jnp.where(hi > t_hi16[None], jnp.int16(I16_MAX), jnp.where(hi < t_hi16[None], jnp.int16(I16_MIN), lo))
        klo_sc[rows_of(c), :] = lo.reshape(TQ, TQ)
        return carry

    lax.fori_loop(0, nchunks, pin_low, 0)
    t_lo = digit_search(klo_sc)
    thr = t_hi * 65536 + (t_lo - I16_MIN)
    cnt_gt = count(lambda kp, pos: kp > thr[None])
    cnt_ge = count(lambda kp, pos: kp >= thr[None])
    need = kf - cnt_gt
    ambiguous = (cnt_ge - cnt_gt != need) & (thr != INT_MIN)

    j_sc[...] = jnp.full((SUBLANES, TQ), 2 ** 30, jnp.int32)

    @pl.when(jnp.max(ambiguous.astype(F32)) > 0.0)
    def _():
        def tie(i, jt):
            cand = jt + jnp.left_shift(jnp.int32(1), 11 - i)
            cnt = count(lambda kp, pos: (kp == thr[None]) & (pos < cand[None]))
            return jnp.where(cnt < need, cand, jt)
        j_sc[...] = lax.fori_loop(0, 12, tie, jnp.zeros((SUBLANES, TQ), jnp.int32))

    jt = j_sc[...]

    m_sc[...] = jnp.full(m_sc.shape, MAX_FLOOR, F32)
    acc_sc[...] = jnp.zeros(acc_sc.shape, F32)

    def mla_chunk(c, bias_row0):
        ckv_c = ckvp_ref[0, rows_of(c), :]
        kp = tiles(key_sc[rows_of(c), :])
        pos = c * TQ + sub_pos
        sel = ((kp > thr[None]) | ((kp == thr[None]) & (pos <= jt[None]))) & (kp != INT_MIN)
        mask = jnp.where(sel, 0.0, -jnp.inf)
        vt = ckv_c.astype(F32).T.astype(BF16)
        vext = jnp.concatenate([vt, jnp.ones((BF16_ROWS, TQ), BF16)], axis=0)
        alphas = []
        for h in range(B_HEADS):
            sh = jnp.dot(ckv_c, qabs_ref[0, h], preferred_element_type=F32)
            if bias_row0 is not None:
                sh = sh + bnear_ref[h, pl.ds(bias_row0, TQ), :]
            sh = tiles(sh) + mask
            s_sc[h] = sh.reshape(TQ, TQ)
            m_prev = m_sc[h]
            mx = jnp.max(_tree_reduce(jnp.maximum, sh), axis=0, keepdims=True)
            m_new = jnp.maximum(m_prev, mx)
            alphas.append(jnp.exp(m_prev - m_new))
            m_sc[h] = m_new
        for h in range(B_HEADS):
            e = jnp.exp(tiles(s_sc[h]) - m_sc[h][None]).reshape(TQ, TQ).astype(BF16)
            pv = jnp.dot(vext, e, preferred_element_type=F32)
            acc = acc_sc[h].reshape(acc_rows // SUBLANES, SUBLANES, TQ)
            acc_sc[h] = (acc * alphas[h][None] + pv.reshape(acc.shape)).reshape(acc_rows, TQ)

    def far_loop(c, carry):
        mla_chunk(c, None)
        return carry

    def near_loop(i, carry):
        mla_chunk(m_blk + i, pl.multiple_of(i * TQ, TQ))
        return carry

    lax.fori_loop(0, m_blk, far_loop, 0)
    lax.fori_loop(0, 2, near_loop, 0)

    outs = []
    for h in range(B_HEADS):
        acc = acc_sc[h]
        lat = (acc[0:B_KV_RANK] / acc[B_KV_RANK:B_KV_RANK + 1]).astype(BF16)
        outs.append(jnp.dot(wuvt_ref[h], lat, preferred_element_type=F32))
    o_ref[0] = jnp.concatenate(outs, axis=0).T.astype(BF16)


def _mla_call(iqs, iw, qabs, ikp, ckvp, bnear, wuvt, topk):
    b, sp, _ = ikp.shape
    s = iw.shape[1]
    nq = s // TQ
    acc_rows = B_KV_RANK + BF16_ROWS
    return pl.pallas_call(
        functools.partial(_mla_body, topk=topk),
        grid=(b, nq),
        in_specs=[pl.BlockSpec((1, IDX_HEADS, IDX_DIM, TQ), lambda i, m: (i * nq + m, 0, 0, 0)),
                  pl.BlockSpec((1, TQ, LANES), lambda i, m: (i, m, 0)),
                  pl.BlockSpec((1, B_HEADS, B_KV_RANK, TQ), lambda i, m: (i * nq + m, 0, 0, 0)),
                  pl.BlockSpec((1, sp, IDX_DIM), lambda i, m: (i, 0, 0)),
                  pl.BlockSpec((1, sp, B_KV_RANK), lambda i, m: (i, 0, 0)),
                  _full_spec(bnear.shape), _full_spec(wuvt.shape)],
        out_specs=pl.BlockSpec((1, TQ, B_HEADS * B_HEAD_DIM), lambda i, m: (i, m, 0)),
        out_shape=jax.ShapeDtypeStruct((b, s, B_HEADS * B_HEAD_DIM), BF16),
        scratch_shapes=[pltpu.VMEM((sp, TQ), jnp.int32),
                        pltpu.VMEM((sp, TQ), jnp.int16),
                        pltpu.VMEM((sp, TQ), jnp.int16),
                        pltpu.VMEM((B_HEADS, TQ, TQ), F32),
                        pltpu.VMEM((B_HEADS, SUBLANES, TQ), F32),
                        pltpu.VMEM((B_HEADS, acc_rows, TQ), F32),
                        pltpu.VMEM((SUBLANES, TQ), jnp.int32)],
        compiler_params=pltpu.CompilerParams(dimension_semantics=("arbitrary", "arbitrary"),
                                             vmem_limit_bytes=VMEM_LIMIT_BYTES),
        name="indexer_topk_mla",
    )(iqs, iw, qabs, ikp, ckvp, bnear, wuvt)


def _out_body(x_ref, oa_ref, ob_ref, ag_ref, wg_ref, bg_ref, wa_ref, wb_ref, wo_ref, fg_ref,
              wfg_ref, wfu_ref, wfd_ref, ng_ref, y_ref):
    d = x_ref.shape[1]
    x = x_ref[...]
    hn = _rms(x, ag_ref[...]).astype(BF16)

    def gated(o_ref, w_ref, lo):
        gate = jax.nn.sigmoid(jnp.dot(hn, wg_ref[:, lo:lo + d], preferred_element_type=F32) + bg_ref[:, lo:lo + d])
        return gate * jnp.dot(o_ref[...], w_ref[...], preferred_element_type=F32)

    mixed = (gated(oa_ref, wa_ref, 0) + gated(ob_ref, wb_ref, d)).astype(BF16)
    h = x + jnp.dot(mixed, wo_ref[...], preferred_element_type=F32)
    hn2 = _rms(h, fg_ref[...]).astype(BF16)
    dff = wfg_ref.shape[1]
    step = dff // FFN_CHUNKS
    for c in range(0, dff, step):
        g = jnp.dot(hn2, wfg_ref[:, c:c + step], preferred_element_type=F32)
        u = jnp.dot(hn2, wfu_ref[:, c:c + step], preferred_element_type=F32)
        act = (g * jax.nn.sigmoid(g) * u).astype(BF16)
        h = h + jnp.dot(act, wfd_ref[c:c + step, :], preferred_element_type=F32)
    y_ref[...] = _rms(h, ng_ref[...])


def _out_call(x2, oa, ob, ag, wg, bg, wa, wb, wo, fg, wfg, wfu, wfd, ng):
    n, d = x2.shape
    tm = TOK_TILE
    row = lambda i: (i, 0)

    def const_spec(a):
        return pl.BlockSpec(a.shape, lambda i: (0,) * a.ndim, pipeline_mode=pl.Buffered(1))

    consts = [ag, wg, bg, wa, wb, wo, fg, wfg, wfu, wfd, ng]
    return pl.pallas_call(
        _out_body,
        grid=(n // tm,),
        in_specs=[pl.BlockSpec((tm, d), row), pl.BlockSpec((tm, oa.shape[1]), row),
                  pl.BlockSpec((tm, ob.shape[1]), row)] + [const_spec(a) for a in consts],
        out_specs=pl.BlockSpec((tm, d), row),
        out_shape=jax.ShapeDtypeStruct((n, d), F32),
        compiler_params=pltpu.CompilerParams(dimension_semantics=("arbitrary",),
                                             vmem_limit_bytes=VMEM_LIMIT_BYTES),
        name="merge_ffn_norm",
    )(x2, oa, ob, *consts)


def kernel(x, meta_tokens, attn_norm_g, w_in, b_gates, q_norm_g, kv_norm_g, w_uq, w_uk, w_uv, idx_k_ln_g,
           idx_k_ln_b, sinks, rel_bias, w_branch_a, w_branch_b, w_out, ffn_norm_g, w_ffn_gate, w_ffn_up,
           w_ffn_down, final_norm_g):
    b, s, d = x.shape
    assert attn_norm_g.shape[0] == 1, "single-layer block"
    assert s % TQ == 0 and (b * s) % TOK_TILE == 0 and TOK_TILE % TQ == 0
    assert w_ffn_gate.shape[2] % (FFN_CHUNKS * LANES) == 0
    topk = min(TOPK_MAX, s // 4)
    far_bkts = np.unique(_t5_bucket_np(np.arange(BLOCK + 1, s + BLOCK + 1)))
    assert far_bkts.size == 1
    far_bkt = int(far_bkts[0])

    wi = w_in[0]
    widths = (A_HEADS * A_HEAD_DIM, 2 * A_KV_HEADS * A_HEAD_DIM, B_Q_RANK, B_KV_RANK, IDX_HEADS * IDX_DIM,
              IDX_DIM, IDX_HEADS, 2 * d)
    starts = np.concatenate([[0], np.cumsum(widths)])
    w_aq, w_akv, w_bq, w_bkv, w_iq, w_ik, w_iw, w_gates = (wi[:, int(a):int(b_)]
                                                            for a, b_ in zip(starts[:-1], starts[1:]))
    zpad = lambda k: jnp.zeros((d, k), wi.dtype)
    w1 = jnp.concatenate([w_aq, w_akv, w_bkv, w_ik, zpad(LANES - IDX_DIM), w_iw, zpad(LANES - IDX_HEADS)],
                         axis=1).astype(BF16)
    wqi = jnp.concatenate([w_bq, w_iq], axis=1).T.astype(BF16)
    wg = w_gates.astype(BF16)
    row2 = lambda v: v.reshape(1, -1).astype(F32)
    col2 = lambda v: v.reshape(-1, 1).astype(F32)
    wuk = jnp.transpose(w_uk[0], (1, 0, 2)).astype(BF16)
    wuvt = jnp.transpose(w_uv[0], (1, 2, 0)).astype(BF16)
    proj_w = (row2(attn_norm_g[0]), w1, wqi, col2(q_norm_g[0]), w_uq[0].T.astype(BF16), wuk,
              row2(kv_norm_g[0]), row2(idx_k_ln_g[0]), row2(idx_k_ln_b[0]))

    bias_a, bnear = _bias_call(rel_bias, far_bkt)

    x2 = x.reshape(b * s, d)
    aq, akv, qabs, ckv, iqs, ik, iw = _proj_call(x2, TOK_TILE, TQ, *proj_w)
    meta_blk = jnp.concatenate([jnp.zeros((PAD, d), x.dtype), meta_tokens.astype(x.dtype)], axis=0)
    _, makv, _, mckv, _, mik, _ = _proj_call(meta_blk, BLOCK, BLOCK, *proj_w)

    def padded_keys(real, meta):
        c = real.shape[-1]
        return jnp.concatenate([jnp.broadcast_to(meta[None], (b, BLOCK, c)), real.reshape(b, s, c),
                                jnp.zeros((b, TQ - BLOCK, c), real.dtype)], axis=1)

    o_a = _swa_call(sinks[0], aq.reshape(b, s, -1), akv.reshape(b, s, -1), makv, bias_a)
    o_b = _mla_call(iqs, iw.reshape(b, s, -1), qabs, padded_keys(ik, mik), padded_keys(ckv, mckv), bnear, wuvt,
                    topk)

    y = _out_call(x2, o_a.reshape(b * s, -1), o_b.reshape(b * s, -1), row2(attn_norm_g[0]), wg,
                  row2(b_gates[0]), w_branch_a[0].astype(BF16), w_branch_b[0].astype(BF16),
                  w_out[0].astype(BF16), row2(ffn_norm_g[0]), w_ffn_gate[0].astype(BF16),
                  w_ffn_up[0].astype(BF16), w_ffn_down[0].astype(BF16), row2(final_norm_g))
    return y.reshape(b, s, d)
```

```python
import functools
import math

import numpy as np
import jax
import jax.numpy as jnp
from jax import lax
from jax.experimental import pallas as pl
from jax.experimental.pallas import tpu as pltpu

N_META = 16
BLOCK = 128
PAD = BLOCK - N_META
WINDOW = 128
A_HEADS = 8
A_KV_HEADS = 2
A_HEAD_DIM = 64
B_HEADS = 8
B_HEAD_DIM = 64
B_Q_RANK = 256
B_KV_RANK = 128
IDX_HEADS = 4
IDX_DIM = 64
TOPK_MAX = 256
N_BUCKETS = 32
MAX_DISTANCE = 128
EPS = 1e-6
NEG = -1e30
MAX_FLOOR = -3.0e38
INT_MIN = -(2 ** 31)
I16_MIN = -(2 ** 15)
I16_MAX = 2 ** 15 - 1

LANES = 128
SUBLANES = 8
BF16_ROWS = 16
VMEM_LIMIT_BYTES = 56 * 1024 * 1024

TOK_TILE = 512
TQ = 256
FFN_CHUNKS = 2
SWA_BLOCKS = 4

C_AK = 0
C_BKV = C_AK + A_KV_HEADS * A_HEAD_DIM
C_IK = C_BKV + B_KV_RANK
C_IW = C_IK + LANES
C_END = C_IW + LANES
R_AQ = 0
R_AV = R_AQ + A_HEADS * A_HEAD_DIM
R_BQ = R_AV + A_KV_HEADS * A_HEAD_DIM
R_IQ = R_BQ + B_Q_RANK
R_END = R_IQ + IDX_HEADS * IDX_DIM

F32 = jnp.float32
BF16 = jnp.bfloat16
NT_DIMS = (((1,), (1,)), ((), ()))


def _t5_bucket_np(dist):
    dist = np.asarray(dist, np.int64)
    max_exact = N_BUCKETS // 2
    d = np.maximum(dist, 1).astype(np.float32)
    large = max_exact + (np.log(d / np.float32(max_exact)) / np.float32(math.log(MAX_DISTANCE / max_exact))
                         * np.float32(N_BUCKETS - max_exact)).astype(np.int32)
    large = np.minimum(large, N_BUCKETS - 1)
    return np.where(dist < max_exact, dist, large).astype(np.int32)


def _rms(x, g):
    return x * lax.rsqrt(jnp.mean(x * x, axis=-1, keepdims=True) + EPS) * g


def _tree_reduce(op, x):
    while x.shape[0] > 1:
        half = x.shape[0] // 2
        x = op(x[:half], x[half:])
    return x[0]


def _full_spec(shape):
    nd = len(shape)
    return pl.BlockSpec(shape, lambda *_: (0,) * nd)


def _bias_body(tab_ref, bkt_a_ref, bkt_near_ref, ba_ref, bnear_ref, *, far_bkt):
    def lookup(bkt, col, fill):
        acc = jnp.full(bkt.shape, fill, F32)
        for b in range(N_BUCKETS):
            acc = jnp.where(bkt == b, tab_ref[b, col], acc)
        return acc

    bkt_a = bkt_a_ref[...]
    bkt_near = bkt_near_ref[...]
    grp = A_HEADS // A_KV_HEADS
    for h in range(A_HEADS):
        ba_ref[h // grp, :, (h % grp) * BLOCK:(h % grp + 1) * BLOCK] = lookup(bkt_a, h, -jnp.inf)
    for h in range(B_HEADS):
        col = A_HEADS + h
        bnear_ref[h] = lookup(bkt_near, col, 0.0) - tab_ref[far_bkt, col]


def _bias_call(rel_bias, far_bkt):
    k = np.arange(2 * BLOCK)[:, None]
    q = np.arange(BLOCK)[None, :]
    dist = q + BLOCK - k
    bkt_a = np.where((dist >= 0) & (dist < WINDOW), _t5_bucket_np(np.maximum(dist, 0)), -1).astype(np.int32)
    k = np.arange(2 * TQ)[:, None]
    q = np.arange(TQ)[None, :]
    bkt_near = _t5_bucket_np(np.maximum(q + BLOCK - k, 0))
    vmem = pl.BlockSpec(memory_space=pltpu.VMEM)
    grp = A_HEADS // A_KV_HEADS
    return pl.pallas_call(
        functools.partial(_bias_body, far_bkt=far_bkt),
        out_shape=(jax.ShapeDtypeStruct((A_KV_HEADS, 2 * BLOCK, grp * BLOCK), F32),
                   jax.ShapeDtypeStruct((B_HEADS, 2 * TQ, TQ), F32)),
        in_specs=[pl.BlockSpec(memory_space=pltpu.SMEM), vmem, vmem],
        out_specs=(vmem, vmem),
        name="bias_tables",
    )(rel_bias, jnp.asarray(bkt_a), jnp.asarray(bkt_near))


def _proj_body(x_ref, g_ref, w1_ref, wt_ref, qg_ref, wuq_ref, wuk_ref, kvg_ref, ikg_ref, ikb_ref,
               aq_ref, ak_ref, av_ref, qabs_ref, ckv_ref, iqs_ref, ik_ref, iw_ref, *, tq):
    tm = x_ref.shape[0]
    nblk = tm // tq
    nblk_a = tm // BLOCK
    grp = A_HEADS // A_KV_HEADS
    hn = _rms(x_ref[...], g_ref[...]).astype(BF16)

    def proj(lo, hi):
        return jnp.dot(hn, w1_ref[:, lo:hi], preferred_element_type=F32)

    ak_ref[...] = proj(C_AK, C_BKV).astype(BF16)

    feat_t = lax.dot_general(wt_ref[...], hn, NT_DIMS, preferred_element_type=F32)
    aq_t = (feat_t[R_AQ:R_AV] * (A_HEAD_DIM ** -0.5)).astype(BF16)
    for j in range(nblk_a):
        tok = slice(j * BLOCK, (j + 1) * BLOCK)
        for h in range(A_HEADS):
            aq_ref[j, h // grp, :, (h % grp) * BLOCK:(h % grp + 1) * BLOCK] = \
                aq_t[h * A_HEAD_DIM:(h + 1) * A_HEAD_DIM, tok]
    av_t = feat_t[R_AV:R_BQ].astype(BF16)
    for j in range(nblk_a):
        for kvh in range(A_KV_HEADS):
            av_ref[j, kvh] = av_t[kvh * A_HEAD_DIM:(kvh + 1) * A_HEAD_DIM, j * BLOCK:(j + 1) * BLOCK]

    bq_t = feat_t[R_BQ:R_IQ]
    qn_t = bq_t * lax.rsqrt(jnp.mean(bq_t * bq_t, axis=0, keepdims=True) + EPS) * qg_ref[...]
    q_t = jnp.dot(wuq_ref[...], qn_t.astype(BF16), preferred_element_type=F32).astype(BF16)
    for h in range(B_HEADS):
        qa_t = jnp.dot(wuk_ref[h], q_t[h * B_HEAD_DIM:(h + 1) * B_HEAD_DIM], preferred_element_type=F32)
        qa_t = (qa_t * (B_HEAD_DIM ** -0.5)).astype(BF16)
        for j in range(nblk):
            qabs_ref[j, h] = qa_t[:, j * tq:(j + 1) * tq]
    iq_t = feat_t[R_IQ:R_END].astype(BF16)
    for h in range(IDX_HEADS):
        for j in range(nblk):
            iqs_ref[j, h] = iq_t[h * IDX_DIM:(h + 1) * IDX_DIM, j * tq:(j + 1) * tq]

    ckv_ref[...] = _rms(proj(C_BKV, C_IK), kvg_ref[...]).astype(BF16)

    ikw = proj(C_IK, C_END)
    ik = ikw[:, :IDX_DIM]
    mu = jnp.mean(ik, axis=-1, keepdims=True)
    xc = ik - mu
    var = jnp.mean(xc * xc, axis=-1, keepdims=True)
    ik_ref[...] = (xc * lax.rsqrt(var + EPS) * ikg_ref[...] + ikb_ref[...]).astype(BF16)
    iw_ref[...] = ikw[:, LANES:] * ((IDX_HEADS * IDX_DIM) ** -0.5)


def _proj_call(x2, tm, tq, *weights):
    n, d = x2.shape
    grid = (n // tm,)
    row = lambda i: (i, 0)
    blk4 = lambda i: (i, 0, 0, 0)
    grp = A_HEADS // A_KV_HEADS
    out_shape = (
        jax.ShapeDtypeStruct((n // BLOCK, A_KV_HEADS, A_HEAD_DIM, grp * BLOCK), BF16),
        jax.ShapeDtypeStruct((n, A_KV_HEADS * A_HEAD_DIM), BF16),
        jax.ShapeDtypeStruct((n // BLOCK, A_KV_HEADS, A_HEAD_DIM, BLOCK), BF16),
        jax.ShapeDtypeStruct((n // tq, B_HEADS, B_KV_RANK, tq), BF16),
        jax.ShapeDtypeStruct((n, B_KV_RANK), BF16),
        jax.ShapeDtypeStruct((n // tq, IDX_HEADS, IDX_DIM, tq), BF16),
        jax.ShapeDtypeStruct((n, IDX_DIM), BF16),
        jax.ShapeDtypeStruct((n, LANES), F32),
    )
    out_specs = (
        pl.BlockSpec((tm // BLOCK, A_KV_HEADS, A_HEAD_DIM, grp * BLOCK), blk4),
        pl.BlockSpec((tm, A_KV_HEADS * A_HEAD_DIM), row),
        pl.BlockSpec((tm // BLOCK, A_KV_HEADS, A_HEAD_DIM, BLOCK), blk4),
        pl.BlockSpec((tm // tq, B_HEADS, B_KV_RANK, tq), blk4),
        pl.BlockSpec((tm, B_KV_RANK), row),
        pl.BlockSpec((tm // tq, IDX_HEADS, IDX_DIM, tq), blk4),
        pl.BlockSpec((tm, IDX_DIM), row),
        pl.BlockSpec((tm, LANES), row),
    )
    in_specs = [pl.BlockSpec((tm, d), row)] + [_full_spec(w.shape) for w in weights]
    return pl.pallas_call(
        functools.partial(_proj_body, tq=tq),
        grid=grid, in_specs=in_specs, out_specs=out_specs, out_shape=out_shape,
        compiler_params=pltpu.CompilerParams(dimension_semantics=("arbitrary",),
                                             vmem_limit_bytes=VMEM_LIMIT_BYTES),
        name="in_proj",
    )(x2, *weights)


def _swa_body(sinks_ref, aq_ref, kcur_ref, kprev_ref, kmeta_ref, vcur_ref, vprev_ref, vmeta_ref, bias_ref, o_ref):
    n = pl.program_id(1)
    first = n == 0
    grp = A_HEADS // A_KV_HEADS
    width = grp * BLOCK
    kall = jnp.concatenate([jnp.where(first, kmeta_ref[...], kprev_ref[0]), kcur_ref[0]], axis=0)
    vall = [jnp.concatenate([jnp.where(first, vmeta_ref[0, kvh], vprev_ref[0, kvh])]
                            + [vcur_ref[j, kvh] for j in range(SWA_BLOCKS)], axis=1)
            for kvh in range(A_KV_HEADS)]
    ntile = 2 * BLOCK // SUBLANES
    key_row = (lax.broadcasted_iota(jnp.int32, (ntile, SUBLANES, width), 0) * SUBLANES
               + lax.broadcasted_iota(jnp.int32, (ntile, SUBLANES, width), 1))
    pad_row = first & (key_row < PAD)
    lane_head = lax.broadcasted_iota(jnp.int32, (1, width), 1) // BLOCK
    ones = jnp.ones((BF16_ROWS, 2 * BLOCK), BF16)
    sinks = []
    for kvh in range(A_KV_HEADS):
        sink = jnp.zeros((1, width), F32)
        for g in range(grp):
            sink = jnp.where(lane_head == g, sinks_ref[kvh * grp + g], sink)
        sinks.append(sink)
    probs_ids = [(j, kvh) for j in range(SWA_BLOCKS) for kvh in range(A_KV_HEADS)]
    scores = [jnp.dot(kall[j * BLOCK:(j + 2) * BLOCK, kvh * A_HEAD_DIM:(kvh + 1) * A_HEAD_DIM], aq_ref[j, kvh],
                      preferred_element_type=F32) for j, kvh in probs_ids]
    maxes, probs = [], []
    for i, (j, kvh) in enumerate(probs_ids):
        s = (scores[i] + bias_ref[kvh]).reshape(ntile, SUBLANES, width)
        if j == 0:
            s = jnp.where(pad_row, -jnp.inf, s)
        m = jnp.maximum(jnp.max(_tree_reduce(jnp.maximum, s), axis=0, keepdims=True), sinks[kvh])
        maxes.append(m)
        probs.append(jnp.exp(s - m[None]).reshape(2 * BLOCK, width).astype(BF16))
    pvs = [jnp.dot(jnp.concatenate([vall[kvh][:, j * BLOCK:(j + 2) * BLOCK], ones], axis=0), probs[i],
                   preferred_element_type=F32) for i, (j, kvh) in enumerate(probs_ids)]
    for j in range(SWA_BLOCKS):
        heads_t = []
        for kvh in range(A_KV_HEADS):
            i = j * A_KV_HEADS + kvh
            den = pvs[i][A_HEAD_DIM:A_HEAD_DIM + 1] + jnp.exp(sinks[kvh] - maxes[i])
            o_t = pvs[i][0:A_HEAD_DIM] / den
            heads_t += [o_t[:, g * BLOCK:(g + 1) * BLOCK] for g in range(grp)]
        o_ref[0, j * BLOCK:(j + 1) * BLOCK, :] = jnp.concatenate(heads_t, axis=0).T.astype(BF16)


def _swa_call(sinks, aq_t, ak, av_t, mak, mav_t, bias_a, b):
    nblocks = aq_t.shape[0]
    nb = nblocks // b
    assert nb % SWA_BLOCKS == 0
    ns = nb // SWA_BLOCKS
    s = nb * BLOCK
    rows = SWA_BLOCKS * BLOCK
    cur4 = lambda i, n: (i * ns + n, 0, 0, 0)
    prev_blk = lambda n: jnp.maximum(n * SWA_BLOCKS - 1, 0)
    kdim = ak.shape[-1]
    ak3 = ak.reshape(b, s, kdim)
    return pl.pallas_call(
        _swa_body,
        grid=(b, ns),
        in_specs=[pl.BlockSpec(memory_space=pltpu.SMEM),
                  pl.BlockSpec((SWA_BLOCKS,) + aq_t.shape[1:], cur4),
                  pl.BlockSpec((1, rows, kdim), lambda i, n: (i, n, 0)),
                  pl.BlockSpec((1, BLOCK, kdim), lambda i, n: (i, prev_blk(n), 0)),
                  _full_spec(mak.shape),
                  pl.BlockSpec((SWA_BLOCKS,) + av_t.shape[1:], cur4),
                  pl.BlockSpec((1,) + av_t.shape[1:], lambda i, n: (i * nb + prev_blk(n), 0, 0, 0)),
                  _full_spec(mav_t.shape), _full_spec(bias_a.shape)],
        out_specs=pl.BlockSpec((1, rows, A_HEADS * A_HEAD_DIM), lambda i, n: (i, n, 0)),
        out_shape=jax.ShapeDtypeStruct((b, s, A_HEADS * A_HEAD_DIM), BF16),
        compiler_params=pltpu.CompilerParams(dimension_semantics=("arbitrary", "arbitrary"),
                                             vmem_limit_bytes=VMEM_LIMIT_BYTES),
        name="swa_sink_attention",
    )(sinks, aq_t, ak3, ak3, mak, av_t, av_t, mav_t, bias_a)


def _mla_body(iqs_ref, iw_ref, qabs_ref, ikp_ref, ckvp_ref, bnear_ref, wuvt_ref, o_ref,
              key_sc, khi_sc, klo_sc, s_sc, m_sc, acc_sc, j_sc, *, topk):
    m_blk = pl.program_id(1)
    nchunks = m_blk + 2
    groups = TQ // SUBLANES
    acc_rows = B_KV_RANK + BF16_ROWS
    iw_t = iw_ref[0].T
    wrow = [iw_t[h:h + 1, :] for h in range(IDX_HEADS)]

    sub_pos = (lax.broadcasted_iota(jnp.int32, (groups, SUBLANES, TQ), 0) * SUBLANES
               + lax.broadcasted_iota(jnp.int32, (groups, SUBLANES, TQ), 1))
    qpos = BLOCK + m_blk * TQ + lax.broadcasted_iota(jnp.int32, (groups, SUBLANES, TQ), 2)

    def rows_of(c):
        return pl.ds(pl.multiple_of(c * TQ, TQ), TQ)

    def tiles(x):
        return x.reshape(groups, SUBLANES, x.shape[-1])

    def sort_key(sc):
        bits = lax.bitcast_convert_type(sc, jnp.int32)
        bits = jnp.where(sc == 0.0, 0, bits)
        return jnp.where(bits < 0, bits ^ 0x7FFFFFFF, bits)

    def score_chunk(c, carry):
        ik_c = ikp_ref[0, rows_of(c), :]
        sc = None
        for h in range(IDX_HEADS):
            logits = jnp.dot(ik_c, iqs_ref[0, h], preferred_element_type=F32)
            term = jnp.maximum(logits, 0.0) * wrow[h]
            sc = term if sc is None else sc + term
        pos = c * TQ + sub_pos
        adm = (pos >= PAD) & (pos <= qpos)
        key = jnp.where(adm, tiles(sort_key(sc)), INT_MIN).reshape(TQ, TQ)
        key_sc[rows_of(c), :] = key
        khi_sc[rows_of(c), :] = jnp.right_shift(key, 16).astype(jnp.int16)
        klo_sc[rows_of(c), :] = (key ^ 0x8000).astype(jnp.int16)
        return carry

    lax.fori_loop(0, nchunks, score_chunk, 0)

    def count(pred):
        def body(c, part):
            hit = pred(tiles(key_sc[rows_of(c), :]), c * TQ + sub_pos)
            return part + _tree_reduce(jnp.add, jnp.where(hit, 1.0, 0.0))
        part = lax.fori_loop(0, nchunks, body, jnp.zeros((SUBLANES, TQ), F32))
        return jnp.broadcast_to(jnp.sum(part, axis=0, keepdims=True), (SUBLANES, TQ))

    kf = float(topk)

    def tiles16(x):
        return x.reshape(TQ // BF16_ROWS, BF16_ROWS, TQ)

    def count16_ge(ref, cand):
        cand16 = jnp.broadcast_to(cand[0:1], (BF16_ROWS, TQ)).astype(jnp.int16)
        one, zero = jnp.int16(1), jnp.int16(0)

        def body(c, part):
            hit = tiles16(ref[rows_of(c), :]) >= cand16[None]
            return part + _tree_reduce(jnp.add, jnp.where(hit, one, zero))
        part = lax.fori_loop(0, nchunks, body, jnp.zeros((BF16_ROWS, TQ), jnp.int16))
        total = jnp.sum(part.astype(jnp.int32), axis=0, keepdims=True)
        return jnp.broadcast_to(total, (SUBLANES, TQ))

    def digit_search(ref):
        def step(i, t):
            cand = t + jnp.left_shift(jnp.int32(1), 15 - i)
            return jnp.where(count16_ge(ref, cand) >= topk, cand, t)
        return lax.fori_loop(0, 16, step, jnp.full((SUBLANES, TQ), I16_MIN, jnp.int32))

    t_hi = digit_search(khi_sc)
    t_hi16 = jnp.broadcast_to(t_hi[0:1], (BF16_ROWS, TQ)).astype(jnp.int16)

    def pin_low(c, carry):
        hi = tiles16(khi_sc[rows_of(c), :])
        lo = tiles16(klo_sc[rows_of(c), :])
        lo = jnp.where(hi > t_hi16[None], jnp.int16(I16_MAX), jnp.where(hi < t_hi16[None], jnp.int16(I16_MIN), lo))
        klo_sc[rows_of(c), :] = lo.reshape(TQ, TQ)
        return carry

    lax.fori_loop(0, nchunks, pin_low, 0)
    t_lo = digit_search(klo_sc)
    thr = t_hi * 65536 + (t_lo - I16_MIN)
    cnt_gt = count(lambda kp, pos: kp > thr[None])
    cnt_ge = count(lambda kp, pos: kp >= thr[None])
    need = kf - cnt_gt
    ambiguous = (cnt_ge - cnt_gt != need) & (thr != INT_MIN)

    j_sc[...] = jnp.full((SUBLANES, TQ), 2 ** 30, jnp.int32)

    @pl.when(jnp.max(ambiguous.astype(F32)) > 0.0)
    def _():
        def tie(i, jt):
            cand = jt + jnp.left_shift(jnp.int32(1), 11 - i)
            cnt = count(lambda kp, pos: (kp == thr[None]) & (pos < cand[None]))
            return jnp.where(cnt < need, cand, jt)
        j_sc[...] = lax.fori_loop(0, 12, tie, jnp.zeros((SUBLANES, TQ), jnp.int32))

    jt = j_sc[...]

    m_sc[...] = jnp.full(m_sc.shape, MAX_FLOOR, F32)
    acc_sc[...] = jnp.zeros(acc_sc.shape, F32)

    def mla_chunk(c, bias_row0):
        ckv_c = ckvp_ref[0, rows_of(c), :]
        kp = tiles(key_sc[rows_of(c), :])
        pos = c * TQ + sub_pos
        sel = ((kp > thr[None]) | ((kp == thr[None]) & (pos <= jt[None]))) & (kp != INT_MIN)
        mask = jnp.where(sel, 0.0, -jnp.inf)
        vt = ckv_c.astype(F32).T.astype(BF16)
        vext = jnp.concatenate([vt, jnp.ones((BF16_ROWS, TQ), BF16)], axis=0)
        alphas = []
        for h in range(B_HEADS):
            sh = jnp.dot(ckv_c, qabs_ref[0, h], preferred_element_type=F32)
            if bias_row0 is not None:
                sh = sh + bnear_ref[h, pl.ds(bias_row0, TQ), :]
            sh = tiles(sh) + mask
            s_sc[h] = sh.reshape(TQ, TQ)
            m_prev = m_sc[h]
            mx = jnp.max(_tree_reduce(jnp.maximum, sh), axis=0, keepdims=True)
            m_new = jnp.maximum(m_prev, mx)
            alphas.append(jnp.exp(m_prev - m_new))
            m_sc[h] = m_new
        for h in range(B_HEADS):
            e = jnp.exp(tiles(s_sc[h]) - m_sc[h][None]).reshape(TQ, TQ).astype(BF16)
            pv = jnp.dot(vext, e, preferred_element_type=F32)
            acc = acc_sc[h].reshape(acc_rows // SUBLANES, SUBLANES, TQ)
            acc_sc[h] = (acc * alphas[h][None] + pv.reshape(acc.shape)).reshape(acc_rows, TQ)

    def far_loop(c, carry):
        mla_chunk(c, None)
        return carry

    def near_loop(i, carry):
        mla_chunk(m_blk + i, pl.multiple_of(i * TQ, TQ))
        return carry

    lax.fori_loop(0, m_blk, far_loop, 0)
    lax.fori_loop(0, 2, near_loop, 0)

    outs = []
    for h in range(B_HEADS):
        acc = acc_sc[h]
        lat = (acc[0:B_KV_RANK] / acc[B_KV_RANK:B_KV_RANK + 1]).astype(BF16)
        outs.append(jnp.dot(wuvt_ref[h], lat, preferred_element_type=F32))
    o_ref[0] = jnp.concatenate(outs, axis=0).T.astype(BF16)


def _mla_call(iqs, iw, qabs, ikp, ckvp, bnear, wuvt, topk):
    b, sp, _ = ikp.shape
    s = iw.shape[1]
    nq = s // TQ
    acc_rows = B_KV_RANK + BF16_ROWS
    return pl.pallas_call(
        functools.partial(_mla_body, topk=topk),
        grid=(b, nq),
        in_specs=[pl.BlockSpec((1, IDX_HEADS, IDX_DIM, TQ), lambda i, m: (i * nq + m, 0, 0, 0)),
                  pl.BlockSpec((1, TQ, LANES), lambda i, m: (i, m, 0)),
                  pl.BlockSpec((1, B_HEADS, B_KV_RANK, TQ), lambda i, m: (i * nq + m, 0, 0, 0)),
                  pl.BlockSpec((1, sp, IDX_DIM), lambda i, m: (i, 0, 0)),
                  pl.BlockSpec((1, sp, B_KV_RANK), lambda i, m: (i, 0, 0)),
                  _full_spec(bnear.shape), _full_spec(wuvt.shape)],
        out_specs=pl.BlockSpec((1, TQ, B_HEADS * B_HEAD_DIM), lambda i, m: (i, m, 0)),
        out_shape=jax.ShapeDtypeStruct((b, s, B_HEADS * B_HEAD_DIM), BF16),
        scratch_shapes=[pltpu.VMEM((sp, TQ), jnp.int32),
                        pltpu.VMEM((sp, TQ), jnp.int16),
                        pltpu.VMEM((sp, TQ), jnp.int16),
                        pltpu.VMEM((B_HEADS, TQ, TQ), F32),
                        pltpu.VMEM((B_HEADS, SUBLANES, TQ), F32),
                        pltpu.VMEM((B_HEADS, acc_rows, TQ), F32),
                        pltpu.VMEM((SUBLANES, TQ), jnp.int32)],
        compiler_params=pltpu.CompilerParams(dimension_semantics=("arbitrary", "arbitrary"),
                                             vmem_limit_bytes=VMEM_LIMIT_BYTES),
        name="indexer_topk_mla",
    )(iqs, iw, qabs, ikp, ckvp, bnear, wuvt)


def _out_body(x_ref, oa_ref, ob_ref, ag_ref, wg_ref, bg_ref, wa_ref, wb_ref, wo_ref, fg_ref,
              wfg_ref, wfu_ref, wfd_ref, ng_ref, y_ref):
    d = x_ref.shape[1]
    x = x_ref[...]
    hn = _rms(x, ag_ref[...]).astype(BF16)

    def gated(o_ref, w_ref, lo):
        gate = jax.nn.sigmoid(jnp.dot(hn, wg_ref[:, lo:lo + d], preferred_element_type=F32) + bg_ref[:, lo:lo + d])
        return gate * jnp.dot(o_ref[...], w_ref[...], preferred_element_type=F32)

    mixed = (gated(oa_ref, wa_ref, 0) + gated(ob_ref, wb_ref, d)).astype(BF16)
    h = x + jnp.dot(mixed, wo_ref[...], preferred_element_type=F32)
    hn2 = _rms(h, fg_ref[...]).astype(BF16)
    dff = wfg_ref.shape[1]
    step = dff // FFN_CHUNKS
    for c in range(0, dff, step):
        g = jnp.dot(hn2, wfg_ref[:, c:c + step], preferred_element_type=F32)
        u = jnp.dot(hn2, wfu_ref[:, c:c + step], preferred_element_type=F32)
        act = (g * jax.nn.sigmoid(g) * u).astype(BF16)
        h = h + jnp.dot(act, wfd_ref[c:c + step, :], preferred_element_type=F32)
    y_ref[...] = _rms(h, ng_ref[...])


def _out_call(x2, oa, ob, ag, wg, bg, wa, wb, wo, fg, wfg, wfu, wfd, ng):
    n, d = x2.shape
    tm = TOK_TILE
    row = lambda i: (i, 0)

    def const_spec(a):
        return pl.BlockSpec(a.shape, lambda i: (0,) * a.ndim, pipeline_mode=pl.Buffered(1))

    consts = [ag, wg, bg, wa, wb, wo, fg, wfg, wfu, wfd, ng]
    return pl.pallas_call(
        _out_body,
        grid=(n // tm,),
        in_specs=[pl.BlockSpec((tm, d), row), pl.BlockSpec((tm, oa.shape[1]), row),
                  pl.BlockSpec((tm, ob.shape[1]), row)] + [const_spec(a) for a in consts],
        out_specs=pl.BlockSpec((tm, d), row),
        out_shape=jax.ShapeDtypeStruct((n, d), F32),
        compiler_params=pltpu.CompilerParams(dimension_semantics=("arbitrary",),
                                             vmem_limit_bytes=VMEM_LIMIT_BYTES),
        name="merge_ffn_norm",
    )(x2, oa, ob, *consts)


def kernel(x, meta_tokens, attn_norm_g, w_in, b_gates, q_norm_g, kv_norm_g, w_uq, w_uk, w_uv, idx_k_ln_g,
           idx_k_ln_b, sinks, rel_bias, w_branch_a, w_branch_b, w_out, ffn_norm_g, w_ffn_gate, w_ffn_up,
           w_ffn_down, final_norm_g):
    b, s, d = x.shape
    assert attn_norm_g.shape[0] == 1, "single-layer block"
    assert s % TQ == 0 and (b * s) % TOK_TILE == 0 and TOK_TILE % TQ == 0
    assert w_ffn_gate.shape[2] % (FFN_CHUNKS * LANES) == 0
    topk = min(TOPK_MAX, s // 4)
    far_bkts = np.unique(_t5_bucket_np(np.arange(BLOCK + 1, s + BLOCK + 1)))
    assert far_bkts.size == 1
    far_bkt = int(far_bkts[0])

    wi = w_in[0]
    widths = (A_HEADS * A_HEAD_DIM, A_KV_HEADS * A_HEAD_DIM, A_KV_HEADS * A_HEAD_DIM, B_Q_RANK, B_KV_RANK,
              IDX_HEADS * IDX_DIM, IDX_DIM, IDX_HEADS, 2 * d)
    starts = np.concatenate([[0], np.cumsum(widths)])
    w_aq, w_ak, w_av, w_bq, w_bkv, w_iq, w_ik, w_iw, w_gates = (wi[:, int(a):int(b_)]
                                                                 for a, b_ in zip(starts[:-1], starts[1:]))
    zpad = lambda k: jnp.zeros((d, k), wi.dtype)
    w1 = jnp.concatenate([w_ak, w_bkv, w_ik, zpad(LANES - IDX_DIM), w_iw, zpad(LANES - IDX_HEADS)],
                         axis=1).astype(BF16)
    wt = jnp.concatenate([w_aq, w_av, w_bq, w_iq], axis=1).T.astype(BF16)
    wg = w_gates.astype(BF16)
    row2 = lambda v: v.reshape(1, -1).astype(F32)
    col2 = lambda v: v.reshape(-1, 1).astype(F32)
    wuk = jnp.transpose(w_uk[0], (1, 0, 2)).astype(BF16)
    wuvt = jnp.transpose(w_uv[0], (1, 2, 0)).astype(BF16)
    proj_w = (row2(attn_norm_g[0]), w1, wt, col2(q_norm_g[0]), w_uq[0].T.astype(BF16), wuk,
              row2(kv_norm_g[0]), row2(idx_k_ln_g[0]), row2(idx_k_ln_b[0]))

    bias_a, bnear = _bias_call(rel_bias, far_bkt)

    x2 = x.reshape(b * s, d)
    aq_t, ak, av_t, qabs, ckv, iqs, ik, iw = _proj_call(x2, TOK_TILE, TQ, *proj_w)
    meta_blk = jnp.concatenate([jnp.zeros((PAD, d), x.dtype), meta_tokens.astype(x.dtype)], axis=0)
    _, mak, mav_t, _, mckv, _, mik, _ = _proj_call(meta_blk, BLOCK, BLOCK, *proj_w)

    def padded_keys(real, meta):
        c = real.shape[-1]
        return jnp.concatenate([jnp.broadcast_to(meta[None], (b, BLOCK, c)), real.reshape(b, s, c),
                                jnp.zeros((b, TQ - BLOCK, c), real.dtype)], axis=1)

    o_a = _swa_call(sinks[0], aq_t, ak, av_t, mak, mav_t, bias_a, b)
    o_b = _mla_call(iqs, iw.reshape(b, s, -1), qabs, padded_keys(ik, mik), padded_keys(ckv, mckv), bnear, wuvt,
                    topk)

    y = _out_call(x2, o_a.reshape(b * s, -1), o_b.reshape(b * s, -1), row2(attn_norm_g[0]), wg,
                  row2(b_gates[0]), w_branch_a[0].astype(BF16), w_branch_b[0].astype(BF16),
                  w_out[0].astype(BF16), row2(ffn_norm_g[0]), w_ffn_gate[0].astype(BF16),
                  w_ffn_up[0].astype(BF16), w_ffn_down[0].astype(BF16), row2(final_norm_g))
    return y.reshape(b, s, d)
```

```python
import functools
import math

import numpy as np
import jax
import jax.numpy as jnp
from jax import lax
from jax.experimental import pallas as pl
from jax.experimental.pallas import tpu as pltpu

N_META = 16
BLOCK = 128
PAD = BLOCK - N_META
WINDOW = 128
A_HEADS = 8
A_KV_HEADS = 2
A_HEAD_DIM = 64
B_HEADS = 8
B_HEAD_DIM = 64
B_Q_RANK = 256
B_KV_RANK = 128
IDX_HEADS = 4
IDX_DIM = 64
TOPK_MAX = 256
N_BUCKETS = 32
MAX_DISTANCE = 128
EPS = 1e-6
NEG = -1e30
MAX_FLOOR = -3.0e38
INT_MIN = -(2 ** 31)
LOG2E = math.log2(math.e)
I16_MIN = -(2 ** 15)
I16_MAX = 2 ** 15 - 1

LANES = 128
SUBLANES = 8
BF16_ROWS = 16
VMEM_LIMIT_BYTES = 56 * 1024 * 1024

TOK_TILE = 512
TQ = 256
FFN_CHUNKS = 2
SWA_BLOCKS = 4

C_AK = 0
C_BKV = C_AK + A_KV_HEADS * A_HEAD_DIM
C_IK = C_BKV + B_KV_RANK
C_IW = C_IK + LANES
C_END = C_IW + LANES
R_AQ = 0
R_AV = R_AQ + A_HEADS * A_HEAD_DIM
R_BQ = R_AV + A_KV_HEADS * A_HEAD_DIM
R_IQ = R_BQ + B_Q_RANK
R_END = R_IQ + IDX_HEADS * IDX_DIM

F32 = jnp.float32
BF16 = jnp.bfloat16
NT_DIMS = (((1,), (1,)), ((), ()))


def _t5_bucket_np(dist):
    dist = np.asarray(dist, np.int64)
    max_exact = N_BUCKETS // 2
    d = np.maximum(dist, 1).astype(np.float32)
    large = max_exact + (np.log(d / np.float32(max_exact)) / np.float32(math.log(MAX_DISTANCE / max_exact))
                         * np.float32(N_BUCKETS - max_exact)).astype(np.int32)
    large = np.minimum(large, N_BUCKETS - 1)
    return np.where(dist < max_exact, dist, large).astype(np.int32)


def _rms(x, g):
    return x * lax.rsqrt(jnp.mean(x * x, axis=-1, keepdims=True) + EPS) * g


def _tree_reduce(op, x):
    while x.shape[0] > 1:
        half = x.shape[0] // 2
        x = op(x[:half], x[half:])
    return x[0]


def _full_spec(shape):
    nd = len(shape)
    return pl.BlockSpec(shape, lambda *_: (0,) * nd)


def _bias_body(tab_ref, bkt_a_ref, bkt_near_ref, ba_ref, bnear_ref, *, far_bkt):
    def lookup(bkt, col, fill):
        acc = jnp.full(bkt.shape, fill, F32)
        for b in range(N_BUCKETS):
            acc = jnp.where(bkt == b, tab_ref[b, col], acc)
        return acc

    bkt_a = bkt_a_ref[...]
    bkt_near = bkt_near_ref[...]
    grp = A_HEADS // A_KV_HEADS
    for h in range(A_HEADS):
        ba_ref[h // grp, :, (h % grp) * BLOCK:(h % grp + 1) * BLOCK] = lookup(bkt_a, h, -jnp.inf)
    for h in range(B_HEADS):
        col = A_HEADS + h
        bnear_ref[h, 0:TQ, :] = jnp.zeros((TQ, TQ), F32)
        bnear_ref[h, TQ:3 * TQ, :] = (lookup(bkt_near, col, 0.0) - tab_ref[far_bkt, col]) * LOG2E


def _bias_call(rel_bias, far_bkt):
    k = np.arange(2 * BLOCK)[:, None]
    q = np.arange(BLOCK)[None, :]
    dist = q + BLOCK - k
    bkt_a = np.where((dist >= 0) & (dist < WINDOW), _t5_bucket_np(np.maximum(dist, 0)), -1).astype(np.int32)
    k = np.arange(2 * TQ)[:, None]
    q = np.arange(TQ)[None, :]
    bkt_near = _t5_bucket_np(np.maximum(q + BLOCK - k, 0))
    vmem = pl.BlockSpec(memory_space=pltpu.VMEM)
    grp = A_HEADS // A_KV_HEADS
    return pl.pallas_call(
        functools.partial(_bias_body, far_bkt=far_bkt),
        out_shape=(jax.ShapeDtypeStruct((A_KV_HEADS, 2 * BLOCK, grp * BLOCK), F32),
                   jax.ShapeDtypeStruct((B_HEADS, 3 * TQ, TQ), F32)),
        in_specs=[pl.BlockSpec(memory_space=pltpu.SMEM), vmem, vmem],
        out_specs=(vmem, vmem),
        name="bias_tables",
    )(rel_bias, jnp.asarray(bkt_a), jnp.asarray(bkt_near))


def _proj_body(x_ref, g_ref, w1_ref, wt_ref, qg_ref, wuq_ref, wuk_ref, kvg_ref, ikg_ref, ikb_ref,
               aq_ref, ak_ref, av_ref, qabs_ref, ckv_ref, iqs_ref, ik_ref, iw_ref, *, tq):
    tm = x_ref.shape[0]
    nblk = tm // tq
    nblk_a = tm // BLOCK
    grp = A_HEADS // A_KV_HEADS
    hn = _rms(x_ref[...], g_ref[...]).astype(BF16)

    def proj(lo, hi):
        return jnp.dot(hn, w1_ref[:, lo:hi], preferred_element_type=F32)

    ak_ref[...] = proj(C_AK, C_BKV).astype(BF16)

    feat_t = lax.dot_general(wt_ref[...], hn, NT_DIMS, preferred_element_type=F32)
    aq_t = (feat_t[R_AQ:R_AV] * (A_HEAD_DIM ** -0.5)).astype(BF16)
    for j in range(nblk_a):
        tok = slice(j * BLOCK, (j + 1) * BLOCK)
        for h in range(A_HEADS):
            aq_ref[j, h // grp, :, (h % grp) * BLOCK:(h % grp + 1) * BLOCK] = \
                aq_t[h * A_HEAD_DIM:(h + 1) * A_HEAD_DIM, tok]
    av_t = feat_t[R_AV:R_BQ].astype(BF16)
    for j in range(nblk_a):
        for kvh in range(A_KV_HEADS):
            av_ref[j, kvh] = av_t[kvh * A_HEAD_DIM:(kvh + 1) * A_HEAD_DIM, j * BLOCK:(j + 1) * BLOCK]

    bq_t = feat_t[R_BQ:R_IQ]
    qn_t = bq_t * lax.rsqrt(jnp.mean(bq_t * bq_t, axis=0, keepdims=True) + EPS) * qg_ref[...]
    q_t = jnp.dot(wuq_ref[...], qn_t.astype(BF16), preferred_element_type=F32).astype(BF16)
    for h in range(B_HEADS):
        qa_t = jnp.dot(wuk_ref[h], q_t[h * B_HEAD_DIM:(h + 1) * B_HEAD_DIM], preferred_element_type=F32)
        qa_t = (qa_t * (B_HEAD_DIM ** -0.5 * LOG2E)).astype(BF16)
        for j in range(nblk):
            qabs_ref[j, h] = qa_t[:, j * tq:(j + 1) * tq]
    iq_t = feat_t[R_IQ:R_END].astype(BF16)
    for h in range(IDX_HEADS):
        for j in range(nblk):
            iqs_ref[j, h] = iq_t[h * IDX_DIM:(h + 1) * IDX_DIM, j * tq:(j + 1) * tq]

    ckv_ref[...] = _rms(proj(C_BKV, C_IK), kvg_ref[...]).astype(BF16)

    ikw = proj(C_IK, C_END)
    ik = ikw[:, :IDX_DIM]
    mu = jnp.mean(ik, axis=-1, keepdims=True)
    xc = ik - mu
    var = jnp.mean(xc * xc, axis=-1, keepdims=True)
    ik_ref[...] = (xc * lax.rsqrt(var + EPS) * ikg_ref[...] + ikb_ref[...]).astype(BF16)
    iw_ref[...] = ikw[:, LANES:] * ((IDX_HEADS * IDX_DIM) ** -0.5)


def _proj_call(x2, tm, tq, *weights):
    n, d = x2.shape
    grid = (n // tm,)
    row = lambda i: (i, 0)
    blk4 = lambda i: (i, 0, 0, 0)
    grp = A_HEADS // A_KV_HEADS
    out_shape = (
        jax.ShapeDtypeStruct((n // BLOCK, A_KV_HEADS, A_HEAD_DIM, grp * BLOCK), BF16),
        jax.ShapeDtypeStruct((n, A_KV_HEADS * A_HEAD_DIM), BF16),
        jax.ShapeDtypeStruct((n // BLOCK, A_KV_HEADS, A_HEAD_DIM, BLOCK), BF16),
        jax.ShapeDtypeStruct((n // tq, B_HEADS, B_KV_RANK, tq), BF16),
        jax.ShapeDtypeStruct((n, B_KV_RANK), BF16),
        jax.ShapeDtypeStruct((n // tq, IDX_HEADS, IDX_DIM, tq), BF16),
        jax.ShapeDtypeStruct((n, IDX_DIM), BF16),
        jax.ShapeDtypeStruct((n, LANES), F32),
    )
    out_specs = (
        pl.BlockSpec((tm // BLOCK, A_KV_HEADS, A_HEAD_DIM, grp * BLOCK), blk4),
        pl.BlockSpec((tm, A_KV_HEADS * A_HEAD_DIM), row),
        pl.BlockSpec((tm // BLOCK, A_KV_HEADS, A_HEAD_DIM, BLOCK), blk4),
        pl.BlockSpec((tm // tq, B_HEADS, B_KV_RANK, tq), blk4),
        pl.BlockSpec((tm, B_KV_RANK), row),
        pl.BlockSpec((tm // tq, IDX_HEADS, IDX_DIM, tq), blk4),
        pl.BlockSpec((tm, IDX_DIM), row),
        pl.BlockSpec((tm, LANES), row),
    )
    in_specs = [pl.BlockSpec((tm, d), row)] + [_full_spec(w.shape) for w in weights]
    return pl.pallas_call(
        functools.partial(_proj_body, tq=tq),
        grid=grid, in_specs=in_specs, out_specs=out_specs, out_shape=out_shape,
        compiler_params=pltpu.CompilerParams(dimension_semantics=("arbitrary",),
                                             vmem_limit_bytes=VMEM_LIMIT_BYTES),
        name="in_proj",
    )(x2, *weights)


def _swa_body(sinks_ref, aq_ref, kcur_ref, kprev_ref, kmeta_ref, vcur_ref, vprev_ref, vmeta_ref, bias_ref, o_ref):
    n = pl.program_id(1)
    first = n == 0
    grp = A_HEADS // A_KV_HEADS
    width = grp * BLOCK
    kall = jnp.concatenate([jnp.where(first, kmeta_ref[...], kprev_ref[0]), kcur_ref[0]], axis=0)
    vall = [jnp.concatenate([jnp.where(first, vmeta_ref[0, kvh], vprev_ref[0, kvh])]
                            + [vcur_ref[j, kvh] for j in range(SWA_BLOCKS)], axis=1)
            for kvh in range(A_KV_HEADS)]
    ntile = 2 * BLOCK // SUBLANES
    key_row = (lax.broadcasted_iota(jnp.int32, (ntile, SUBLANES, width), 0) * SUBLANES
               + lax.broadcasted_iota(jnp.int32, (ntile, SUBLANES, width), 1))
    pad_row = first & (key_row < PAD)
    lane_head = lax.broadcasted_iota(jnp.int32, (1, width), 1) // BLOCK
    ones = jnp.ones((BF16_ROWS, 2 * BLOCK), BF16)
    sinks = []
    for kvh in range(A_KV_HEADS):
        sink = jnp.zeros((1, width), F32)
        for g in range(grp):
            sink = jnp.where(lane_head == g, sinks_ref[kvh * grp + g], sink)
        sinks.append(sink)
    probs_ids = [(j, kvh) for j in range(SWA_BLOCKS) for kvh in range(A_KV_HEADS)]
    scores = [jnp.dot(kall[j * BLOCK:(j + 2) * BLOCK, kvh * A_HEAD_DIM:(kvh + 1) * A_HEAD_DIM], aq_ref[j, kvh],
                      preferred_element_type=F32) for j, kvh in probs_ids]
    maxes, probs = [], []
    for i, (j, kvh) in enumerate(probs_ids):
        s = (scores[i] + bias_ref[kvh]).reshape(ntile, SUBLANES, width)
        if j == 0:
            s = jnp.where(pad_row, -jnp.inf, s)
        m = jnp.maximum(jnp.max(_tree_reduce(jnp.maximum, s), axis=0, keepdims=True), sinks[kvh])
        maxes.append(m)
        probs.append(jnp.exp(s - m[None]).reshape(2 * BLOCK, width).astype(BF16))
    pvs = [jnp.dot(jnp.concatenate([vall[kvh][:, j * BLOCK:(j + 2) * BLOCK], ones], axis=0), probs[i],
                   preferred_element_type=F32) for i, (j, kvh) in enumerate(probs_ids)]
    for j in range(SWA_BLOCKS):
        heads_t = []
        for kvh in range(A_KV_HEADS):
            i = j * A_KV_HEADS + kvh
            den = pvs[i][A_HEAD_DIM:A_HEAD_DIM + 1] + jnp.exp(sinks[kvh] - maxes[i])
            o_t = pvs[i][0:A_HEAD_DIM] / den
            heads_t += [o_t[:, g * BLOCK:(g + 1) * BLOCK] for g in range(grp)]
        o_ref[0, j * BLOCK:(j + 1) * BLOCK, :] = jnp.concatenate(heads_t, axis=0).T.astype(BF16)


def _swa_call(sinks, aq_t, ak, av_t, mak, mav_t, bias_a, b):
    nblocks = aq_t.shape[0]
    nb = nblocks // b
    assert nb % SWA_BLOCKS == 0
    ns = nb // SWA_BLOCKS
    s = nb * BLOCK
    rows = SWA_BLOCKS * BLOCK
    cur4 = lambda i, n: (i * ns + n, 0, 0, 0)
    prev_blk = lambda n: jnp.maximum(n * SWA_BLOCKS - 1, 0)
    kdim = ak.shape[-1]
    ak3 = ak.reshape(b, s, kdim)
    return pl.pallas_call(
        _swa_body,
        grid=(b, ns),
        in_specs=[pl.BlockSpec(memory_space=pltpu.SMEM),
                  pl.BlockSpec((SWA_BLOCKS,) + aq_t.shape[1:], cur4),
                  pl.BlockSpec((1, rows, kdim), lambda i, n: (i, n, 0)),
                  pl.BlockSpec((1, BLOCK, kdim), lambda i, n: (i, prev_blk(n), 0)),
                  _full_spec(mak.shape),
                  pl.BlockSpec((SWA_BLOCKS,) + av_t.shape[1:], cur4),
                  pl.BlockSpec((1,) + av_t.shape[1:], lambda i, n: (i * nb + prev_blk(n), 0, 0, 0)),
                  _full_spec(mav_t.shape), _full_spec(bias_a.shape)],
        out_specs=pl.BlockSpec((1, rows, A_HEADS * A_HEAD_DIM), lambda i, n: (i, n, 0)),
        out_shape=jax.ShapeDtypeStruct((b, s, A_HEADS * A_HEAD_DIM), BF16),
        compiler_params=pltpu.CompilerParams(dimension_semantics=("arbitrary", "arbitrary"),
                                             vmem_limit_bytes=VMEM_LIMIT_BYTES),
        name="swa_sink_attention",
    )(sinks, aq_t, ak3, ak3, mak, av_t, av_t, mav_t, bias_a)


def _mla_body(iqs_ref, iw_ref, qabs_ref, ikp_ref, ckvp_ref, bnear_ref, wuvt_ref, o_ref,
              key_sc, khi_sc, klo_sc, s0_sc, s1_sc, mb0_sc, mb1_sc, al0_sc, al1_sc, m_sc, acc_sc, j_sc, *, topk):
    m_blk = pl.program_id(1)
    nchunks = m_blk + 2
    groups = TQ // SUBLANES
    acc_rows = B_KV_RANK + BF16_ROWS
    iw_t = iw_ref[0].T
    wrow = [iw_t[h:h + 1, :] for h in range(IDX_HEADS)]

    sub_pos = (lax.broadcasted_iota(jnp.int32, (groups, SUBLANES, TQ), 0) * SUBLANES
               + lax.broadcasted_iota(jnp.int32, (groups, SUBLANES, TQ), 1))
    qpos = BLOCK + m_blk * TQ + lax.broadcasted_iota(jnp.int32, (groups, SUBLANES, TQ), 2)

    def rows_of(c):
        return pl.ds(pl.multiple_of(c * TQ, TQ), TQ)

    def tiles(x):
        return x.reshape(groups, SUBLANES, x.shape[-1])

    def sort_key(sc):
        bits = lax.bitcast_convert_type(sc, jnp.int32)
        bits = jnp.where(sc == 0.0, 0, bits)
        return jnp.where(bits < 0, bits ^ 0x7FFFFFFF, bits)

    def score_chunk(c, carry):
        ik_c = ikp_ref[0, rows_of(c), :]
        sc = None
        for h in range(IDX_HEADS):
            logits = jnp.dot(ik_c, iqs_ref[0, h], preferred_element_type=F32)
            term = jnp.maximum(logits, 0.0) * wrow[h]
            sc = term if sc is None else sc + term
        pos = c * TQ + sub_pos
        adm = (pos >= PAD) & (pos <= qpos)
        key = jnp.where(adm, tiles(sort_key(sc)), INT_MIN).reshape(TQ, TQ)
        key_sc[rows_of(c), :] = key
        khi_sc[rows_of(c), :] = jnp.right_shift(key, 16).astype(jnp.int16)
        klo_sc[rows_of(c), :] = (key ^ 0x8000).astype(jnp.int16)
        return carry

    lax.fori_loop(0, nchunks, score_chunk, 0)

    def count(pred):
        def body(c, part):
            hit = pred(tiles(key_sc[rows_of(c), :]), c * TQ + sub_pos)
            return part + _tree_reduce(jnp.add, jnp.where(hit, 1.0, 0.0))
        part = lax.fori_loop(0, nchunks, body, jnp.zeros((SUBLANES, TQ), F32))
        return jnp.broadcast_to(jnp.sum(part, axis=0, keepdims=True), (SUBLANES, TQ))

    kf = float(topk)

    def tiles16(x):
        return x.reshape(TQ // BF16_ROWS, BF16_ROWS, TQ)

    def count16_ge(ref, cand):
        cand16 = jnp.broadcast_to(cand[0:1], (BF16_ROWS, TQ)).astype(jnp.int16)
        one, zero = jnp.int16(1), jnp.int16(0)

        def body(c, part):
            hit = tiles16(ref[rows_of(c), :]) >= cand16[None]
            return part + _tree_reduce(jnp.add, jnp.where(hit, one, zero))
        part = lax.fori_loop(0, nchunks, body, jnp.zeros((BF16_ROWS, TQ), jnp.int16))
        total = jnp.sum(part.astype(jnp.int32), axis=0, keepdims=True)
        return jnp.broadcast_to(total, (SUBLANES, TQ))

    def digit_search(ref):
        def step(i, t):
            cand = t + jnp.left_shift(jnp.int32(1), 15 - i)
            return jnp.where(count16_ge(ref, cand) >= topk, cand, t)
        return lax.fori_loop(0, 16, step, jnp.full((SUBLANES, TQ), I16_MIN, jnp.int32))

    t_hi = digit_search(khi_sc)
    t_hi16 = jnp.broadcast_to(t_hi[0:1], (BF16_ROWS, TQ)).astype(jnp.int16)

    def pin_low(c, carry):
        hi = tiles16(khi_sc[rows_of(c), :])
        lo = tiles16(klo_sc[rows_of(c), :])
        lo = jnp.where(hi > t_hi16[None], jnp.int16(I16_MAX), jnp.where(hi < t_hi16[None], jnp.int16(I16_MIN), lo))
        klo_sc[rows_of(c), :] = lo.reshape(TQ, TQ)
        return carry

    lax.fori_loop(0, nchunks, pin_low, 0)
    t_lo = digit_search(klo_sc)
    thr = t_hi * 65536 + (t_lo - I16_MIN)
    cnt_gt = count(lambda kp, pos: kp > thr[None])
    cnt_ge = count(lambda kp, pos: kp >= thr[None])
    need = kf - cnt_gt
    ambiguous = (cnt_ge - cnt_gt != need) & (thr != INT_MIN)

    j_sc[...] = jnp.full((SUBLANES, TQ), 2 ** 30, jnp.int32)

    @pl.when(jnp.max(ambiguous.astype(F32)) > 0.0)
    def _():
        def tie(i, jt):
            cand = jt + jnp.left_shift(jnp.int32(1), 11 - i)
            cnt = count(lambda kp, pos: (kp == thr[None]) & (pos < cand[None]))
            return jnp.where(cnt < need, cand, jt)
        j_sc[...] = lax.fori_loop(0, 12, tie, jnp.zeros((SUBLANES, TQ), jnp.int32))

    jt = j_sc[...]

    m_sc[...] = jnp.full(m_sc.shape, MAX_FLOOR, F32)
    acc_sc[...] = jnp.zeros(acc_sc.shape, F32)

    def score_stage(c, buf):
        s_buf, m_buf, al_buf = buf
        ckv_c = ckvp_ref[0, rows_of(c), :]
        kp = tiles(key_sc[rows_of(c), :])
        pos = c * TQ + sub_pos
        sel = ((kp > thr[None]) | ((kp == thr[None]) & (pos <= jt[None]))) & (kp != INT_MIN)
        mask = jnp.where(sel, 0.0, -jnp.inf)
        bias_row0 = pl.multiple_of(jnp.maximum(c - m_blk + 1, 0) * TQ, TQ)
        for h in range(B_HEADS):
            sh = jnp.dot(ckv_c, qabs_ref[0, h], preferred_element_type=F32)
            sh = tiles(sh + bnear_ref[h, pl.ds(bias_row0, TQ), :]) + mask
            s_buf[h] = sh.reshape(TQ, TQ)
            m_prev = m_sc[h]
            mx = jnp.max(_tree_reduce(jnp.maximum, sh), axis=0, keepdims=True)
            m_new = jnp.maximum(m_prev, mx)
            al_buf[h] = jnp.exp2(m_prev - m_new)
            m_buf[h] = m_new
            m_sc[h] = m_new

    def value_stage(c, buf):
        s_buf, m_buf, al_buf = buf
        vt = ckvp_ref[0, rows_of(c), :].astype(F32).T.astype(BF16)
        vext = jnp.concatenate([vt, jnp.ones((BF16_ROWS, TQ), BF16)], axis=0)
        for h in range(B_HEADS):
            e = jnp.exp2(tiles(s_buf[h]) - m_buf[h][None]).reshape(TQ, TQ).astype(BF16)
            pv = jnp.dot(vext, e, preferred_element_type=F32)
            acc = acc_sc[h].reshape(acc_rows // SUBLANES, SUBLANES, TQ)
            acc_sc[h] = (acc * al_buf[h][None] + pv.reshape(acc.shape)).reshape(acc_rows, TQ)

    buf0, buf1 = (s0_sc, mb0_sc, al0_sc), (s1_sc, mb1_sc, al1_sc)
    score_stage(0, buf0)

    def chunk_pair(i, carry):
        c = 2 * i + 1
        value_stage(c - 1, buf0)
        score_stage(c, buf1)
        value_stage(c, buf1)
        score_stage(c + 1, buf0)
        return carry

    npairs = (nchunks - 1) // 2
    lax.fori_loop(0, npairs, chunk_pair, 0)
    last = nchunks - 1

    @pl.when(last == 2 * npairs)
    def _():
        value_stage(last, buf0)

    @pl.when(last != 2 * npairs)
    def _():
        value_stage(last - 1, buf0)
        score_stage(last, buf1)
        value_stage(last, buf1)

    outs = []
    for h in range(B_HEADS):
        acc = acc_sc[h]
        lat = (acc[0:B_KV_RANK] / acc[B_KV_RANK:B_KV_RANK + 1]).astype(BF16)
        outs.append(jnp.dot(wuvt_ref[h], lat, preferred_element_type=F32))
    o_ref[0] = jnp.concatenate(outs, axis=0).T.astype(BF16)


def _mla_call(iqs, iw, qabs, ikp, ckvp, bnear, wuvt, topk):
    b, sp, _ = ikp.shape
    s = iw.shape[1]
    nq = s // TQ
    acc_rows = B_KV_RANK + BF16_ROWS
    return pl.pallas_call(
        functools.partial(_mla_body, topk=topk),
        grid=(b, nq),
        in_specs=[pl.BlockSpec((1, IDX_HEADS, IDX_DIM, TQ), lambda i, m: (i * nq + m, 0, 0, 0)),
                  pl.BlockSpec((1, TQ, LANES), lambda i, m: (i, m, 0)),
                  pl.BlockSpec((1, B_HEADS, B_KV_RANK, TQ), lambda i, m: (i * nq + m, 0, 0, 0)),
                  pl.BlockSpec((1, sp, IDX_DIM), lambda i, m: (i, 0, 0)),
                  pl.BlockSpec((1, sp, B_KV_RANK), lambda i, m: (i, 0, 0)),
                  _full_spec(bnear.shape), _full_spec(wuvt.shape)],
        out_specs=pl.BlockSpec((1, TQ, B_HEADS * B_HEAD_DIM), lambda i, m: (i, m, 0)),
        out_shape=jax.ShapeDtypeStruct((b, s, B_HEADS * B_HEAD_DIM), BF16),
        scratch_shapes=[pltpu.VMEM((sp, TQ), jnp.int32),
                        pltpu.VMEM((sp, TQ), jnp.int16),
                        pltpu.VMEM((sp, TQ), jnp.int16),
                        pltpu.VMEM((B_HEADS, TQ, TQ), F32), pltpu.VMEM((B_HEADS, TQ, TQ), F32),
                        pltpu.VMEM((B_HEADS, SUBLANES, TQ), F32), pltpu.VMEM((B_HEADS, SUBLANES, TQ), F32),
                        pltpu.VMEM((B_HEADS, SUBLANES, TQ), F32), pltpu.VMEM((B_HEADS, SUBLANES, TQ), F32),
                        pltpu.VMEM((B_HEADS, SUBLANES, TQ), F32),
                        pltpu.VMEM((B_HEADS, acc_rows, TQ), F32),
                        pltpu.VMEM((SUBLANES, TQ), jnp.int32)],
        compiler_params=pltpu.CompilerParams(dimension_semantics=("arbitrary", "arbitrary"),
                                             vmem_limit_bytes=VMEM_LIMIT_BYTES),
        name="indexer_topk_mla",
    )(iqs, iw, qabs, ikp, ckvp, bnear, wuvt)


def _out_body(x_ref, oa_ref, ob_ref, ag_ref, wg_ref, bg_ref, wa_ref, wb_ref, wo_ref, fg_ref,
              wfg_ref, wfu_ref, wfd_ref, ng_ref, y_ref):
    d = x_ref.shape[1]
    x = x_ref[...]
    hn = _rms(x, ag_ref[...]).astype(BF16)

    def gated(o_ref, w_ref, lo):
        gate = jax.nn.sigmoid(jnp.dot(hn, wg_ref[:, lo:lo + d], preferred_element_type=F32) + bg_ref[:, lo:lo + d])
        return gate * jnp.dot(o_ref[...], w_ref[...], preferred_element_type=F32)

    mixed = (gated(oa_ref, wa_ref, 0) + gated(ob_ref, wb_ref, d)).astype(BF16)
    h = x + jnp.dot(mixed, wo_ref[...], preferred_element_type=F32)
    hn2 = _rms(h, fg_ref[...]).astype(BF16)
    dff = wfg_ref.shape[1]
    step = dff // FFN_CHUNKS
    for c in range(0, dff, step):
        g = jnp.dot(hn2, wfg_ref[:, c:c + step], preferred_element_type=F32)
        u = jnp.dot(hn2, wfu_ref[:, c:c + step], preferred_element_type=F32)
        act = (g * jax.nn.sigmoid(g) * u).astype(BF16)
        h = h + jnp.dot(act, wfd_ref[c:c + step, :], preferred_element_type=F32)
    y_ref[...] = _rms(h, ng_ref[...])


def _out_call(x2, oa, ob, ag, wg, bg, wa, wb, wo, fg, wfg, wfu, wfd, ng):
    n, d = x2.shape
    tm = TOK_TILE
    row = lambda i: (i, 0)

    def const_spec(a):
        return pl.BlockSpec(a.shape, lambda i: (0,) * a.ndim, pipeline_mode=pl.Buffered(1))

    consts = [ag, wg, bg, wa, wb, wo, fg, wfg, wfu, wfd, ng]
    return pl.pallas_call(
        _out_body,
        grid=(n // tm,),
        in_specs=[pl.BlockSpec((tm, d), row), pl.BlockSpec((tm, oa.shape[1]), row),
                  pl.BlockSpec((tm, ob.shape[1]), row)] + [const_spec(a) for a in consts],
        out_specs=pl.BlockSpec((tm, d), row),
        out_shape=jax.ShapeDtypeStruct((n, d), F32),
        compiler_params=pltpu.CompilerParams(dimension_semantics=("arbitrary",),
                                             vmem_limit_bytes=VMEM_LIMIT_BYTES),
        name="merge_ffn_norm",
    )(x2, oa, ob, *consts)


def kernel(x, meta_tokens, attn_norm_g, w_in, b_gates, q_norm_g, kv_norm_g, w_uq, w_uk, w_uv, idx_k_ln_g,
           idx_k_ln_b, sinks, rel_bias, w_branch_a, w_branch_b, w_out, ffn_norm_g, w_ffn_gate, w_ffn_up,
           w_ffn_down, final_norm_g):
    b, s, d = x.shape
    assert attn_norm_g.shape[0] == 1, "single-layer block"
    assert s % TQ == 0 and (b * s) % TOK_TILE == 0 and TOK_TILE % TQ == 0
    assert w_ffn_gate.shape[2] % (FFN_CHUNKS * LANES) == 0
    topk = min(TOPK_MAX, s // 4)
    far_bkts = np.unique(_t5_bucket_np(np.arange(BLOCK + 1, s + BLOCK + 1)))
    assert far_bkts.size == 1
    far_bkt = int(far_bkts[0])

    wi = w_in[0]
    widths = (A_HEADS * A_HEAD_DIM, A_KV_HEADS * A_HEAD_DIM, A_KV_HEADS * A_HEAD_DIM, B_Q_RANK, B_KV_RANK,
              IDX_HEADS * IDX_DIM, IDX_DIM, IDX_HEADS, 2 * d)
    starts = np.concatenate([[0], np.cumsum(widths)])
    w_aq, w_ak, w_av, w_bq, w_bkv, w_iq, w_ik, w_iw, w_gates = (wi[:, int(a):int(b_)]
                                                                 for a, b_ in zip(starts[:-1], starts[1:]))
    zpad = lambda k: jnp.zeros((d, k), wi.dtype)
    w1 = jnp.concatenate([w_ak, w_bkv, w_ik, zpad(LANES - IDX_DIM), w_iw, zpad(LANES - IDX_HEADS)],
                         axis=1).astype(BF16)
    wt = jnp.concatenate([w_aq, w_av, w_bq, w_iq], axis=1).T.astype(BF16)
    wg = w_gates.astype(BF16)
    row2 = lambda v: v.reshape(1, -1).astype(F32)
    col2 = lambda v: v.reshape(-1, 1).astype(F32)
    wuk = jnp.transpose(w_uk[0], (1, 0, 2)).astype(BF16)
    wuvt = jnp.transpose(w_uv[0], (1, 2, 0)).astype(BF16)
    proj_w = (row2(attn_norm_g[0]), w1, wt, col2(q_norm_g[0]), w_uq[0].T.astype(BF16), wuk,
              row2(kv_norm_g[0]), row2(idx_k_ln_g[0]), row2(idx_k_ln_b[0]))

    bias_a, bnear = _bias_call(rel_bias, far_bkt)

    x2 = x.reshape(b * s, d)
    aq_t, ak, av_t, qabs, ckv, iqs, ik, iw = _proj_call(x2, TOK_TILE, TQ, *proj_w)
    meta_blk = jnp.concatenate([jnp.zeros((PAD, d), x.dtype), meta_tokens.astype(x.dtype)], axis=0)
    _, mak, mav_t, _, mckv, _, mik, _ = _proj_call(meta_blk, BLOCK, BLOCK, *proj_w)

    def padded_keys(real, meta):
        c = real.shape[-1]
        return jnp.concatenate([jnp.broadcast_to(meta[None], (b, BLOCK, c)), real.reshape(b, s, c),
                                jnp.zeros((b, TQ - BLOCK, c), real.dtype)], axis=1)

    o_a = _swa_call(sinks[0], aq_t, ak, av_t, mak, mav_t, bias_a, b)
    o_b = _mla_call(iqs, iw.reshape(b, s, -1), qabs, padded_keys(ik, mik), padded_keys(ckv, mckv), bnear, wuvt,
                    topk)

    y = _out_call(x2, o_a.reshape(b * s, -1), o_b.reshape(b * s, -1), row2(attn_norm_g[0]), wg,
                  row2(b_gates[0]), w_branch_a[0].astype(BF16), w_branch_b[0].astype(BF16),
                  w_out[0].astype(BF16), row2(ffn_norm_g[0]), w_ffn_gate[0].astype(BF16),
                  w_ffn_up[0].astype(BF16), w_ffn_down[0].astype(BF16), row2(final_norm_g))
    return y.reshape(b, s, d)
```

```python
import functools
import math

import numpy as np
import jax
import jax.numpy as jnp
from jax import lax
from jax.experimental import pallas as pl
from jax.experimental.pallas import tpu as pltpu

N_META = 16
BLOCK = 128
PAD = BLOCK - N_META
WINDOW = 128
A_HEADS = 8
A_KV_HEADS = 2
A_HEAD_DIM = 64
B_HEADS = 8
B_HEAD_DIM = 64
B_Q_RANK = 256
B_KV_RANK = 128
IDX_HEADS = 4
IDX_DIM = 64
TOPK_MAX = 256
N_BUCKETS = 32
MAX_DISTANCE = 128
EPS = 1e-6
NEG = -1e30
MAX_FLOOR = -3.0e38
INT_MIN = -(2 ** 31)
LOG2E = math.log2(math.e)
I16_MIN = -(2 ** 15)
I16_MAX = 2 ** 15 - 1

LANES = 128
SUBLANES = 8
BF16_ROWS = 16
VMEM_LIMIT_BYTES = 56 * 1024 * 1024

TOK_TILE = 512
TQ = 256
FFN_CHUNKS = 2
SWA_BLOCKS = 4

C_AK = 0
C_BKV = C_AK + A_KV_HEADS * A_HEAD_DIM
C_IK = C_BKV + B_KV_RANK
C_IW = C_IK + LANES
C_END = C_IW + LANES
R_AQ = 0
R_AV = R_AQ + A_HEADS * A_HEAD_DIM
R_BQ = R_AV + A_KV_HEADS * A_HEAD_DIM
R_IQ = R_BQ + B_Q_RANK
R_END = R_IQ + IDX_HEADS * IDX_DIM

F32 = jnp.float32
BF16 = jnp.bfloat16
NT_DIMS = (((1,), (1,)), ((), ()))


def _t5_bucket_np(dist):
    dist = np.asarray(dist, np.int64)
    max_exact = N_BUCKETS // 2
    d = np.maximum(dist, 1).astype(np.float32)
    large = max_exact + (np.log(d / np.float32(max_exact)) / np.float32(math.log(MAX_DISTANCE / max_exact))
                         * np.float32(N_BUCKETS - max_exact)).astype(np.int32)
    large = np.minimum(large, N_BUCKETS - 1)
    return np.where(dist < max_exact, dist, large).astype(np.int32)


def _rms(x, g):
    return x * lax.rsqrt(jnp.mean(x * x, axis=-1, keepdims=True) + EPS) * g


def _tree_reduce(op, x):
    while x.shape[0] > 1:
        half = x.shape[0] // 2
        x = op(x[:half], x[half:])
    return x[0]


def _full_spec(shape):
    nd = len(shape)
    return pl.BlockSpec(shape, lambda *_: (0,) * nd)


def _bias_body(tab_ref, bkt_a_ref, bkt_near_ref, ba_ref, bnear_ref, *, far_bkt):
    def lookup(bkt, col, fill):
        acc = jnp.full(bkt.shape, fill, F32)
        for b in range(N_BUCKETS):
            acc = jnp.where(bkt == b, tab_ref[b, col], acc)
        return acc

    bkt_a = bkt_a_ref[...]
    bkt_near = bkt_near_ref[...]
    grp = A_HEADS // A_KV_HEADS
    for h in range(A_HEADS):
        ba_ref[h // grp, :, (h % grp) * BLOCK:(h % grp + 1) * BLOCK] = lookup(bkt_a, h, -jnp.inf)
    for h in range(B_HEADS):
        col = A_HEADS + h
        bnear_ref[h, 0:TQ, :] = jnp.zeros((TQ, TQ), F32)
        bnear_ref[h, TQ:3 * TQ, :] = (lookup(bkt_near, col, 0.0) - tab_ref[far_bkt, col]) * LOG2E


def _bias_call(rel_bias, far_bkt):
    k = np.arange(2 * BLOCK)[:, None]
    q = np.arange(BLOCK)[None, :]
    dist = q + BLOCK - k
    bkt_a = np.where((dist >= 0) & (dist < WINDOW), _t5_bucket_np(np.maximum(dist, 0)), -1).astype(np.int32)
    k = np.arange(2 * TQ)[:, None]
    q = np.arange(TQ)[None, :]
    bkt_near = _t5_bucket_np(np.maximum(q + BLOCK - k, 0))
    vmem = pl.BlockSpec(memory_space=pltpu.VMEM)
    grp = A_HEADS // A_KV_HEADS
    return pl.pallas_call(
        functools.partial(_bias_body, far_bkt=far_bkt),
        out_shape=(jax.ShapeDtypeStruct((A_KV_HEADS, 2 * BLOCK, grp * BLOCK), F32),
                   jax.ShapeDtypeStruct((B_HEADS, 3 * TQ, TQ), F32)),
        in_specs=[pl.BlockSpec(memory_space=pltpu.SMEM), vmem, vmem],
        out_specs=(vmem, vmem),
        name="bias_tables",
    )(rel_bias, jnp.asarray(bkt_a), jnp.asarray(bkt_near))


def _proj_body(x_ref, g_ref, w1_ref, wt_ref, qg_ref, wuq_ref, wuk_ref, kvg_ref, ikg_ref, ikb_ref,
               aq_ref, ak_ref, av_ref, qabs_ref, ckv_ref, iqs_ref, ik_ref, iw_ref, *, tq):
    tm = x_ref.shape[0]
    nblk = tm // tq
    nblk_a = tm // BLOCK
    grp = A_HEADS // A_KV_HEADS
    hn = _rms(x_ref[...], g_ref[...]).astype(BF16)

    def proj(lo, hi):
        return jnp.dot(hn, w1_ref[:, lo:hi], preferred_element_type=F32)

    ak_ref[...] = proj(C_AK, C_BKV).astype(BF16)

    feat_t = lax.dot_general(wt_ref[...], hn, NT_DIMS, preferred_element_type=F32)
    aq_t = (feat_t[R_AQ:R_AV] * (A_HEAD_DIM ** -0.5)).astype(BF16)
    for j in range(nblk_a):
        tok = slice(j * BLOCK, (j + 1) * BLOCK)
        for h in range(A_HEADS):
            aq_ref[j, h // grp, :, (h % grp) * BLOCK:(h % grp + 1) * BLOCK] = \
                aq_t[h * A_HEAD_DIM:(h + 1) * A_HEAD_DIM, tok]
    av_t = feat_t[R_AV:R_BQ].astype(BF16)
    for j in range(nblk_a):
        for kvh in range(A_KV_HEADS):
            av_ref[j, kvh] = av_t[kvh * A_HEAD_DIM:(kvh + 1) * A_HEAD_DIM, j * BLOCK:(j + 1) * BLOCK]

    bq_t = feat_t[R_BQ:R_IQ]
    qn_t = bq_t * lax.rsqrt(jnp.mean(bq_t * bq_t, axis=0, keepdims=True) + EPS) * qg_ref[...]
    q_t = jnp.dot(wuq_ref[...], qn_t.astype(BF16), preferred_element_type=F32).astype(BF16)
    for h in range(B_HEADS):
        qa_t = jnp.dot(wuk_ref[h], q_t[h * B_HEAD_DIM:(h + 1) * B_HEAD_DIM], preferred_element_type=F32)
        qa_t = (qa_t * (B_HEAD_DIM ** -0.5 * LOG2E)).astype(BF16)
        for j in range(nblk):
            qabs_ref[j, h] = qa_t[:, j * tq:(j + 1) * tq]
    iq_t = feat_t[R_IQ:R_END].astype(BF16)
    for h in range(IDX_HEADS):
        for j in range(nblk):
            iqs_ref[j, h] = iq_t[h * IDX_DIM:(h + 1) * IDX_DIM, j * tq:(j + 1) * tq]

    ckv_ref[...] = _rms(proj(C_BKV, C_IK), kvg_ref[...]).astype(BF16)

    ikw = proj(C_IK, C_END)
    ik = ikw[:, :IDX_DIM]
    mu = jnp.mean(ik, axis=-1, keepdims=True)
    xc = ik - mu
    var = jnp.mean(xc * xc, axis=-1, keepdims=True)
    ik_ref[...] = (xc * lax.rsqrt(var + EPS) * ikg_ref[...] + ikb_ref[...]).astype(BF16)
    iw_ref[...] = ikw[:, LANES:] * ((IDX_HEADS * IDX_DIM) ** -0.5)


def _proj_call(x2, tm, tq, *weights):
    n, d = x2.shape
    grid = (n // tm,)
    row = lambda i: (i, 0)
    blk4 = lambda i: (i, 0, 0, 0)
    grp = A_HEADS // A_KV_HEADS
    out_shape = (
        jax.ShapeDtypeStruct((n // BLOCK, A_KV_HEADS, A_HEAD_DIM, grp * BLOCK), BF16),
        jax.ShapeDtypeStruct((n, A_KV_HEADS * A_HEAD_DIM), BF16),
        jax.ShapeDtypeStruct((n // BLOCK, A_KV_HEADS, A_HEAD_DIM, BLOCK), BF16),
        jax.ShapeDtypeStruct((n // tq, B_HEADS, B_KV_RANK, tq), BF16),
        jax.ShapeDtypeStruct((n, B_KV_RANK), BF16),
        jax.ShapeDtypeStruct((n // tq, IDX_HEADS, IDX_DIM, tq), BF16),
        jax.ShapeDtypeStruct((n, IDX_DIM), BF16),
        jax.ShapeDtypeStruct((n, LANES), F32),
    )
    out_specs = (
        pl.BlockSpec((tm // BLOCK, A_KV_HEADS, A_HEAD_DIM, grp * BLOCK), blk4),
        pl.BlockSpec((tm, A_KV_HEADS * A_HEAD_DIM), row),
        pl.BlockSpec((tm // BLOCK, A_KV_HEADS, A_HEAD_DIM, BLOCK), blk4),
        pl.BlockSpec((tm // tq, B_HEADS, B_KV_RANK, tq), blk4),
        pl.BlockSpec((tm, B_KV_RANK), row),
        pl.BlockSpec((tm // tq, IDX_HEADS, IDX_DIM, tq), blk4),
        pl.BlockSpec((tm, IDX_DIM), row),
        pl.BlockSpec((tm, LANES), row),
    )
    in_specs = [pl.BlockSpec((tm, d), row)] + [_full_spec(w.shape) for w in weights]
    return pl.pallas_call(
        functools.partial(_proj_body, tq=tq),
        grid=grid, in_specs=in_specs, out_specs=out_specs, out_shape=out_shape,
        compiler_params=pltpu.CompilerParams(dimension_semantics=("arbitrary",),
                                             vmem_limit_bytes=VMEM_LIMIT_BYTES),
        name="in_proj",
    )(x2, *weights)


def _swa_body(sinks_ref, aq_ref, kcur_ref, kprev_ref, kmeta_ref, vcur_ref, vprev_ref, vmeta_ref, bias_ref, o_ref):
    n = pl.program_id(1)
    first = n == 0
    grp = A_HEADS // A_KV_HEADS
    width = grp * BLOCK
    kall = jnp.concatenate([jnp.where(first, kmeta_ref[...], kprev_ref[0]), kcur_ref[0]], axis=0)
    vall = [jnp.concatenate([jnp.where(first, vmeta_ref[0, kvh], vprev_ref[0, kvh])]
                            + [vcur_ref[j, kvh] for j in range(SWA_BLOCKS)], axis=1)
            for kvh in range(A_KV_HEADS)]
    ntile = 2 * BLOCK // SUBLANES
    key_row = (lax.broadcasted_iota(jnp.int32, (ntile, SUBLANES, width), 0) * SUBLANES
               + lax.broadcasted_iota(jnp.int32, (ntile, SUBLANES, width), 1))
    pad_row = first & (key_row < PAD)
    lane_head = lax.broadcasted_iota(jnp.int32, (1, width), 1) // BLOCK
    ones = jnp.ones((BF16_ROWS, 2 * BLOCK), BF16)
    sinks = []
    for kvh in range(A_KV_HEADS):
        sink = jnp.zeros((1, width), F32)
        for g in range(grp):
            sink = jnp.where(lane_head == g, sinks_ref[kvh * grp + g], sink)
        sinks.append(sink)
    probs_ids = [(j, kvh) for j in range(SWA_BLOCKS) for kvh in range(A_KV_HEADS)]
    scores = [jnp.dot(kall[j * BLOCK:(j + 2) * BLOCK, kvh * A_HEAD_DIM:(kvh + 1) * A_HEAD_DIM], aq_ref[j, kvh],
                      preferred_element_type=F32) for j, kvh in probs_ids]
    maxes, probs = [], []
    for i, (j, kvh) in enumerate(probs_ids):
        s = (scores[i] + bias_ref[kvh]).reshape(ntile, SUBLANES, width)
        if j == 0:
            s = jnp.where(pad_row, -jnp.inf, s)
        m = jnp.maximum(jnp.max(_tree_reduce(jnp.maximum, s), axis=0, keepdims=True), sinks[kvh])
        maxes.append(m)
        probs.append(jnp.exp(s - m[None]).reshape(2 * BLOCK, width).astype(BF16))
    pvs = [jnp.dot(jnp.concatenate([vall[kvh][:, j * BLOCK:(j + 2) * BLOCK], ones], axis=0), probs[i],
                   preferred_element_type=F32) for i, (j, kvh) in enumerate(probs_ids)]
    for j in range(SWA_BLOCKS):
        heads_t = []
        for kvh in range(A_KV_HEADS):
            i = j * A_KV_HEADS + kvh
            den = pvs[i][A_HEAD_DIM:A_HEAD_DIM + 1] + jnp.exp(sinks[kvh] - maxes[i])
            o_t = pvs[i][0:A_HEAD_DIM] / den
            heads_t += [o_t[:, g * BLOCK:(g + 1) * BLOCK] for g in range(grp)]
        o_ref[0, j * BLOCK:(j + 1) * BLOCK, :] = jnp.concatenate(heads_t, axis=0).T.astype(BF16)


def _swa_call(sinks, aq_t, ak, av_t, mak, mav_t, bias_a, b):
    nblocks = aq_t.shape[0]
    nb = nblocks // b
    assert nb % SWA_BLOCKS == 0
    ns = nb // SWA_BLOCKS
    s = nb * BLOCK
    rows = SWA_BLOCKS * BLOCK
    cur4 = lambda i, n: (i * ns + n, 0, 0, 0)
    prev_blk = lambda n: jnp.maximum(n * SWA_BLOCKS - 1, 0)
    kdim = ak.shape[-1]
    ak3 = ak.reshape(b, s, kdim)
    return pl.pallas_call(
        _swa_body,
        grid=(b, ns),
        in_specs=[pl.BlockSpec(memory_space=pltpu.SMEM),
                  pl.BlockSpec((SWA_BLOCKS,) + aq_t.shape[1:], cur4),
                  pl.BlockSpec((1, rows, kdim), lambda i, n: (i, n, 0)),
                  pl.BlockSpec((1, BLOCK, kdim), lambda i, n: (i, prev_blk(n), 0)),
                  _full_spec(mak.shape),
                  pl.BlockSpec((SWA_BLOCKS,) + av_t.shape[1:], cur4),
                  pl.BlockSpec((1,) + av_t.shape[1:], lambda i, n: (i * nb + prev_blk(n), 0, 0, 0)),
                  _full_spec(mav_t.shape), _full_spec(bias_a.shape)],
        out_specs=pl.BlockSpec((1, rows, A_HEADS * A_HEAD_DIM), lambda i, n: (i, n, 0)),
        out_shape=jax.ShapeDtypeStruct((b, s, A_HEADS * A_HEAD_DIM), BF16),
        compiler_params=pltpu.CompilerParams(dimension_semantics=("arbitrary", "arbitrary"),
                                             vmem_limit_bytes=VMEM_LIMIT_BYTES),
        name="swa_sink_attention",
    )(sinks, aq_t, ak3, ak3, mak, av_t, av_t, mav_t, bias_a)


def _mla_body(iqs_ref, iw_ref, qabs_ref, ikp_ref, ckvp_ref, bnear_ref, wuvt_ref, tril_ref, o_ref,
              key_sc, khi_sc, klo_sc, s0_sc, s1_sc, mb0_sc, mb1_sc, al0_sc, al1_sc, m_sc, acc_sc, tie_sc, *, topk):
    m_blk = pl.program_id(1)
    nchunks = m_blk + 2
    groups = TQ // SUBLANES
    acc_rows = B_KV_RANK + BF16_ROWS
    iw_t = iw_ref[0].T
    wrow = [iw_t[h:h + 1, :] for h in range(IDX_HEADS)]

    sub_pos = (lax.broadcasted_iota(jnp.int32, (groups, SUBLANES, TQ), 0) * SUBLANES
               + lax.broadcasted_iota(jnp.int32, (groups, SUBLANES, TQ), 1))
    qpos = BLOCK + m_blk * TQ + lax.broadcasted_iota(jnp.int32, (groups, SUBLANES, TQ), 2)

    def rows_of(c):
        return pl.ds(pl.multiple_of(c * TQ, TQ), TQ)

    def tiles(x):
        return x.reshape(groups, SUBLANES, x.shape[-1])

    def sort_key(sc):
        bits = lax.bitcast_convert_type(sc, jnp.int32)
        bits = jnp.where(sc == 0.0, 0, bits)
        return jnp.where(bits < 0, bits ^ 0x7FFFFFFF, bits)

    def score_chunk(c, carry):
        ik_c = ikp_ref[0, rows_of(c), :]
        sc = None
        for h in range(IDX_HEADS):
            logits = jnp.dot(ik_c, iqs_ref[0, h], preferred_element_type=F32)
            term = jnp.maximum(logits, 0.0) * wrow[h]
            sc = term if sc is None else sc + term
        pos = c * TQ + sub_pos
        adm = (pos >= PAD) & (pos <= qpos)
        key = jnp.where(adm, tiles(sort_key(sc)), INT_MIN).reshape(TQ, TQ)
        key_sc[rows_of(c), :] = key
        khi_sc[rows_of(c), :] = jnp.right_shift(key, 16).astype(jnp.int16)
        klo_sc[rows_of(c), :] = (key ^ 0x8000).astype(jnp.int16)
        return carry

    lax.fori_loop(0, nchunks, score_chunk, 0)

    def count(pred):
        def body(c, part):
            hit = pred(tiles(key_sc[rows_of(c), :]))
            return part + _tree_reduce(jnp.add, jnp.where(hit, 1.0, 0.0))
        part = lax.fori_loop(0, nchunks, body, jnp.zeros((SUBLANES, TQ), F32))
        return jnp.broadcast_to(jnp.sum(part, axis=0, keepdims=True), (SUBLANES, TQ))

    kf = float(topk)

    def tiles16(x):
        return x.reshape(TQ // BF16_ROWS, BF16_ROWS, TQ)

    def count16_ge(ref, cand):
        cand16 = jnp.broadcast_to(cand[0:1], (BF16_ROWS, TQ)).astype(jnp.int16)
        one, zero = jnp.int16(1), jnp.int16(0)

        def body(c, part):
            hit = tiles16(ref[rows_of(c), :]) >= cand16[None]
            return part + _tree_reduce(jnp.add, jnp.where(hit, one, zero))
        part = lax.fori_loop(0, nchunks, body, jnp.zeros((BF16_ROWS, TQ), jnp.int16))
        total = jnp.sum(part.astype(jnp.int32), axis=0, keepdims=True)
        return jnp.broadcast_to(total, (SUBLANES, TQ))

    def digit_search(ref):
        def step(i, t):
            cand = t + jnp.left_shift(jnp.int32(1), 15 - i)
            return jnp.where(count16_ge(ref, cand) >= topk, cand, t)
        return lax.fori_loop(0, 16, step, jnp.full((SUBLANES, TQ), I16_MIN, jnp.int32))

    t_hi = digit_search(khi_sc)
    t_hi16 = jnp.broadcast_to(t_hi[0:1], (BF16_ROWS, TQ)).astype(jnp.int16)

    def pin_low(c, carry):
        hi = tiles16(khi_sc[rows_of(c), :])
        lo = tiles16(klo_sc[rows_of(c), :])
        lo = jnp.where(hi > t_hi16[None], jnp.int16(I16_MAX), jnp.where(hi < t_hi16[None], jnp.int16(I16_MIN), lo))
        klo_sc[rows_of(c), :] = lo.reshape(TQ, TQ)
        return carry

    lax.fori_loop(0, nchunks, pin_low, 0)
    t_lo = digit_search(klo_sc)
    thr = t_hi * 65536 + (t_lo - I16_MIN)
    need = kf - count(lambda kp: kp > thr[None])

    m_sc[...] = jnp.full(m_sc.shape, MAX_FLOOR, F32)
    acc_sc[...] = jnp.zeros(acc_sc.shape, F32)
    tie_sc[...] = jnp.zeros(tie_sc.shape, F32)

    def score_stage(c, buf):
        s_buf, m_buf, al_buf = buf
        ckv_c = ckvp_ref[0, rows_of(c), :]
        kp = tiles(key_sc[rows_of(c), :])
        tied = kp == thr[None]
        tied_b = jnp.where(tied, 1.0, 0.0).reshape(TQ, TQ).astype(BF16)
        rank = tiles(jnp.dot(tril_ref[...], tied_b, preferred_element_type=F32)) + tie_sc[...][None]
        tie_sc[...] = jnp.broadcast_to(rank[groups - 1, SUBLANES - 1:SUBLANES, :], (SUBLANES, TQ))
        sel = ((kp > thr[None]) | (tied & (rank <= need[None]))) & (kp != INT_MIN)
        mask = jnp.where(sel, 0.0, -jnp.inf)
        bias_row0 = pl.multiple_of(jnp.maximum(c - m_blk + 1, 0) * TQ, TQ)
        for h in range(B_HEADS):
            sh = jnp.dot(ckv_c, qabs_ref[0, h], preferred_element_type=F32)
            sh = tiles(sh + bnear_ref[h, pl.ds(bias_row0, TQ), :]) + mask
            s_buf[h] = sh.reshape(TQ, TQ)
            m_prev = m_sc[h]
            mx = jnp.max(_tree_reduce(jnp.maximum, sh), axis=0, keepdims=True)
            m_new = jnp.maximum(m_prev, mx)
            al_buf[h] = jnp.exp2(m_prev - m_new)
            m_buf[h] = m_new
            m_sc[h] = m_new

    def value_stage(c, buf):
        s_buf, m_buf, al_buf = buf
        vt = ckvp_ref[0, rows_of(c), :].astype(F32).T.astype(BF16)
        vext = jnp.concatenate([vt, jnp.ones((BF16_ROWS, TQ), BF16)], axis=0)
        for h in range(B_HEADS):
            e = jnp.exp2(tiles(s_buf[h]) - m_buf[h][None]).reshape(TQ, TQ).astype(BF16)
            pv = jnp.dot(vext, e, preferred_element_type=F32)
            acc = acc_sc[h].reshape(acc_rows // SUBLANES, SUBLANES, TQ)
            acc_sc[h] = (acc * al_buf[h][None] + pv.reshape(acc.shape)).reshape(acc_rows, TQ)

    buf0, buf1 = (s0_sc, mb0_sc, al0_sc), (s1_sc, mb1_sc, al1_sc)
    score_stage(0, buf0)

    def chunk_pair(i, carry):
        c = 2 * i + 1
        value_stage(c - 1, buf0)
        score_stage(c, buf1)
        value_stage(c, buf1)
        score_stage(c + 1, buf0)
        return carry

    npairs = (nchunks - 1) // 2
    lax.fori_loop(0, npairs, chunk_pair, 0)
    last = nchunks - 1

    @pl.when(last == 2 * npairs)
    def _():
        value_stage(last, buf0)

    @pl.when(last != 2 * npairs)
    def _():
        value_stage(last - 1, buf0)
        score_stage(last, buf1)
        value_stage(last, buf1)

    outs = []
    for h in range(B_HEADS):
        acc = acc_sc[h]
        lat = (acc[0:B_KV_RANK] / acc[B_KV_RANK:B_KV_RANK + 1]).astype(BF16)
        outs.append(jnp.dot(wuvt_ref[h], lat, preferred_element_type=F32))
    o_ref[0] = jnp.concatenate(outs, axis=0).T.astype(BF16)


def _mla_call(iqs, iw, qabs, ikp, ckvp, bnear, wuvt, topk):
    b, sp, _ = ikp.shape
    s = iw.shape[1]
    nq = s // TQ
    acc_rows = B_KV_RANK + BF16_ROWS
    tril = jnp.asarray(np.tril(np.ones((TQ, TQ), np.float32)), BF16)
    return pl.pallas_call(
        functools.partial(_mla_body, topk=topk),
        grid=(b, nq),
        in_specs=[pl.BlockSpec((1, IDX_HEADS, IDX_DIM, TQ), lambda i, m: (i * nq + m, 0, 0, 0)),
                  pl.BlockSpec((1, TQ, LANES), lambda i, m: (i, m, 0)),
                  pl.BlockSpec((1, B_HEADS, B_KV_RANK, TQ), lambda i, m: (i * nq + m, 0, 0, 0)),
                  pl.BlockSpec((1, sp, IDX_DIM), lambda i, m: (i, 0, 0)),
                  pl.BlockSpec((1, sp, B_KV_RANK), lambda i, m: (i, 0, 0)),
                  _full_spec(bnear.shape), _full_spec(wuvt.shape), _full_spec(tril.shape)],
        out_specs=pl.BlockSpec((1, TQ, B_HEADS * B_HEAD_DIM), lambda i, m: (i, m, 0)),
        out_shape=jax.ShapeDtypeStruct((b, s, B_HEADS * B_HEAD_DIM), BF16),
        scratch_shapes=[pltpu.VMEM((sp, TQ), jnp.int32),
                        pltpu.VMEM((sp, TQ), jnp.int16),
                        pltpu.VMEM((sp, TQ), jnp.int16),
                        pltpu.VMEM((B_HEADS, TQ, TQ), F32), pltpu.VMEM((B_HEADS, TQ, TQ), F32),
                        pltpu.VMEM((B_HEADS, SUBLANES, TQ), F32), pltpu.VMEM((B_HEADS, SUBLANES, TQ), F32),
                        pltpu.VMEM((B_HEADS, SUBLANES, TQ), F32), pltpu.VMEM((B_HEADS, SUBLANES, TQ), F32),
                        pltpu.VMEM((B_HEADS, SUBLANES, TQ), F32),
                        pltpu.VMEM((B_HEADS, acc_rows, TQ), F32),
                        pltpu.VMEM((SUBLANES, TQ), F32)],
        compiler_params=pltpu.CompilerParams(dimension_semantics=("arbitrary", "arbitrary"),
                                             vmem_limit_bytes=VMEM_LIMIT_BYTES),
        name="indexer_topk_mla",
    )(iqs, iw, qabs, ikp, ckvp, bnear, wuvt, tril)


def _out_body(x_ref, oa_ref, ob_ref, ag_ref, wg_ref, bg_ref, wa_ref, wb_ref, wo_ref, fg_ref,
              wfg_ref, wfu_ref, wfd_ref, ng_ref, y_ref):
    d = x_ref.shape[1]
    x = x_ref[...]
    hn = _rms(x, ag_ref[...]).astype(BF16)

    def gated(o_ref, w_ref, lo):
        gate = jax.nn.sigmoid(jnp.dot(hn, wg_ref[:, lo:lo + d], preferred_element_type=F32) + bg_ref[:, lo:lo + d])
        return gate * jnp.dot(o_ref[...], w_ref[...], preferred_element_type=F32)

    mixed = (gated(oa_ref, wa_ref, 0) + gated(ob_ref, wb_ref, d)).astype(BF16)
    h = x + jnp.dot(mixed, wo_ref[...], preferred_element_type=F32)
    hn2 = _rms(h, fg_ref[...]).astype(BF16)
    dff = wfg_ref.shape[1]
    step = dff // FFN_CHUNKS
    for c in range(0, dff, step):
        g = jnp.dot(hn2, wfg_ref[:, c:c + step], preferred_element_type=F32)
        u = jnp.dot(hn2, wfu_ref[:, c:c + step], preferred_element_type=F32)
        act = (g * jax.nn.sigmoid(g) * u).astype(BF16)
        h = h + jnp.dot(act, wfd_ref[c:c + step, :], preferred_element_type=F32)
    y_ref[...] = _rms(h, ng_ref[...])


def _out_call(x2, oa, ob, ag, wg, bg, wa, wb, wo, fg, wfg, wfu, wfd, ng):
    n, d = x2.shape
    tm = TOK_TILE
    row = lambda i: (i, 0)

    def const_spec(a):
        return pl.BlockSpec(a.shape, lambda i: (0,) * a.ndim, pipeline_mode=pl.Buffered(1))

    consts = [ag, wg, bg, wa, wb, wo, fg, wfg, wfu, wfd, ng]
    return pl.pallas_call(
        _out_body,
        grid=(n // tm,),
        in_specs=[pl.BlockSpec((tm, d), row), pl.BlockSpec((tm, oa.shape[1]), row),
                  pl.BlockSpec((tm, ob.shape[1]), row)] + [const_spec(a) for a in consts],
        out_specs=pl.BlockSpec((tm, d), row),
        out_shape=jax.ShapeDtypeStruct((n, d), F32),
        compiler_params=pltpu.CompilerParams(dimension_semantics=("arbitrary",),
                                             vmem_limit_bytes=VMEM_LIMIT_BYTES),
        name="merge_ffn_norm",
    )(x2, oa, ob, *consts)


def kernel(x, meta_tokens, attn_norm_g, w_in, b_gates, q_norm_g, kv_norm_g, w_uq, w_uk, w_uv, idx_k_ln_g,
           idx_k_ln_b, sinks, rel_bias, w_branch_a, w_branch_b, w_out, ffn_norm_g, w_ffn_gate, w_ffn_up,
           w_ffn_down, final_norm_g):
    b, s, d = x.shape
    assert attn_norm_g.shape[0] == 1, "single-layer block"
    assert s % TQ == 0 and (b * s) % TOK_TILE == 0 and TOK_TILE % TQ == 0
    assert w_ffn_gate.shape[2] % (FFN_CHUNKS * LANES) == 0
    topk = min(TOPK_MAX, s // 4)
    far_bkts = np.unique(_t5_bucket_np(np.arange(BLOCK + 1, s + BLOCK + 1)))
    assert far_bkts.size == 1
    far_bkt = int(far_bkts[0])

    wi = w_in[0]
    widths = (A_HEADS * A_HEAD_DIM, A_KV_HEADS * A_HEAD_DIM, A_KV_HEADS * A_HEAD_DIM, B_Q_RANK, B_KV_RANK,
              IDX_HEADS * IDX_DIM, IDX_DIM, IDX_HEADS, 2 * d)
    starts = np.concatenate([[0], np.cumsum(widths)])
    w_aq, w_ak, w_av, w_bq, w_bkv, w_iq, w_ik, w_iw, w_gates = (wi[:, int(a):int(b_)]
                                                                 for a, b_ in zip(starts[:-1], starts[1:]))
    zpad = lambda k: jnp.zeros((d, k), wi.dtype)
    w1 = jnp.concatenate([w_ak, w_bkv, w_ik, zpad(LANES - IDX_DIM), w_iw, zpad(LANES - IDX_HEADS)],
                         axis=1).astype(BF16)
    wt = jnp.concatenate([w_aq, w_av, w_bq, w_iq], axis=1).T.astype(BF16)
    wg = w_gates.astype(BF16)
    row2 = lambda v: v.reshape(1, -1).astype(F32)
    col2 = lambda v: v.reshape(-1, 1).astype(F32)
    wuk = jnp.transpose(w_uk[0], (1, 0, 2)).astype(BF16)
    wuvt = jnp.transpose(w_uv[0], (1, 2, 0)).astype(BF16)
    proj_w = (row2(attn_norm_g[0]), w1, wt, col2(q_norm_g[0]), w_uq[0].T.astype(BF16), wuk,
              row2(kv_norm_g[0]), row2(idx_k_ln_g[0]), row2(idx_k_ln_b[0]))

    bias_a, bnear = _bias_call(rel_bias, far_bkt)

    x2 = x.reshape(b * s, d)
    aq_t, ak, av_t, qabs, ckv, iqs, ik, iw = _proj_call(x2, TOK_TILE, TQ, *proj_w)
    meta_blk = jnp.concatenate([jnp.zeros((PAD, d), x.dtype), meta_tokens.astype(x.dtype)], axis=0)
    _, mak, mav_t, _, mckv, _, mik, _ = _proj_call(meta_blk, BLOCK, BLOCK, *proj_w)

    def padded_keys(real, meta):
        c = real.shape[-1]
        return jnp.concatenate([jnp.broadcast_to(meta[None], (b, BLOCK, c)), real.reshape(b, s, c),
                                jnp.zeros((b, TQ - BLOCK, c), real.dtype)], axis=1)

    o_a = _swa_call(sinks[0], aq_t, ak, av_t, mak, mav_t, bias_a, b)
    o_b = _mla_call(iqs, iw.reshape(b, s, -1), qabs, padded_keys(ik, mik), padded_keys(ckv, mckv), bnear, wuvt,
                    topk)

    y = _out_call(x2, o_a.reshape(b * s, -1), o_b.reshape(b * s, -1), row2(attn_norm_g[0]), wg,
                  row2(b_gates[0]), w_branch_a[0].astype(BF16), w_branch_b[0].astype(BF16),
                  w_out[0].astype(BF16), row2(ffn_norm_g[0]), w_ffn_gate[0].astype(BF16),
                  w_ffn_up[0].astype(BF16), w_ffn_down[0].astype(BF16), row2(final_norm_g))
    return y.reshape(b, s, d)
```

```python
import functools
import math

import numpy as np
import jax
import jax.numpy as jnp
from jax import lax
from jax.experimental import pallas as pl
from jax.experimental.pallas import tpu as pltpu

N_META = 16
BLOCK = 128
PAD = BLOCK - N_META
WINDOW = 128
A_HEADS = 8
A_KV_HEADS = 2
A_HEAD_DIM = 64
B_HEADS = 8
B_HEAD_DIM = 64
B_Q_RANK = 256
B_KV_RANK = 128
IDX_HEADS = 4
IDX_DIM = 64
TOPK_MAX = 256
N_BUCKETS = 32
MAX_DISTANCE = 128
EPS = 1e-6
NEG = -1e30
MAX_FLOOR = -3.0e38
INT_MIN = -(2 ** 31)
LOG2E = math.log2(math.e)
I16_MIN = -(2 ** 15)
I16_MAX = 2 ** 15 - 1

LANES = 128
SUBLANES = 8
BF16_ROWS = 16
VMEM_LIMIT_BYTES = 56 * 1024 * 1024

TOK_TILE = 512
TQ = 256
FFN_CHUNKS = 2
SWA_BLOCKS = 4

C_AK = 0
C_BKV = C_AK + A_KV_HEADS * A_HEAD_DIM
C_IK = C_BKV + B_KV_RANK
C_END = C_IK + LANES
R_AQ = 0
R_AV = R_AQ + A_HEADS * A_HEAD_DIM
R_BQ = R_AV + A_KV_HEADS * A_HEAD_DIM
R_IQ = R_BQ + B_Q_RANK
R_IW = R_IQ + IDX_HEADS * IDX_DIM
R_END = R_IW + BF16_ROWS

F32 = jnp.float32
BF16 = jnp.bfloat16
NT_DIMS = (((1,), (1,)), ((), ()))


def _t5_bucket_np(dist):
    dist = np.asarray(dist, np.int64)
    max_exact = N_BUCKETS // 2
    d = np.maximum(dist, 1).astype(np.float32)
    large = max_exact + (np.log(d / np.float32(max_exact)) / np.float32(math.log(MAX_DISTANCE / max_exact))
                         * np.float32(N_BUCKETS - max_exact)).astype(np.int32)
    large = np.minimum(large, N_BUCKETS - 1)
    return np.where(dist < max_exact, dist, large).astype(np.int32)


def _rms(x, g):
    return x * lax.rsqrt(jnp.mean(x * x, axis=-1, keepdims=True) + EPS) * g


def _tree_reduce(op, x):
    while x.shape[0] > 1:
        half = x.shape[0] // 2
        x = op(x[:half], x[half:])
    return x[0]


def _full_spec(shape):
    nd = len(shape)
    return pl.BlockSpec(shape, lambda *_: (0,) * nd)


def _bias_body(tab_ref, bkt_a_ref, bkt_near_ref, ba_ref, bnear_ref, *, far_bkt):
    def lookup(bkt, col, fill):
        acc = jnp.full(bkt.shape, fill, F32)
        for b in range(N_BUCKETS):
            acc = jnp.where(bkt == b, tab_ref[b, col], acc)
        return acc

    bkt_a = bkt_a_ref[...]
    bkt_near = bkt_near_ref[...]
    grp = A_HEADS // A_KV_HEADS
    for h in range(A_HEADS):
        ba_ref[h // grp, :, (h % grp) * BLOCK:(h % grp + 1) * BLOCK] = lookup(bkt_a, h, -jnp.inf)
    for h in range(B_HEADS):
        col = A_HEADS + h
        bnear_ref[h, 0:TQ, :] = jnp.zeros((TQ, TQ), F32)
        bnear_ref[h, TQ:3 * TQ, :] = (lookup(bkt_near, col, 0.0) - tab_ref[far_bkt, col]) * LOG2E


def _bias_call(rel_bias, far_bkt):
    k = np.arange(2 * BLOCK)[:, None]
    q = np.arange(BLOCK)[None, :]
    dist = q + BLOCK - k
    bkt_a = np.where((dist >= 0) & (dist < WINDOW), _t5_bucket_np(np.maximum(dist, 0)), -1).astype(np.int32)
    k = np.arange(2 * TQ)[:, None]
    q = np.arange(TQ)[None, :]
    bkt_near = _t5_bucket_np(np.maximum(q + BLOCK - k, 0))
    vmem = pl.BlockSpec(memory_space=pltpu.VMEM)
    grp = A_HEADS // A_KV_HEADS
    return pl.pallas_call(
        functools.partial(_bias_body, far_bkt=far_bkt),
        out_shape=(jax.ShapeDtypeStruct((A_KV_HEADS, 2 * BLOCK, grp * BLOCK), F32),
                   jax.ShapeDtypeStruct((B_HEADS, 3 * TQ, TQ), F32)),
        in_specs=[pl.BlockSpec(memory_space=pltpu.SMEM), vmem, vmem],
        out_specs=(vmem, vmem),
        name="bias_tables",
    )(rel_bias, jnp.asarray(bkt_a), jnp.asarray(bkt_near))


def _proj_body(x_ref, g_ref, w1_ref, wt_ref, qg_ref, wuq_ref, wuk_ref, kvg_ref, ikg_ref, ikb_ref,
               aq_ref, ak_ref, av_ref, qabs_ref, ckv_ref, iqs_ref, ik_ref, iw_ref, *, tq):
    tm = x_ref.shape[0]
    nblk = tm // tq
    nblk_a = tm // BLOCK
    grp = A_HEADS // A_KV_HEADS
    hn = _rms(x_ref[...], g_ref[...]).astype(BF16)

    def proj(lo, hi):
        return jnp.dot(hn, w1_ref[:, lo:hi], preferred_element_type=F32)

    ak_ref[...] = proj(C_AK, C_BKV).astype(BF16)

    feat_t = lax.dot_general(wt_ref[...], hn, NT_DIMS, preferred_element_type=F32)
    aq_t = (feat_t[R_AQ:R_AV] * (A_HEAD_DIM ** -0.5)).astype(BF16)
    for j in range(nblk_a):
        tok = slice(j * BLOCK, (j + 1) * BLOCK)
        for h in range(A_HEADS):
            aq_ref[j, h // grp, :, (h % grp) * BLOCK:(h % grp + 1) * BLOCK] = \
                aq_t[h * A_HEAD_DIM:(h + 1) * A_HEAD_DIM, tok]
    av_t = feat_t[R_AV:R_BQ].astype(BF16)
    for j in range(nblk_a):
        for kvh in range(A_KV_HEADS):
            av_ref[j, kvh] = av_t[kvh * A_HEAD_DIM:(kvh + 1) * A_HEAD_DIM, j * BLOCK:(j + 1) * BLOCK]

    bq_t = feat_t[R_BQ:R_IQ]
    qn_t = bq_t * lax.rsqrt(jnp.mean(bq_t * bq_t, axis=0, keepdims=True) + EPS) * qg_ref[...]
    q_t = jnp.dot(wuq_ref[...], qn_t.astype(BF16), preferred_element_type=F32).astype(BF16)
    for h in range(B_HEADS):
        qa_t = jnp.dot(wuk_ref[h], q_t[h * B_HEAD_DIM:(h + 1) * B_HEAD_DIM], preferred_element_type=F32)
        qa_t = (qa_t * (B_HEAD_DIM ** -0.5 * LOG2E)).astype(BF16)
        for j in range(nblk):
            qabs_ref[j, h] = qa_t[:, j * tq:(j + 1) * tq]
    iq_t = feat_t[R_IQ:R_IW].astype(BF16)
    for h in range(IDX_HEADS):
        for j in range(nblk):
            iqs_ref[j, h] = iq_t[h * IDX_DIM:(h + 1) * IDX_DIM, j * tq:(j + 1) * tq]

    ckv_ref[...] = _rms(proj(C_BKV, C_IK), kvg_ref[...]).astype(BF16)

    iw_t = feat_t[R_IW:R_IW + SUBLANES] * ((IDX_HEADS * IDX_DIM) ** -0.5)
    for j in range(nblk):
        iw_ref[j] = iw_t[:, j * tq:(j + 1) * tq]

    ik = proj(C_IK, C_END)[:, :IDX_DIM]
    mu = jnp.mean(ik, axis=-1, keepdims=True)
    xc = ik - mu
    var = jnp.mean(xc * xc, axis=-1, keepdims=True)
    ik_ref[...] = (xc * lax.rsqrt(var + EPS) * ikg_ref[...] + ikb_ref[...]).astype(BF16)


def _proj_call(x2, tm, tq, *weights):
    n, d = x2.shape
    grid = (n // tm,)
    row = lambda i: (i, 0)
    blk4 = lambda i: (i, 0, 0, 0)
    grp = A_HEADS // A_KV_HEADS
    out_shape = (
        jax.ShapeDtypeStruct((n // BLOCK, A_KV_HEADS, A_HEAD_DIM, grp * BLOCK), BF16),
        jax.ShapeDtypeStruct((n, A_KV_HEADS * A_HEAD_DIM), BF16),
        jax.ShapeDtypeStruct((n // BLOCK, A_KV_HEADS, A_HEAD_DIM, BLOCK), BF16),
        jax.ShapeDtypeStruct((n // tq, B_HEADS, B_KV_RANK, tq), BF16),
        jax.ShapeDtypeStruct((n, B_KV_RANK), BF16),
        jax.ShapeDtypeStruct((n // tq, IDX_HEADS, IDX_DIM, tq), BF16),
        jax.ShapeDtypeStruct((n, IDX_DIM), BF16),
        jax.ShapeDtypeStruct((n // tq, SUBLANES, tq), F32),
    )
    out_specs = (
        pl.BlockSpec((tm // BLOCK, A_KV_HEADS, A_HEAD_DIM, grp * BLOCK), blk4),
        pl.BlockSpec((tm, A_KV_HEADS * A_HEAD_DIM), row),
        pl.BlockSpec((tm // BLOCK, A_KV_HEADS, A_HEAD_DIM, BLOCK), blk4),
        pl.BlockSpec((tm // tq, B_HEADS, B_KV_RANK, tq), blk4),
        pl.BlockSpec((tm, B_KV_RANK), row),
        pl.BlockSpec((tm // tq, IDX_HEADS, IDX_DIM, tq), blk4),
        pl.BlockSpec((tm, IDX_DIM), row),
        pl.BlockSpec((tm // tq, SUBLANES, tq), lambda i: (i, 0, 0)),
    )
    in_specs = [pl.BlockSpec((tm, d), row)] + [_full_spec(w.shape) for w in weights]
    return pl.pallas_call(
        functools.partial(_proj_body, tq=tq),
        grid=grid, in_specs=in_specs, out_specs=out_specs, out_shape=out_shape,
        compiler_params=pltpu.CompilerParams(dimension_semantics=("arbitrary",),
                                             vmem_limit_bytes=VMEM_LIMIT_BYTES),
        name="in_proj",
    )(x2, *weights)


def _swa_body(sinks_ref, aq_ref, kcur_ref, kprev_ref, kmeta_ref, vcur_ref, vprev_ref, vmeta_ref, bias_ref, o_ref):
    n = pl.program_id(1)
    first = n == 0
    grp = A_HEADS // A_KV_HEADS
    width = grp * BLOCK
    kall = jnp.concatenate([jnp.where(first, kmeta_ref[...], kprev_ref[0]), kcur_ref[0]], axis=0)
    vall = [jnp.concatenate([jnp.where(first, vmeta_ref[0, kvh], vprev_ref[0, kvh])]
                            + [vcur_ref[j, kvh] for j in range(SWA_BLOCKS)], axis=1)
            for kvh in range(A_KV_HEADS)]
    ntile = 2 * BLOCK // SUBLANES
    key_row = (lax.broadcasted_iota(jnp.int32, (ntile, SUBLANES, width), 0) * SUBLANES
               + lax.broadcasted_iota(jnp.int32, (ntile, SUBLANES, width), 1))
    pad_row = first & (key_row < PAD)
    lane_head = lax.broadcasted_iota(jnp.int32, (1, width), 1) // BLOCK
    ones = jnp.ones((BF16_ROWS, 2 * BLOCK), BF16)
    sinks = []
    for kvh in range(A_KV_HEADS):
        sink = jnp.zeros((1, width), F32)
        for g in range(grp):
            sink = jnp.where(lane_head == g, sinks_ref[kvh * grp + g], sink)
        sinks.append(sink)
    probs_ids = [(j, kvh) for j in range(SWA_BLOCKS) for kvh in range(A_KV_HEADS)]
    scores = [jnp.dot(kall[j * BLOCK:(j + 2) * BLOCK, kvh * A_HEAD_DIM:(kvh + 1) * A_HEAD_DIM], aq_ref[j, kvh],
                      preferred_element_type=F32) for j, kvh in probs_ids]
    maxes, probs = [], []
    for i, (j, kvh) in enumerate(probs_ids):
        s = (scores[i] + bias_ref[kvh]).reshape(ntile, SUBLANES, width)
        if j == 0:
            s = jnp.where(pad_row, -jnp.inf, s)
        m = jnp.maximum(jnp.max(_tree_reduce(jnp.maximum, s), axis=0, keepdims=True), sinks[kvh])
        maxes.append(m)
        probs.append(jnp.exp(s - m[None]).reshape(2 * BLOCK, width).astype(BF16))
    pvs = [jnp.dot(jnp.concatenate([vall[kvh][:, j * BLOCK:(j + 2) * BLOCK], ones], axis=0), probs[i],
                   preferred_element_type=F32) for i, (j, kvh) in enumerate(probs_ids)]
    for j in range(SWA_BLOCKS):
        heads_t = []
        for kvh in range(A_KV_HEADS):
            i = j * A_KV_HEADS + kvh
            den = pvs[i][A_HEAD_DIM:A_HEAD_DIM + 1] + jnp.exp(sinks[kvh] - maxes[i])
            o_t = pvs[i][0:A_HEAD_DIM] / den
            heads_t += [o_t[:, g * BLOCK:(g + 1) * BLOCK] for g in range(grp)]
        o_ref[0, j * BLOCK:(j + 1) * BLOCK, :] = jnp.concatenate(heads_t, axis=0).T.astype(BF16)


def _swa_call(sinks, aq_t, ak, av_t, mak, mav_t, bias_a, b):
    nblocks = aq_t.shape[0]
    nb = nblocks // b
    assert nb % SWA_BLOCKS == 0
    ns = nb // SWA_BLOCKS
    s = nb * BLOCK
    rows = SWA_BLOCKS * BLOCK
    cur4 = lambda i, n: (i * ns + n, 0, 0, 0)
    prev_blk = lambda n: jnp.maximum(n * SWA_BLOCKS - 1, 0)
    kdim = ak.shape[-1]
    ak3 = ak.reshape(b, s, kdim)
    return pl.pallas_call(
        _swa_body,
        grid=(b, ns),
        in_specs=[pl.BlockSpec(memory_space=pltpu.SMEM),
                  pl.BlockSpec((SWA_BLOCKS,) + aq_t.shape[1:], cur4),
                  pl.BlockSpec((1, rows, kdim), lambda i, n: (i, n, 0)),
                  pl.BlockSpec((1, BLOCK, kdim), lambda i, n: (i, prev_blk(n), 0)),
                  _full_spec(mak.shape),
                  pl.BlockSpec((SWA_BLOCKS,) + av_t.shape[1:], cur4),
                  pl.BlockSpec((1,) + av_t.shape[1:], lambda i, n: (i * nb + prev_blk(n), 0, 0, 0)),
                  _full_spec(mav_t.shape), _full_spec(bias_a.shape)],
        out_specs=pl.BlockSpec((1, rows, A_HEADS * A_HEAD_DIM), lambda i, n: (i, n, 0)),
        out_shape=jax.ShapeDtypeStruct((b, s, A_HEADS * A_HEAD_DIM), BF16),
        compiler_params=pltpu.CompilerParams(dimension_semantics=("arbitrary", "arbitrary"),
                                             vmem_limit_bytes=VMEM_LIMIT_BYTES),
        name="swa_sink_attention",
    )(sinks, aq_t, ak3, ak3, mak, av_t, av_t, mav_t, bias_a)


def _mla_body(iqs_ref, iw_ref, qabs_ref, ikp_ref, ckvp_ref, bnear_ref, wuvt_ref, tril_ref, o_ref,
              key_sc, khi_sc, klo_sc, s0_sc, s1_sc, mb0_sc, mb1_sc, al0_sc, al1_sc, m_sc, acc_sc, tie_sc, *, topk):
    m_blk = pl.program_id(1)
    nchunks = m_blk + 2
    groups = TQ // SUBLANES
    acc_rows = B_KV_RANK + BF16_ROWS
    wrow = [iw_ref[0, h:h + 1, :] for h in range(IDX_HEADS)]

    sub_pos = (lax.broadcasted_iota(jnp.int32, (groups, SUBLANES, TQ), 0) * SUBLANES
               + lax.broadcasted_iota(jnp.int32, (groups, SUBLANES, TQ), 1))
    qpos = BLOCK + m_blk * TQ + lax.broadcasted_iota(jnp.int32, (groups, SUBLANES, TQ), 2)

    def rows_of(c):
        return pl.ds(pl.multiple_of(c * TQ, TQ), TQ)

    def tiles(x):
        return x.reshape(groups, SUBLANES, x.shape[-1])

    def sort_key(sc):
        bits = lax.bitcast_convert_type(sc, jnp.int32)
        return jnp.where(bits < 0, INT_MIN - bits, bits)

    def score_chunk(c, carry):
        ik_c = ikp_ref[0, rows_of(c), :]
        sc = None
        for h in range(IDX_HEADS):
            logits = jnp.dot(ik_c, iqs_ref[0, h], preferred_element_type=F32)
            term = jnp.maximum(logits, 0.0) * wrow[h]
            sc = term if sc is None else sc + term
        pos = c * TQ + sub_pos
        adm = (pos >= PAD) & (pos <= qpos)
        key = jnp.where(adm, tiles(sort_key(sc)), INT_MIN).reshape(TQ, TQ)
        key_sc[rows_of(c), :] = key
        khi_sc[rows_of(c), :] = jnp.right_shift(key, 16).astype(jnp.int16)
        klo_sc[rows_of(c), :] = (key ^ 0x8000).astype(jnp.int16)
        return carry

    lax.fori_loop(0, nchunks, score_chunk, 0)

    def count(pred):
        def body(c, part):
            hit = pred(tiles(key_sc[rows_of(c), :]))
            return part + _tree_reduce(jnp.add, jnp.where(hit, 1.0, 0.0))
        part = lax.fori_loop(0, nchunks, body, jnp.zeros((SUBLANES, TQ), F32))
        return jnp.broadcast_to(jnp.sum(part, axis=0, keepdims=True), (SUBLANES, TQ))

    kf = float(topk)

    def tiles16(x):
        return x.reshape(TQ // BF16_ROWS, BF16_ROWS, TQ)

    def count16_ge(ref, cand):
        cand16 = jnp.broadcast_to(cand[0:1], (BF16_ROWS, TQ)).astype(jnp.int16)
        one, zero = jnp.int16(1), jnp.int16(0)

        def hits(c):
            hit = tiles16(ref[rows_of(c), :]) >= cand16[None]
            return _tree_reduce(jnp.add, jnp.where(hit, one, zero))

        def two_chunks(i, part):
            return part + hits(2 * i) + hits(2 * i + 1)
        part = lax.fori_loop(0, lax.shift_right_logical(nchunks, 1), two_chunks,
                             jnp.zeros((BF16_ROWS, TQ), jnp.int16))
        part = lax.cond((nchunks & 1) == 1, lambda p: p + hits(nchunks - 1), lambda p: p, part)
        total = jnp.sum(part.astype(jnp.int32), axis=0, keepdims=True)
        return jnp.broadcast_to(total, (SUBLANES, TQ))

    def digit_search(ref):
        def step(i, t):
            cand = t + jnp.left_shift(jnp.int32(1), 15 - i)
            return jnp.where(count16_ge(ref, cand) >= topk, cand, t)
        return lax.fori_loop(0, 16, step, jnp.full((SUBLANES, TQ), I16_MIN, jnp.int32))

    t_hi = digit_search(khi_sc)
    t_hi16 = jnp.broadcast_to(t_hi[0:1], (BF16_ROWS, TQ)).astype(jnp.int16)

    def pin_low(c, carry):
        hi = tiles16(khi_sc[rows_of(c), :])
        lo = tiles16(klo_sc[rows_of(c), :])
        lo = jnp.where(hi > t_hi16[None], jnp.int16(I16_MAX), jnp.where(hi < t_hi16[None], jnp.int16(I16_MIN), lo))
        klo_sc[rows_of(c), :] = lo.reshape(TQ, TQ)
        return carry

    lax.fori_loop(0, nchunks, pin_low, 0)
    t_lo = digit_search(klo_sc)
    thr = t_hi * 65536 + (t_lo - I16_MIN)
    need = kf - count(lambda kp: kp > thr[None])

    m_sc[...] = jnp.full(m_sc.shape, MAX_FLOOR, F32)
    acc_sc[...] = jnp.zeros(acc_sc.shape, F32)
    tie_sc[...] = jnp.zeros(tie_sc.shape, F32)

    def score_stage(c, buf):
        s_buf, m_buf, al_buf = buf
        ckv_c = ckvp_ref[0, rows_of(c), :]
        kp = tiles(key_sc[rows_of(c), :])
        tied = kp == thr[None]
        tied_b = jnp.where(tied, 1.0, 0.0).reshape(TQ, TQ).astype(BF16)
        rank = tiles(jnp.dot(tril_ref[...], tied_b, preferred_element_type=F32)) + tie_sc[...][None]
        tie_sc[...] = jnp.broadcast_to(rank[groups - 1, SUBLANES - 1:SUBLANES, :], (SUBLANES, TQ))
        sel = ((kp > thr[None]) | (tied & (rank <= need[None]))) & (kp != INT_MIN)
        mask = jnp.where(sel, 0.0, -jnp.inf)
        bias_row0 = pl.multiple_of(jnp.maximum(c - m_blk + 1, 0) * TQ, TQ)
        for h in range(B_HEADS):
            sh = jnp.dot(ckv_c, qabs_ref[0, h], preferred_element_type=F32)
            sh = tiles(sh + bnear_ref[h, pl.ds(bias_row0, TQ), :]) + mask
            s_buf[h] = sh.reshape(TQ, TQ)
            m_prev = m_sc[h]
            mx = jnp.max(_tree_reduce(jnp.maximum, sh), axis=0, keepdims=True)
            m_new = jnp.maximum(m_prev, mx)
            al_buf[h] = jnp.exp2(m_prev - m_new)
            m_buf[h] = m_new
            m_sc[h] = m_new

    def value_stage(c, buf):
        s_buf, m_buf, al_buf = buf
        vt = ckvp_ref[0, rows_of(c), :].astype(F32).T.astype(BF16)
        vext = jnp.concatenate([vt, jnp.ones((BF16_ROWS, TQ), BF16)], axis=0)
        for h in range(B_HEADS):
            e = jnp.exp2(tiles(s_buf[h]) - m_buf[h][None]).reshape(TQ, TQ).astype(BF16)
            pv = jnp.dot(vext, e, preferred_element_type=F32)
            acc = acc_sc[h].reshape(acc_rows // SUBLANES, SUBLANES, TQ)
            acc_sc[h] = (acc * al_buf[h][None] + pv.reshape(acc.shape)).reshape(acc_rows, TQ)

    buf0, buf1 = (s0_sc, mb0_sc, al0_sc), (s1_sc, mb1_sc, al1_sc)
    score_stage(0, buf0)

    def chunk_pair(i, carry):
        c = 2 * i + 1
        value_stage(c - 1, buf0)
        score_stage(c, buf1)
        value_stage(c, buf1)
        score_stage(c + 1, buf0)
        return carry

    npairs = (nchunks - 1) // 2
    lax.fori_loop(0, npairs, chunk_pair, 0)
    last = nchunks - 1

    @pl.when(last == 2 * npairs)
    def _():
        value_stage(last, buf0)

    @pl.when(last != 2 * npairs)
    def _():
        value_stage(last - 1, buf0)
        score_stage(last, buf1)
        value_stage(last, buf1)

    outs = []
    for h in range(B_HEADS):
        acc = acc_sc[h]
        lat = (acc[0:B_KV_RANK] / acc[B_KV_RANK:B_KV_RANK + 1]).astype(BF16)
        outs.append(jnp.dot(wuvt_ref[h], lat, preferred_element_type=F32))
    o_ref[0] = jnp.concatenate(outs, axis=0).T.astype(BF16)


def _mla_call(iqs, iw, qabs, ikp, ckvp, bnear, wuvt, topk):
    b, sp, _ = ikp.shape
    nq = iw.shape[0] // b
    s = nq * TQ
    acc_rows = B_KV_RANK + BF16_ROWS
    tril = jnp.asarray(np.tril(np.ones((TQ, TQ), np.float32)), BF16)
    return pl.pallas_call(
        functools.partial(_mla_body, topk=topk),
        grid=(b, nq),
        in_specs=[pl.BlockSpec((1, IDX_HEADS, IDX_DIM, TQ), lambda i, m: (i * nq + m, 0, 0, 0)),
                  pl.BlockSpec((1, SUBLANES, TQ), lambda i, m: (i * nq + m, 0, 0)),
                  pl.BlockSpec((1, B_HEADS, B_KV_RANK, TQ), lambda i, m: (i * nq + m, 0, 0, 0)),
                  pl.BlockSpec((1, sp, IDX_DIM), lambda i, m: (i, 0, 0)),
                  pl.BlockSpec((1, sp, B_KV_RANK), lambda i, m: (i, 0, 0)),
                  _full_spec(bnear.shape), _full_spec(wuvt.shape), _full_spec(tril.shape)],
        out_specs=pl.BlockSpec((1, TQ, B_HEADS * B_HEAD_DIM), lambda i, m: (i, m, 0)),
        out_shape=jax.ShapeDtypeStruct((b, s, B_HEADS * B_HEAD_DIM), BF16),
        scratch_shapes=[pltpu.VMEM((sp, TQ), jnp.int32),
                        pltpu.VMEM((sp, TQ), jnp.int16),
                        pltpu.VMEM((sp, TQ), jnp.int16),
                        pltpu.VMEM((B_HEADS, TQ, TQ), F32), pltpu.VMEM((B_HEADS, TQ, TQ), F32),
                        pltpu.VMEM((B_HEADS, SUBLANES, TQ), F32), pltpu.VMEM((B_HEADS, SUBLANES, TQ), F32),
                        pltpu.VMEM((B_HEADS, SUBLANES, TQ), F32), pltpu.VMEM((B_HEADS, SUBLANES, TQ), F32),
                        pltpu.VMEM((B_HEADS, SUBLANES, TQ), F32),
                        pltpu.VMEM((B_HEADS, acc_rows, TQ), F32),
                        pltpu.VMEM((SUBLANES, TQ), F32)],
        compiler_params=pltpu.CompilerParams(dimension_semantics=("arbitrary", "arbitrary"),
                                             vmem_limit_bytes=VMEM_LIMIT_BYTES),
        name="indexer_topk_mla",
    )(iqs, iw, qabs, ikp, ckvp, bnear, wuvt, tril)


def _out_body(x_ref, oa_ref, ob_ref, ag_ref, wg_ref, bg_ref, wa_ref, wb_ref, wo_ref, fg_ref,
              wfg_ref, wfu_ref, wfd_ref, ng_ref, y_ref):
    d = x_ref.shape[1]
    x = x_ref[...]
    hn = _rms(x, ag_ref[...]).astype(BF16)

    def gated(o_ref, w_ref, lo):
        gate = jax.nn.sigmoid(jnp.dot(hn, wg_ref[:, lo:lo + d], preferred_element_type=F32) + bg_ref[:, lo:lo + d])
        return gate * jnp.dot(o_ref[...], w_ref[...], preferred_element_type=F32)

    mixed = (gated(oa_ref, wa_ref, 0) + gated(ob_ref, wb_ref, d)).astype(BF16)
    h = x + jnp.dot(mixed, wo_ref[...], preferred_element_type=F32)
    hn2 = _rms(h, fg_ref[...]).astype(BF16)
    dff = wfg_ref.shape[1]
    step = dff // FFN_CHUNKS
    for c in range(0, dff, step):
        g = jnp.dot(hn2, wfg_ref[:, c:c + step], preferred_element_type=F32)
        u = jnp.dot(hn2, wfu_ref[:, c:c + step], preferred_element_type=F32)
        act = (g * jax.nn.sigmoid(g) * u).astype(BF16)
        h = h + jnp.dot(act, wfd_ref[c:c + step, :], preferred_element_type=F32)
    y_ref[...] = _rms(h, ng_ref[...])


def _out_call(x2, oa, ob, ag, wg, bg, wa, wb, wo, fg, wfg, wfu, wfd, ng):
    n, d = x2.shape
    tm = TOK_TILE
    row = lambda i: (i, 0)

    def const_spec(a):
        return pl.BlockSpec(a.shape, lambda i: (0,) * a.ndim, pipeline_mode=pl.Buffered(1))

    consts = [ag, wg, bg, wa, wb, wo, fg, wfg, wfu, wfd, ng]
    return pl.pallas_call(
        _out_body,
        grid=(n // tm,),
        in_specs=[pl.BlockSpec((tm, d), row), pl.BlockSpec((tm, oa.shape[1]), row),
                  pl.BlockSpec((tm, ob.shape[1]), row)] + [const_spec(a) for a in consts],
        out_specs=pl.BlockSpec((tm, d), row),
        out_shape=jax.ShapeDtypeStruct((n, d), F32),
        compiler_params=pltpu.CompilerParams(dimension_semantics=("arbitrary",),
                                             vmem_limit_bytes=VMEM_LIMIT_BYTES),
        name="merge_ffn_norm",
    )(x2, oa, ob, *consts)


def kernel(x, meta_tokens, attn_norm_g, w_in, b_gates, q_norm_g, kv_norm_g, w_uq, w_uk, w_uv, idx_k_ln_g,
           idx_k_ln_b, sinks, rel_bias, w_branch_a, w_branch_b, w_out, ffn_norm_g, w_ffn_gate, w_ffn_up,
           w_ffn_down, final_norm_g):
    b, s, d = x.shape
    assert attn_norm_g.shape[0] == 1, "single-layer block"
    assert s % TQ == 0 and (b * s) % TOK_TILE == 0 and TOK_TILE % TQ == 0
    assert w_ffn_gate.shape[2] % (FFN_CHUNKS * LANES) == 0
    topk = min(TOPK_MAX, s // 4)
    far_bkts = np.unique(_t5_bucket_np(np.arange(BLOCK + 1, s + BLOCK + 1)))
    assert far_bkts.size == 1
    far_bkt = int(far_bkts[0])

    wi = w_in[0]
    widths = (A_HEADS * A_HEAD_DIM, A_KV_HEADS * A_HEAD_DIM, A_KV_HEADS * A_HEAD_DIM, B_Q_RANK, B_KV_RANK,
              IDX_HEADS * IDX_DIM, IDX_DIM, IDX_HEADS, 2 * d)
    starts = np.concatenate([[0], np.cumsum(widths)])
    w_aq, w_ak, w_av, w_bq, w_bkv, w_iq, w_ik, w_iw, w_gates = (wi[:, int(a):int(b_)]
                                                                 for a, b_ in zip(starts[:-1], starts[1:]))
    zpad = lambda k: jnp.zeros((d, k), wi.dtype)
    w1 = jnp.concatenate([w_ak, w_bkv, w_ik, zpad(LANES - IDX_DIM)], axis=1).astype(BF16)
    wt = jnp.concatenate([w_aq, w_av, w_bq, w_iq, w_iw, zpad(BF16_ROWS - IDX_HEADS)],
                         axis=1).T.astype(BF16)
    wg = w_gates.astype(BF16)
    row2 = lambda v: v.reshape(1, -1).astype(F32)
    col2 = lambda v: v.reshape(-1, 1).astype(F32)
    wuk = jnp.transpose(w_uk[0], (1, 0, 2)).astype(BF16)
    wuvt = jnp.transpose(w_uv[0], (1, 2, 0)).astype(BF16)
    proj_w = (row2(attn_norm_g[0]), w1, wt, col2(q_norm_g[0]), w_uq[0].T.astype(BF16), wuk,
              row2(kv_norm_g[0]), row2(idx_k_ln_g[0]), row2(idx_k_ln_b[0]))

    bias_a, bnear = _bias_call(rel_bias, far_bkt)

    x2 = x.reshape(b * s, d)
    aq_t, ak, av_t, qabs, ckv, iqs, ik, iw = _proj_call(x2, TOK_TILE, TQ, *proj_w)
    meta_blk = jnp.concatenate([jnp.zeros((PAD, d), x.dtype), meta_tokens.astype(x.dtype)], axis=0)
    _, mak, mav_t, _, mckv, _, mik, _ = _proj_call(meta_blk, BLOCK, BLOCK, *proj_w)

    def padded_keys(real, meta):
        c = real.shape[-1]
        return jnp.concatenate([jnp.broadcast_to(meta[None], (b, BLOCK, c)), real.reshape(b, s, c),
                                jnp.zeros((b, TQ - BLOCK, c), real.dtype)], axis=1)

    o_a = _swa_call(sinks[0], aq_t, ak, av_t, mak, mav_t, bias_a, b)
    o_b = _mla_call(iqs, iw, qabs, padded_keys(ik, mik), padded_keys(ckv, mckv), bnear, wuvt,
                    topk)

    y = _out_call(x2, o_a.reshape(b * s, -1), o_b.reshape(b * s, -1), row2(attn_norm_g[0]), wg,
                  row2(b_gates[0]), w_branch_a[0].astype(BF16), w_branch_b[0].astype(BF16),
                  w_out[0].astype(BF16), row2(ffn_norm_g[0]), w_ffn_gate[0].astype(BF16),
                  w_ffn_up[0].astype(BF16), w_ffn_down[0].astype(BF16), row2(final_norm_g))
    return y.reshape(b, s, d)
```

```python
import functools
import math

import numpy as np
import jax
import jax.numpy as jnp
from jax import lax
from jax.experimental import pallas as pl
from jax.experimental.pallas import tpu as pltpu

N_META = 16
BLOCK = 128
PAD = BLOCK - N_META
WINDOW = 128
A_HEADS = 8
A_KV_HEADS = 2
A_HEAD_DIM = 64
B_HEADS = 8
B_HEAD_DIM = 64
B_Q_RANK = 256
B_KV_RANK = 128
IDX_HEADS = 4
IDX_DIM = 64
TOPK_MAX = 256
N_BUCKETS = 32
MAX_DISTANCE = 128
EPS = 1e-6
NEG = -1e30
MAX_FLOOR = -3.0e38
INT_MIN = -(2 ** 31)
LOG2E = math.log2(math.e)
I16_MIN = -(2 ** 15)
I16_MAX = 2 ** 15 - 1

LANES = 128
SUBLANES = 8
BF16_ROWS = 16
MXU_DIM = 256
VMEM_LIMIT_BYTES = 56 * 1024 * 1024

TOK_TILE = 512
TQ = 256
FFN_CHUNKS = 2
SWA_BLOCKS = 4

C_AK = 0
C_BKV = C_AK + A_KV_HEADS * A_HEAD_DIM
C_IK = C_BKV + B_KV_RANK
C_END = C_IK + LANES
R_AQ = 0
R_AV = R_AQ + A_HEADS * A_HEAD_DIM
R_BQ = R_AV + A_KV_HEADS * A_HEAD_DIM
R_IQ = R_BQ + B_Q_RANK
R_IW = R_IQ + IDX_HEADS * IDX_DIM
R_END = R_IW + BF16_ROWS

F32 = jnp.float32
BF16 = jnp.bfloat16
NT_DIMS = (((1,), (1,)), ((), ()))


def _t5_bucket_np(dist):
    dist = np.asarray(dist, np.int64)
    max_exact = N_BUCKETS // 2
    d = np.maximum(dist, 1).astype(np.float32)
    large = max_exact + (np.log(d / np.float32(max_exact)) / np.float32(math.log(MAX_DISTANCE / max_exact))
                         * np.float32(N_BUCKETS - max_exact)).astype(np.int32)
    large = np.minimum(large, N_BUCKETS - 1)
    return np.where(dist < max_exact, dist, large).astype(np.int32)


def _rms(x, g):
    return x * lax.rsqrt(jnp.mean(x * x, axis=-1, keepdims=True) + EPS) * g


def _tree_reduce(op, x):
    while x.shape[0] > 1:
        half = x.shape[0] // 2
        x = op(x[:half], x[half:])
    return x[0]


def _full_spec(shape):
    nd = len(shape)
    return pl.BlockSpec(shape, lambda *_: (0,) * nd)


def _bias_body(tab_ref, bkt_a_ref, bkt_near_ref, ba_ref, bnear_ref, *, far_bkt):
    def lookup(bkt, col, fill):
        acc = jnp.full(bkt.shape, fill, F32)
        for b in range(N_BUCKETS):
            acc = jnp.where(bkt == b, tab_ref[b, col], acc)
        return acc

    bkt_a = bkt_a_ref[...]
    bkt_near = bkt_near_ref[...]
    grp = A_HEADS // A_KV_HEADS
    for h in range(A_HEADS):
        ba_ref[h // grp, :, (h % grp) * BLOCK:(h % grp + 1) * BLOCK] = lookup(bkt_a, h, -jnp.inf)
    for h in range(B_HEADS):
        col = A_HEADS + h
        bnear_ref[h, 0:TQ, :] = jnp.zeros((TQ, TQ), F32)
        bnear_ref[h, TQ:3 * TQ, :] = (lookup(bkt_near, col, 0.0) - tab_ref[far_bkt, col]) * LOG2E


def _bias_call(rel_bias, far_bkt):
    k = np.arange(2 * BLOCK)[:, None]
    q = np.arange(BLOCK)[None, :]
    dist = q + BLOCK - k
    bkt_a = np.where((dist >= 0) & (dist < WINDOW), _t5_bucket_np(np.maximum(dist, 0)), -1).astype(np.int32)
    k = np.arange(2 * TQ)[:, None]
    q = np.arange(TQ)[None, :]
    bkt_near = _t5_bucket_np(np.maximum(q + BLOCK - k, 0))
    vmem = pl.BlockSpec(memory_space=pltpu.VMEM)
    grp = A_HEADS // A_KV_HEADS
    return pl.pallas_call(
        functools.partial(_bias_body, far_bkt=far_bkt),
        out_shape=(jax.ShapeDtypeStruct((A_KV_HEADS, 2 * BLOCK, grp * BLOCK), F32),
                   jax.ShapeDtypeStruct((B_HEADS, 3 * TQ, TQ), F32)),
        in_specs=[pl.BlockSpec(memory_space=pltpu.SMEM), vmem, vmem],
        out_specs=(vmem, vmem),
        name="bias_tables",
    )(rel_bias, jnp.asarray(bkt_a), jnp.asarray(bkt_near))


def _proj_body(x_ref, g_ref, w1_ref, wt_ref, qg_ref, wuq_ref, wuk_ref, kvg_ref, ikg_ref, ikb_ref,
               aq_ref, ak_ref, av_ref, qabs_ref, ckv_ref, iqs_ref, ik_ref, iw_ref, *, tq):
    tm = x_ref.shape[0]
    nblk = tm // tq
    nblk_a = tm // BLOCK
    grp = A_HEADS // A_KV_HEADS
    hn = _rms(x_ref[...], g_ref[...]).astype(BF16)

    def proj(lo, hi):
        return jnp.dot(hn, w1_ref[:, lo:hi], preferred_element_type=F32)

    ak_ref[...] = proj(C_AK, C_BKV).astype(BF16)

    feat_t = lax.dot_general(wt_ref[...], hn, NT_DIMS, preferred_element_type=F32)
    aq_t = (feat_t[R_AQ:R_AV] * (A_HEAD_DIM ** -0.5)).astype(BF16)
    for j in range(nblk_a):
        tok = slice(j * BLOCK, (j + 1) * BLOCK)
        for h in range(A_HEADS):
            aq_ref[j, h // grp, :, (h % grp) * BLOCK:(h % grp + 1) * BLOCK] = \
                aq_t[h * A_HEAD_DIM:(h + 1) * A_HEAD_DIM, tok]
    av_t = feat_t[R_AV:R_BQ].astype(BF16)
    for j in range(nblk_a):
        for kvh in range(A_KV_HEADS):
            av_ref[j, kvh] = av_t[kvh * A_HEAD_DIM:(kvh + 1) * A_HEAD_DIM, j * BLOCK:(j + 1) * BLOCK]

    bq_t = feat_t[R_BQ:R_IQ]
    qn_t = bq_t * lax.rsqrt(jnp.mean(bq_t * bq_t, axis=0, keepdims=True) + EPS) * qg_ref[...]
    q_t = jnp.dot(wuq_ref[...], qn_t.astype(BF16), preferred_element_type=F32).astype(BF16)
    for h in range(B_HEADS):
        qa_t = jnp.dot(wuk_ref[h], q_t[h * B_HEAD_DIM:(h + 1) * B_HEAD_DIM], preferred_element_type=F32)
        qa_t = (qa_t * (B_HEAD_DIM ** -0.5 * LOG2E)).astype(BF16)
        for j in range(nblk):
            qabs_ref[j, h] = qa_t[:, j * tq:(j + 1) * tq]
    iq_t = feat_t[R_IQ:R_IW].astype(BF16)
    for h in range(IDX_HEADS):
        for j in range(nblk):
            iqs_ref[j, h] = iq_t[h * IDX_DIM:(h + 1) * IDX_DIM, j * tq:(j + 1) * tq]

    ckv_ref[...] = _rms(proj(C_BKV, C_IK), kvg_ref[...]).astype(BF16)

    iw_t = feat_t[R_IW:R_IW + SUBLANES] * ((IDX_HEADS * IDX_DIM) ** -0.5)
    for j in range(nblk):
        iw_ref[j] = iw_t[:, j * tq:(j + 1) * tq]

    ik = proj(C_IK, C_END)[:, :IDX_DIM]
    mu = jnp.mean(ik, axis=-1, keepdims=True)
    xc = ik - mu
    var = jnp.mean(xc * xc, axis=-1, keepdims=True)
    ik_ref[...] = (xc * lax.rsqrt(var + EPS) * ikg_ref[...] + ikb_ref[...]).astype(BF16)


def _proj_call(x2, tm, tq, *weights):
    n, d = x2.shape
    grid = (n // tm,)
    row = lambda i: (i, 0)
    blk4 = lambda i: (i, 0, 0, 0)
    grp = A_HEADS // A_KV_HEADS
    out_shape = (
        jax.ShapeDtypeStruct((n // BLOCK, A_KV_HEADS, A_HEAD_DIM, grp * BLOCK), BF16),
        jax.ShapeDtypeStruct((n, A_KV_HEADS * A_HEAD_DIM), BF16),
        jax.ShapeDtypeStruct((n // BLOCK, A_KV_HEADS, A_HEAD_DIM, BLOCK), BF16),
        jax.ShapeDtypeStruct((n // tq, B_HEADS, B_KV_RANK, tq), BF16),
        jax.ShapeDtypeStruct((n, B_KV_RANK), BF16),
        jax.ShapeDtypeStruct((n // tq, IDX_HEADS, IDX_DIM, tq), BF16),
        jax.ShapeDtypeStruct((n, IDX_DIM), BF16),
        jax.ShapeDtypeStruct((n // tq, SUBLANES, tq), F32),
    )
    out_specs = (
        pl.BlockSpec((tm // BLOCK, A_KV_HEADS, A_HEAD_DIM, grp * BLOCK), blk4),
        pl.BlockSpec((tm, A_KV_HEADS * A_HEAD_DIM), row),
        pl.BlockSpec((tm // BLOCK, A_KV_HEADS, A_HEAD_DIM, BLOCK), blk4),
        pl.BlockSpec((tm // tq, B_HEADS, B_KV_RANK, tq), blk4),
        pl.BlockSpec((tm, B_KV_RANK), row),
        pl.BlockSpec((tm // tq, IDX_HEADS, IDX_DIM, tq), blk4),
        pl.BlockSpec((tm, IDX_DIM), row),
        pl.BlockSpec((tm // tq, SUBLANES, tq), lambda i: (i, 0, 0)),
    )
    in_specs = [pl.BlockSpec((tm, d), row)] + [_full_spec(w.shape) for w in weights]
    return pl.pallas_call(
        functools.partial(_proj_body, tq=tq),
        grid=grid, in_specs=in_specs, out_specs=out_specs, out_shape=out_shape,
        compiler_params=pltpu.CompilerParams(dimension_semantics=("arbitrary",),
                                             vmem_limit_bytes=VMEM_LIMIT_BYTES),
        name="in_proj",
    )(x2, *weights)


def _swa_body(sinks_ref, aq_ref, kcur_ref, kprev_ref, kmeta_ref, vcur_ref, vprev_ref, vmeta_ref, bias_ref, o_ref):
    n = pl.program_id(1)
    first = n == 0
    grp = A_HEADS // A_KV_HEADS
    width = grp * BLOCK
    kall = jnp.concatenate([jnp.where(first, kmeta_ref[...], kprev_ref[0]), kcur_ref[0]], axis=0)
    vall = [jnp.concatenate([jnp.where(first, vmeta_ref[0, kvh], vprev_ref[0, kvh])]
                            + [vcur_ref[j, kvh] for j in range(SWA_BLOCKS)], axis=1)
            for kvh in range(A_KV_HEADS)]
    ntile = 2 * BLOCK // SUBLANES
    key_row = (lax.broadcasted_iota(jnp.int32, (ntile, SUBLANES, width), 0) * SUBLANES
               + lax.broadcasted_iota(jnp.int32, (ntile, SUBLANES, width), 1))
    pad_row = first & (key_row < PAD)
    lane_head = lax.broadcasted_iota(jnp.int32, (1, width), 1) // BLOCK
    ones = jnp.ones((BF16_ROWS, 2 * BLOCK), BF16)
    sinks = []
    for kvh in range(A_KV_HEADS):
        sink = jnp.zeros((1, width), F32)
        for g in range(grp):
            sink = jnp.where(lane_head == g, sinks_ref[kvh * grp + g], sink)
        sinks.append(sink)
    probs_ids = [(j, kvh) for j in range(SWA_BLOCKS) for kvh in range(A_KV_HEADS)]
    scores = [jnp.dot(kall[j * BLOCK:(j + 2) * BLOCK, kvh * A_HEAD_DIM:(kvh + 1) * A_HEAD_DIM], aq_ref[j, kvh],
                      preferred_element_type=F32) for j, kvh in probs_ids]
    maxes, probs = [], []
    for i, (j, kvh) in enumerate(probs_ids):
        s = (scores[i] + bias_ref[kvh]).reshape(ntile, SUBLANES, width)
        if j == 0:
            s = jnp.where(pad_row, -jnp.inf, s)
        m = jnp.maximum(jnp.max(_tree_reduce(jnp.maximum, s), axis=0, keepdims=True), sinks[kvh])
        maxes.append(m)
        probs.append(jnp.exp(s - m[None]).reshape(2 * BLOCK, width).astype(BF16))
    pvs = [jnp.dot(jnp.concatenate([vall[kvh][:, j * BLOCK:(j + 2) * BLOCK], ones], axis=0), probs[i],
                   preferred_element_type=F32) for i, (j, kvh) in enumerate(probs_ids)]
    for j in range(SWA_BLOCKS):
        heads_t = []
        for kvh in range(A_KV_HEADS):
            i = j * A_KV_HEADS + kvh
            den = pvs[i][A_HEAD_DIM:A_HEAD_DIM + 1] + jnp.exp(sinks[kvh] - maxes[i])
            o_t = pvs[i][0:A_HEAD_DIM] / den
            heads_t += [o_t[:, g * BLOCK:(g + 1) * BLOCK] for g in range(grp)]
        o_ref[0, j * BLOCK:(j + 1) * BLOCK, :] = jnp.concatenate(heads_t, axis=0).T.astype(BF16)


def _swa_call(sinks, aq_t, ak, av_t, mak, mav_t, bias_a, b):
    nblocks = aq_t.shape[0]
    nb = nblocks // b
    assert nb % SWA_BLOCKS == 0
    ns = nb // SWA_BLOCKS
    s = nb * BLOCK
    rows = SWA_BLOCKS * BLOCK
    cur4 = lambda i, n: (i * ns + n, 0, 0, 0)
    prev_blk = lambda n: jnp.maximum(n * SWA_BLOCKS - 1, 0)
    kdim = ak.shape[-1]
    ak3 = ak.reshape(b, s, kdim)
    return pl.pallas_call(
        _swa_body,
        grid=(b, ns),
        in_specs=[pl.BlockSpec(memory_space=pltpu.SMEM),
                  pl.BlockSpec((SWA_BLOCKS,) + aq_t.shape[1:], cur4),
                  pl.BlockSpec((1, rows, kdim), lambda i, n: (i, n, 0)),
                  pl.BlockSpec((1, BLOCK, kdim), lambda i, n: (i, prev_blk(n), 0)),
                  _full_spec(mak.shape),
                  pl.BlockSpec((SWA_BLOCKS,) + av_t.shape[1:], cur4),
                  pl.BlockSpec((1,) + av_t.shape[1:], lambda i, n: (i * nb + prev_blk(n), 0, 0, 0)),
                  _full_spec(mav_t.shape), _full_spec(bias_a.shape)],
        out_specs=pl.BlockSpec((1, rows, A_HEADS * A_HEAD_DIM), lambda i, n: (i, n, 0)),
        out_shape=jax.ShapeDtypeStruct((b, s, A_HEADS * A_HEAD_DIM), BF16),
        compiler_params=pltpu.CompilerParams(dimension_semantics=("arbitrary", "arbitrary"),
                                             vmem_limit_bytes=VMEM_LIMIT_BYTES),
        name="swa_sink_attention",
    )(sinks, aq_t, ak3, ak3, mak, av_t, av_t, mav_t, bias_a)


def _mla_body(iqs_ref, iw_ref, qabs_ref, ikp_ref, ckvp_ref, bnear_ref, wuvt_ref, tril_ref, o_ref,
              key_sc, khi_sc, klo_sc, s0_sc, s1_sc, mb0_sc, mb1_sc, al0_sc, al1_sc, m_sc, acc_sc, tie_sc, *, topk):
    m_blk = pl.program_id(1)
    nchunks = m_blk + 2
    groups = TQ // SUBLANES
    acc_rows = B_KV_RANK + BF16_ROWS
    wrow = [iw_ref[0, h:h + 1, :] for h in range(IDX_HEADS)]

    sub_pos = (lax.broadcasted_iota(jnp.int32, (groups, SUBLANES, TQ), 0) * SUBLANES
               + lax.broadcasted_iota(jnp.int32, (groups, SUBLANES, TQ), 1))
    qpos = BLOCK + m_blk * TQ + lax.broadcasted_iota(jnp.int32, (groups, SUBLANES, TQ), 2)

    def rows_of(c):
        return pl.ds(pl.multiple_of(c * TQ, TQ), TQ)

    def tiles(x):
        return x.reshape(groups, SUBLANES, x.shape[-1])

    def sort_key(sc):
        bits = lax.bitcast_convert_type(sc, jnp.int32)
        return jnp.where(bits < 0, INT_MIN - bits, bits)

    def score_chunk(c, carry):
        ik_c = ikp_ref[0, rows_of(c), :]
        sc = None
        for h in range(IDX_HEADS):
            logits = jnp.dot(ik_c, iqs_ref[0, h], preferred_element_type=F32)
            term = jnp.maximum(logits, 0.0) * wrow[h]
            sc = term if sc is None else sc + term
        pos = c * TQ + sub_pos
        adm = (pos >= PAD) & (pos <= qpos)
        key = jnp.where(adm, tiles(sort_key(sc)), INT_MIN).reshape(TQ, TQ)
        key_sc[rows_of(c), :] = key
        khi_sc[rows_of(c), :] = jnp.right_shift(key, 16).astype(jnp.int16)
        klo_sc[rows_of(c), :] = (key ^ 0x8000).astype(jnp.int16)
        return carry

    def score_pair(i, carry):
        score_chunk(2 * i, carry)
        return score_chunk(2 * i + 1, carry)

    lax.fori_loop(0, lax.shift_right_logical(nchunks, 1), score_pair, 0)

    @pl.when((nchunks & 1) == 1)
    def _():
        score_chunk(nchunks - 1, 0)

    def count(pred):
        def body(c, part):
            hit = pred(tiles(key_sc[rows_of(c), :]))
            return part + _tree_reduce(jnp.add, jnp.where(hit, 1.0, 0.0))
        part = lax.fori_loop(0, nchunks, body, jnp.zeros((SUBLANES, TQ), F32))
        return jnp.broadcast_to(jnp.sum(part, axis=0, keepdims=True), (SUBLANES, TQ))

    kf = float(topk)

    def tiles16(x):
        return x.reshape(TQ // BF16_ROWS, BF16_ROWS, TQ)

    def count16_ge(ref, cand):
        cand16 = jnp.broadcast_to(cand[0:1], (BF16_ROWS, TQ)).astype(jnp.int16)
        one, zero = jnp.int16(1), jnp.int16(0)

        def hits(c):
            hit = tiles16(ref[rows_of(c), :]) >= cand16[None]
            return _tree_reduce(jnp.add, jnp.where(hit, one, zero))

        def two_chunks(i, part):
            return part + hits(2 * i) + hits(2 * i + 1)
        part = lax.fori_loop(0, lax.shift_right_logical(nchunks, 1), two_chunks,
                             jnp.zeros((BF16_ROWS, TQ), jnp.int16))
        part = lax.cond((nchunks & 1) == 1, lambda p: p + hits(nchunks - 1), lambda p: p, part)
        total = jnp.sum(part.astype(jnp.int32), axis=0, keepdims=True)
        return jnp.broadcast_to(total, (SUBLANES, TQ))

    def digit_search(ref):
        def step(i, t):
            cand = t + jnp.left_shift(jnp.int32(1), 15 - i)
            return jnp.where(count16_ge(ref, cand) >= topk, cand, t)
        return lax.fori_loop(0, 16, step, jnp.full((SUBLANES, TQ), I16_MIN, jnp.int32))

    t_hi = digit_search(khi_sc)
    t_hi16 = jnp.broadcast_to(t_hi[0:1], (BF16_ROWS, TQ)).astype(jnp.int16)

    def pin_low(c, carry):
        hi = tiles16(khi_sc[rows_of(c), :])
        lo = tiles16(klo_sc[rows_of(c), :])
        lo = jnp.where(hi > t_hi16[None], jnp.int16(I16_MAX), jnp.where(hi < t_hi16[None], jnp.int16(I16_MIN), lo))
        klo_sc[rows_of(c), :] = lo.reshape(TQ, TQ)
        return carry

    lax.fori_loop(0, nchunks, pin_low, 0)
    t_lo = digit_search(klo_sc)
    thr = t_hi * 65536 + (t_lo - I16_MIN)
    need = kf - count(lambda kp: kp > thr[None])

    m_sc[...] = jnp.full(m_sc.shape, MAX_FLOOR, F32)
    acc_sc[...] = jnp.zeros(acc_sc.shape, F32)
    tie_sc[...] = jnp.zeros(tie_sc.shape, F32)

    def score_stage(c, buf):
        s_buf, m_buf, al_buf = buf
        ckv_c = ckvp_ref[0, rows_of(c), :]
        kp = tiles(key_sc[rows_of(c), :])
        tied = kp == thr[None]
        tied_b = jnp.where(tied, 1.0, 0.0).reshape(TQ, TQ).astype(BF16)
        rank = tiles(jnp.dot(tril_ref[...], tied_b, preferred_element_type=F32)) + tie_sc[...][None]
        tie_sc[...] = jnp.broadcast_to(rank[groups - 1, SUBLANES - 1:SUBLANES, :], (SUBLANES, TQ))
        sel = ((kp > thr[None]) | (tied & (rank <= need[None]))) & (kp != INT_MIN)
        mask = jnp.where(sel, 0.0, -jnp.inf)
        bias_row0 = pl.multiple_of(jnp.maximum(c - m_blk + 1, 0) * TQ, TQ)
        for h in range(B_HEADS):
            sh = jnp.dot(ckv_c, qabs_ref[0, h], preferred_element_type=F32)
            sh = tiles(sh + bnear_ref[h, pl.ds(bias_row0, TQ), :]) + mask
            s_buf[h] = sh.reshape(TQ, TQ)
            m_prev = m_sc[h]
            mx = jnp.max(_tree_reduce(jnp.maximum, sh), axis=0, keepdims=True)
            m_new = jnp.maximum(m_prev, mx)
            al_buf[h] = jnp.exp2(m_prev - m_new)
            m_buf[h] = m_new
            m_sc[h] = m_new

    def value_stage(c, buf):
        s_buf, m_buf, al_buf = buf
        vt = ckvp_ref[0, rows_of(c), :].astype(F32).T.astype(BF16)
        vext = jnp.concatenate([vt, jnp.ones((BF16_ROWS, TQ), BF16)], axis=0)
        for h in range(B_HEADS):
            e = jnp.exp2(tiles(s_buf[h]) - m_buf[h][None]).reshape(TQ, TQ).astype(BF16)
            pv = jnp.dot(vext, e, preferred_element_type=F32)
            acc = acc_sc[h].reshape(acc_rows // SUBLANES, SUBLANES, TQ)
            acc_sc[h] = (acc * al_buf[h][None] + pv.reshape(acc.shape)).reshape(acc_rows, TQ)

    buf0, buf1 = (s0_sc, mb0_sc, al0_sc), (s1_sc, mb1_sc, al1_sc)
    score_stage(0, buf0)

    def chunk_pair(i, carry):
        c = 2 * i + 1
        value_stage(c - 1, buf0)
        score_stage(c, buf1)
        value_stage(c, buf1)
        score_stage(c + 1, buf0)
        return carry

    npairs = (nchunks - 1) // 2
    lax.fori_loop(0, npairs, chunk_pair, 0)
    last = nchunks - 1

    @pl.when(last == 2 * npairs)
    def _():
        value_stage(last, buf0)

    @pl.when(last != 2 * npairs)
    def _():
        value_stage(last - 1, buf0)
        score_stage(last, buf1)
        value_stage(last, buf1)

    outs = []
    for h in range(B_HEADS):
        acc = acc_sc[h]
        lat = (acc[0:B_KV_RANK] / acc[B_KV_RANK:B_KV_RANK + 1]).astype(BF16)
        outs.append(jnp.dot(wuvt_ref[h], lat, preferred_element_type=F32))
    o_ref[0] = jnp.concatenate(outs, axis=0).T.astype(BF16)


def _mla_call(iqs, iw, qabs, ikp, ckvp, bnear, wuvt, topk):
    b, sp, _ = ikp.shape
    nq = iw.shape[0] // b
    s = nq * TQ
    acc_rows = B_KV_RANK + BF16_ROWS
    tril = jnp.asarray(np.tril(np.ones((TQ, TQ), np.float32)), BF16)
    return pl.pallas_call(
        functools.partial(_mla_body, topk=topk),
        grid=(b, nq),
        in_specs=[pl.BlockSpec((1, IDX_HEADS, IDX_DIM, TQ), lambda i, m: (i * nq + m, 0, 0, 0)),
                  pl.BlockSpec((1, SUBLANES, TQ), lambda i, m: (i * nq + m, 0, 0)),
                  pl.BlockSpec((1, B_HEADS, B_KV_RANK, TQ), lambda i, m: (i * nq + m, 0, 0, 0)),
                  pl.BlockSpec((1, sp, IDX_DIM), lambda i, m: (i, 0, 0)),
                  pl.BlockSpec((1, sp, B_KV_RANK), lambda i, m: (i, 0, 0)),
                  _full_spec(bnear.shape), _full_spec(wuvt.shape), _full_spec(tril.shape)],
        out_specs=pl.BlockSpec((1, TQ, B_HEADS * B_HEAD_DIM), lambda i, m: (i, m, 0)),
        out_shape=jax.ShapeDtypeStruct((b, s, B_HEADS * B_HEAD_DIM), BF16),
        scratch_shapes=[pltpu.VMEM((sp, TQ), jnp.int32),
                        pltpu.VMEM((sp, TQ), jnp.int16),
                        pltpu.VMEM((sp, TQ), jnp.int16),
                        pltpu.VMEM((B_HEADS, TQ, TQ), F32), pltpu.VMEM((B_HEADS, TQ, TQ), F32),
                        pltpu.VMEM((B_HEADS, SUBLANES, TQ), F32), pltpu.VMEM((B_HEADS, SUBLANES, TQ), F32),
                        pltpu.VMEM((B_HEADS, SUBLANES, TQ), F32), pltpu.VMEM((B_HEADS, SUBLANES, TQ), F32),
                        pltpu.VMEM((B_HEADS, SUBLANES, TQ), F32),
                        pltpu.VMEM((B_HEADS, acc_rows, TQ), F32),
                        pltpu.VMEM((SUBLANES, TQ), F32)],
        compiler_params=pltpu.CompilerParams(dimension_semantics=("arbitrary", "arbitrary"),
                                             vmem_limit_bytes=VMEM_LIMIT_BYTES),
        name="indexer_topk_mla",
    )(iqs, iw, qabs, ikp, ckvp, bnear, wuvt, tril)


def _out_body(x_ref, oa_ref, ob_ref, ag_ref, wg_ref, bg_ref, wa_ref, wb_ref, wo_ref, fg_ref,
              wfg_ref, wfu_ref, wfd_ref, ng_ref, y_ref):
    d = x_ref.shape[1]
    x = x_ref[...]
    hn = _rms(x, ag_ref[...]).astype(BF16)

    def gated(o_ref, w_ref, lo):
        gate = jax.nn.sigmoid(jnp.dot(hn, wg_ref[:, lo:lo + d], preferred_element_type=F32) + bg_ref[:, lo:lo + d])
        return gate * jnp.dot(o_ref[...], w_ref[...], preferred_element_type=F32)

    mixed = (gated(oa_ref, wa_ref, 0) + gated(ob_ref, wb_ref, d)).astype(BF16)
    h = x + jnp.dot(mixed, wo_ref[...], preferred_element_type=F32)
    hn2 = _rms(h, fg_ref[...]).astype(BF16)
    tiles_ff = wfg_ref.shape[1] // MXU_DIM
    bounds = [MXU_DIM * ((tiles_ff * i + FFN_CHUNKS - 1) // FFN_CHUNKS) for i in range(FFN_CHUNKS + 1)]
    for lo, hi in zip(bounds[:-1], bounds[1:]):
        g = jnp.dot(hn2, wfg_ref[:, lo:hi], preferred_element_type=F32)
        u = jnp.dot(hn2, wfu_ref[:, lo:hi], preferred_element_type=F32)
        act = (g * jax.nn.sigmoid(g) * u).astype(BF16)
        h = h + jnp.dot(act, wfd_ref[lo:hi, :], preferred_element_type=F32)
    y_ref[...] = _rms(h, ng_ref[...])


def _out_call(x2, oa, ob, ag, wg, bg, wa, wb, wo, fg, wfg, wfu, wfd, ng):
    n, d = x2.shape
    tm = TOK_TILE
    row = lambda i: (i, 0)

    def const_spec(a):
        return pl.BlockSpec(a.shape, lambda i: (0,) * a.ndim, pipeline_mode=pl.Buffered(1))

    consts = [ag, wg, bg, wa, wb, wo, fg, wfg, wfu, wfd, ng]
    return pl.pallas_call(
        _out_body,
        grid=(n // tm,),
        in_specs=[pl.BlockSpec((tm, d), row), pl.BlockSpec((tm, oa.shape[1]), row),
                  pl.BlockSpec((tm, ob.shape[1]), row)] + [const_spec(a) for a in consts],
        out_specs=pl.BlockSpec((tm, d), row),
        out_shape=jax.ShapeDtypeStruct((n, d), F32),
        compiler_params=pltpu.CompilerParams(dimension_semantics=("arbitrary",),
                                             vmem_limit_bytes=VMEM_LIMIT_BYTES),
        name="merge_ffn_norm",
    )(x2, oa, ob, *consts)


def kernel(x, meta_tokens, attn_norm_g, w_in, b_gates, q_norm_g, kv_norm_g, w_uq, w_uk, w_uv, idx_k_ln_g,
           idx_k_ln_b, sinks, rel_bias, w_branch_a, w_branch_b, w_out, ffn_norm_g, w_ffn_gate, w_ffn_up,
           w_ffn_down, final_norm_g):
    b, s, d = x.shape
    assert attn_norm_g.shape[0] == 1, "single-layer block"
    assert s % TQ == 0 and (b * s) % TOK_TILE == 0 and TOK_TILE % TQ == 0
    assert w_ffn_gate.shape[2] % MXU_DIM == 0
    topk = min(TOPK_MAX, s // 4)
    far_bkts = np.unique(_t5_bucket_np(np.arange(BLOCK + 1, s + BLOCK + 1)))
    assert far_bkts.size == 1
    far_bkt = int(far_bkts[0])

    wi = w_in[0]
    widths = (A_HEADS * A_HEAD_DIM, A_KV_HEADS * A_HEAD_DIM, A_KV_HEADS * A_HEAD_DIM, B_Q_RANK, B_KV_RANK,
              IDX_HEADS * IDX_DIM, IDX_DIM, IDX_HEADS, 2 * d)
    starts = np.concatenate([[0], np.cumsum(widths)])
    w_aq, w_ak, w_av, w_bq, w_bkv, w_iq, w_ik, w_iw, w_gates = (wi[:, int(a):int(b_)]
                                                                 for a, b_ in zip(starts[:-1], starts[1:]))
    zpad = lambda k: jnp.zeros((d, k), wi.dtype)
    w1 = jnp.concatenate([w_ak, w_bkv, w_ik, zpad(LANES - IDX_DIM)], axis=1).astype(BF16)
    wt = jnp.concatenate([w_aq, w_av, w_bq, w_iq, w_iw, zpad(BF16_ROWS - IDX_HEADS)],
                         axis=1).T.astype(BF16)
    wg = w_gates.astype(BF16)
    row2 = lambda v: v.reshape(1, -1).astype(F32)
    col2 = lambda v: v.reshape(-1, 1).astype(F32)
    wuk = jnp.transpose(w_uk[0], (1, 0, 2)).astype(BF16)
    wuvt = jnp.transpose(w_uv[0], (1, 2, 0)).astype(BF16)
    proj_w = (row2(attn_norm_g[0]), w1, wt, col2(q_norm_g[0]), w_uq[0].T.astype(BF16), wuk,
              row2(kv_norm_g[0]), row2(idx_k_ln_g[0]), row2(idx_k_ln_b[0]))

    bias_a, bnear = _bias_call(rel_bias, far_bkt)

    x2 = x.reshape(b * s, d)
    aq_t, ak, av_t, qabs, ckv, iqs, ik, iw = _proj_call(x2, TOK_TILE, TQ, *proj_w)
    meta_blk = jnp.concatenate([jnp.zeros((PAD, d), x.dtype), meta_tokens.astype(x.dtype)], axis=0)
    _, mak, mav_t, _, mckv, _, mik, _ = _proj_call(meta_blk, BLOCK, BLOCK, *proj_w)

    def padded_keys(real, meta):
        c = real.shape[-1]
        return jnp.concatenate([jnp.broadcast_to(meta[None], (b, BLOCK, c)), real.reshape(b, s, c),
                                jnp.zeros((b, TQ - BLOCK, c), real.dtype)], axis=1)

    o_a = _swa_call(sinks[0], aq_t, ak, av_t, mak, mav_t, bias_a, b)
    o_b = _mla_call(iqs, iw, qabs, padded_keys(ik, mik), padded_keys(ckv, mckv), bnear, wuvt,
                    topk)

    y = _out_call(x2, o_a.reshape(b * s, -1), o_b.reshape(b * s, -1), row2(attn_norm_g[0]), wg,
                  row2(b_gates[0]), w_branch_a[0].astype(BF16), w_branch_b[0].astype(BF16),
                  w_out[0].astype(BF16), row2(ffn_norm_g[0]), w_ffn_gate[0].astype(BF16),
                  w_ffn_up[0].astype(BF16), w_ffn_down[0].astype(BF16), row2(final_norm_g))
    return y.reshape(b, s, d)
```

```python
import functools
import math

import numpy as np
import jax
import jax.numpy as jnp
from jax import lax
from jax.experimental import pallas as pl
from jax.experimental.pallas import tpu as pltpu

N_META = 16
BLOCK = 128
PAD = BLOCK - N_META
WINDOW = 128
A_HEADS = 8
A_KV_HEADS = 2
A_HEAD_DIM = 64
B_HEADS = 8
B_HEAD_DIM = 64
B_Q_RANK = 256
B_KV_RANK = 128
IDX_HEADS = 4
IDX_DIM = 64
TOPK_MAX = 256
N_BUCKETS = 32
MAX_DISTANCE = 128
EPS = 1e-6
NEG = -1e30
MAX_FLOOR = -3.0e38
INT_MIN = -(2 ** 31)
LOG2E = math.log2(math.e)
I16_MIN = -(2 ** 15)
I16_MAX = 2 ** 15 - 1

LANES = 128
SUBLANES = 8
BF16_ROWS = 16
MXU_DIM = 256
VMEM_LIMIT_BYTES = 56 * 1024 * 1024

TOK_TILE = 512
TQ = 256
FFN_CHUNKS = 2
SWA_BLOCKS = 4

C_AK = 0
C_BKV = C_AK + A_KV_HEADS * A_HEAD_DIM
C_IK = C_BKV + B_KV_RANK
C_END = C_IK + LANES
R_AQ = 0
R_AV = R_AQ + A_HEADS * A_HEAD_DIM
R_BQ = R_AV + A_KV_HEADS * A_HEAD_DIM
R_IQ = R_BQ + B_Q_RANK
R_IW = R_IQ + IDX_HEADS * IDX_DIM
R_END = R_IW + BF16_ROWS

F32 = jnp.float32
BF16 = jnp.bfloat16
NT_DIMS = (((1,), (1,)), ((), ()))


def _t5_bucket_np(dist):
    dist = np.asarray(dist, np.int64)
    max_exact = N_BUCKETS // 2
    d = np.maximum(dist, 1).astype(np.float32)
    large = max_exact + (np.log(d / np.float32(max_exact)) / np.float32(math.log(MAX_DISTANCE / max_exact))
                         * np.float32(N_BUCKETS - max_exact)).astype(np.int32)
    large = np.minimum(large, N_BUCKETS - 1)
    return np.where(dist < max_exact, dist, large).astype(np.int32)


def _rms(x, g):
    return x * lax.rsqrt(jnp.mean(x * x, axis=-1, keepdims=True) + EPS) * g


def _tree_reduce(op, x):
    while x.shape[0] > 1:
        half = x.shape[0] // 2
        x = op(x[:half], x[half:])
    return x[0]


def _full_spec(shape):
    nd = len(shape)
    return pl.BlockSpec(shape, lambda *_: (0,) * nd)


def _bias_body(tab_ref, bkt_a_ref, bkt_near_ref, bkt_meta_ref, ba_ref, bnear_ref, bmeta_ref, *, far_bkt):
    def lookup(bkt, col, fill):
        acc = jnp.full(bkt.shape, fill, F32)
        for b in range(N_BUCKETS):
            acc = jnp.where(bkt == b, tab_ref[b, col], acc)
        return acc

    bkt_a = bkt_a_ref[...]
    bkt_near = bkt_near_ref[...]
    grp = A_HEADS // A_KV_HEADS
    for h in range(A_HEADS):
        ba_ref[h // grp, :, (h % grp) * BLOCK:(h % grp + 1) * BLOCK] = lookup(bkt_a, h, -jnp.inf)
    for h in range(B_HEADS):
        col = A_HEADS + h
        bnear_ref[h, 0:TQ, :] = jnp.zeros((TQ, TQ), F32)
        bnear_ref[h, TQ:3 * TQ, :] = (lookup(bkt_near, col, 0.0) - tab_ref[far_bkt, col]) * LOG2E
        bmeta_ref[h] = (lookup(bkt_meta_ref[...], col, 0.0) - tab_ref[far_bkt, col]) * LOG2E


def _bias_call(rel_bias, far_bkt):
    k = np.arange(2 * BLOCK)[:, None]
    q = np.arange(BLOCK)[None, :]
    dist = q + BLOCK - k
    bkt_a = np.where((dist >= 0) & (dist < WINDOW), _t5_bucket_np(np.maximum(dist, 0)), -1).astype(np.int32)
    k = np.arange(2 * TQ)[:, None]
    q = np.arange(TQ)[None, :]
    bkt_near = _t5_bucket_np(np.maximum(q + TQ - k, 0))
    k = np.arange(N_META)[:, None]
    bkt_meta = _t5_bucket_np(q + N_META - k)
    vmem = pl.BlockSpec(memory_space=pltpu.VMEM)
    grp = A_HEADS // A_KV_HEADS
    return pl.pallas_call(
        functools.partial(_bias_body, far_bkt=far_bkt),
        out_shape=(jax.ShapeDtypeStruct((A_KV_HEADS, 2 * BLOCK, grp * BLOCK), F32),
                   jax.ShapeDtypeStruct((B_HEADS, 3 * TQ, TQ), F32),
                   jax.ShapeDtypeStruct((B_HEADS, N_META, TQ), F32)),
        in_specs=[pl.BlockSpec(memory_space=pltpu.SMEM), vmem, vmem, vmem],
        out_specs=(vmem, vmem, vmem),
        name="bias_tables",
    )(rel_bias, jnp.asarray(bkt_a), jnp.asarray(bkt_near), jnp.asarray(bkt_meta))


def _proj_body(x_ref, g_ref, w1_ref, wt_ref, qg_ref, wuq_ref, wuk_ref, kvg_ref, ikg_ref, ikb_ref,
               aq_ref, ak_ref, av_ref, qabs_ref, ckv_ref, iqs_ref, ik_ref, iw_ref, *, tq):
    tm = x_ref.shape[0]
    nblk = tm // tq
    nblk_a = tm // BLOCK
    grp = A_HEADS // A_KV_HEADS
    hn = _rms(x_ref[...], g_ref[...]).astype(BF16)

    def proj(lo, hi):
        return jnp.dot(hn, w1_ref[:, lo:hi], preferred_element_type=F32)

    ak_ref[...] = proj(C_AK, C_BKV).astype(BF16)

    feat_t = lax.dot_general(wt_ref[...], hn, NT_DIMS, preferred_element_type=F32)
    aq_t = (feat_t[R_AQ:R_AV] * (A_HEAD_DIM ** -0.5)).astype(BF16)
    for j in range(nblk_a):
        tok = slice(j * BLOCK, (j + 1) * BLOCK)
        for h in range(A_HEADS):
            aq_ref[j, h // grp, :, (h % grp) * BLOCK:(h % grp + 1) * BLOCK] = \
                aq_t[h * A_HEAD_DIM:(h + 1) * A_HEAD_DIM, tok]
    av_t = feat_t[R_AV:R_BQ].astype(BF16)
    for j in range(nblk_a):
        for kvh in range(A_KV_HEADS):
            av_ref[j, kvh] = av_t[kvh * A_HEAD_DIM:(kvh + 1) * A_HEAD_DIM, j * BLOCK:(j + 1) * BLOCK]

    bq_t = feat_t[R_BQ:R_IQ]
    qn_t = bq_t * lax.rsqrt(jnp.mean(bq_t * bq_t, axis=0, keepdims=True) + EPS) * qg_ref[...]
    q_t = jnp.dot(wuq_ref[...], qn_t.astype(BF16), preferred_element_type=F32).astype(BF16)
    for h in range(B_HEADS):
        qa_t = jnp.dot(wuk_ref[h], q_t[h * B_HEAD_DIM:(h + 1) * B_HEAD_DIM], preferred_element_type=F32)
        qa_t = (qa_t * (B_HEAD_DIM ** -0.5 * LOG2E)).astype(BF16)
        for j in range(nblk):
            qabs_ref[j, h] = qa_t[:, j * tq:(j + 1) * tq]
    iq_t = feat_t[R_IQ:R_IW].astype(BF16)
    for h in range(IDX_HEADS):
        for j in range(nblk):
            iqs_ref[j, h] = iq_t[h * IDX_DIM:(h + 1) * IDX_DIM, j * tq:(j + 1) * tq]

    ckv_ref[...] = _rms(proj(C_BKV, C_IK), kvg_ref[...]).astype(BF16)

    iw_t = feat_t[R_IW:R_IW + SUBLANES] * ((IDX_HEADS * IDX_DIM) ** -0.5)
    for j in range(nblk):
        iw_ref[j] = iw_t[:, j * tq:(j + 1) * tq]

    ik = proj(C_IK, C_END)[:, :IDX_DIM]
    mu = jnp.mean(ik, axis=-1, keepdims=True)
    xc = ik - mu
    var = jnp.mean(xc * xc, axis=-1, keepdims=True)
    ik_ref[...] = (xc * lax.rsqrt(var + EPS) * ikg_ref[...] + ikb_ref[...]).astype(BF16)


def _proj_call(x2, tm, tq, *weights):
    n, d = x2.shape
    grid = (n // tm,)
    row = lambda i: (i, 0)
    blk4 = lambda i: (i, 0, 0, 0)
    grp = A_HEADS // A_KV_HEADS
    out_shape = (
        jax.ShapeDtypeStruct((n // BLOCK, A_KV_HEADS, A_HEAD_DIM, grp * BLOCK), BF16),
        jax.ShapeDtypeStruct((n, A_KV_HEADS * A_HEAD_DIM), BF16),
        jax.ShapeDtypeStruct((n // BLOCK, A_KV_HEADS, A_HEAD_DIM, BLOCK), BF16),
        jax.ShapeDtypeStruct((n // tq, B_HEADS, B_KV_RANK, tq), BF16),
        jax.ShapeDtypeStruct((n, B_KV_RANK), BF16),
        jax.ShapeDtypeStruct((n // tq, IDX_HEADS, IDX_DIM, tq), BF16),
        jax.ShapeDtypeStruct((n, IDX_DIM), BF16),
        jax.ShapeDtypeStruct((n // tq, SUBLANES, tq), F32),
    )
    out_specs = (
        pl.BlockSpec((tm // BLOCK, A_KV_HEADS, A_HEAD_DIM, grp * BLOCK), blk4),
        pl.BlockSpec((tm, A_KV_HEADS * A_HEAD_DIM), row),
        pl.BlockSpec((tm // BLOCK, A_KV_HEADS, A_HEAD_DIM, BLOCK), blk4),
        pl.BlockSpec((tm // tq, B_HEADS, B_KV_RANK, tq), blk4),
        pl.BlockSpec((tm, B_KV_RANK), row),
        pl.BlockSpec((tm // tq, IDX_HEADS, IDX_DIM, tq), blk4),
        pl.BlockSpec((tm, IDX_DIM), row),
        pl.BlockSpec((tm // tq, SUBLANES, tq), lambda i: (i, 0, 0)),
    )
    in_specs = [pl.BlockSpec((tm, d), row)] + [_full_spec(w.shape) for w in weights]
    return pl.pallas_call(
        functools.partial(_proj_body, tq=tq),
        grid=grid, in_specs=in_specs, out_specs=out_specs, out_shape=out_shape,
        compiler_params=pltpu.CompilerParams(dimension_semantics=("arbitrary",),
                                             vmem_limit_bytes=VMEM_LIMIT_BYTES),
        name="in_proj",
    )(x2, *weights)


def _swa_body(sinks_ref, aq_ref, kcur_ref, kprev_ref, kmeta_ref, vcur_ref, vprev_ref, vmeta_ref, bias_ref, o_ref):
    n = pl.program_id(1)
    first = n == 0
    grp = A_HEADS // A_KV_HEADS
    width = grp * BLOCK
    kall = jnp.concatenate([jnp.where(first, kmeta_ref[...], kprev_ref[0]), kcur_ref[0]], axis=0)
    vall = [jnp.concatenate([jnp.where(first, vmeta_ref[0, kvh], vprev_ref[0, kvh])]
                            + [vcur_ref[j, kvh] for j in range(SWA_BLOCKS)], axis=1)
            for kvh in range(A_KV_HEADS)]
    ntile = 2 * BLOCK // SUBLANES
    key_row = (lax.broadcasted_iota(jnp.int32, (ntile, SUBLANES, width), 0) * SUBLANES
               + lax.broadcasted_iota(jnp.int32, (ntile, SUBLANES, width), 1))
    pad_row = first & (key_row < PAD)
    lane_head = lax.broadcasted_iota(jnp.int32, (1, width), 1) // BLOCK
    ones = jnp.ones((BF16_ROWS, 2 * BLOCK), BF16)
    sinks = []
    for kvh in range(A_KV_HEADS):
        sink = jnp.zeros((1, width), F32)
        for g in range(grp):
            sink = jnp.where(lane_head == g, sinks_ref[kvh * grp + g], sink)
        sinks.append(sink)
    probs_ids = [(j, kvh) for j in range(SWA_BLOCKS) for kvh in range(A_KV_HEADS)]
    scores = [jnp.dot(kall[j * BLOCK:(j + 2) * BLOCK, kvh * A_HEAD_DIM:(kvh + 1) * A_HEAD_DIM], aq_ref[j, kvh],
                      preferred_element_type=F32) for j, kvh in probs_ids]
    maxes, probs = [], []
    for i, (j, kvh) in enumerate(probs_ids):
        s = (scores[i] + bias_ref[kvh]).reshape(ntile, SUBLANES, width)
        if j == 0:
            s = jnp.where(pad_row, -jnp.inf, s)
        m = jnp.maximum(jnp.max(_tree_reduce(jnp.maximum, s), axis=0, keepdims=True), sinks[kvh])
        maxes.append(m)
        probs.append(jnp.exp(s - m[None]).reshape(2 * BLOCK, width).astype(BF16))
    pvs = [jnp.dot(jnp.concatenate([vall[kvh][:, j * BLOCK:(j + 2) * BLOCK], ones], axis=0), probs[i],
                   preferred_element_type=F32) for i, (j, kvh) in enumerate(probs_ids)]
    for j in range(SWA_BLOCKS):
        heads_t = []
        for kvh in range(A_KV_HEADS):
            i = j * A_KV_HEADS + kvh
            den = pvs[i][A_HEAD_DIM:A_HEAD_DIM + 1] + jnp.exp(sinks[kvh] - maxes[i])
            o_t = pvs[i][0:A_HEAD_DIM] / den
            heads_t += [o_t[:, g * BLOCK:(g + 1) * BLOCK] for g in range(grp)]
        o_ref[0, j * BLOCK:(j + 1) * BLOCK, :] = jnp.concatenate(heads_t, axis=0).T.astype(BF16)


def _swa_call(sinks, aq_t, ak, av_t, mak, mav_t, bias_a, b):
    nblocks = aq_t.shape[0]
    nb = nblocks // b
    assert nb % SWA_BLOCKS == 0
    ns = nb // SWA_BLOCKS
    s = nb * BLOCK
    rows = SWA_BLOCKS * BLOCK
    cur4 = lambda i, n: (i * ns + n, 0, 0, 0)
    prev_blk = lambda n: jnp.maximum(n * SWA_BLOCKS - 1, 0)
    kdim = ak.shape[-1]
    ak3 = ak.reshape(b, s, kdim)
    return pl.pallas_call(
        _swa_body,
        grid=(b, ns),
        in_specs=[pl.BlockSpec(memory_space=pltpu.SMEM),
                  pl.BlockSpec((SWA_BLOCKS,) + aq_t.shape[1:], cur4),
                  pl.BlockSpec((1, rows, kdim), lambda i, n: (i, n, 0)),
                  pl.BlockSpec((1, BLOCK, kdim), lambda i, n: (i, prev_blk(n), 0)),
                  _full_spec(mak.shape),
                  pl.BlockSpec((SWA_BLOCKS,) + av_t.shape[1:], cur4),
                  pl.BlockSpec((1,) + av_t.shape[1:], lambda i, n: (i * nb + prev_blk(n), 0, 0, 0)),
                  _full_spec(mav_t.shape), _full_spec(bias_a.shape)],
        out_specs=pl.BlockSpec((1, rows, A_HEADS * A_HEAD_DIM), lambda i, n: (i, n, 0)),
        out_shape=jax.ShapeDtypeStruct((b, s, A_HEADS * A_HEAD_DIM), BF16),
        compiler_params=pltpu.CompilerParams(dimension_semantics=("arbitrary", "arbitrary"),
                                             vmem_limit_bytes=VMEM_LIMIT_BYTES),
        name="swa_sink_attention",
    )(sinks, aq_t, ak3, ak3, mak, av_t, av_t, mav_t, bias_a)


def _mla_body(iqs_ref, iw_ref, qabs_ref, ik_ref, ckv_ref, mik_ref, mckv_ref, mvext_ref, bnear_ref, bmeta_ref,
              wuvt_ref, tril_ref, mtril_ref, o_ref,
              key_sc, khi_sc, klo_sc, s0_sc, s1_sc, mb0_sc, mb1_sc, al0_sc, al1_sc, m_sc, acc_sc, tie_sc, *, topk):
    m_blk = pl.program_id(1)
    first = m_blk == 0
    nchunks = m_blk + 1
    groups = TQ // SUBLANES
    acc_rows = B_KV_RANK + BF16_ROWS
    wrow = [iw_ref[0, h:h + 1, :] for h in range(IDX_HEADS)]

    sub_pos = (lax.broadcasted_iota(jnp.int32, (groups, SUBLANES, TQ), 0) * SUBLANES
               + lax.broadcasted_iota(jnp.int32, (groups, SUBLANES, TQ), 1))
    qpos = m_blk * TQ + lax.broadcasted_iota(jnp.int32, (groups, SUBLANES, TQ), 2)

    def rows_of(c):
        return pl.ds(pl.multiple_of(c * TQ, TQ), TQ)

    def tiles(x):
        return x.reshape(groups, SUBLANES, x.shape[-1])

    def sort_key(sc):
        bits = lax.bitcast_convert_type(sc, jnp.int32)
        return jnp.where(bits < 0, INT_MIN - bits, bits)

    def index_score(ik_c):
        sc = None
        for h in range(IDX_HEADS):
            logits = jnp.dot(ik_c, iqs_ref[0, h], preferred_element_type=F32)
            term = jnp.maximum(logits, 0.0) * wrow[h]
            sc = term if sc is None else sc + term
        return sort_key(sc)

    def digits(key):
        return jnp.right_shift(key, 16).astype(jnp.int16), (key ^ 0x8000).astype(jnp.int16)

    def score_chunk(c, carry):
        adm = c * TQ + sub_pos <= qpos
        key = jnp.where(adm, tiles(index_score(ik_ref[0, rows_of(c), :])), INT_MIN).reshape(TQ, TQ)
        key_sc[rows_of(c), :] = key
        khi_sc[rows_of(c), :], klo_sc[rows_of(c), :] = digits(key)
        return carry

    key_m = index_score(mik_ref[...])
    khi_m, klo_m = digits(key_m)

    def score_pair(i, carry):
        score_chunk(2 * i, carry)
        return score_chunk(2 * i + 1, carry)

    lax.fori_loop(0, lax.shift_right_logical(nchunks, 1), score_pair, 0)

    @pl.when((nchunks & 1) == 1)
    def _():
        score_chunk(nchunks - 1, 0)

    def count(pred):
        def body(c, part):
            hit = pred(tiles(key_sc[rows_of(c), :]))
            return part + _tree_reduce(jnp.add, jnp.where(hit, 1.0, 0.0))
        meta_hit = pred(key_m.reshape(N_META // SUBLANES, SUBLANES, TQ))
        part = lax.fori_loop(0, nchunks, body, _tree_reduce(jnp.add, jnp.where(meta_hit, 1.0, 0.0)))
        return jnp.broadcast_to(jnp.sum(part, axis=0, keepdims=True), (SUBLANES, TQ))

    kf = float(topk)

    def tiles16(x):
        return x.reshape(TQ // BF16_ROWS, BF16_ROWS, TQ)

    def count16_ge(ref, meta_digit, cand):
        cand16 = jnp.broadcast_to(cand[0:1], (BF16_ROWS, TQ)).astype(jnp.int16)
        one, zero = jnp.int16(1), jnp.int16(0)

        def hits(c):
            hit = tiles16(ref[rows_of(c), :]) >= cand16[None]
            return _tree_reduce(jnp.add, jnp.where(hit, one, zero))

        def two_chunks(i, part):
            return part + hits(2 * i) + hits(2 * i + 1)
        part = lax.fori_loop(0, lax.shift_right_logical(nchunks, 1), two_chunks,
                             jnp.where(meta_digit >= cand16, one, zero))
        part = lax.cond((nchunks & 1) == 1, lambda p: p + hits(nchunks - 1), lambda p: p, part)
        total = jnp.sum(part.astype(jnp.int32), axis=0, keepdims=True)
        return jnp.broadcast_to(total, (SUBLANES, TQ))

    def digit_search(ref, meta_digit):
        def step(i, t):
            cand = t + jnp.left_shift(jnp.int32(1), 15 - i)
            return jnp.where(count16_ge(ref, meta_digit, cand) >= topk, cand, t)
        return lax.fori_loop(0, 16, step, jnp.full((SUBLANES, TQ), I16_MIN, jnp.int32))

    t_hi = digit_search(khi_sc, khi_m)
    t_hi16 = jnp.broadcast_to(t_hi[0:1], (BF16_ROWS, TQ)).astype(jnp.int16)

    def pinned(hi, lo, t):
        return jnp.where(hi > t, jnp.int16(I16_MAX), jnp.where(hi < t, jnp.int16(I16_MIN), lo))

    def pin_low(c, carry):
        lo = pinned(tiles16(khi_sc[rows_of(c), :]), tiles16(klo_sc[rows_of(c), :]), t_hi16[None])
        klo_sc[rows_of(c), :] = lo.reshape(TQ, TQ)
        return carry

    lax.fori_loop(0, nchunks, pin_low, 0)
    t_lo = digit_search(klo_sc, pinned(khi_m, klo_m, t_hi16))
    thr = t_hi * 65536 + (t_lo - I16_MIN)
    need = kf - count(lambda kp: kp > thr[None])

    def pad_rows(x):
        return jnp.concatenate([x, jnp.zeros((LANES - N_META, TQ), x.dtype)], axis=0)

    thr_m = jnp.broadcast_to(thr[0:1], (N_META, TQ))
    tied_m = key_m == thr_m
    rank_m = jnp.dot(mtril_ref[...], pad_rows(jnp.where(tied_m, 1.0, 0.0).astype(BF16)),
                     preferred_element_type=F32)
    tie_sc[...] = jnp.broadcast_to(rank_m[N_META - 1:N_META], (SUBLANES, TQ))
    sel_m = (key_m > thr_m) | (tied_m & (rank_m <= jnp.broadcast_to(need[0:1], (N_META, TQ))))
    mask_m = jnp.where(sel_m, 0.0, -jnp.inf)
    heads = range(B_HEADS)
    s_m = [jnp.dot(mckv_ref[...], qabs_ref[0, h], preferred_element_type=F32) for h in heads]
    e_m = []
    for h in heads:
        sh = s_m[h] + jnp.where(first, bmeta_ref[h], 0.0) + mask_m
        m_new = jnp.maximum(jnp.max(sh, axis=0, keepdims=True), MAX_FLOOR)
        m_sc[h] = jnp.broadcast_to(m_new, (SUBLANES, TQ))
        e_m.append(pad_rows(jnp.exp2(sh - m_new).astype(BF16)))
    for h in heads:
        acc_sc[h] = jnp.dot(mvext_ref[...], e_m[h], preferred_element_type=F32)

    def score_stage(c, buf):
        s_buf, m_buf, al_buf = buf
        ckv_c = ckv_ref[0, rows_of(c), :]
        kp = tiles(key_sc[rows_of(c), :])
        tied = kp == thr[None]
        tied_b = jnp.where(tied, 1.0, 0.0).reshape(TQ, TQ).astype(BF16)
        rank = tiles(jnp.dot(tril_ref[...], tied_b, preferred_element_type=F32)) + tie_sc[...][None]
        tie_sc[...] = jnp.broadcast_to(rank[groups - 1, SUBLANES - 1:SUBLANES, :], (SUBLANES, TQ))
        sel = ((kp > thr[None]) | (tied & (rank <= need[None]))) & (kp != INT_MIN)
        mask = jnp.where(sel, 0.0, -jnp.inf)
        bias_row0 = pl.multiple_of(jnp.maximum(c - m_blk + 2, 0) * TQ, TQ)
        for h in range(B_HEADS):
            sh = jnp.dot(ckv_c, qabs_ref[0, h], preferred_element_type=F32)
            sh = tiles(sh + bnear_ref[h, pl.ds(bias_row0, TQ), :]) + mask
            s_buf[h] = sh.reshape(TQ, TQ)
            m_prev = m_sc[h]
            mx = jnp.max(_tree_reduce(jnp.maximum, sh), axis=0, keepdims=True)
            m_new = jnp.maximum(m_prev, mx)
            al_buf[h] = jnp.exp2(m_prev - m_new)
            m_buf[h] = m_new
            m_sc[h] = m_new

    def value_stage(c, buf):
        s_buf, m_buf, al_buf = buf
        vt = ckv_ref[0, rows_of(c), :].astype(F32).T.astype(BF16)
        vext = jnp.concatenate([vt, jnp.ones((BF16_ROWS, TQ), BF16)], axis=0)
        for h in range(B_HEADS):
            e = jnp.exp2(tiles(s_buf[h]) - m_buf[h][None]).reshape(TQ, TQ).astype(BF16)
            pv = jnp.dot(vext, e, preferred_element_type=F32)
            acc = acc_sc[h].reshape(acc_rows // SUBLANES, SUBLANES, TQ)
            acc_sc[h] = (acc * al_buf[h][None] + pv.reshape(acc.shape)).reshape(acc_rows, TQ)

    buf0, buf1 = (s0_sc, mb0_sc, al0_sc), (s1_sc, mb1_sc, al1_sc)
    score_stage(0, buf0)

    def chunk_pair(i, carry):
        c = 2 * i + 1
        value_stage(c - 1, buf0)
        score_stage(c, buf1)
        value_stage(c, buf1)
        score_stage(c + 1, buf0)
        return carry

    npairs = (nchunks - 1) // 2
    lax.fori_loop(0, npairs, chunk_pair, 0)
    last = nchunks - 1

    @pl.when(last == 2 * npairs)
    def _():
        value_stage(last, buf0)

    @pl.when(last != 2 * npairs)
    def _():
        value_stage(last - 1, buf0)
        score_stage(last, buf1)
        value_stage(last, buf1)

    outs = []
    for h in range(B_HEADS):
        acc = acc_sc[h]
        lat = (acc[0:B_KV_RANK] / acc[B_KV_RANK:B_KV_RANK + 1]).astype(BF16)
        outs.append(jnp.dot(wuvt_ref[h], lat, preferred_element_type=F32))
    o_ref[0] = jnp.concatenate(outs, axis=0).T.astype(BF16)


def _mla_call(iqs, iw, qabs, ik, ckv, mik, mckv, bnear, bmeta, wuvt, topk):
    b, s, _ = ik.shape
    nq = s // TQ
    acc_rows = B_KV_RANK + BF16_ROWS
    assert N_META == BF16_ROWS, "the meta keys are handled as one packed bf16 tile"
    tril = jnp.asarray(np.tril(np.ones((TQ, TQ), np.float32)), BF16)
    mtril = jnp.asarray(np.pad(np.tril(np.ones((N_META, N_META), np.float32)), ((0, 0), (0, LANES - N_META))), BF16)
    mvext = jnp.pad(jnp.concatenate([mckv.T, jnp.ones((BF16_ROWS, N_META), BF16)], axis=0),
                    ((0, 0), (0, LANES - N_META)))
    consts = (mik, mckv, mvext, bnear, bmeta, wuvt, tril, mtril)
    return pl.pallas_call(
        functools.partial(_mla_body, topk=topk),
        grid=(b, nq),
        in_specs=[pl.BlockSpec((1, IDX_HEADS, IDX_DIM, TQ), lambda i, m: (i * nq + m, 0, 0, 0)),
                  pl.BlockSpec((1, SUBLANES, TQ), lambda i, m: (i * nq + m, 0, 0)),
                  pl.BlockSpec((1, B_HEADS, B_KV_RANK, TQ), lambda i, m: (i * nq + m, 0, 0, 0)),
                  pl.BlockSpec((1, s, IDX_DIM), lambda i, m: (i, 0, 0)),
                  pl.BlockSpec((1, s, B_KV_RANK), lambda i, m: (i, 0, 0))]
                 + [_full_spec(c.shape) for c in consts],
        out_specs=pl.BlockSpec((1, TQ, B_HEADS * B_HEAD_DIM), lambda i, m: (i, m, 0)),
        out_shape=jax.ShapeDtypeStruct((b, s, B_HEADS * B_HEAD_DIM), BF16),
        scratch_shapes=[pltpu.VMEM((s, TQ), jnp.int32),
                        pltpu.VMEM((s, TQ), jnp.int16),
                        pltpu.VMEM((s, TQ), jnp.int16),
                        pltpu.VMEM((B_HEADS, TQ, TQ), F32), pltpu.VMEM((B_HEADS, TQ, TQ), F32),
                        pltpu.VMEM((B_HEADS, SUBLANES, TQ), F32), pltpu.VMEM((B_HEADS, SUBLANES, TQ), F32),
                        pltpu.VMEM((B_HEADS, SUBLANES, TQ), F32), pltpu.VMEM((B_HEADS, SUBLANES, TQ), F32),
                        pltpu.VMEM((B_HEADS, SUBLANES, TQ), F32),
                        pltpu.VMEM((B_HEADS, acc_rows, TQ), F32),
                        pltpu.VMEM((SUBLANES, TQ), F32)],
        compiler_params=pltpu.CompilerParams(dimension_semantics=("arbitrary", "arbitrary"),
                                             vmem_limit_bytes=VMEM_LIMIT_BYTES),
        name="indexer_topk_mla",
    )(iqs, iw, qabs, ik, ckv, *consts)


def _out_body(x_ref, oa_ref, ob_ref, ag_ref, wg_ref, bg_ref, wa_ref, wb_ref, wo_ref, fg_ref,
              wfg_ref, wfu_ref, wfd_ref, ng_ref, y_ref):
    d = x_ref.shape[1]
    x = x_ref[...]
    hn = _rms(x, ag_ref[...]).astype(BF16)

    def gated(o_ref, w_ref, lo):
        gate = jax.nn.sigmoid(jnp.dot(hn, wg_ref[:, lo:lo + d], preferred_element_type=F32) + bg_ref[:, lo:lo + d])
        return gate * jnp.dot(o_ref[...], w_ref[...], preferred_element_type=F32)

    mixed = (gated(oa_ref, wa_ref, 0) + gated(ob_ref, wb_ref, d)).astype(BF16)
    h = x + jnp.dot(mixed, wo_ref[...], preferred_element_type=F32)
    hn2 = _rms(h, fg_ref[...]).astype(BF16)
    tiles_ff = wfg_ref.shape[1] // MXU_DIM
    bounds = [MXU_DIM * ((tiles_ff * i + FFN_CHUNKS - 1) // FFN_CHUNKS) for i in range(FFN_CHUNKS + 1)]
    for lo, hi in zip(bounds[:-1], bounds[1:]):
        g = jnp.dot(hn2, wfg_ref[:, lo:hi], preferred_element_type=F32)
        u = jnp.dot(hn2, wfu_ref[:, lo:hi], preferred_element_type=F32)
        act = (g * jax.nn.sigmoid(g) * u).astype(BF16)
        h = h + jnp.dot(act, wfd_ref[lo:hi, :], preferred_element_type=F32)
    y_ref[...] = _rms(h, ng_ref[...])


def _out_call(x2, oa, ob, ag, wg, bg, wa, wb, wo, fg, wfg, wfu, wfd, ng):
    n, d = x2.shape
    tm = TOK_TILE
    row = lambda i: (i, 0)

    def const_spec(a):
        return pl.BlockSpec(a.shape, lambda i: (0,) * a.ndim, pipeline_mode=pl.Buffered(1))

    consts = [ag, wg, bg, wa, wb, wo, fg, wfg, wfu, wfd, ng]
    return pl.pallas_call(
        _out_body,
        grid=(n // tm,),
        in_specs=[pl.BlockSpec((tm, d), row), pl.BlockSpec((tm, oa.shape[1]), row),
                  pl.BlockSpec((tm, ob.shape[1]), row)] + [const_spec(a) for a in consts],
        out_specs=pl.BlockSpec((tm, d), row),
        out_shape=jax.ShapeDtypeStruct((n, d), F32),
        compiler_params=pltpu.CompilerParams(dimension_semantics=("arbitrary",),
                                             vmem_limit_bytes=VMEM_LIMIT_BYTES),
        name="merge_ffn_norm",
    )(x2, oa, ob, *consts)


def kernel(x, meta_tokens, attn_norm_g, w_in, b_gates, q_norm_g, kv_norm_g, w_uq, w_uk, w_uv, idx_k_ln_g,
           idx_k_ln_b, sinks, rel_bias, w_branch_a, w_branch_b, w_out, ffn_norm_g, w_ffn_gate, w_ffn_up,
           w_ffn_down, final_norm_g):
    b, s, d = x.shape
    assert attn_norm_g.shape[0] == 1, "single-layer block"
    assert s % TQ == 0 and (b * s) % TOK_TILE == 0 and TOK_TILE % TQ == 0
    assert w_ffn_gate.shape[2] % MXU_DIM == 0
    topk = min(TOPK_MAX, s // 4)
    far_bkts = np.unique(_t5_bucket_np(np.arange(BLOCK + 1, s + BLOCK + 1)))
    assert far_bkts.size == 1
    far_bkt = int(far_bkts[0])

    wi = w_in[0]
    widths = (A_HEADS * A_HEAD_DIM, A_KV_HEADS * A_HEAD_DIM, A_KV_HEADS * A_HEAD_DIM, B_Q_RANK, B_KV_RANK,
              IDX_HEADS * IDX_DIM, IDX_DIM, IDX_HEADS, 2 * d)
    starts = np.concatenate([[0], np.cumsum(widths)])
    w_aq, w_ak, w_av, w_bq, w_bkv, w_iq, w_ik, w_iw, w_gates = (wi[:, int(a):int(b_)]
                                                                 for a, b_ in zip(starts[:-1], starts[1:]))
    zpad = lambda k: jnp.zeros((d, k), wi.dtype)
    w1 = jnp.concatenate([w_ak, w_bkv, w_ik, zpad(LANES - IDX_DIM)], axis=1).astype(BF16)
    wt = jnp.concatenate([w_aq, w_av, w_bq, w_iq, w_iw, zpad(BF16_ROWS - IDX_HEADS)],
                         axis=1).T.astype(BF16)
    wg = w_gates.astype(BF16)
    row2 = lambda v: v.reshape(1, -1).astype(F32)
    col2 = lambda v: v.reshape(-1, 1).astype(F32)
    wuk = jnp.transpose(w_uk[0], (1, 0, 2)).astype(BF16)
    wuvt = jnp.transpose(w_uv[0], (1, 2, 0)).astype(BF16)
    proj_w = (row2(attn_norm_g[0]), w1, wt, col2(q_norm_g[0]), w_uq[0].T.astype(BF16), wuk,
              row2(kv_norm_g[0]), row2(idx_k_ln_g[0]), row2(idx_k_ln_b[0]))

    bias_a, bnear, bmeta = _bias_call(rel_bias, far_bkt)

    x2 = x.reshape(b * s, d)
    aq_t, ak, av_t, qabs, ckv, iqs, ik, iw = _proj_call(x2, TOK_TILE, TQ, *proj_w)
    meta_blk = jnp.concatenate([jnp.zeros((PAD, d), x.dtype), meta_tokens.astype(x.dtype)], axis=0)
    _, mak, mav_t, _, mckv, _, mik, _ = _proj_call(meta_blk, BLOCK, BLOCK, *proj_w)

    o_a = _swa_call(sinks[0], aq_t, ak, av_t, mak, mav_t, bias_a, b)
    o_b = _mla_call(iqs, iw, qabs, ik.reshape(b, s, -1), ckv.reshape(b, s, -1), mik[PAD:], mckv[PAD:],
                    bnear, bmeta, wuvt, topk)

    y = _out_call(x2, o_a.reshape(b * s, -1), o_b.reshape(b * s, -1), row2(attn_norm_g[0]), wg,
                  row2(b_gates[0]), w_branch_a[0].astype(BF16), w_branch_b[0].astype(BF16),
                  w_out[0].astype(BF16), row2(ffn_norm_g[0]), w_ffn_gate[0].astype(BF16),
                  w_ffn_up[0].astype(BF16), w_ffn_down[0].astype(BF16), row2(final_norm_g))
    return y.reshape(b, s, d)
```

```python
import functools
import math

import numpy as np
import jax
import jax.numpy as jnp
from jax import lax
from jax.experimental import pallas as pl
from jax.experimental.pallas import tpu as pltpu

N_META = 16
BLOCK = 128
PAD = BLOCK - N_META
WINDOW = 128
A_HEADS = 8
A_KV_HEADS = 2
A_HEAD_DIM = 64
B_HEADS = 8
B_HEAD_DIM = 64
B_Q_RANK = 256
B_KV_RANK = 128
IDX_HEADS = 4
IDX_DIM = 64
TOPK_MAX = 256
N_BUCKETS = 32
MAX_DISTANCE = 128
EPS = 1e-6
NEG = -1e30
MAX_FLOOR = -3.0e38
INT_MIN = -(2 ** 31)
LOG2E = math.log2(math.e)
I16_MIN = -(2 ** 15)
I16_MAX = 2 ** 15 - 1

LANES = 128
SUBLANES = 8
BF16_ROWS = 16
MXU_DIM = 256
VMEM_LIMIT_BYTES = 56 * 1024 * 1024

TOK_TILE = 512
TQ = 256
FFN_CHUNKS = 2
SWA_BLOCKS = 8

C_AK = 0
C_BKV = C_AK + A_KV_HEADS * A_HEAD_DIM
C_IK = C_BKV + B_KV_RANK
C_END = C_IK + LANES
R_AQ = 0
R_AV = R_AQ + A_HEADS * A_HEAD_DIM
R_BQ = R_AV + A_KV_HEADS * A_HEAD_DIM
R_IQ = R_BQ + B_Q_RANK
R_IW = R_IQ + IDX_HEADS * IDX_DIM
R_END = R_IW + BF16_ROWS

F32 = jnp.float32
BF16 = jnp.bfloat16
NT_DIMS = (((1,), (1,)), ((), ()))


def _t5_bucket_np(dist):
    dist = np.asarray(dist, np.int64)
    max_exact = N_BUCKETS // 2
    d = np.maximum(dist, 1).astype(np.float32)
    large = max_exact + (np.log(d / np.float32(max_exact)) / np.float32(math.log(MAX_DISTANCE / max_exact))
                         * np.float32(N_BUCKETS - max_exact)).astype(np.int32)
    large = np.minimum(large, N_BUCKETS - 1)
    return np.where(dist < max_exact, dist, large).astype(np.int32)


def _rms(x, g):
    return x * lax.rsqrt(jnp.mean(x * x, axis=-1, keepdims=True) + EPS) * g


def _tree_reduce(op, x):
    while x.shape[0] > 1:
        half = x.shape[0] // 2
        x = op(x[:half], x[half:])
    return x[0]


def _full_spec(shape):
    nd = len(shape)
    return pl.BlockSpec(shape, lambda *_: (0,) * nd)


def _bias_body(tab_ref, bkt_a_ref, bkt_near_ref, bkt_meta_ref, ba_ref, bnear_ref, bmeta_ref, *, far_bkt):
    def lookup(bkt, col, fill):
        acc = jnp.full(bkt.shape, fill, F32)
        for b in range(N_BUCKETS):
            acc = jnp.where(bkt == b, tab_ref[b, col], acc)
        return acc

    bkt_a = bkt_a_ref[...]
    bkt_near = bkt_near_ref[...]
    grp = A_HEADS // A_KV_HEADS
    for h in range(A_HEADS):
        ba_ref[h // grp, :, (h % grp) * BLOCK:(h % grp + 1) * BLOCK] = lookup(bkt_a, h, -jnp.inf) * LOG2E
    for h in range(B_HEADS):
        col = A_HEADS + h
        bnear_ref[h, 0:TQ, :] = jnp.zeros((TQ, TQ), F32)
        bnear_ref[h, TQ:3 * TQ, :] = (lookup(bkt_near, col, 0.0) - tab_ref[far_bkt, col]) * LOG2E
        bmeta_ref[h] = (lookup(bkt_meta_ref[...], col, 0.0) - tab_ref[far_bkt, col]) * LOG2E


def _bias_call(rel_bias, far_bkt):
    k = np.arange(2 * BLOCK)[:, None]
    q = np.arange(BLOCK)[None, :]
    dist = q + BLOCK - k
    bkt_a = np.where((dist >= 0) & (dist < WINDOW), _t5_bucket_np(np.maximum(dist, 0)), -1).astype(np.int32)
    k = np.arange(2 * TQ)[:, None]
    q = np.arange(TQ)[None, :]
    bkt_near = _t5_bucket_np(np.maximum(q + TQ - k, 0))
    k = np.arange(N_META)[:, None]
    bkt_meta = _t5_bucket_np(q + N_META - k)
    vmem = pl.BlockSpec(memory_space=pltpu.VMEM)
    grp = A_HEADS // A_KV_HEADS
    return pl.pallas_call(
        functools.partial(_bias_body, far_bkt=far_bkt),
        out_shape=(jax.ShapeDtypeStruct((A_KV_HEADS, 2 * BLOCK, grp * BLOCK), F32),
                   jax.ShapeDtypeStruct((B_HEADS, 3 * TQ, TQ), F32),
                   jax.ShapeDtypeStruct((B_HEADS, N_META, TQ), F32)),
        in_specs=[pl.BlockSpec(memory_space=pltpu.SMEM), vmem, vmem, vmem],
        out_specs=(vmem, vmem, vmem),
        name="bias_tables",
    )(rel_bias, jnp.asarray(bkt_a), jnp.asarray(bkt_near), jnp.asarray(bkt_meta))


def _proj_body(x_ref, g_ref, w1_ref, wt_ref, qg_ref, wuq_ref, wuk_ref, kvg_ref, ikg_ref, ikb_ref,
               aq_ref, ak_ref, av_ref, qabs_ref, ckv_ref, iqs_ref, ik_ref, iw_ref, *, tq):
    tm = x_ref.shape[0]
    nblk = tm // tq
    nblk_a = tm // BLOCK
    grp = A_HEADS // A_KV_HEADS
    hn = _rms(x_ref[...], g_ref[...]).astype(BF16)

    def proj(lo, hi):
        return jnp.dot(hn, w1_ref[:, lo:hi], preferred_element_type=F32)

    ak_ref[...] = proj(C_AK, C_BKV).astype(BF16)

    feat_t = lax.dot_general(wt_ref[...], hn, NT_DIMS, preferred_element_type=F32)
    aq_t = (feat_t[R_AQ:R_AV] * (A_HEAD_DIM ** -0.5 * LOG2E)).astype(BF16)
    for j in range(nblk_a):
        tok = slice(j * BLOCK, (j + 1) * BLOCK)
        for h in range(A_HEADS):
            aq_ref[j, h // grp, :, (h % grp) * BLOCK:(h % grp + 1) * BLOCK] = \
                aq_t[h * A_HEAD_DIM:(h + 1) * A_HEAD_DIM, tok]
    av_t = feat_t[R_AV:R_BQ].astype(BF16)
    for j in range(nblk_a):
        for kvh in range(A_KV_HEADS):
            av_ref[j, kvh] = av_t[kvh * A_HEAD_DIM:(kvh + 1) * A_HEAD_DIM, j * BLOCK:(j + 1) * BLOCK]

    bq_t = feat_t[R_BQ:R_IQ]
    qn_t = bq_t * lax.rsqrt(jnp.mean(bq_t * bq_t, axis=0, keepdims=True) + EPS) * qg_ref[...]
    q_t = jnp.dot(wuq_ref[...], qn_t.astype(BF16), preferred_element_type=F32).astype(BF16)
    for h in range(B_HEADS):
        qa_t = jnp.dot(wuk_ref[h], q_t[h * B_HEAD_DIM:(h + 1) * B_HEAD_DIM], preferred_element_type=F32)
        qa_t = (qa_t * (B_HEAD_DIM ** -0.5 * LOG2E)).astype(BF16)
        for j in range(nblk):
            qabs_ref[j, h] = qa_t[:, j * tq:(j + 1) * tq]
    iq_t = feat_t[R_IQ:R_IW].astype(BF16)
    for h in range(IDX_HEADS):
        for j in range(nblk):
            iqs_ref[j, h] = iq_t[h * IDX_DIM:(h + 1) * IDX_DIM, j * tq:(j + 1) * tq]

    ckv_ref[...] = _rms(proj(C_BKV, C_IK), kvg_ref[...]).astype(BF16)

    iw_t = feat_t[R_IW:R_IW + SUBLANES] * ((IDX_HEADS * IDX_DIM) ** -0.5)
    for j in range(nblk):
        iw_ref[j] = iw_t[:, j * tq:(j + 1) * tq]

    ik = proj(C_IK, C_END)[:, :IDX_DIM]
    mu = jnp.mean(ik, axis=-1, keepdims=True)
    xc = ik - mu
    var = jnp.mean(xc * xc, axis=-1, keepdims=True)
    ik_ref[...] = (xc * lax.rsqrt(var + EPS) * ikg_ref[...] + ikb_ref[...]).astype(BF16)


def _proj_call(x2, tm, tq, *weights):
    n, d = x2.shape
    grid = (n // tm,)
    row = lambda i: (i, 0)
    blk4 = lambda i: (i, 0, 0, 0)
    grp = A_HEADS // A_KV_HEADS
    out_shape = (
        jax.ShapeDtypeStruct((n // BLOCK, A_KV_HEADS, A_HEAD_DIM, grp * BLOCK), BF16),
        jax.ShapeDtypeStruct((n, A_KV_HEADS * A_HEAD_DIM), BF16),
        jax.ShapeDtypeStruct((n // BLOCK, A_KV_HEADS, A_HEAD_DIM, BLOCK), BF16),
        jax.ShapeDtypeStruct((n // tq, B_HEADS, B_KV_RANK, tq), BF16),
        jax.ShapeDtypeStruct((n, B_KV_RANK), BF16),
        jax.ShapeDtypeStruct((n // tq, IDX_HEADS, IDX_DIM, tq), BF16),
        jax.ShapeDtypeStruct((n, IDX_DIM), BF16),
        jax.ShapeDtypeStruct((n // tq, SUBLANES, tq), F32),
    )
    out_specs = (
        pl.BlockSpec((tm // BLOCK, A_KV_HEADS, A_HEAD_DIM, grp * BLOCK), blk4),
        pl.BlockSpec((tm, A_KV_HEADS * A_HEAD_DIM), row),
        pl.BlockSpec((tm // BLOCK, A_KV_HEADS, A_HEAD_DIM, BLOCK), blk4),
        pl.BlockSpec((tm // tq, B_HEADS, B_KV_RANK, tq), blk4),
        pl.BlockSpec((tm, B_KV_RANK), row),
        pl.BlockSpec((tm // tq, IDX_HEADS, IDX_DIM, tq), blk4),
        pl.BlockSpec((tm, IDX_DIM), row),
        pl.BlockSpec((tm // tq, SUBLANES, tq), lambda i: (i, 0, 0)),
    )
    in_specs = [pl.BlockSpec((tm, d), row)] + [_full_spec(w.shape) for w in weights]
    return pl.pallas_call(
        functools.partial(_proj_body, tq=tq),
        grid=grid, in_specs=in_specs, out_specs=out_specs, out_shape=out_shape,
        compiler_params=pltpu.CompilerParams(dimension_semantics=("arbitrary",),
                                             vmem_limit_bytes=VMEM_LIMIT_BYTES),
        name="in_proj",
    )(x2, *weights)


def _swa_body(sinks_ref, aq_ref, kcur_ref, kprev_ref, kmeta_ref, vcur_ref, vprev_ref, vmeta_ref, bias_ref, o_ref):
    n = pl.program_id(1)
    first = n == 0
    grp = A_HEADS // A_KV_HEADS
    width = grp * BLOCK
    kall = jnp.concatenate([jnp.where(first, kmeta_ref[...], kprev_ref[0]), kcur_ref[0]], axis=0)
    vall = [jnp.concatenate([jnp.where(first, vmeta_ref[0, kvh], vprev_ref[0, kvh])]
                            + [vcur_ref[j, kvh] for j in range(SWA_BLOCKS)], axis=1)
            for kvh in range(A_KV_HEADS)]
    ntile = 2 * BLOCK // SUBLANES
    key_row = (lax.broadcasted_iota(jnp.int32, (ntile, SUBLANES, width), 0) * SUBLANES
               + lax.broadcasted_iota(jnp.int32, (ntile, SUBLANES, width), 1))
    pad_row = first & (key_row < PAD)
    lane_head = lax.broadcasted_iota(jnp.int32, (1, width), 1) // BLOCK
    ones = jnp.ones((BF16_ROWS, 2 * BLOCK), BF16)
    sinks = []
    for kvh in range(A_KV_HEADS):
        sink = jnp.zeros((1, width), F32)
        for g in range(grp):
            sink = jnp.where(lane_head == g, sinks_ref[kvh * grp + g] * LOG2E, sink)
        sinks.append(sink)
    probs_ids = [(j, kvh) for j in range(SWA_BLOCKS) for kvh in range(A_KV_HEADS)]
    scores = [jnp.dot(kall[j * BLOCK:(j + 2) * BLOCK, kvh * A_HEAD_DIM:(kvh + 1) * A_HEAD_DIM], aq_ref[j, kvh],
                      preferred_element_type=F32) for j, kvh in probs_ids]
    maxes, probs = [], []
    for i, (j, kvh) in enumerate(probs_ids):
        s = (scores[i] + bias_ref[kvh]).reshape(ntile, SUBLANES, width)
        if j == 0:
            s = jnp.where(pad_row, -jnp.inf, s)
        m = jnp.maximum(jnp.max(_tree_reduce(jnp.maximum, s), axis=0, keepdims=True), sinks[kvh])
        maxes.append(m)
        probs.append(jnp.exp2(s - m[None]).reshape(2 * BLOCK, width).astype(BF16))
    pvs = [jnp.dot(jnp.concatenate([vall[kvh][:, j * BLOCK:(j + 2) * BLOCK], ones], axis=0), probs[i],
                   preferred_element_type=F32) for i, (j, kvh) in enumerate(probs_ids)]
    for j in range(SWA_BLOCKS):
        heads_t = []
        for kvh in range(A_KV_HEADS):
            i = j * A_KV_HEADS + kvh
            den = pvs[i][A_HEAD_DIM:A_HEAD_DIM + 1] + jnp.exp2(sinks[kvh] - maxes[i])
            o_t = pvs[i][0:A_HEAD_DIM] * (1.0 / den)
            heads_t += [o_t[:, g * BLOCK:(g + 1) * BLOCK] for g in range(grp)]
        o_ref[0, j * BLOCK:(j + 1) * BLOCK, :] = jnp.concatenate(heads_t, axis=0).T.astype(BF16)


def _swa_call(sinks, aq_t, ak, av_t, mak, mav_t, bias_a, b):
    nblocks = aq_t.shape[0]
    nb = nblocks // b
    assert nb % SWA_BLOCKS == 0
    ns = nb // SWA_BLOCKS
    s = nb * BLOCK
    rows = SWA_BLOCKS * BLOCK
    cur4 = lambda i, n: (i * ns + n, 0, 0, 0)
    prev_blk = lambda n: jnp.maximum(n * SWA_BLOCKS - 1, 0)
    kdim = ak.shape[-1]
    ak3 = ak.reshape(b, s, kdim)
    return pl.pallas_call(
        _swa_body,
        grid=(b, ns),
        in_specs=[pl.BlockSpec(memory_space=pltpu.SMEM),
                  pl.BlockSpec((SWA_BLOCKS,) + aq_t.shape[1:], cur4),
                  pl.BlockSpec((1, rows, kdim), lambda i, n: (i, n, 0)),
                  pl.BlockSpec((1, BLOCK, kdim), lambda i, n: (i, prev_blk(n), 0)),
                  _full_spec(mak.shape),
                  pl.BlockSpec((SWA_BLOCKS,) + av_t.shape[1:], cur4),
                  pl.BlockSpec((1,) + av_t.shape[1:], lambda i, n: (i * nb + prev_blk(n), 0, 0, 0)),
                  _full_spec(mav_t.shape), _full_spec(bias_a.shape)],
        out_specs=pl.BlockSpec((1, rows, A_HEADS * A_HEAD_DIM), lambda i, n: (i, n, 0)),
        out_shape=jax.ShapeDtypeStruct((b, s, A_HEADS * A_HEAD_DIM), BF16),
        compiler_params=pltpu.CompilerParams(dimension_semantics=("arbitrary", "arbitrary"),
                                             vmem_limit_bytes=VMEM_LIMIT_BYTES),
        name="swa_sink_attention",
    )(sinks, aq_t, ak3, ak3, mak, av_t, av_t, mav_t, bias_a)


def _mla_body(iqs_ref, iw_ref, qabs_ref, ik_ref, ckv_ref, mik_ref, mckv_ref, mvext_ref, bnear_ref, bmeta_ref,
              wuvt_ref, tril_ref, mtril_ref, o_ref,
              key_sc, khi_sc, klo_sc, s0_sc, s1_sc, mb0_sc, mb1_sc, al0_sc, al1_sc, m_sc, acc_sc, tie_sc, *, topk):
    m_blk = pl.program_id(1)
    first = m_blk == 0
    nchunks = m_blk + 1
    groups = TQ // SUBLANES
    acc_rows = B_KV_RANK + BF16_ROWS
    wrow = [iw_ref[0, h:h + 1, :] for h in range(IDX_HEADS)]

    sub_pos = (lax.broadcasted_iota(jnp.int32, (groups, SUBLANES, TQ), 0) * SUBLANES
               + lax.broadcasted_iota(jnp.int32, (groups, SUBLANES, TQ), 1))
    qpos = m_blk * TQ + lax.broadcasted_iota(jnp.int32, (groups, SUBLANES, TQ), 2)

    def rows_of(c):
        return pl.ds(pl.multiple_of(c * TQ, TQ), TQ)

    def tiles(x):
        return x.reshape(groups, SUBLANES, x.shape[-1])

    def sort_key(sc):
        bits = lax.bitcast_convert_type(sc, jnp.int32)
        return jnp.where(bits < 0, INT_MIN - bits, bits)

    def index_score(ik_c):
        sc = None
        for h in range(IDX_HEADS):
            logits = jnp.dot(ik_c, iqs_ref[0, h], preferred_element_type=F32)
            term = jnp.maximum(logits, 0.0) * wrow[h]
            sc = term if sc is None else sc + term
        return sort_key(sc)

    def digits(key):
        return jnp.right_shift(key, 16).astype(jnp.int16), (key ^ 0x8000).astype(jnp.int16)

    def score_chunk(c, carry):
        adm = c * TQ + sub_pos <= qpos
        key = jnp.where(adm, tiles(index_score(ik_ref[0, rows_of(c), :])), INT_MIN).reshape(TQ, TQ)
        key_sc[rows_of(c), :] = key
        khi_sc[rows_of(c), :], klo_sc[rows_of(c), :] = digits(key)
        return carry

    key_m = index_score(mik_ref[...])
    khi_m, klo_m = digits(key_m)

    def score_pair(i, carry):
        score_chunk(2 * i, carry)
        return score_chunk(2 * i + 1, carry)

    lax.fori_loop(0, lax.shift_right_logical(nchunks, 1), score_pair, 0)

    @pl.when((nchunks & 1) == 1)
    def _():
        score_chunk(nchunks - 1, 0)

    def count(pred):
        def body(c, part):
            hit = pred(tiles(key_sc[rows_of(c), :]))
            return part + _tree_reduce(jnp.add, jnp.where(hit, 1.0, 0.0))
        meta_hit = pred(key_m.reshape(N_META // SUBLANES, SUBLANES, TQ))
        part = lax.fori_loop(0, nchunks, body, _tree_reduce(jnp.add, jnp.where(meta_hit, 1.0, 0.0)))
        return jnp.broadcast_to(jnp.sum(part, axis=0, keepdims=True), (SUBLANES, TQ))

    kf = float(topk)

    def tiles16(x):
        return x.reshape(TQ // BF16_ROWS, BF16_ROWS, TQ)

    def count16_ge(ref, meta_digit, cand):
        cand16 = jnp.broadcast_to(cand[0:1], (BF16_ROWS, TQ)).astype(jnp.int16)
        one, zero = jnp.int16(1), jnp.int16(0)

        def hits(c):
            hit = tiles16(ref[rows_of(c), :]) >= cand16[None]
            return _tree_reduce(jnp.add, jnp.where(hit, one, zero))

        def two_chunks(i, part):
            return part + hits(2 * i) + hits(2 * i + 1)
        part = lax.fori_loop(0, lax.shift_right_logical(nchunks, 1), two_chunks,
                             jnp.where(meta_digit >= cand16, one, zero))
        part = lax.cond((nchunks & 1) == 1, lambda p: p + hits(nchunks - 1), lambda p: p, part)
        total = jnp.sum(part.astype(jnp.int32), axis=0, keepdims=True)
        return jnp.broadcast_to(total, (SUBLANES, TQ))

    def digit_search(ref, meta_digit):
        def step(i, t):
            cand = t + jnp.left_shift(jnp.int32(1), 15 - i)
            return jnp.where(count16_ge(ref, meta_digit, cand) >= topk, cand, t)
        return lax.fori_loop(0, 16, step, jnp.full((SUBLANES, TQ), I16_MIN, jnp.int32))

    t_hi = digit_search(khi_sc, khi_m)
    t_hi16 = jnp.broadcast_to(t_hi[0:1], (BF16_ROWS, TQ)).astype(jnp.int16)

    def pinned(hi, lo, t):
        return jnp.where(hi > t, jnp.int16(I16_MAX), jnp.where(hi < t, jnp.int16(I16_MIN), lo))

    def pin_low(c, carry):
        lo = pinned(tiles16(khi_sc[rows_of(c), :]), tiles16(klo_sc[rows_of(c), :]), t_hi16[None])
        klo_sc[rows_of(c), :] = lo.reshape(TQ, TQ)
        return carry

    lax.fori_loop(0, nchunks, pin_low, 0)
    t_lo = digit_search(klo_sc, pinned(khi_m, klo_m, t_hi16))
    thr = t_hi * 65536 + (t_lo - I16_MIN)
    need = kf - count(lambda kp: kp > thr[None])

    def pad_rows(x):
        return jnp.concatenate([x, jnp.zeros((LANES - N_META, TQ), x.dtype)], axis=0)

    thr_m = jnp.broadcast_to(thr[0:1], (N_META, TQ))
    tied_m = key_m == thr_m
    rank_m = jnp.dot(mtril_ref[...], pad_rows(jnp.where(tied_m, 1.0, 0.0).astype(BF16)),
                     preferred_element_type=F32)
    tie_sc[...] = jnp.broadcast_to(rank_m[N_META - 1:N_META], (SUBLANES, TQ))
    sel_m = (key_m > thr_m) | (tied_m & (rank_m <= jnp.broadcast_to(need[0:1], (N_META, TQ))))
    mask_m = jnp.where(sel_m, 0.0, -jnp.inf)
    heads = range(B_HEADS)
    s_m = [jnp.dot(mckv_ref[...], qabs_ref[0, h], preferred_element_type=F32) for h in heads]
    e_m = []
    for h in heads:
        sh = s_m[h] + jnp.where(first, bmeta_ref[h], 0.0) + mask_m
        m_new = jnp.maximum(jnp.max(sh, axis=0, keepdims=True), MAX_FLOOR)
        m_sc[h] = jnp.broadcast_to(m_new, (SUBLANES, TQ))
        e_m.append(pad_rows(jnp.exp2(sh - m_new).astype(BF16)))
    for h in heads:
        acc_sc[h] = jnp.dot(mvext_ref[...], e_m[h], preferred_element_type=F32)

    def score_stage(c, buf):
        s_buf, m_buf, al_buf = buf
        ckv_c = ckv_ref[0, rows_of(c), :]
        kp = tiles(key_sc[rows_of(c), :])
        tied = kp == thr[None]
        tied_b = jnp.where(tied, 1.0, 0.0).reshape(TQ, TQ).astype(BF16)
        rank = tiles(jnp.dot(tril_ref[...], tied_b, preferred_element_type=F32)) + tie_sc[...][None]
        tie_sc[...] = jnp.broadcast_to(rank[groups - 1, SUBLANES - 1:SUBLANES, :], (SUBLANES, TQ))
        sel = ((kp > thr[None]) | (tied & (rank <= need[None]))) & (kp != INT_MIN)
        mask = jnp.where(sel, 0.0, -jnp.inf)
        bias_row0 = pl.multiple_of(jnp.maximum(c - m_blk + 2, 0) * TQ, TQ)
        for h in range(B_HEADS):
            sh = jnp.dot(ckv_c, qabs_ref[0, h], preferred_element_type=F32)
            sh = tiles(sh + bnear_ref[h, pl.ds(bias_row0, TQ), :]) + mask
            s_buf[h] = sh.reshape(TQ, TQ)
            m_prev = m_sc[h]
            mx = jnp.max(_tree_reduce(jnp.maximum, sh), axis=0, keepdims=True)
            m_new = jnp.maximum(m_prev, mx)
            al_buf[h] = jnp.exp2(m_prev - m_new)
            m_buf[h] = m_new
            m_sc[h] = m_new

    def value_stage(c, buf):
        s_buf, m_buf, al_buf = buf
        vt = ckv_ref[0, rows_of(c), :].astype(F32).T.astype(BF16)
        vext = jnp.concatenate([vt, jnp.ones((BF16_ROWS, TQ), BF16)], axis=0)
        for h in range(B_HEADS):
            e = jnp.exp2(tiles(s_buf[h]) - m_buf[h][None]).reshape(TQ, TQ).astype(BF16)
            pv = jnp.dot(vext, e, preferred_element_type=F32)
            acc = acc_sc[h].reshape(acc_rows // SUBLANES, SUBLANES, TQ)
            acc_sc[h] = (acc * al_buf[h][None] + pv.reshape(acc.shape)).reshape(acc_rows, TQ)

    buf0, buf1 = (s0_sc, mb0_sc, al0_sc), (s1_sc, mb1_sc, al1_sc)
    score_stage(0, buf0)

    def chunk_pair(i, carry):
        c = 2 * i + 1
        value_stage(c - 1, buf0)
        score_stage(c, buf1)
        value_stage(c, buf1)
        score_stage(c + 1, buf0)
        return carry

    npairs = (nchunks - 1) // 2
    lax.fori_loop(0, npairs, chunk_pair, 0)
    last = nchunks - 1

    @pl.when(last == 2 * npairs)
    def _():
        value_stage(last, buf0)

    @pl.when(last != 2 * npairs)
    def _():
        value_stage(last - 1, buf0)
        score_stage(last, buf1)
        value_stage(last, buf1)

    outs = []
    for h in range(B_HEADS):
        acc = acc_sc[h]
        lat = (acc[0:B_KV_RANK] * (1.0 / acc[B_KV_RANK:B_KV_RANK + 1])).astype(BF16)
        outs.append(jnp.dot(wuvt_ref[h], lat, preferred_element_type=F32))
    o_ref[0] = jnp.concatenate(outs, axis=0).T.astype(BF16)


def _mla_call(iqs, iw, qabs, ik, ckv, mik, mckv, bnear, bmeta, wuvt, topk):
    b, s, _ = ik.shape
    nq = s // TQ
    acc_rows = B_KV_RANK + BF16_ROWS
    assert N_META == BF16_ROWS, "the meta keys are handled as one packed bf16 tile"
    tril = jnp.asarray(np.tril(np.ones((TQ, TQ), np.float32)), BF16)
    mtril = jnp.asarray(np.pad(np.tril(np.ones((N_META, N_META), np.float32)), ((0, 0), (0, LANES - N_META))), BF16)
    mvext = jnp.pad(jnp.concatenate([mckv.T, jnp.ones((BF16_ROWS, N_META), BF16)], axis=0),
                    ((0, 0), (0, LANES - N_META)))
    consts = (mik, mckv, mvext, bnear, bmeta, wuvt, tril, mtril)
    return pl.pallas_call(
        functools.partial(_mla_body, topk=topk),
        grid=(b, nq),
        in_specs=[pl.BlockSpec((1, IDX_HEADS, IDX_DIM, TQ), lambda i, m: (i * nq + m, 0, 0, 0)),
                  pl.BlockSpec((1, SUBLANES, TQ), lambda i, m: (i * nq + m, 0, 0)),
                  pl.BlockSpec((1, B_HEADS, B_KV_RANK, TQ), lambda i, m: (i * nq + m, 0, 0, 0)),
                  pl.BlockSpec((1, s, IDX_DIM), lambda i, m: (i, 0, 0)),
                  pl.BlockSpec((1, s, B_KV_RANK), lambda i, m: (i, 0, 0))]
                 + [_full_spec(c.shape) for c in consts],
        out_specs=pl.BlockSpec((1, TQ, B_HEADS * B_HEAD_DIM), lambda i, m: (i, m, 0)),
        out_shape=jax.ShapeDtypeStruct((b, s, B_HEADS * B_HEAD_DIM), BF16),
        scratch_shapes=[pltpu.VMEM((s, TQ), jnp.int32),
                        pltpu.VMEM((s, TQ), jnp.int16),
                        pltpu.VMEM((s, TQ), jnp.int16),
                        pltpu.VMEM((B_HEADS, TQ, TQ), F32), pltpu.VMEM((B_HEADS, TQ, TQ), F32),
                        pltpu.VMEM((B_HEADS, SUBLANES, TQ), F32), pltpu.VMEM((B_HEADS, SUBLANES, TQ), F32),
                        pltpu.VMEM((B_HEADS, SUBLANES, TQ), F32), pltpu.VMEM((B_HEADS, SUBLANES, TQ), F32),
                        pltpu.VMEM((B_HEADS, SUBLANES, TQ), F32),
                        pltpu.VMEM((B_HEADS, acc_rows, TQ), F32),
                        pltpu.VMEM((SUBLANES, TQ), F32)],
        compiler_params=pltpu.CompilerParams(dimension_semantics=("arbitrary", "arbitrary"),
                                             vmem_limit_bytes=VMEM_LIMIT_BYTES),
        name="indexer_topk_mla",
    )(iqs, iw, qabs, ik, ckv, *consts)


def _out_body(x_ref, oa_ref, ob_ref, ag_ref, wg_ref, bg_ref, wa_ref, wb_ref, wo_ref, fg_ref,
              wfg_ref, wfu_ref, wfd_ref, ng_ref, y_ref):
    d = x_ref.shape[1]
    x = x_ref[...]
    hn = _rms(x, ag_ref[...]).astype(BF16)

    def gated(o_ref, w_ref, lo):
        gate = jax.nn.sigmoid(jnp.dot(hn, wg_ref[:, lo:lo + d], preferred_element_type=F32) + bg_ref[:, lo:lo + d])
        return gate * jnp.dot(o_ref[...], w_ref[...], preferred_element_type=F32)

    mixed = (gated(oa_ref, wa_ref, 0) + gated(ob_ref, wb_ref, d)).astype(BF16)
    h = x + jnp.dot(mixed, wo_ref[...], preferred_element_type=F32)
    hn2 = _rms(h, fg_ref[...]).astype(BF16)
    tiles_ff = wfg_ref.shape[1] // MXU_DIM
    bounds = [MXU_DIM * ((tiles_ff * i + FFN_CHUNKS - 1) // FFN_CHUNKS) for i in range(FFN_CHUNKS + 1)]
    for lo, hi in zip(bounds[:-1], bounds[1:]):
        g = jnp.dot(hn2, wfg_ref[:, lo:hi], preferred_element_type=F32)
        u = jnp.dot(hn2, wfu_ref[:, lo:hi], preferred_element_type=F32)
        act = (g * jax.nn.sigmoid(g) * u).astype(BF16)
        h = h + jnp.dot(act, wfd_ref[lo:hi, :], preferred_element_type=F32)
    y_ref[...] = _rms(h, ng_ref[...])


def _out_call(x2, oa, ob, ag, wg, bg, wa, wb, wo, fg, wfg, wfu, wfd, ng):
    n, d = x2.shape
    tm = TOK_TILE
    row = lambda i: (i, 0)

    def const_spec(a):
        return pl.BlockSpec(a.shape, lambda i: (0,) * a.ndim, pipeline_mode=pl.Buffered(1))

    consts = [ag, wg, bg, wa, wb, wo, fg, wfg, wfu, wfd, ng]
    return pl.pallas_call(
        _out_body,
        grid=(n // tm,),
        in_specs=[pl.BlockSpec((tm, d), row), pl.BlockSpec((tm, oa.shape[1]), row),
                  pl.BlockSpec((tm, ob.shape[1]), row)] + [const_spec(a) for a in consts],
        out_specs=pl.BlockSpec((tm, d), row),
        out_shape=jax.ShapeDtypeStruct((n, d), F32),
        compiler_params=pltpu.CompilerParams(dimension_semantics=("arbitrary",),
                                             vmem_limit_bytes=VMEM_LIMIT_BYTES),
        name="merge_ffn_norm",
    )(x2, oa, ob, *consts)


def kernel(x, meta_tokens, attn_norm_g, w_in, b_gates, q_norm_g, kv_norm_g, w_uq, w_uk, w_uv, idx_k_ln_g,
           idx_k_ln_b, sinks, rel_bias, w_branch_a, w_branch_b, w_out, ffn_norm_g, w_ffn_gate, w_ffn_up,
           w_ffn_down, final_norm_g):
    b, s, d = x.shape
    assert attn_norm_g.shape[0] == 1, "single-layer block"
    assert s % TQ == 0 and (b * s) % TOK_TILE == 0 and TOK_TILE % TQ == 0
    assert w_ffn_gate.shape[2] % MXU_DIM == 0
    topk = min(TOPK_MAX, s // 4)
    far_bkts = np.unique(_t5_bucket_np(np.arange(BLOCK + 1, s + BLOCK + 1)))
    assert far_bkts.size == 1
    far_bkt = int(far_bkts[0])

    wi = w_in[0]
    widths = (A_HEADS * A_HEAD_DIM, A_KV_HEADS * A_HEAD_DIM, A_KV_HEADS * A_HEAD_DIM, B_Q_RANK, B_KV_RANK,
              IDX_HEADS * IDX_DIM, IDX_DIM, IDX_HEADS, 2 * d)
    starts = np.concatenate([[0], np.cumsum(widths)])
    w_aq, w_ak, w_av, w_bq, w_bkv, w_iq, w_ik, w_iw, w_gates = (wi[:, int(a):int(b_)]
                                                                 for a, b_ in zip(starts[:-1], starts[1:]))
    zpad = lambda k: jnp.zeros((d, k), wi.dtype)
    w1 = jnp.concatenate([w_ak, w_bkv, w_ik, zpad(LANES - IDX_DIM)], axis=1).astype(BF16)
    wt = jnp.concatenate([w_aq, w_av, w_bq, w_iq, w_iw, zpad(BF16_ROWS - IDX_HEADS)],
                         axis=1).T.astype(BF16)
    wg = w_gates.astype(BF16)
    row2 = lambda v: v.reshape(1, -1).astype(F32)
    col2 = lambda v: v.reshape(-1, 1).astype(F32)
    wuk = jnp.transpose(w_uk[0], (1, 0, 2)).astype(BF16)
    wuvt = jnp.transpose(w_uv[0], (1, 2, 0)).astype(BF16)
    proj_w = (row2(attn_norm_g[0]), w1, wt, col2(q_norm_g[0]), w_uq[0].T.astype(BF16), wuk,
              row2(kv_norm_g[0]), row2(idx_k_ln_g[0]), row2(idx_k_ln_b[0]))

    bias_a, bnear, bmeta = _bias_call(rel_bias, far_bkt)

    x2 = x.reshape(b * s, d)
    aq_t, ak, av_t, qabs, ckv, iqs, ik, iw = _proj_call(x2, TOK_TILE, TQ, *proj_w)
    meta_blk = jnp.concatenate([jnp.zeros((PAD, d), x.dtype), meta_tokens.astype(x.dtype)], axis=0)
    _, mak, mav_t, _, mckv, _, mik, _ = _proj_call(meta_blk, BLOCK, BLOCK, *proj_w)

    o_a = _swa_call(sinks[0], aq_t, ak, av_t, mak, mav_t, bias_a, b)
    o_b = _mla_call(iqs, iw, qabs, ik.reshape(b, s, -1), ckv.reshape(b, s, -1), mik[PAD:], mckv[PAD:],
                    bnear, bmeta, wuvt, topk)

    y = _out_call(x2, o_a.reshape(b * s, -1), o_b.reshape(b * s, -1), row2(attn_norm_g[0]), wg,
                  row2(b_gates[0]), w_branch_a[0].astype(BF16), w_branch_b[0].astype(BF16),
                  w_out[0].astype(BF16), row2(ffn_norm_g[0]), w_ffn_gate[0].astype(BF16),
                  w_ffn_up[0].astype(BF16), w_ffn_down[0].astype(BF16), row2(final_norm_g))
    return y.reshape(b, s, d)
```

```python
import functools
import math

import numpy as np
import jax
import jax.numpy as jnp
from jax import lax
from jax.experimental import pallas as pl
from jax.experimental.pallas import tpu as pltpu

N_META = 16
BLOCK = 128
PAD = BLOCK - N_META
WINDOW = 128
A_HEADS = 8
A_KV_HEADS = 2
A_HEAD_DIM = 64
B_HEADS = 8
B_HEAD_DIM = 64
B_Q_RANK = 256
B_KV_RANK = 128
IDX_HEADS = 4
IDX_DIM = 64
TOPK_MAX = 256
N_BUCKETS = 32
MAX_DISTANCE = 128
EPS = 1e-6
NEG = -1e30
MAX_FLOOR = -3.0e38
INT_MIN = -(2 ** 31)
LOG2E = math.log2(math.e)
I16_MIN = -(2 ** 15)
I16_MAX = 2 ** 15 - 1

LANES = 128
SUBLANES = 8
BF16_ROWS = 16
MXU_DIM = 256
VMEM_LIMIT_BYTES = 56 * 1024 * 1024

TOK_TILE = 512
TQ = 256
FFN_CHUNKS = 2
SWA_BLOCKS = 8
MLA_SEQS = 2

C_AK = 0
C_BKV = C_AK + A_KV_HEADS * A_HEAD_DIM
C_IK = C_BKV + B_KV_RANK
C_END = C_IK + LANES
R_AQ = 0
R_AV = R_AQ + A_HEADS * A_HEAD_DIM
R_BQ = R_AV + A_KV_HEADS * A_HEAD_DIM
R_IQ = R_BQ + B_Q_RANK
R_IW = R_IQ + IDX_HEADS * IDX_DIM
R_END = R_IW + BF16_ROWS

F32 = jnp.float32
BF16 = jnp.bfloat16
NT_DIMS = (((1,), (1,)), ((), ()))


def _t5_bucket_np(dist):
    dist = np.asarray(dist, np.int64)
    max_exact = N_BUCKETS // 2
    d = np.maximum(dist, 1).astype(np.float32)
    large = max_exact + (np.log(d / np.float32(max_exact)) / np.float32(math.log(MAX_DISTANCE / max_exact))
                         * np.float32(N_BUCKETS - max_exact)).astype(np.int32)
    large = np.minimum(large, N_BUCKETS - 1)
    return np.where(dist < max_exact, dist, large).astype(np.int32)


def _rms(x, g):
    return x * lax.rsqrt(jnp.mean(x * x, axis=-1, keepdims=True) + EPS) * g


def _tree_reduce(op, x):
    while x.shape[0] > 1:
        half = x.shape[0] // 2
        x = op(x[:half], x[half:])
    return x[0]


def _full_spec(shape):
    nd = len(shape)
    return pl.BlockSpec(shape, lambda *_: (0,) * nd)


def _bias_body(tab_ref, bkt_a_ref, bkt_near_ref, bkt_meta_ref, ba_ref, bnear_ref, bmeta_ref, *, far_bkt):
    def lookup(bkt, col, fill):
        acc = jnp.full(bkt.shape, fill, F32)
        for b in range(N_BUCKETS):
            acc = jnp.where(bkt == b, tab_ref[b, col], acc)
        return acc

    bkt_a = bkt_a_ref[...]
    bkt_near = bkt_near_ref[...]
    grp = A_HEADS // A_KV_HEADS
    for h in range(A_HEADS):
        ba_ref[h // grp, :, (h % grp) * BLOCK:(h % grp + 1) * BLOCK] = lookup(bkt_a, h, -jnp.inf) * LOG2E
    for h in range(B_HEADS):
        col = A_HEADS + h
        bnear_ref[h, 0:TQ, :] = jnp.zeros((TQ, TQ), F32)
        bnear_ref[h, TQ:3 * TQ, :] = (lookup(bkt_near, col, 0.0) - tab_ref[far_bkt, col]) * LOG2E
        bmeta_ref[h] = (lookup(bkt_meta_ref[...], col, 0.0) - tab_ref[far_bkt, col]) * LOG2E


def _bias_call(rel_bias, far_bkt):
    k = np.arange(2 * BLOCK)[:, None]
    q = np.arange(BLOCK)[None, :]
    dist = q + BLOCK - k
    bkt_a = np.where((dist >= 0) & (dist < WINDOW), _t5_bucket_np(np.maximum(dist, 0)), -1).astype(np.int32)
    k = np.arange(2 * TQ)[:, None]
    q = np.arange(TQ)[None, :]
    bkt_near = _t5_bucket_np(np.maximum(q + TQ - k, 0))
    k = np.arange(N_META)[:, None]
    bkt_meta = _t5_bucket_np(q + N_META - k)
    vmem = pl.BlockSpec(memory_space=pltpu.VMEM)
    grp = A_HEADS // A_KV_HEADS
    return pl.pallas_call(
        functools.partial(_bias_body, far_bkt=far_bkt),
        out_shape=(jax.ShapeDtypeStruct((A_KV_HEADS, 2 * BLOCK, grp * BLOCK), F32),
                   jax.ShapeDtypeStruct((B_HEADS, 3 * TQ, TQ), F32),
                   jax.ShapeDtypeStruct((B_HEADS, N_META, TQ), F32)),
        in_specs=[pl.BlockSpec(memory_space=pltpu.SMEM), vmem, vmem, vmem],
        out_specs=(vmem, vmem, vmem),
        name="bias_tables",
    )(rel_bias, jnp.asarray(bkt_a), jnp.asarray(bkt_near), jnp.asarray(bkt_meta))


def _proj_body(x_ref, g_ref, w1_ref, wt_ref, qg_ref, wuq_ref, wuk_ref, kvg_ref, ikg_ref, ikb_ref,
               aq_ref, ak_ref, av_ref, qabs_ref, ckv_ref, iqs_ref, ik_ref, iw_ref, *, tq):
    tm = x_ref.shape[0]
    nblk = tm // tq
    nblk_a = tm // BLOCK
    grp = A_HEADS // A_KV_HEADS
    hn = _rms(x_ref[...], g_ref[...]).astype(BF16)

    def proj(lo, hi):
        return jnp.dot(hn, w1_ref[:, lo:hi], preferred_element_type=F32)

    ak_ref[...] = proj(C_AK, C_BKV).astype(BF16)

    feat_t = lax.dot_general(wt_ref[...], hn, NT_DIMS, preferred_element_type=F32)
    aq_t = (feat_t[R_AQ:R_AV] * (A_HEAD_DIM ** -0.5 * LOG2E)).astype(BF16)
    for j in range(nblk_a):
        tok = slice(j * BLOCK, (j + 1) * BLOCK)
        for h in range(A_HEADS):
            aq_ref[j, h // grp, :, (h % grp) * BLOCK:(h % grp + 1) * BLOCK] = \
                aq_t[h * A_HEAD_DIM:(h + 1) * A_HEAD_DIM, tok]
    av_t = feat_t[R_AV:R_BQ].astype(BF16)
    for j in range(nblk_a):
        for kvh in range(A_KV_HEADS):
            av_ref[j, kvh] = av_t[kvh * A_HEAD_DIM:(kvh + 1) * A_HEAD_DIM, j * BLOCK:(j + 1) * BLOCK]

    bq_t = feat_t[R_BQ:R_IQ]
    qn_t = bq_t * lax.rsqrt(jnp.mean(bq_t * bq_t, axis=0, keepdims=True) + EPS) * qg_ref[...]
    q_t = jnp.dot(wuq_ref[...], qn_t.astype(BF16), preferred_element_type=F32).astype(BF16)
    for h in range(B_HEADS):
        qa_t = jnp.dot(wuk_ref[h], q_t[h * B_HEAD_DIM:(h + 1) * B_HEAD_DIM], preferred_element_type=F32)
        qa_t = (qa_t * (B_HEAD_DIM ** -0.5 * LOG2E)).astype(BF16)
        for j in range(nblk):
            qabs_ref[j, h] = qa_t[:, j * tq:(j + 1) * tq]
    iq_t = feat_t[R_IQ:R_IW].astype(BF16)
    for h in range(IDX_HEADS):
        for j in range(nblk):
            iqs_ref[j, h] = iq_t[h * IDX_DIM:(h + 1) * IDX_DIM, j * tq:(j + 1) * tq]

    ckv_ref[...] = _rms(proj(C_BKV, C_IK), kvg_ref[...]).astype(BF16)

    iw_t = feat_t[R_IW:R_IW + SUBLANES] * ((IDX_HEADS * IDX_DIM) ** -0.5)
    for j in range(nblk):
        iw_ref[j] = iw_t[:, j * tq:(j + 1) * tq]

    ik = proj(C_IK, C_END)[:, :IDX_DIM]
    mu = jnp.mean(ik, axis=-1, keepdims=True)
    xc = ik - mu
    var = jnp.mean(xc * xc, axis=-1, keepdims=True)
    ik_ref[...] = (xc * lax.rsqrt(var + EPS) * ikg_ref[...] + ikb_ref[...]).astype(BF16)


def _proj_call(x2, tm, tq, *weights):
    n, d = x2.shape
    grid = (n // tm,)
    row = lambda i: (i, 0)
    blk4 = lambda i: (i, 0, 0, 0)
    grp = A_HEADS // A_KV_HEADS
    out_shape = (
        jax.ShapeDtypeStruct((n // BLOCK, A_KV_HEADS, A_HEAD_DIM, grp * BLOCK), BF16),
        jax.ShapeDtypeStruct((n, A_KV_HEADS * A_HEAD_DIM), BF16),
        jax.ShapeDtypeStruct((n // BLOCK, A_KV_HEADS, A_HEAD_DIM, BLOCK), BF16),
        jax.ShapeDtypeStruct((n // tq, B_HEADS, B_KV_RANK, tq), BF16),
        jax.ShapeDtypeStruct((n, B_KV_RANK), BF16),
        jax.ShapeDtypeStruct((n // tq, IDX_HEADS, IDX_DIM, tq), BF16),
        jax.ShapeDtypeStruct((n, IDX_DIM), BF16),
        jax.ShapeDtypeStruct((n // tq, SUBLANES, tq), F32),
    )
    out_specs = (
        pl.BlockSpec((tm // BLOCK, A_KV_HEADS, A_HEAD_DIM, grp * BLOCK), blk4),
        pl.BlockSpec((tm, A_KV_HEADS * A_HEAD_DIM), row),
        pl.BlockSpec((tm // BLOCK, A_KV_HEADS, A_HEAD_DIM, BLOCK), blk4),
        pl.BlockSpec((tm // tq, B_HEADS, B_KV_RANK, tq), blk4),
        pl.BlockSpec((tm, B_KV_RANK), row),
        pl.BlockSpec((tm // tq, IDX_HEADS, IDX_DIM, tq), blk4),
        pl.BlockSpec((tm, IDX_DIM), row),
        pl.BlockSpec((tm // tq, SUBLANES, tq), lambda i: (i, 0, 0)),
    )
    in_specs = [pl.BlockSpec((tm, d), row)] + [_full_spec(w.shape) for w in weights]
    return pl.pallas_call(
        functools.partial(_proj_body, tq=tq),
        grid=grid, in_specs=in_specs, out_specs=out_specs, out_shape=out_shape,
        compiler_params=pltpu.CompilerParams(dimension_semantics=("arbitrary",),
                                             vmem_limit_bytes=VMEM_LIMIT_BYTES),
        name="in_proj",
    )(x2, *weights)


def _swa_body(sinks_ref, aq_ref, kcur_ref, kprev_ref, kmeta_ref, vcur_ref, vprev_ref, vmeta_ref, bias_ref, o_ref):
    n = pl.program_id(1)
    first = n == 0
    grp = A_HEADS // A_KV_HEADS
    width = grp * BLOCK
    kall = jnp.concatenate([jnp.where(first, kmeta_ref[...], kprev_ref[0]), kcur_ref[0]], axis=0)
    vall = [jnp.concatenate([jnp.where(first, vmeta_ref[0, kvh], vprev_ref[0, kvh])]
                            + [vcur_ref[j, kvh] for j in range(SWA_BLOCKS)], axis=1)
            for kvh in range(A_KV_HEADS)]
    ntile = 2 * BLOCK // SUBLANES
    key_row = (lax.broadcasted_iota(jnp.int32, (ntile, SUBLANES, width), 0) * SUBLANES
               + lax.broadcasted_iota(jnp.int32, (ntile, SUBLANES, width), 1))
    pad_row = first & (key_row < PAD)
    lane_head = lax.broadcasted_iota(jnp.int32, (1, width), 1) // BLOCK
    ones = jnp.ones((BF16_ROWS, 2 * BLOCK), BF16)
    sinks = []
    for kvh in range(A_KV_HEADS):
        sink = jnp.zeros((1, width), F32)
        for g in range(grp):
            sink = jnp.where(lane_head == g, sinks_ref[kvh * grp + g] * LOG2E, sink)
        sinks.append(sink)
    probs_ids = [(j, kvh) for j in range(SWA_BLOCKS) for kvh in range(A_KV_HEADS)]
    scores = [jnp.dot(kall[j * BLOCK:(j + 2) * BLOCK, kvh * A_HEAD_DIM:(kvh + 1) * A_HEAD_DIM], aq_ref[j, kvh],
                      preferred_element_type=F32) for j, kvh in probs_ids]
    maxes, probs = [], []
    for i, (j, kvh) in enumerate(probs_ids):
        s = (scores[i] + bias_ref[kvh]).reshape(ntile, SUBLANES, width)
        if j == 0:
            s = jnp.where(pad_row, -jnp.inf, s)
        m = jnp.maximum(jnp.max(_tree_reduce(jnp.maximum, s), axis=0, keepdims=True), sinks[kvh])
        maxes.append(m)
        probs.append(jnp.exp2(s - m[None]).reshape(2 * BLOCK, width).astype(BF16))
    pvs = [jnp.dot(jnp.concatenate([vall[kvh][:, j * BLOCK:(j + 2) * BLOCK], ones], axis=0), probs[i],
                   preferred_element_type=F32) for i, (j, kvh) in enumerate(probs_ids)]
    for j in range(SWA_BLOCKS):
        heads_t = []
        for kvh in range(A_KV_HEADS):
            i = j * A_KV_HEADS + kvh
            den = pvs[i][A_HEAD_DIM:A_HEAD_DIM + 1] + jnp.exp2(sinks[kvh] - maxes[i])
            o_t = pvs[i][0:A_HEAD_DIM] * (1.0 / den)
            heads_t += [o_t[:, g * BLOCK:(g + 1) * BLOCK] for g in range(grp)]
        o_ref[0, j * BLOCK:(j + 1) * BLOCK, :] = jnp.concatenate(heads_t, axis=0).T.astype(BF16)


def _swa_call(sinks, aq_t, ak, av_t, mak, mav_t, bias_a, b):
    nblocks = aq_t.shape[0]
    nb = nblocks // b
    assert nb % SWA_BLOCKS == 0
    ns = nb // SWA_BLOCKS
    s = nb * BLOCK
    rows = SWA_BLOCKS * BLOCK
    cur4 = lambda i, n: (i * ns + n, 0, 0, 0)
    prev_blk = lambda n: jnp.maximum(n * SWA_BLOCKS - 1, 0)
    kdim = ak.shape[-1]
    ak3 = ak.reshape(b, s, kdim)
    return pl.pallas_call(
        _swa_body,
        grid=(b, ns),
        in_specs=[pl.BlockSpec(memory_space=pltpu.SMEM),
                  pl.BlockSpec((SWA_BLOCKS,) + aq_t.shape[1:], cur4),
                  pl.BlockSpec((1, rows, kdim), lambda i, n: (i, n, 0)),
                  pl.BlockSpec((1, BLOCK, kdim), lambda i, n: (i, prev_blk(n), 0)),
                  _full_spec(mak.shape),
                  pl.BlockSpec((SWA_BLOCKS,) + av_t.shape[1:], cur4),
                  pl.BlockSpec((1,) + av_t.shape[1:], lambda i, n: (i * nb + prev_blk(n), 0, 0, 0)),
                  _full_spec(mav_t.shape), _full_spec(bias_a.shape)],
        out_specs=pl.BlockSpec((1, rows, A_HEADS * A_HEAD_DIM), lambda i, n: (i, n, 0)),
        out_shape=jax.ShapeDtypeStruct((b, s, A_HEADS * A_HEAD_DIM), BF16),
        compiler_params=pltpu.CompilerParams(dimension_semantics=("arbitrary", "arbitrary"),
                                             vmem_limit_bytes=VMEM_LIMIT_BYTES),
        name="swa_sink_attention",
    )(sinks, aq_t, ak3, ak3, mak, av_t, av_t, mav_t, bias_a)


def _mla_body(*refs, topk):
    it = iter(refs)
    take = lambda n: [next(it) for _ in range(n)]
    iqs_refs, iw_refs, qabs_refs, ik_refs, ckv_refs = (take(MLA_SEQS) for _ in range(5))
    mik_ref, mckv_ref, mvext_ref, bnear_ref, bmeta_ref, wuvt_ref, tril_ref, mtril_ref = take(8)
    o_all, = take(1)
    o_refs = [o_all.at[0, p] for p in range(MLA_SEQS)]
    scratch = take(12)
    seqs = range(MLA_SEQS)
    m_blk = pl.program_id(1)
    first = m_blk == 0
    nchunks = m_blk + 1
    groups = TQ // SUBLANES
    acc_rows = B_KV_RANK + BF16_ROWS
    kf = float(topk)

    sub_pos = (lax.broadcasted_iota(jnp.int32, (groups, SUBLANES, TQ), 0) * SUBLANES
               + lax.broadcasted_iota(jnp.int32, (groups, SUBLANES, TQ), 1))
    qpos = m_blk * TQ + lax.broadcasted_iota(jnp.int32, (groups, SUBLANES, TQ), 2)

    def rows_of(c):
        return pl.ds(pl.multiple_of(c * TQ, TQ), TQ)

    def tiles(x):
        return x.reshape(groups, SUBLANES, x.shape[-1])

    def sort_key(sc):
        bits = lax.bitcast_convert_type(sc, jnp.int32)
        return jnp.where(bits < 0, INT_MIN - bits, bits)

    iqs_ref, iw_ref, qabs_ref, ik_ref, ckv_ref, o_ref = iqs_refs, iw_refs, qabs_refs, ik_refs, ckv_refs, o_refs
    (key_sc, khi_sc, klo_sc, s0_sc, s1_sc, mb0_sc, mb1_sc, al0_sc, al1_sc, m_sc, acc_sc, tie_sc) = (
        [r.at[p] for p in seqs] for r in scratch)
    wrow = [[iw_ref[p][0, h:h + 1, :] for h in range(IDX_HEADS)] for p in seqs]

    def index_score(p, ik_c):
        sc = None
        for h in range(IDX_HEADS):
            logits = jnp.dot(ik_c, iqs_ref[p][0, h], preferred_element_type=F32)
            term = jnp.maximum(logits, 0.0) * wrow[p][h]
            sc = term if sc is None else sc + term
        return sort_key(sc)

    def digits(key):
        return jnp.right_shift(key, 16).astype(jnp.int16), (key ^ 0x8000).astype(jnp.int16)

    def score_chunk(c):
        adm = c * TQ + sub_pos <= qpos
        for p in seqs:
            key = jnp.where(adm, tiles(index_score(p, ik_ref[p][0, rows_of(c), :])), INT_MIN).reshape(TQ, TQ)
            key_sc[p][rows_of(c), :] = key
            khi_sc[p][rows_of(c), :], klo_sc[p][rows_of(c), :] = digits(key)

    key_m = [index_score(p, mik_ref[...]) for p in seqs]
    khi_m, klo_m = zip(*[digits(k) for k in key_m])

    def score_pair(i, carry):
        score_chunk(2 * i)
        score_chunk(2 * i + 1)
        return carry

    lax.fori_loop(0, lax.shift_right_logical(nchunks, 1), score_pair, 0)

    @pl.when((nchunks & 1) == 1)
    def _():
        score_chunk(nchunks - 1)

    def count_gt(thrs):
        def hits(p, kp):
            return _tree_reduce(jnp.add, jnp.where(kp > thrs[p][None], 1.0, 0.0))

        def body(c, parts):
            return tuple(parts[p] + hits(p, tiles(key_sc[p][rows_of(c), :])) for p in seqs)
        parts = lax.fori_loop(0, nchunks, body,
                              tuple(hits(p, key_m[p].reshape(N_META // SUBLANES, SUBLANES, TQ)) for p in seqs))
        return [jnp.broadcast_to(jnp.sum(part, axis=0, keepdims=True), (SUBLANES, TQ)) for part in parts]

    def tiles16(x):
        return x.reshape(TQ // BF16_ROWS, BF16_ROWS, TQ)

    def count16_ge(refs16, meta_digits, cands):
        cand16 = [jnp.broadcast_to(cands[p][0:1], (BF16_ROWS, TQ)).astype(jnp.int16) for p in seqs]
        one, zero = jnp.int16(1), jnp.int16(0)

        def hits(p, c):
            hit = tiles16(refs16[p][rows_of(c), :]) >= cand16[p][None]
            return _tree_reduce(jnp.add, jnp.where(hit, one, zero))

        def two_chunks(i, parts):
            return tuple(parts[p] + hits(p, 2 * i) + hits(p, 2 * i + 1) for p in seqs)
        parts = lax.fori_loop(0, lax.shift_right_logical(nchunks, 1), two_chunks,
                              tuple(jnp.where(meta_digits[p] >= cand16[p], one, zero) for p in seqs))
        parts = lax.cond((nchunks & 1) == 1,
                         lambda ps: tuple(ps[p] + hits(p, nchunks - 1) for p in seqs), lambda ps: ps, parts)
        return [jnp.broadcast_to(jnp.sum(part.astype(jnp.int32), axis=0, keepdims=True), (SUBLANES, TQ))
                for part in parts]

    def digit_search(refs16, meta_digits):
        def step(i, ts):
            cands = [t + jnp.left_shift(jnp.int32(1), 15 - i) for t in ts]
            counts = count16_ge(refs16, meta_digits, cands)
            return tuple(jnp.where(counts[p] >= topk, cands[p], ts[p]) for p in seqs)
        return lax.fori_loop(0, 16, step, tuple(jnp.full((SUBLANES, TQ), I16_MIN, jnp.int32) for _ in seqs))

    t_hi = digit_search(khi_sc, khi_m)
    t_hi16 = [jnp.broadcast_to(t[0:1], (BF16_ROWS, TQ)).astype(jnp.int16) for t in t_hi]

    def pinned(hi, lo, t):
        return jnp.where(hi > t, jnp.int16(I16_MAX), jnp.where(hi < t, jnp.int16(I16_MIN), lo))

    def pin_low(c, carry):
        for p in seqs:
            lo = pinned(tiles16(khi_sc[p][rows_of(c), :]), tiles16(klo_sc[p][rows_of(c), :]), t_hi16[p][None])
            klo_sc[p][rows_of(c), :] = lo.reshape(TQ, TQ)
        return carry

    lax.fori_loop(0, nchunks, pin_low, 0)
    t_lo = digit_search(klo_sc, [pinned(khi_m[p], klo_m[p], t_hi16[p]) for p in seqs])
    thr = [t_hi[p] * 65536 + (t_lo[p] - I16_MIN) for p in seqs]
    need = [kf - cnt for cnt in count_gt(thr)]

    def pad_rows(x):
        return jnp.concatenate([x, jnp.zeros((LANES - N_META, TQ), x.dtype)], axis=0)

    mask_m = []
    for p in seqs:
        thr_m = jnp.broadcast_to(thr[p][0:1], (N_META, TQ))
        tied_m = key_m[p] == thr_m
        rank_m = jnp.dot(mtril_ref[...], pad_rows(jnp.where(tied_m, 1.0, 0.0).astype(BF16)),
                         preferred_element_type=F32)
        tie_sc[p][...] = jnp.broadcast_to(rank_m[N_META - 1:N_META], (SUBLANES, TQ))
        sel_m = (key_m[p] > thr_m) | (tied_m & (rank_m <= jnp.broadcast_to(need[p][0:1], (N_META, TQ))))
        mask_m.append(jnp.where(sel_m, 0.0, -jnp.inf))
    heads = [(p, h) for p in seqs for h in range(B_HEADS)]
    s_m = [jnp.dot(mckv_ref[...], qabs_ref[p][0, h], preferred_element_type=F32) for p, h in heads]
    e_m = []
    for i, (p, h) in enumerate(heads):
        sh = s_m[i] + jnp.where(first, bmeta_ref[h], 0.0) + mask_m[p]
        m_new = jnp.maximum(jnp.max(sh, axis=0, keepdims=True), MAX_FLOOR)
        m_sc[p][h] = jnp.broadcast_to(m_new, (SUBLANES, TQ))
        e_m.append(pad_rows(jnp.exp2(sh - m_new).astype(BF16)))
    for i, (p, h) in enumerate(heads):
        acc_sc[p][h] = jnp.dot(mvext_ref[...], e_m[i], preferred_element_type=F32)

    def score_stage(c, buf):
        bias_row0 = pl.multiple_of(jnp.maximum(c - m_blk + 2, 0) * TQ, TQ)
        for p in seqs:
            s_buf, m_buf, al_buf = (b[p] for b in buf)
            ckv_c = ckv_ref[p][0, rows_of(c), :]
            kp = tiles(key_sc[p][rows_of(c), :])
            tied = kp == thr[p][None]
            tied_b = jnp.where(tied, 1.0, 0.0).reshape(TQ, TQ).astype(BF16)
            rank = tiles(jnp.dot(tril_ref[...], tied_b, preferred_element_type=F32)) + tie_sc[p][...][None]
            tie_sc[p][...] = jnp.broadcast_to(rank[groups - 1, SUBLANES - 1:SUBLANES, :], (SUBLANES, TQ))
            sel = ((kp > thr[p][None]) | (tied & (rank <= need[p][None]))) & (kp != INT_MIN)
            mask = jnp.where(sel, 0.0, -jnp.inf)
            for h in range(B_HEADS):
                sh = jnp.dot(ckv_c, qabs_ref[p][0, h], preferred_element_type=F32)
                sh = tiles(sh + bnear_ref[h, pl.ds(bias_row0, TQ), :]) + mask
                s_buf[h] = sh.reshape(TQ, TQ)
                m_prev = m_sc[p][h]
                mx = jnp.max(_tree_reduce(jnp.maximum, sh), axis=0, keepdims=True)
                m_new = jnp.maximum(m_prev, mx)
                al_buf[h] = jnp.exp2(m_prev - m_new)
                m_buf[h] = m_new
                m_sc[p][h] = m_new

    def value_stage(c, buf):
        for p in seqs:
            s_buf, m_buf, al_buf = (b[p] for b in buf)
            vt = ckv_ref[p][0, rows_of(c), :].astype(F32).T.astype(BF16)
            vext = jnp.concatenate([vt, jnp.ones((BF16_ROWS, TQ), BF16)], axis=0)
            for h in range(B_HEADS):
                e = jnp.exp2(tiles(s_buf[h]) - m_buf[h][None]).reshape(TQ, TQ).astype(BF16)
                pv = jnp.dot(vext, e, preferred_element_type=F32)
                acc = acc_sc[p][h].reshape(acc_rows // SUBLANES, SUBLANES, TQ)
                acc_sc[p][h] = (acc * al_buf[h][None] + pv.reshape(acc.shape)).reshape(acc_rows, TQ)

    buf0, buf1 = (s0_sc, mb0_sc, al0_sc), (s1_sc, mb1_sc, al1_sc)
    score_stage(0, buf0)

    def chunk_pair(i, carry):
        c = 2 * i + 1
        value_stage(c - 1, buf0)
        score_stage(c, buf1)
        value_stage(c, buf1)
        score_stage(c + 1, buf0)
        return carry

    npairs = (nchunks - 1) // 2
    lax.fori_loop(0, npairs, chunk_pair, 0)
    last = nchunks - 1

    @pl.when(last == 2 * npairs)
    def _():
        value_stage(last, buf0)

    @pl.when(last != 2 * npairs)
    def _():
        value_stage(last - 1, buf0)
        score_stage(last, buf1)
        value_stage(last, buf1)

    for p in seqs:
        outs = []
        for h in range(B_HEADS):
            acc = acc_sc[p][h]
            lat = (acc[0:B_KV_RANK] * (1.0 / acc[B_KV_RANK:B_KV_RANK + 1])).astype(BF16)
            outs.append(jnp.dot(wuvt_ref[h], lat, preferred_element_type=F32))
        o_ref[p][...] = jnp.concatenate(outs, axis=0).T.astype(BF16)


def _mla_call(iqs, iw, qabs, ik, ckv, mik, mckv, bnear, bmeta, wuvt, topk):
    b, s, _ = ik.shape
    nq = s // TQ
    acc_rows = B_KV_RANK + BF16_ROWS
    assert N_META == BF16_ROWS, "the meta keys are handled as one packed bf16 tile"
    tril = jnp.asarray(np.tril(np.ones((TQ, TQ), np.float32)), BF16)
    mtril = jnp.asarray(np.pad(np.tril(np.ones((N_META, N_META), np.float32)), ((0, 0), (0, LANES - N_META))), BF16)
    mvext = jnp.pad(jnp.concatenate([mckv.T, jnp.ones((BF16_ROWS, N_META), BF16)], axis=0),
                    ((0, 0), (0, LANES - N_META)))
    consts = (mik, mckv, mvext, bnear, bmeta, wuvt, tril, mtril)
    nseq = MLA_SEQS
    assert b % nseq == 0

    def per_seq(block, index_of):
        return [pl.BlockSpec(block, functools.partial(index_of, p)) for p in range(nseq)]

    blk4 = lambda p, i, m: ((i * nseq + p) * nq + m, 0, 0, 0)
    blk3 = lambda p, i, m: ((i * nseq + p) * nq + m, 0, 0)
    seq3 = lambda p, i, m: (i * nseq + p, 0, 0)
    out_dim = B_HEADS * B_HEAD_DIM
    vmem = lambda shape, dtype: pltpu.VMEM((nseq,) + shape, dtype)
    o = pl.pallas_call(
        functools.partial(_mla_body, topk=topk),
        grid=(b // nseq, nq),
        in_specs=per_seq((1, IDX_HEADS, IDX_DIM, TQ), blk4) + per_seq((1, SUBLANES, TQ), blk3)
                 + per_seq((1, B_HEADS, B_KV_RANK, TQ), blk4) + per_seq((1, s, IDX_DIM), seq3)
                 + per_seq((1, s, B_KV_RANK), seq3) + [_full_spec(c.shape) for c in consts],
        out_specs=pl.BlockSpec((1, nseq, TQ, out_dim), lambda i, m: (i, 0, m, 0)),
        out_shape=jax.ShapeDtypeStruct((b // nseq, nseq, s, out_dim), BF16),
        scratch_shapes=[vmem((s, TQ), jnp.int32), vmem((s, TQ), jnp.int16), vmem((s, TQ), jnp.int16),
                        vmem((B_HEADS, TQ, TQ), F32), vmem((B_HEADS, TQ, TQ), F32),
                        vmem((B_HEADS, SUBLANES, TQ), F32), vmem((B_HEADS, SUBLANES, TQ), F32),
                        vmem((B_HEADS, SUBLANES, TQ), F32), vmem((B_HEADS, SUBLANES, TQ), F32),
                        vmem((B_HEADS, SUBLANES, TQ), F32),
                        vmem((B_HEADS, acc_rows, TQ), F32),
                        vmem((SUBLANES, TQ), F32)],
        compiler_params=pltpu.CompilerParams(dimension_semantics=("arbitrary", "arbitrary"),
                                             vmem_limit_bytes=VMEM_LIMIT_BYTES),
        name="indexer_topk_mla",
    )(*([iqs] * nseq + [iw] * nseq + [qabs] * nseq + [ik] * nseq + [ckv] * nseq), *consts)
    return o.reshape(b, s, out_dim)


def _out_body(x_ref, oa_ref, ob_ref, ag_ref, wg_ref, bg_ref, wa_ref, wb_ref, wo_ref, fg_ref,
              wfg_ref, wfu_ref, wfd_ref, ng_ref, y_ref):
    d = x_ref.shape[1]
    x = x_ref[...]
    hn = _rms(x, ag_ref[...]).astype(BF16)

    def gated(o_ref, w_ref, lo):
        gate = jax.nn.sigmoid(jnp.dot(hn, wg_ref[:, lo:lo + d], preferred_element_type=F32) + bg_ref[:, lo:lo + d])
        return gate * jnp.dot(o_ref[...], w_ref[...], preferred_element_type=F32)

    mixed = (gated(oa_ref, wa_ref, 0) + gated(ob_ref, wb_ref, d)).astype(BF16)
    h = x + jnp.dot(mixed, wo_ref[...], preferred_element_type=F32)
    hn2 = _rms(h, fg_ref[...]).astype(BF16)
    tiles_ff = wfg_ref.shape[1] // MXU_DIM
    bounds = [MXU_DIM * ((tiles_ff * i + FFN_CHUNKS - 1) // FFN_CHUNKS) for i in range(FFN_CHUNKS + 1)]
    for lo, hi in zip(bounds[:-1], bounds[1:]):
        g = jnp.dot(hn2, wfg_ref[:, lo:hi], preferred_element_type=F32)
        u = jnp.dot(hn2, wfu_ref[:, lo:hi], preferred_element_type=F32)
        act = (g * jax.nn.sigmoid(g) * u).astype(BF16)
        h = h + jnp.dot(act, wfd_ref[lo:hi, :], preferred_element_type=F32)
    y_ref[...] = _rms(h, ng_ref[...])


def _out_call(x2, oa, ob, ag, wg, bg, wa, wb, wo, fg, wfg, wfu, wfd, ng):
    n, d = x2.shape
    tm = TOK_TILE
    row = lambda i: (i, 0)

    def const_spec(a):
        return pl.BlockSpec(a.shape, lambda i: (0,) * a.ndim, pipeline_mode=pl.Buffered(1))

    consts = [ag, wg, bg, wa, wb, wo, fg, wfg, wfu, wfd, ng]
    return pl.pallas_call(
        _out_body,
        grid=(n // tm,),
        in_specs=[pl.BlockSpec((tm, d), row), pl.BlockSpec((tm, oa.shape[1]), row),
                  pl.BlockSpec((tm, ob.shape[1]), row)] + [const_spec(a) for a in consts],
        out_specs=pl.BlockSpec((tm, d), row),
        out_shape=jax.ShapeDtypeStruct((n, d), F32),
        compiler_params=pltpu.CompilerParams(dimension_semantics=("arbitrary",),
                                             vmem_limit_bytes=VMEM_LIMIT_BYTES),
        name="merge_ffn_norm",
    )(x2, oa, ob, *consts)


def kernel(x, meta_tokens, attn_norm_g, w_in, b_gates, q_norm_g, kv_norm_g, w_uq, w_uk, w_uv, idx_k_ln_g,
           idx_k_ln_b, sinks, rel_bias, w_branch_a, w_branch_b, w_out, ffn_norm_g, w_ffn_gate, w_ffn_up,
           w_ffn_down, final_norm_g):
    b, s, d = x.shape
    assert attn_norm_g.shape[0] == 1, "single-layer block"
    assert s % TQ == 0 and (b * s) % TOK_TILE == 0 and TOK_TILE % TQ == 0
    assert w_ffn_gate.shape[2] % MXU_DIM == 0
    topk = min(TOPK_MAX, s // 4)
    far_bkts = np.unique(_t5_bucket_np(np.arange(BLOCK + 1, s + BLOCK + 1)))
    assert far_bkts.size == 1
    far_bkt = int(far_bkts[0])

    wi = w_in[0]
    widths = (A_HEADS * A_HEAD_DIM, A_KV_HEADS * A_HEAD_DIM, A_KV_HEADS * A_HEAD_DIM, B_Q_RANK, B_KV_RANK,
              IDX_HEADS * IDX_DIM, IDX_DIM, IDX_HEADS, 2 * d)
    starts = np.concatenate([[0], np.cumsum(widths)])
    w_aq, w_ak, w_av, w_bq, w_bkv, w_iq, w_ik, w_iw, w_gates = (wi[:, int(a):int(b_)]
                                                                 for a, b_ in zip(starts[:-1], starts[1:]))
    zpad = lambda k: jnp.zeros((d, k), wi.dtype)
    w1 = jnp.concatenate([w_ak, w_bkv, w_ik, zpad(LANES - IDX_DIM)], axis=1).astype(BF16)
    wt = jnp.concatenate([w_aq, w_av, w_bq, w_iq, w_iw, zpad(BF16_ROWS - IDX_HEADS)],
                         axis=1).T.astype(BF16)
    wg = w_gates.astype(BF16)
    row2 = lambda v: v.reshape(1, -1).astype(F32)
    col2 = lambda v: v.reshape(-1, 1).astype(F32)
    wuk = jnp.transpose(w_uk[0], (1, 0, 2)).astype(BF16)
    wuvt = jnp.transpose(w_uv[0], (1, 2, 0)).astype(BF16)
    proj_w = (row2(attn_norm_g[0]), w1, wt, col2(q_norm_g[0]), w_uq[0].T.astype(BF16), wuk,
              row2(kv_norm_g[0]), row2(idx_k_ln_g[0]), row2(idx_k_ln_b[0]))

    bias_a, bnear, bmeta = _bias_call(rel_bias, far_bkt)

    x2 = x.reshape(b * s, d)
    aq_t, ak, av_t, qabs, ckv, iqs, ik, iw = _proj_call(x2, TOK_TILE, TQ, *proj_w)
    meta_blk = jnp.concatenate([jnp.zeros((PAD, d), x.dtype), meta_tokens.astype(x.dtype)], axis=0)
    _, mak, mav_t, _, mckv, _, mik, _ = _proj_call(meta_blk, BLOCK, BLOCK, *proj_w)

    o_a = _swa_call(sinks[0], aq_t, ak, av_t, mak, mav_t, bias_a, b)
    o_b = _mla_call(iqs, iw, qabs, ik.reshape(b, s, -1), ckv.reshape(b, s, -1), mik[PAD:], mckv[PAD:],
                    bnear, bmeta, wuvt, topk)

    y = _out_call(x2, o_a.reshape(b * s, -1), o_b.reshape(b * s, -1), row2(attn_norm_g[0]), wg,
                  row2(b_gates[0]), w_branch_a[0].astype(BF16), w_branch_b[0].astype(BF16),
                  w_out[0].astype(BF16), row2(ffn_norm_g[0]), w_ffn_gate[0].astype(BF16),
                  w_ffn_up[0].astype(BF16), w_ffn_down[0].astype(BF16), row2(final_norm_g))
    return y.reshape(b, s, d)
```

```python
import functools
import math

import numpy as np
import jax
import jax.numpy as jnp
from jax import lax
from jax.experimental import pallas as pl
from jax.experimental.pallas import tpu as pltpu

N_META = 16
BLOCK = 128
PAD = BLOCK - N_META
WINDOW = 128
A_HEADS = 8
A_KV_HEADS = 2
A_HEAD_DIM = 64
B_HEADS = 8
B_HEAD_DIM = 64
B_Q_RANK = 256
B_KV_RANK = 128
IDX_HEADS = 4
IDX_DIM = 64
TOPK_MAX = 256
N_BUCKETS = 32
MAX_DISTANCE = 128
EPS = 1e-6
MAX_FLOOR = -3.0e38
INT_MIN = -(2 ** 31)
LOG2E = math.log2(math.e)
I16_MIN = -(2 ** 15)
I16_MAX = 2 ** 15 - 1

LANES = 128
SUBLANES = 8
BF16_ROWS = 16
MXU_DIM = 256
VMEM_LIMIT_BYTES = 56 * 1024 * 1024

TOK_TILE = 512
TQ = 256
FFN_CHUNKS = 1
SWA_BLOCKS = 8
MLA_SEQS = 2

C_AK = 0
C_BKV = C_AK + A_KV_HEADS * A_HEAD_DIM
C_IK = C_BKV + B_KV_RANK
C_END = C_IK + LANES
R_AQ = 0
R_AV = R_AQ + A_HEADS * A_HEAD_DIM
R_BQ = R_AV + A_KV_HEADS * A_HEAD_DIM
R_IQ = R_BQ + B_Q_RANK
R_IW = R_IQ + IDX_HEADS * IDX_DIM
R_END = R_IW + BF16_ROWS

F32 = jnp.float32
BF16 = jnp.bfloat16
NT_DIMS = (((1,), (1,)), ((), ()))


def _t5_bucket_np(dist):
    dist = np.asarray(dist, np.int64)
    max_exact = N_BUCKETS // 2
    d = np.maximum(dist, 1).astype(np.float32)
    large = max_exact + (np.log(d / np.float32(max_exact)) / np.float32(math.log(MAX_DISTANCE / max_exact))
                         * np.float32(N_BUCKETS - max_exact)).astype(np.int32)
    large = np.minimum(large, N_BUCKETS - 1)
    return np.where(dist < max_exact, dist, large).astype(np.int32)


def _rms(x, g):
    return x * lax.rsqrt(jnp.mean(x * x, axis=-1, keepdims=True) + EPS) * g


def _tree_reduce(op, x):
    while x.shape[0] > 1:
        half = x.shape[0] // 2
        x = op(x[:half], x[half:])
    return x[0]


def _full_spec(shape):
    nd = len(shape)
    return pl.BlockSpec(shape, lambda *_: (0,) * nd)


def _bias_body(tab_ref, bkt_a_ref, bkt_near_ref, bkt_meta_ref, ba_ref, bnear_ref, bmeta_ref, *, far_bkt):
    def lookup(bkt, col, fill):
        acc = jnp.full(bkt.shape, fill, F32)
        for b in range(N_BUCKETS):
            acc = jnp.where(bkt == b, tab_ref[b, col], acc)
        return acc

    bkt_a = bkt_a_ref[...]
    bkt_near = bkt_near_ref[...]
    grp = A_HEADS // A_KV_HEADS
    for h in range(A_HEADS):
        ba_ref[h // grp, :, (h % grp) * BLOCK:(h % grp + 1) * BLOCK] = lookup(bkt_a, h, -jnp.inf) * LOG2E
    for h in range(B_HEADS):
        col = A_HEADS + h
        bnear_ref[h, 0:TQ, :] = jnp.zeros((TQ, TQ), F32)
        bnear_ref[h, TQ:3 * TQ, :] = (lookup(bkt_near, col, 0.0) - tab_ref[far_bkt, col]) * LOG2E
        bmeta_ref[h] = (lookup(bkt_meta_ref[...], col, 0.0) - tab_ref[far_bkt, col]) * LOG2E


def _bias_call(rel_bias, far_bkt):
    k = np.arange(2 * BLOCK)[:, None]
    q = np.arange(BLOCK)[None, :]
    dist = q + BLOCK - k
    bkt_a = np.where((dist >= 0) & (dist < WINDOW), _t5_bucket_np(np.maximum(dist, 0)), -1).astype(np.int32)
    k = np.arange(2 * TQ)[:, None]
    q = np.arange(TQ)[None, :]
    bkt_near = _t5_bucket_np(np.maximum(q + TQ - k, 0))
    k = np.arange(N_META)[:, None]
    bkt_meta = _t5_bucket_np(q + N_META - k)
    vmem = pl.BlockSpec(memory_space=pltpu.VMEM)
    grp = A_HEADS // A_KV_HEADS
    return pl.pallas_call(
        functools.partial(_bias_body, far_bkt=far_bkt),
        out_shape=(jax.ShapeDtypeStruct((A_KV_HEADS, 2 * BLOCK, grp * BLOCK), F32),
                   jax.ShapeDtypeStruct((B_HEADS, 3 * TQ, TQ), F32),
                   jax.ShapeDtypeStruct((B_HEADS, N_META, TQ), F32)),
        in_specs=[pl.BlockSpec(memory_space=pltpu.SMEM), vmem, vmem, vmem],
        out_specs=(vmem, vmem, vmem),
        name="bias_tables",
    )(rel_bias, jnp.asarray(bkt_a), jnp.asarray(bkt_near), jnp.asarray(bkt_meta))


def _proj_body(x_ref, g_ref, w1_ref, wt_ref, qg_ref, wuq_ref, wuk_ref, kvg_ref, ikg_ref, ikb_ref,
               aq_ref, ak_ref, av_ref, qabs_ref, ckv_ref, iqs_ref, ik_ref, iw_ref, *, tq):
    tm = x_ref.shape[0]
    nblk = tm // tq
    nblk_a = tm // BLOCK
    grp = A_HEADS // A_KV_HEADS
    hn = _rms(x_ref[...], g_ref[...]).astype(BF16)

    def proj(lo, hi):
        return jnp.dot(hn, w1_ref[:, lo:hi], preferred_element_type=F32)

    ak_ref[...] = proj(C_AK, C_BKV).astype(BF16)

    feat_t = lax.dot_general(wt_ref[...], hn, NT_DIMS, preferred_element_type=F32)
    aq_t = (feat_t[R_AQ:R_AV] * (A_HEAD_DIM ** -0.5 * LOG2E)).astype(BF16)
    for j in range(nblk_a):
        tok = slice(j * BLOCK, (j + 1) * BLOCK)
        for h in range(A_HEADS):
            aq_ref[j, h // grp, :, (h % grp) * BLOCK:(h % grp + 1) * BLOCK] = \
                aq_t[h * A_HEAD_DIM:(h + 1) * A_HEAD_DIM, tok]
    av_t = feat_t[R_AV:R_BQ].astype(BF16)
    for j in range(nblk_a):
        for kvh in range(A_KV_HEADS):
            av_ref[j, kvh] = av_t[kvh * A_HEAD_DIM:(kvh + 1) * A_HEAD_DIM, j * BLOCK:(j + 1) * BLOCK]

    bq_t = feat_t[R_BQ:R_IQ]
    qn_t = bq_t * lax.rsqrt(jnp.mean(bq_t * bq_t, axis=0, keepdims=True) + EPS) * qg_ref[...]
    q_t = jnp.dot(wuq_ref[...], qn_t.astype(BF16), preferred_element_type=F32).astype(BF16)
    for h in range(B_HEADS):
        qa_t = jnp.dot(wuk_ref[h], q_t[h * B_HEAD_DIM:(h + 1) * B_HEAD_DIM], preferred_element_type=F32)
        qa_t = (qa_t * (B_HEAD_DIM ** -0.5 * LOG2E)).astype(BF16)
        for j in range(nblk):
            qabs_ref[j, h] = qa_t[:, j * tq:(j + 1) * tq]
    iq_t = feat_t[R_IQ:R_IW].astype(BF16)
    for h in range(IDX_HEADS):
        for j in range(nblk):
            iqs_ref[j, h] = iq_t[h * IDX_DIM:(h + 1) * IDX_DIM, j * tq:(j + 1) * tq]

    ckv_ref[...] = _rms(proj(C_BKV, C_IK), kvg_ref[...]).astype(BF16)

    iw_t = feat_t[R_IW:R_IW + SUBLANES] * ((IDX_HEADS * IDX_DIM) ** -0.5)
    for j in range(nblk):
        iw_ref[j] = iw_t[:, j * tq:(j + 1) * tq]

    ik = proj(C_IK, C_END)[:, :IDX_DIM]
    mu = jnp.mean(ik, axis=-1, keepdims=True)
    xc = ik - mu
    var = jnp.mean(xc * xc, axis=-1, keepdims=True)
    ik_ref[...] = (xc * lax.rsqrt(var + EPS) * ikg_ref[...] + ikb_ref[...]).astype(BF16)


def _proj_call(x2, tm, tq, *weights):
    n, d = x2.shape
    grid = (n // tm,)
    row = lambda i: (i, 0)
    blk4 = lambda i: (i, 0, 0, 0)
    grp = A_HEADS // A_KV_HEADS
    out_shape = (
        jax.ShapeDtypeStruct((n // BLOCK, A_KV_HEADS, A_HEAD_DIM, grp * BLOCK), BF16),
        jax.ShapeDtypeStruct((n, A_KV_HEADS * A_HEAD_DIM), BF16),
        jax.ShapeDtypeStruct((n // BLOCK, A_KV_HEADS, A_HEAD_DIM, BLOCK), BF16),
        jax.ShapeDtypeStruct((n // tq, B_HEADS, B_KV_RANK, tq), BF16),
        jax.ShapeDtypeStruct((n, B_KV_RANK), BF16),
        jax.ShapeDtypeStruct((n // tq, IDX_HEADS, IDX_DIM, tq), BF16),
        jax.ShapeDtypeStruct((n, IDX_DIM), BF16),
        jax.ShapeDtypeStruct((n // tq, SUBLANES, tq), F32),
    )
    out_specs = (
        pl.BlockSpec((tm // BLOCK, A_KV_HEADS, A_HEAD_DIM, grp * BLOCK), blk4),
        pl.BlockSpec((tm, A_KV_HEADS * A_HEAD_DIM), row),
        pl.BlockSpec((tm // BLOCK, A_KV_HEADS, A_HEAD_DIM, BLOCK), blk4),
        pl.BlockSpec((tm // tq, B_HEADS, B_KV_RANK, tq), blk4),
        pl.BlockSpec((tm, B_KV_RANK), row),
        pl.BlockSpec((tm // tq, IDX_HEADS, IDX_DIM, tq), blk4),
        pl.BlockSpec((tm, IDX_DIM), row),
        pl.BlockSpec((tm // tq, SUBLANES, tq), lambda i: (i, 0, 0)),
    )
    in_specs = [pl.BlockSpec((tm, d), row)] + [_full_spec(w.shape) for w in weights]
    return pl.pallas_call(
        functools.partial(_proj_body, tq=tq),
        grid=grid, in_specs=in_specs, out_specs=out_specs, out_shape=out_shape,
        compiler_params=pltpu.CompilerParams(dimension_semantics=("arbitrary",),
                                             vmem_limit_bytes=VMEM_LIMIT_BYTES),
        name="in_proj",
    )(x2, *weights)


def _swa_body(sinks_ref, aq_ref, kcur_ref, kprev_ref, kmeta_ref, vcur_ref, vprev_ref, vmeta_ref, bias_ref, o_ref):
    n = pl.program_id(1)
    first = n == 0
    grp = A_HEADS // A_KV_HEADS
    width = grp * BLOCK
    kall = jnp.concatenate([jnp.where(first, kmeta_ref[...], kprev_ref[0]), kcur_ref[0]], axis=0)
    vall = [jnp.concatenate([jnp.where(first, vmeta_ref[0, kvh], vprev_ref[0, kvh])]
                            + [vcur_ref[j, kvh] for j in range(SWA_BLOCKS)], axis=1)
            for kvh in range(A_KV_HEADS)]
    ntile = 2 * BLOCK // SUBLANES
    key_row = (lax.broadcasted_iota(jnp.int32, (ntile, SUBLANES, width), 0) * SUBLANES
               + lax.broadcasted_iota(jnp.int32, (ntile, SUBLANES, width), 1))
    pad_row = first & (key_row < PAD)
    lane_head = lax.broadcasted_iota(jnp.int32, (1, width), 1) // BLOCK
    ones = jnp.ones((BF16_ROWS, 2 * BLOCK), BF16)
    sinks = []
    for kvh in range(A_KV_HEADS):
        sink = jnp.zeros((1, width), F32)
        for g in range(grp):
            sink = jnp.where(lane_head == g, sinks_ref[kvh * grp + g] * LOG2E, sink)
        sinks.append(sink)
    probs_ids = [(j, kvh) for j in range(SWA_BLOCKS) for kvh in range(A_KV_HEADS)]
    scores = [jnp.dot(kall[j * BLOCK:(j + 2) * BLOCK, kvh * A_HEAD_DIM:(kvh + 1) * A_HEAD_DIM], aq_ref[j, kvh],
                      preferred_element_type=F32) for j, kvh in probs_ids]
    maxes, probs = [], []
    for i, (j, kvh) in enumerate(probs_ids):
        s = (scores[i] + bias_ref[kvh]).reshape(ntile, SUBLANES, width)
        if j == 0:
            s = jnp.where(pad_row, -jnp.inf, s)
        m = jnp.maximum(jnp.max(_tree_reduce(jnp.maximum, s), axis=0, keepdims=True), sinks[kvh])
        maxes.append(m)
        probs.append(jnp.exp2(s - m[None]).reshape(2 * BLOCK, width).astype(BF16))
    pvs = [jnp.dot(jnp.concatenate([vall[kvh][:, j * BLOCK:(j + 2) * BLOCK], ones], axis=0), probs[i],
                   preferred_element_type=F32) for i, (j, kvh) in enumerate(probs_ids)]
    for j in range(SWA_BLOCKS):
        heads_t = []
        for kvh in range(A_KV_HEADS):
            i = j * A_KV_HEADS + kvh
            den = pvs[i][A_HEAD_DIM:A_HEAD_DIM + 1] + jnp.exp2(sinks[kvh] - maxes[i])
            o_t = pvs[i][0:A_HEAD_DIM] * (1.0 / den)
            heads_t += [o_t[:, g * BLOCK:(g + 1) * BLOCK] for g in range(grp)]
        o_ref[0, j * BLOCK:(j + 1) * BLOCK, :] = jnp.concatenate(heads_t, axis=0).T.astype(BF16)


def _swa_call(sinks, aq_t, ak, av_t, mak, mav_t, bias_a, b):
    nblocks = aq_t.shape[0]
    nb = nblocks // b
    assert nb % SWA_BLOCKS == 0
    ns = nb // SWA_BLOCKS
    s = nb * BLOCK
    rows = SWA_BLOCKS * BLOCK
    cur4 = lambda i, n: (i * ns + n, 0, 0, 0)
    prev_blk = lambda n: jnp.maximum(n * SWA_BLOCKS - 1, 0)
    kdim = ak.shape[-1]
    ak3 = ak.reshape(b, s, kdim)
    return pl.pallas_call(
        _swa_body,
        grid=(b, ns),
        in_specs=[pl.BlockSpec(memory_space=pltpu.SMEM),
                  pl.BlockSpec((SWA_BLOCKS,) + aq_t.shape[1:], cur4),
                  pl.BlockSpec((1, rows, kdim), lambda i, n: (i, n, 0)),
                  pl.BlockSpec((1, BLOCK, kdim), lambda i, n: (i, prev_blk(n), 0)),
                  _full_spec(mak.shape),
                  pl.BlockSpec((SWA_BLOCKS,) + av_t.shape[1:], cur4),
                  pl.BlockSpec((1,) + av_t.shape[1:], lambda i, n: (i * nb + prev_blk(n), 0, 0, 0)),
                  _full_spec(mav_t.shape), _full_spec(bias_a.shape)],
        out_specs=pl.BlockSpec((1, rows, A_HEADS * A_HEAD_DIM), lambda i, n: (i, n, 0)),
        out_shape=jax.ShapeDtypeStruct((b, s, A_HEADS * A_HEAD_DIM), BF16),
        compiler_params=pltpu.CompilerParams(dimension_semantics=("arbitrary", "arbitrary"),
                                             vmem_limit_bytes=VMEM_LIMIT_BYTES),
        name="swa_sink_attention",
    )(sinks, aq_t, ak3, ak3, mak, av_t, av_t, mav_t, bias_a)


def _mla_body(*refs, topk):
    it = iter(refs)
    take = lambda n: [next(it) for _ in range(n)]
    iqs_refs, iw_refs, qabs_refs, ik_refs, ckv_refs = (take(MLA_SEQS) for _ in range(5))
    mik_ref, mckv_ref, mvext_ref, bnear_ref, bmeta_ref, wuvt_ref, tril_ref, mtril_ref = take(8)
    o_all, = take(1)
    o_refs = [o_all.at[0, p] for p in range(MLA_SEQS)]
    scratch = take(12)
    seqs = range(MLA_SEQS)
    m_blk = pl.program_id(1)
    first = m_blk == 0
    nchunks = m_blk + 1
    groups = TQ // SUBLANES
    acc_rows = B_KV_RANK + BF16_ROWS
    kf = float(topk)

    sub_pos = (lax.broadcasted_iota(jnp.int32, (groups, SUBLANES, TQ), 0) * SUBLANES
               + lax.broadcasted_iota(jnp.int32, (groups, SUBLANES, TQ), 1))
    qpos = m_blk * TQ + lax.broadcasted_iota(jnp.int32, (groups, SUBLANES, TQ), 2)

    def rows_of(c):
        return pl.ds(pl.multiple_of(c * TQ, TQ), TQ)

    def tiles(x):
        return x.reshape(groups, SUBLANES, x.shape[-1])

    def sort_key(sc):
        bits = lax.bitcast_convert_type(sc, jnp.int32)
        return jnp.where(bits < 0, INT_MIN - bits, bits)

    iqs_ref, iw_ref, qabs_ref, ik_ref, ckv_ref, o_ref = iqs_refs, iw_refs, qabs_refs, ik_refs, ckv_refs, o_refs
    (key_sc, khi_sc, klo_sc, s0_sc, s1_sc, mb0_sc, mb1_sc, al0_sc, al1_sc, m_sc, acc_sc, tie_sc) = (
        [r.at[p] for p in seqs] for r in scratch)
    wrow = [[iw_ref[p][0, h:h + 1, :] for h in range(IDX_HEADS)] for p in seqs]

    def index_score(p, ik_c):
        sc = None
        for h in range(IDX_HEADS):
            logits = jnp.dot(ik_c, iqs_ref[p][0, h], preferred_element_type=F32)
            term = jnp.maximum(logits, 0.0) * wrow[p][h]
            sc = term if sc is None else sc + term
        return sort_key(sc)

    def digits(key):
        return jnp.right_shift(key, 16).astype(jnp.int16), (key ^ 0x8000).astype(jnp.int16)

    def score_chunk(c):
        adm = c * TQ + sub_pos <= qpos
        for p in seqs:
            key = jnp.where(adm, tiles(index_score(p, ik_ref[p][0, rows_of(c), :])), INT_MIN).reshape(TQ, TQ)
            key_sc[p][rows_of(c), :] = key
            khi_sc[p][rows_of(c), :], klo_sc[p][rows_of(c), :] = digits(key)

    key_m = [index_score(p, mik_ref[...]) for p in seqs]
    khi_m, klo_m = zip(*[digits(k) for k in key_m])

    def score_pair(i, carry):
        score_chunk(2 * i)
        score_chunk(2 * i + 1)
        return carry

    lax.fori_loop(0, lax.shift_right_logical(nchunks, 1), score_pair, 0)

    @pl.when((nchunks & 1) == 1)
    def _():
        score_chunk(nchunks - 1)

    def count_gt(thrs):
        def hits(p, kp):
            return _tree_reduce(jnp.add, jnp.where(kp > thrs[p][None], 1.0, 0.0))

        def body(c, parts):
            return tuple(parts[p] + hits(p, tiles(key_sc[p][rows_of(c), :])) for p in seqs)
        parts = lax.fori_loop(0, nchunks, body,
                              tuple(hits(p, key_m[p].reshape(N_META // SUBLANES, SUBLANES, TQ)) for p in seqs))
        return [jnp.broadcast_to(jnp.sum(part, axis=0, keepdims=True), (SUBLANES, TQ)) for part in parts]

    def tiles16(x):
        return x.reshape(TQ // BF16_ROWS, BF16_ROWS, TQ)

    def count16_ge(refs16, meta_digits, cands):
        cand16 = [jnp.broadcast_to(cands[p][0:1], (BF16_ROWS, TQ)).astype(jnp.int16) for p in seqs]
        one, zero = jnp.int16(1), jnp.int16(0)

        def hits(p, c):
            hit = tiles16(refs16[p][rows_of(c), :]) >= cand16[p][None]
            return _tree_reduce(jnp.add, jnp.where(hit, one, zero))

        def two_chunks(i, parts):
            return tuple(parts[p] + hits(p, 2 * i) + hits(p, 2 * i + 1) for p in seqs)
        parts = lax.fori_loop(0, lax.shift_right_logical(nchunks, 1), two_chunks,
                              tuple(jnp.where(meta_digits[p] >= cand16[p], one, zero) for p in seqs))
        parts = lax.cond((nchunks & 1) == 1,
                         lambda ps: tuple(ps[p] + hits(p, nchunks - 1) for p in seqs), lambda ps: ps, parts)
        return [jnp.broadcast_to(jnp.sum(part.astype(jnp.int32), axis=0, keepdims=True), (SUBLANES, TQ))
                for part in parts]

    def digit_search(refs16, meta_digits):
        def step(i, ts):
            cands = [t + jnp.left_shift(jnp.int32(1), 15 - i) for t in ts]
            counts = count16_ge(refs16, meta_digits, cands)
            return tuple(jnp.where(counts[p] >= topk, cands[p], ts[p]) for p in seqs)
        return lax.fori_loop(0, 16, step, tuple(jnp.full((SUBLANES, TQ), I16_MIN, jnp.int32) for _ in seqs))

    t_hi = digit_search(khi_sc, khi_m)
    t_hi16 = [jnp.broadcast_to(t[0:1], (BF16_ROWS, TQ)).astype(jnp.int16) for t in t_hi]

    def pinned(hi, lo, t):
        return jnp.where(hi > t, jnp.int16(I16_MAX), jnp.where(hi < t, jnp.int16(I16_MIN), lo))

    def pin_low(c, carry):
        for p in seqs:
            lo = pinned(tiles16(khi_sc[p][rows_of(c), :]), tiles16(klo_sc[p][rows_of(c), :]), t_hi16[p][None])
            klo_sc[p][rows_of(c), :] = lo.reshape(TQ, TQ)
        return carry

    lax.fori_loop(0, nchunks, pin_low, 0)
    t_lo = digit_search(klo_sc, [pinned(khi_m[p], klo_m[p], t_hi16[p]) for p in seqs])
    thr = [t_hi[p] * 65536 + (t_lo[p] - I16_MIN) for p in seqs]
    need = [kf - cnt for cnt in count_gt(thr)]

    def pad_rows(x):
        return jnp.concatenate([x, jnp.zeros((LANES - N_META, TQ), x.dtype)], axis=0)

    mask_m = []
    for p in seqs:
        thr_m = jnp.broadcast_to(thr[p][0:1], (N_META, TQ))
        tied_m = key_m[p] == thr_m
        rank_m = jnp.dot(mtril_ref[...], pad_rows(jnp.where(tied_m, 1.0, 0.0).astype(BF16)),
                         preferred_element_type=F32)
        tie_sc[p][...] = jnp.broadcast_to(rank_m[N_META - 1:N_META], (SUBLANES, TQ))
        sel_m = (key_m[p] > thr_m) | (tied_m & (rank_m <= jnp.broadcast_to(need[p][0:1], (N_META, TQ))))
        mask_m.append(jnp.where(sel_m, 0.0, -jnp.inf))
    heads = [(p, h) for p in seqs for h in range(B_HEADS)]
    s_m = [jnp.dot(mckv_ref[...], qabs_ref[p][0, h], preferred_element_type=F32) for p, h in heads]
    e_m = []
    for i, (p, h) in enumerate(heads):
        sh = s_m[i] + jnp.where(first, bmeta_ref[h], 0.0) + mask_m[p]
        m_new = jnp.maximum(jnp.max(sh, axis=0, keepdims=True), MAX_FLOOR)
        m_sc[p][h] = jnp.broadcast_to(m_new, (SUBLANES, TQ))
        e_m.append(pad_rows(jnp.exp2(sh - m_new).astype(BF16)))
    for i, (p, h) in enumerate(heads):
        acc_sc[p][h] = jnp.dot(mvext_ref[...], e_m[i], preferred_element_type=F32)

    def score_stage(c, buf):
        bias_row0 = pl.multiple_of(jnp.maximum(c - m_blk + 2, 0) * TQ, TQ)
        for p in seqs:
            s_buf, m_buf, al_buf = (b[p] for b in buf)
            ckv_c = ckv_ref[p][0, rows_of(c), :]
            kp = tiles(key_sc[p][rows_of(c), :])
            tied = kp == thr[p][None]
            tied_b = jnp.where(tied, 1.0, 0.0).reshape(TQ, TQ).astype(BF16)
            rank = tiles(jnp.dot(tril_ref[...], tied_b, preferred_element_type=F32)) + tie_sc[p][...][None]
            tie_sc[p][...] = jnp.broadcast_to(rank[groups - 1, SUBLANES - 1:SUBLANES, :], (SUBLANES, TQ))
            sel = ((kp > thr[p][None]) | (tied & (rank <= need[p][None]))) & (kp != INT_MIN)
            mask = jnp.where(sel, 0.0, -jnp.inf)
            for h in range(B_HEADS):
                sh = jnp.dot(ckv_c, qabs_ref[p][0, h], preferred_element_type=F32)
                sh = tiles(sh + bnear_ref[h, pl.ds(bias_row0, TQ), :]) + mask
                s_buf[h] = sh.reshape(TQ, TQ)
                m_prev = m_sc[p][h]
                mx = jnp.max(_tree_reduce(jnp.maximum, sh), axis=0, keepdims=True)
                m_new = jnp.maximum(m_prev, mx)
                al_buf[h] = jnp.exp2(m_prev - m_new)
                m_buf[h] = m_new
                m_sc[p][h] = m_new

    def value_stage(c, buf):
        for p in seqs:
            s_buf, m_buf, al_buf = (b[p] for b in buf)
            vt = ckv_ref[p][0, rows_of(c), :].astype(F32).T.astype(BF16)
            vext = jnp.concatenate([vt, jnp.ones((BF16_ROWS, TQ), BF16)], axis=0)
            for h in range(B_HEADS):
                e = jnp.exp2(tiles(s_buf[h]) - m_buf[h][None]).reshape(TQ, TQ).astype(BF16)
                pv = jnp.dot(vext, e, preferred_element_type=F32)
                acc = acc_sc[p][h].reshape(acc_rows // SUBLANES, SUBLANES, TQ)
                acc_sc[p][h] = (acc * al_buf[h][None] + pv.reshape(acc.shape)).reshape(acc_rows, TQ)

    buf0, buf1 = (s0_sc, mb0_sc, al0_sc), (s1_sc, mb1_sc, al1_sc)
    score_stage(0, buf0)

    def chunk_pair(i, carry):
        c = 2 * i + 1
        value_stage(c - 1, buf0)
        score_stage(c, buf1)
        value_stage(c, buf1)
        score_stage(c + 1, buf0)
        return carry

    npairs = (nchunks - 1) // 2
    lax.fori_loop(0, npairs, chunk_pair, 0)
    last = nchunks - 1

    @pl.when(last == 2 * npairs)
    def _():
        value_stage(last, buf0)

    @pl.when(last != 2 * npairs)
    def _():
        value_stage(last - 1, buf0)
        score_stage(last, buf1)
        value_stage(last, buf1)

    for p in seqs:
        outs = []
        for h in range(B_HEADS):
            acc = acc_sc[p][h]
            lat = (acc[0:B_KV_RANK] * (1.0 / acc[B_KV_RANK:B_KV_RANK + 1])).astype(BF16)
            outs.append(jnp.dot(wuvt_ref[h], lat, preferred_element_type=F32))
        o_ref[p][...] = jnp.concatenate(outs, axis=0).T.astype(BF16)


def _mla_call(iqs, iw, qabs, ik, ckv, mik, mckv, bnear, bmeta, wuvt, topk):
    b, s, _ = ik.shape
    nq = s // TQ
    acc_rows = B_KV_RANK + BF16_ROWS
    assert N_META == BF16_ROWS, "the meta keys are handled as one packed bf16 tile"
    tril = jnp.asarray(np.tril(np.ones((TQ, TQ), np.float32)), BF16)
    mtril = jnp.asarray(np.pad(np.tril(np.ones((N_META, N_META), np.float32)), ((0, 0), (0, LANES - N_META))), BF16)
    mvext = jnp.pad(jnp.concatenate([mckv.T, jnp.ones((BF16_ROWS, N_META), BF16)], axis=0),
                    ((0, 0), (0, LANES - N_META)))
    consts = (mik, mckv, mvext, bnear, bmeta, wuvt, tril, mtril)
    nseq = MLA_SEQS
    assert b % nseq == 0

    def per_seq(block, index_of):
        return [pl.BlockSpec(block, functools.partial(index_of, p)) for p in range(nseq)]

    blk4 = lambda p, i, m: ((i * nseq + p) * nq + m, 0, 0, 0)
    blk3 = lambda p, i, m: ((i * nseq + p) * nq + m, 0, 0)
    seq3 = lambda p, i, m: (i * nseq + p, 0, 0)
    out_dim = B_HEADS * B_HEAD_DIM
    vmem = lambda shape, dtype: pltpu.VMEM((nseq,) + shape, dtype)
    o = pl.pallas_call(
        functools.partial(_mla_body, topk=topk),
        grid=(b // nseq, nq),
        in_specs=per_seq((1, IDX_HEADS, IDX_DIM, TQ), blk4) + per_seq((1, SUBLANES, TQ), blk3)
                 + per_seq((1, B_HEADS, B_KV_RANK, TQ), blk4) + per_seq((1, s, IDX_DIM), seq3)
                 + per_seq((1, s, B_KV_RANK), seq3) + [_full_spec(c.shape) for c in consts],
        out_specs=pl.BlockSpec((1, nseq, TQ, out_dim), lambda i, m: (i, 0, m, 0)),
        out_shape=jax.ShapeDtypeStruct((b // nseq, nseq, s, out_dim), BF16),
        scratch_shapes=[vmem((s, TQ), jnp.int32), vmem((s, TQ), jnp.int16), vmem((s, TQ), jnp.int16),
                        vmem((B_HEADS, TQ, TQ), F32), vmem((B_HEADS, TQ, TQ), F32),
                        vmem((B_HEADS, SUBLANES, TQ), F32), vmem((B_HEADS, SUBLANES, TQ), F32),
                        vmem((B_HEADS, SUBLANES, TQ), F32), vmem((B_HEADS, SUBLANES, TQ), F32),
                        vmem((B_HEADS, SUBLANES, TQ), F32),
                        vmem((B_HEADS, acc_rows, TQ), F32),
                        vmem((SUBLANES, TQ), F32)],
        compiler_params=pltpu.CompilerParams(dimension_semantics=("arbitrary", "arbitrary"),
                                             vmem_limit_bytes=VMEM_LIMIT_BYTES),
        name="indexer_topk_mla",
    )(*([iqs] * nseq + [iw] * nseq + [qabs] * nseq + [ik] * nseq + [ckv] * nseq), *consts)
    return o.reshape(b, s, out_dim)


def _out_body(x_ref, oa_ref, ob_ref, ag_ref, wg_ref, bg_ref, wa_ref, wb_ref, wo_ref, fg_ref,
              wfg_ref, wfu_ref, wfd_ref, ng_ref, y_ref):
    d = x_ref.shape[1]
    x = x_ref[...]
    hn = _rms(x, ag_ref[...]).astype(BF16)

    def gated(o_ref, w_ref, lo):
        gate = jax.nn.sigmoid(jnp.dot(hn, wg_ref[:, lo:lo + d], preferred_element_type=F32) + bg_ref[:, lo:lo + d])
        return gate * jnp.dot(o_ref[...], w_ref[...], preferred_element_type=F32)

    mixed = (gated(oa_ref, wa_ref, 0) + gated(ob_ref, wb_ref, d)).astype(BF16)
    h = x + jnp.dot(mixed, wo_ref[...], preferred_element_type=F32)
    hn2 = _rms(h, fg_ref[...]).astype(BF16)
    tiles_ff = wfg_ref.shape[1] // MXU_DIM
    bounds = [MXU_DIM * ((tiles_ff * i + FFN_CHUNKS - 1) // FFN_CHUNKS) for i in range(FFN_CHUNKS + 1)]
    for lo, hi in zip(bounds[:-1], bounds[1:]):
        g = jnp.dot(hn2, wfg_ref[:, lo:hi], preferred_element_type=F32)
        u = jnp.dot(hn2, wfu_ref[:, lo:hi], preferred_element_type=F32)
        act = (g * jax.nn.sigmoid(g) * u).astype(BF16)
        h = h + jnp.dot(act, wfd_ref[lo:hi, :], preferred_element_type=F32)
    y_ref[...] = _rms(h, ng_ref[...])


def _out_call(x2, oa, ob, ag, wg, bg, wa, wb, wo, fg, wfg, wfu, wfd, ng):
    n, d = x2.shape
    tm = TOK_TILE
    row = lambda i: (i, 0)

    def const_spec(a):
        return pl.BlockSpec(a.shape, lambda i: (0,) * a.ndim, pipeline_mode=pl.Buffered(1))

    consts = [ag, wg, bg, wa, wb, wo, fg, wfg, wfu, wfd, ng]
    return pl.pallas_call(
        _out_body,
        grid=(n // tm,),
        in_specs=[pl.BlockSpec((tm, d), row), pl.BlockSpec((tm, oa.shape[1]), row),
                  pl.BlockSpec((tm, ob.shape[1]), row)] + [const_spec(a) for a in consts],
        out_specs=pl.BlockSpec((tm, d), row),
        out_shape=jax.ShapeDtypeStruct((n, d), F32),
        compiler_params=pltpu.CompilerParams(dimension_semantics=("arbitrary",),
                                             vmem_limit_bytes=VMEM_LIMIT_BYTES),
        name="merge_ffn_norm",
    )(x2, oa, ob, *consts)


def kernel(x, meta_tokens, attn_norm_g, w_in, b_gates, q_norm_g, kv_norm_g, w_uq, w_uk, w_uv, idx_k_ln_g,
           idx_k_ln_b, sinks, rel_bias, w_branch_a, w_branch_b, w_out, ffn_norm_g, w_ffn_gate, w_ffn_up,
           w_ffn_down, final_norm_g):
    b, s, d = x.shape
    assert attn_norm_g.shape[0] == 1, "single-layer block"
    assert s % TQ == 0 and (b * s) % TOK_TILE == 0 and TOK_TILE % TQ == 0
    assert w_ffn_gate.shape[2] % MXU_DIM == 0
    topk = min(TOPK_MAX, s // 4)
    far_bkts = np.unique(_t5_bucket_np(np.arange(BLOCK + 1, s + BLOCK + 1)))
    assert far_bkts.size == 1
    far_bkt = int(far_bkts[0])

    wi = w_in[0]
    widths = (A_HEADS * A_HEAD_DIM, A_KV_HEADS * A_HEAD_DIM, A_KV_HEADS * A_HEAD_DIM, B_Q_RANK, B_KV_RANK,
              IDX_HEADS * IDX_DIM, IDX_DIM, IDX_HEADS, 2 * d)
    starts = np.concatenate([[0], np.cumsum(widths)])
    w_aq, w_ak, w_av, w_bq, w_bkv, w_iq, w_ik, w_iw, w_gates = (wi[:, int(a):int(b_)]
                                                                 for a, b_ in zip(starts[:-1], starts[1:]))
    zpad = lambda k: jnp.zeros((d, k), wi.dtype)
    w1 = jnp.concatenate([w_ak, w_bkv, w_ik, zpad(LANES - IDX_DIM)], axis=1).astype(BF16)
    wt = jnp.concatenate([w_aq, w_av, w_bq, w_iq, w_iw, zpad(BF16_ROWS - IDX_HEADS)],
                         axis=1).T.astype(BF16)
    wg = w_gates.astype(BF16)
    row2 = lambda v: v.reshape(1, -1).astype(F32)
    col2 = lambda v: v.reshape(-1, 1).astype(F32)
    wuk = jnp.transpose(w_uk[0], (1, 0, 2)).astype(BF16)
    wuvt = jnp.transpose(w_uv[0], (1, 2, 0)).astype(BF16)
    proj_w = (row2(attn_norm_g[0]), w1, wt, col2(q_norm_g[0]), w_uq[0].T.astype(BF16), wuk,
              row2(kv_norm_g[0]), row2(idx_k_ln_g[0]), row2(idx_k_ln_b[0]))

    bias_a, bnear, bmeta = _bias_call(rel_bias, far_bkt)

    x2 = x.reshape(b * s, d)
    aq_t, ak, av_t, qabs, ckv, iqs, ik, iw = _proj_call(x2, TOK_TILE, TQ, *proj_w)
    meta_blk = jnp.concatenate([jnp.zeros((PAD, d), x.dtype), meta_tokens.astype(x.dtype)], axis=0)
    _, mak, mav_t, _, mckv, _, mik, _ = _proj_call(meta_blk, BLOCK, BLOCK, *proj_w)

    o_a = _swa_call(sinks[0], aq_t, ak, av_t, mak, mav_t, bias_a, b)
    o_b = _mla_call(iqs, iw, qabs, ik.reshape(b, s, -1), ckv.reshape(b, s, -1), mik[PAD:], mckv[PAD:],
                    bnear, bmeta, wuvt, topk)

    y = _out_call(x2, o_a.reshape(b * s, -1), o_b.reshape(b * s, -1), row2(attn_norm_g[0]), wg,
                  row2(b_gates[0]), w_branch_a[0].astype(BF16), w_branch_b[0].astype(BF16),
                  w_out[0].astype(BF16), row2(ffn_norm_g[0]), w_ffn_gate[0].astype(BF16),
                  w_ffn_up[0].astype(BF16), w_ffn_down[0].astype(BF16), row2(final_norm_g))
    return y.reshape(b, s, d)
```

```python
import functools
import math

import numpy as np
import jax
import jax.numpy as jnp
from jax import lax
from jax.experimental import pallas as pl
from jax.experimental.pallas import tpu as pltpu

N_META = 16
BLOCK = 128
PAD = BLOCK - N_META
WINDOW = 128
A_HEADS = 8
A_KV_HEADS = 2
A_HEAD_DIM = 64
B_HEADS = 8
B_HEAD_DIM = 64
B_Q_RANK = 256
B_KV_RANK = 128
IDX_HEADS = 4
IDX_DIM = 64
TOPK_MAX = 256
N_BUCKETS = 32
MAX_DISTANCE = 128
EPS = 1e-6
MAX_FLOOR = -3.0e38
INT_MIN = -(2 ** 31)
LOG2E = math.log2(math.e)
I16_MIN = -(2 ** 15)
I16_MAX = 2 ** 15 - 1

LANES = 128
SUBLANES = 8
BF16_ROWS = 16
MXU_DIM = 256
VMEM_LIMIT_BYTES = 56 * 1024 * 1024

TOK_TILE = 512
PROJ_TILE = 1024
TQ = 256
FFN_CHUNKS = 1
SWA_BLOCKS = 8
MLA_SEQS = 2

C_AK = 0
C_BKV = C_AK + A_KV_HEADS * A_HEAD_DIM
C_IK = C_BKV + B_KV_RANK
C_END = C_IK + LANES
R_AQ = 0
R_AV = R_AQ + A_HEADS * A_HEAD_DIM
R_BQ = R_AV + A_KV_HEADS * A_HEAD_DIM
R_IQ = R_BQ + B_Q_RANK
R_IW = R_IQ + IDX_HEADS * IDX_DIM
R_END = R_IW + BF16_ROWS

F32 = jnp.float32
BF16 = jnp.bfloat16
NT_DIMS = (((1,), (1,)), ((), ()))


def _t5_bucket_np(dist):
    dist = np.asarray(dist, np.int64)
    max_exact = N_BUCKETS // 2
    d = np.maximum(dist, 1).astype(np.float32)
    large = max_exact + (np.log(d / np.float32(max_exact)) / np.float32(math.log(MAX_DISTANCE / max_exact))
                         * np.float32(N_BUCKETS - max_exact)).astype(np.int32)
    large = np.minimum(large, N_BUCKETS - 1)
    return np.where(dist < max_exact, dist, large).astype(np.int32)


def _rms(x, g):
    return x * lax.rsqrt(jnp.mean(x * x, axis=-1, keepdims=True) + EPS) * g


def _tree_reduce(op, x):
    while x.shape[0] > 1:
        half = x.shape[0] // 2
        x = op(x[:half], x[half:])
    return x[0]


def _full_spec(shape):
    nd = len(shape)
    return pl.BlockSpec(shape, lambda *_: (0,) * nd)


def _bias_body(tab_ref, bkt_a_ref, bkt_near_ref, bkt_meta_ref, ba_ref, bnear_ref, bmeta_ref, *, far_bkt):
    def lookup(bkt, col, fill):
        acc = jnp.full(bkt.shape, fill, F32)
        for b in range(N_BUCKETS):
            acc = jnp.where(bkt == b, tab_ref[b, col], acc)
        return acc

    bkt_a = bkt_a_ref[...]
    bkt_near = bkt_near_ref[...]
    grp = A_HEADS // A_KV_HEADS
    for h in range(A_HEADS):
        ba_ref[h // grp, :, (h % grp) * BLOCK:(h % grp + 1) * BLOCK] = lookup(bkt_a, h, -jnp.inf) * LOG2E
    for h in range(B_HEADS):
        col = A_HEADS + h
        bnear_ref[h, 0:TQ, :] = jnp.zeros((TQ, TQ), F32)
        bnear_ref[h, TQ:3 * TQ, :] = (lookup(bkt_near, col, 0.0) - tab_ref[far_bkt, col]) * LOG2E
        bmeta_ref[h] = (lookup(bkt_meta_ref[...], col, 0.0) - tab_ref[far_bkt, col]) * LOG2E


def _bias_call(rel_bias, far_bkt):
    k = np.arange(2 * BLOCK)[:, None]
    q = np.arange(BLOCK)[None, :]
    dist = q + BLOCK - k
    bkt_a = np.where((dist >= 0) & (dist < WINDOW), _t5_bucket_np(np.maximum(dist, 0)), -1).astype(np.int32)
    k = np.arange(2 * TQ)[:, None]
    q = np.arange(TQ)[None, :]
    bkt_near = _t5_bucket_np(np.maximum(q + TQ - k, 0))
    k = np.arange(N_META)[:, None]
    bkt_meta = _t5_bucket_np(q + N_META - k)
    vmem = pl.BlockSpec(memory_space=pltpu.VMEM)
    grp = A_HEADS // A_KV_HEADS
    return pl.pallas_call(
        functools.partial(_bias_body, far_bkt=far_bkt),
        out_shape=(jax.ShapeDtypeStruct((A_KV_HEADS, 2 * BLOCK, grp * BLOCK), F32),
                   jax.ShapeDtypeStruct((B_HEADS, 3 * TQ, TQ), F32),
                   jax.ShapeDtypeStruct((B_HEADS, N_META, TQ), F32)),
        in_specs=[pl.BlockSpec(memory_space=pltpu.SMEM), vmem, vmem, vmem],
        out_specs=(vmem, vmem, vmem),
        name="bias_tables",
    )(rel_bias, jnp.asarray(bkt_a), jnp.asarray(bkt_near), jnp.asarray(bkt_meta))


def _proj_body(x_ref, g_ref, w1_ref, wt_ref, qg_ref, wuq_ref, wuk_ref, kvg_ref, ikg_ref, ikb_ref,
               aq_ref, ak_ref, av_ref, qabs_ref, ckv_ref, iqs_ref, ik_ref, iw_ref, *, tq):
    tm = x_ref.shape[0]
    nblk = tm // tq
    nblk_a = tm // BLOCK
    grp = A_HEADS // A_KV_HEADS
    hn = _rms(x_ref[...], g_ref[...]).astype(BF16)

    def proj(lo, hi):
        return jnp.dot(hn, w1_ref[:, lo:hi], preferred_element_type=F32)

    ak_ref[...] = proj(C_AK, C_BKV).astype(BF16)

    feat_t = lax.dot_general(wt_ref[...], hn, NT_DIMS, preferred_element_type=F32)
    aq_t = (feat_t[R_AQ:R_AV] * (A_HEAD_DIM ** -0.5 * LOG2E)).astype(BF16)
    for j in range(nblk_a):
        tok = slice(j * BLOCK, (j + 1) * BLOCK)
        for h in range(A_HEADS):
            aq_ref[j, h // grp, :, (h % grp) * BLOCK:(h % grp + 1) * BLOCK] = \
                aq_t[h * A_HEAD_DIM:(h + 1) * A_HEAD_DIM, tok]
    av_t = feat_t[R_AV:R_BQ].astype(BF16)
    for j in range(nblk_a):
        for kvh in range(A_KV_HEADS):
            av_ref[j, kvh] = av_t[kvh * A_HEAD_DIM:(kvh + 1) * A_HEAD_DIM, j * BLOCK:(j + 1) * BLOCK]

    bq_t = feat_t[R_BQ:R_IQ]
    qn_t = bq_t * lax.rsqrt(jnp.mean(bq_t * bq_t, axis=0, keepdims=True) + EPS) * qg_ref[...]
    q_t = jnp.dot(wuq_ref[...], qn_t.astype(BF16), preferred_element_type=F32).astype(BF16)
    for h in range(B_HEADS):
        qa_t = jnp.dot(wuk_ref[h], q_t[h * B_HEAD_DIM:(h + 1) * B_HEAD_DIM], preferred_element_type=F32)
        qa_t = (qa_t * (B_HEAD_DIM ** -0.5 * LOG2E)).astype(BF16)
        for j in range(nblk):
            qabs_ref[j, h] = qa_t[:, j * tq:(j + 1) * tq]
    iq_t = feat_t[R_IQ:R_IW].astype(BF16)
    for h in range(IDX_HEADS):
        for j in range(nblk):
            iqs_ref[j, h] = iq_t[h * IDX_DIM:(h + 1) * IDX_DIM, j * tq:(j + 1) * tq]

    ckv_ref[...] = _rms(proj(C_BKV, C_IK), kvg_ref[...]).astype(BF16)

    iw_t = feat_t[R_IW:R_IW + SUBLANES] * ((IDX_HEADS * IDX_DIM) ** -0.5)
    for j in range(nblk):
        iw_ref[j] = iw_t[:, j * tq:(j + 1) * tq]

    ik = proj(C_IK, C_END)[:, :IDX_DIM]
    mu = jnp.mean(ik, axis=-1, keepdims=True)
    xc = ik - mu
    var = jnp.mean(xc * xc, axis=-1, keepdims=True)
    ik_ref[...] = (xc * lax.rsqrt(var + EPS) * ikg_ref[...] + ikb_ref[...]).astype(BF16)


def _proj_call(x2, tm, tq, *weights):
    n, d = x2.shape
    grid = (n // tm,)
    row = lambda i: (i, 0)
    blk4 = lambda i: (i, 0, 0, 0)
    grp = A_HEADS // A_KV_HEADS
    out_shape = (
        jax.ShapeDtypeStruct((n // BLOCK, A_KV_HEADS, A_HEAD_DIM, grp * BLOCK), BF16),
        jax.ShapeDtypeStruct((n, A_KV_HEADS * A_HEAD_DIM), BF16),
        jax.ShapeDtypeStruct((n // BLOCK, A_KV_HEADS, A_HEAD_DIM, BLOCK), BF16),
        jax.ShapeDtypeStruct((n // tq, B_HEADS, B_KV_RANK, tq), BF16),
        jax.ShapeDtypeStruct((n, B_KV_RANK), BF16),
        jax.ShapeDtypeStruct((n // tq, IDX_HEADS, IDX_DIM, tq), BF16),
        jax.ShapeDtypeStruct((n, IDX_DIM), BF16),
        jax.ShapeDtypeStruct((n // tq, SUBLANES, tq), F32),
    )
    out_specs = (
        pl.BlockSpec((tm // BLOCK, A_KV_HEADS, A_HEAD_DIM, grp * BLOCK), blk4),
        pl.BlockSpec((tm, A_KV_HEADS * A_HEAD_DIM), row),
        pl.BlockSpec((tm // BLOCK, A_KV_HEADS, A_HEAD_DIM, BLOCK), blk4),
        pl.BlockSpec((tm // tq, B_HEADS, B_KV_RANK, tq), blk4),
        pl.BlockSpec((tm, B_KV_RANK), row),
        pl.BlockSpec((tm // tq, IDX_HEADS, IDX_DIM, tq), blk4),
        pl.BlockSpec((tm, IDX_DIM), row),
        pl.BlockSpec((tm // tq, SUBLANES, tq), lambda i: (i, 0, 0)),
    )
    in_specs = [pl.BlockSpec((tm, d), row)] + [_full_spec(w.shape) for w in weights]
    return pl.pallas_call(
        functools.partial(_proj_body, tq=tq),
        grid=grid, in_specs=in_specs, out_specs=out_specs, out_shape=out_shape,
        compiler_params=pltpu.CompilerParams(dimension_semantics=("arbitrary",),
                                             vmem_limit_bytes=VMEM_LIMIT_BYTES),
        name="in_proj",
    )(x2, *weights)


def _swa_body(sinks_ref, aq_ref, kcur_ref, kprev_ref, kmeta_ref, vcur_ref, vprev_ref, vmeta_ref, bias_ref, o_ref):
    n = pl.program_id(1)
    first = n == 0
    grp = A_HEADS // A_KV_HEADS
    width = grp * BLOCK
    kall = jnp.concatenate([jnp.where(first, kmeta_ref[...], kprev_ref[0]), kcur_ref[0]], axis=0)
    vall = [jnp.concatenate([jnp.where(first, vmeta_ref[0, kvh], vprev_ref[0, kvh])]
                            + [vcur_ref[j, kvh] for j in range(SWA_BLOCKS)], axis=1)
            for kvh in range(A_KV_HEADS)]
    ntile = 2 * BLOCK // SUBLANES
    key_row = (lax.broadcasted_iota(jnp.int32, (ntile, SUBLANES, width), 0) * SUBLANES
               + lax.broadcasted_iota(jnp.int32, (ntile, SUBLANES, width), 1))
    pad_row = first & (key_row < PAD)
    lane_head = lax.broadcasted_iota(jnp.int32, (1, width), 1) // BLOCK
    ones = jnp.ones((BF16_ROWS, 2 * BLOCK), BF16)
    sinks = []
    for kvh in range(A_KV_HEADS):
        sink = jnp.zeros((1, width), F32)
        for g in range(grp):
            sink = jnp.where(lane_head == g, sinks_ref[kvh * grp + g] * LOG2E, sink)
        sinks.append(sink)
    probs_ids = [(j, kvh) for j in range(SWA_BLOCKS) for kvh in range(A_KV_HEADS)]
    scores = [jnp.dot(kall[j * BLOCK:(j + 2) * BLOCK, kvh * A_HEAD_DIM:(kvh + 1) * A_HEAD_DIM], aq_ref[j, kvh],
                      preferred_element_type=F32) for j, kvh in probs_ids]
    maxes, probs = [], []
    for i, (j, kvh) in enumerate(probs_ids):
        s = (scores[i] + bias_ref[kvh]).reshape(ntile, SUBLANES, width)
        if j == 0:
            s = jnp.where(pad_row, -jnp.inf, s)
        m = jnp.maximum(jnp.max(_tree_reduce(jnp.maximum, s), axis=0, keepdims=True), sinks[kvh])
        maxes.append(m)
        probs.append(jnp.exp2(s - m[None]).reshape(2 * BLOCK, width).astype(BF16))
    pvs = [jnp.dot(jnp.concatenate([vall[kvh][:, j * BLOCK:(j + 2) * BLOCK], ones], axis=0), probs[i],
                   preferred_element_type=F32) for i, (j, kvh) in enumerate(probs_ids)]
    for j in range(SWA_BLOCKS):
        heads_t = []
        for kvh in range(A_KV_HEADS):
            i = j * A_KV_HEADS + kvh
            den = pvs[i][A_HEAD_DIM:A_HEAD_DIM + 1] + jnp.exp2(sinks[kvh] - maxes[i])
            o_t = pvs[i][0:A_HEAD_DIM] * (1.0 / den)
            heads_t += [o_t[:, g * BLOCK:(g + 1) * BLOCK] for g in range(grp)]
        o_ref[0, j * BLOCK:(j + 1) * BLOCK, :] = jnp.concatenate(heads_t, axis=0).T.astype(BF16)


def _swa_call(sinks, aq_t, ak, av_t, mak, mav_t, bias_a, b):
    nblocks = aq_t.shape[0]
    nb = nblocks // b
    assert nb % SWA_BLOCKS == 0
    ns = nb // SWA_BLOCKS
    s = nb * BLOCK
    rows = SWA_BLOCKS * BLOCK
    cur4 = lambda i, n: (i * ns + n, 0, 0, 0)
    prev_blk = lambda n: jnp.maximum(n * SWA_BLOCKS - 1, 0)
    kdim = ak.shape[-1]
    ak3 = ak.reshape(b, s, kdim)
    return pl.pallas_call(
        _swa_body,
        grid=(b, ns),
        in_specs=[pl.BlockSpec(memory_space=pltpu.SMEM),
                  pl.BlockSpec((SWA_BLOCKS,) + aq_t.shape[1:], cur4),
                  pl.BlockSpec((1, rows, kdim), lambda i, n: (i, n, 0)),
                  pl.BlockSpec((1, BLOCK, kdim), lambda i, n: (i, prev_blk(n), 0)),
                  _full_spec(mak.shape),
                  pl.BlockSpec((SWA_BLOCKS,) + av_t.shape[1:], cur4),
                  pl.BlockSpec((1,) + av_t.shape[1:], lambda i, n: (i * nb + prev_blk(n), 0, 0, 0)),
                  _full_spec(mav_t.shape), _full_spec(bias_a.shape)],
        out_specs=pl.BlockSpec((1, rows, A_HEADS * A_HEAD_DIM), lambda i, n: (i, n, 0)),
        out_shape=jax.ShapeDtypeStruct((b, s, A_HEADS * A_HEAD_DIM), BF16),
        compiler_params=pltpu.CompilerParams(dimension_semantics=("arbitrary", "arbitrary"),
                                             vmem_limit_bytes=VMEM_LIMIT_BYTES),
        name="swa_sink_attention",
    )(sinks, aq_t, ak3, ak3, mak, av_t, av_t, mav_t, bias_a)


def _mla_body(*refs, topk):
    it = iter(refs)
    take = lambda n: [next(it) for _ in range(n)]
    iqs_refs, iw_refs, qabs_refs, ik_refs, ckv_refs = (take(MLA_SEQS) for _ in range(5))
    mik_ref, mckv_ref, mvext_ref, bnear_ref, bmeta_ref, wuvt_ref, tril_ref, mtril_ref = take(8)
    o_all, = take(1)
    o_refs = [o_all.at[0, p] for p in range(MLA_SEQS)]
    scratch = take(12)
    seqs = range(MLA_SEQS)
    m_blk = pl.program_id(1)
    first = m_blk == 0
    nchunks = m_blk + 1
    groups = TQ // SUBLANES
    acc_rows = B_KV_RANK + BF16_ROWS
    kf = float(topk)

    sub_pos = (lax.broadcasted_iota(jnp.int32, (groups, SUBLANES, TQ), 0) * SUBLANES
               + lax.broadcasted_iota(jnp.int32, (groups, SUBLANES, TQ), 1))
    qpos = m_blk * TQ + lax.broadcasted_iota(jnp.int32, (groups, SUBLANES, TQ), 2)

    def rows_of(c):
        return pl.ds(pl.multiple_of(c * TQ, TQ), TQ)

    def tiles(x):
        return x.reshape(groups, SUBLANES, x.shape[-1])

    def sort_key(sc):
        bits = lax.bitcast_convert_type(sc, jnp.int32)
        return jnp.where(bits < 0, INT_MIN - bits, bits)

    iqs_ref, iw_ref, qabs_ref, ik_ref, ckv_ref, o_ref = iqs_refs, iw_refs, qabs_refs, ik_refs, ckv_refs, o_refs
    (key_sc, khi_sc, klo_sc, s0_sc, s1_sc, mb0_sc, mb1_sc, al0_sc, al1_sc, m_sc, acc_sc, tie_sc) = (
        [r.at[p] for p in seqs] for r in scratch)
    wrow = [[iw_ref[p][0, h:h + 1, :] for h in range(IDX_HEADS)] for p in seqs]

    def index_score(p, ik_c):
        sc = None
        for h in range(IDX_HEADS):
            logits = jnp.dot(ik_c, iqs_ref[p][0, h], preferred_element_type=F32)
            term = jnp.maximum(logits, 0.0) * wrow[p][h]
            sc = term if sc is None else sc + term
        return sort_key(sc)

    def digits(key):
        return jnp.right_shift(key, 16).astype(jnp.int16), (key ^ 0x8000).astype(jnp.int16)

    def score_chunk(c):
        adm = c * TQ + sub_pos <= qpos
        for p in seqs:
            key = jnp.where(adm, tiles(index_score(p, ik_ref[p][0, rows_of(c), :])), INT_MIN).reshape(TQ, TQ)
            key_sc[p][rows_of(c), :] = key
            khi_sc[p][rows_of(c), :], klo_sc[p][rows_of(c), :] = digits(key)

    key_m = [index_score(p, mik_ref[...]) for p in seqs]
    khi_m, klo_m = zip(*[digits(k) for k in key_m])

    def score_pair(i, carry):
        score_chunk(2 * i)
        score_chunk(2 * i + 1)
        return carry

    lax.fori_loop(0, lax.shift_right_logical(nchunks, 1), score_pair, 0)

    @pl.when((nchunks & 1) == 1)
    def _():
        score_chunk(nchunks - 1)

    def count_gt(thrs):
        def hits(p, kp):
            return _tree_reduce(jnp.add, jnp.where(kp > thrs[p][None], 1.0, 0.0))

        def body(c, parts):
            return tuple(parts[p] + hits(p, tiles(key_sc[p][rows_of(c), :])) for p in seqs)
        parts = lax.fori_loop(0, nchunks, body,
                              tuple(hits(p, key_m[p].reshape(N_META // SUBLANES, SUBLANES, TQ)) for p in seqs))
        return [jnp.broadcast_to(jnp.sum(part, axis=0, keepdims=True), (SUBLANES, TQ)) for part in parts]

    def tiles16(x):
        return x.reshape(TQ // BF16_ROWS, BF16_ROWS, TQ)

    def count16_ge(refs16, meta_digits, cands):
        cand16 = [jnp.broadcast_to(cands[p][0:1], (BF16_ROWS, TQ)).astype(jnp.int16) for p in seqs]
        one, zero = jnp.int16(1), jnp.int16(0)

        def hits(p, c):
            hit = tiles16(refs16[p][rows_of(c), :]) >= cand16[p][None]
            return _tree_reduce(jnp.add, jnp.where(hit, one, zero))

        def two_chunks(i, parts):
            return tuple(parts[p] + hits(p, 2 * i) + hits(p, 2 * i + 1) for p in seqs)
        parts = lax.fori_loop(0, lax.shift_right_logical(nchunks, 1), two_chunks,
                              tuple(jnp.where(meta_digits[p] >= cand16[p], one, zero) for p in seqs))
        parts = lax.cond((nchunks & 1) == 1,
                         lambda ps: tuple(ps[p] + hits(p, nchunks - 1) for p in seqs), lambda ps: ps, parts)
        return [jnp.broadcast_to(jnp.sum(part.astype(jnp.int32), axis=0, keepdims=True), (SUBLANES, TQ))
                for part in parts]

    def digit_search(refs16, meta_digits):
        def step(i, ts):
            cands = [t + jnp.left_shift(jnp.int32(1), 15 - i) for t in ts]
            counts = count16_ge(refs16, meta_digits, cands)
            return tuple(jnp.where(counts[p] >= topk, cands[p], ts[p]) for p in seqs)
        return lax.fori_loop(0, 16, step, tuple(jnp.full((SUBLANES, TQ), I16_MIN, jnp.int32) for _ in seqs))

    t_hi = digit_search(khi_sc, khi_m)
    t_hi16 = [jnp.broadcast_to(t[0:1], (BF16_ROWS, TQ)).astype(jnp.int16) for t in t_hi]

    def pinned(hi, lo, t):
        return jnp.where(hi > t, jnp.int16(I16_MAX), jnp.where(hi < t, jnp.int16(I16_MIN), lo))

    def pin_low(c, carry):
        for p in seqs:
            lo = pinned(tiles16(khi_sc[p][rows_of(c), :]), tiles16(klo_sc[p][rows_of(c), :]), t_hi16[p][None])
            klo_sc[p][rows_of(c), :] = lo.reshape(TQ, TQ)
        return carry

    lax.fori_loop(0, nchunks, pin_low, 0)
    t_lo = digit_search(klo_sc, [pinned(khi_m[p], klo_m[p], t_hi16[p]) for p in seqs])
    thr = [t_hi[p] * 65536 + (t_lo[p] - I16_MIN) for p in seqs]
    need = [kf - cnt for cnt in count_gt(thr)]

    def pad_rows(x):
        return jnp.concatenate([x, jnp.zeros((LANES - N_META, TQ), x.dtype)], axis=0)

    mask_m = []
    for p in seqs:
        thr_m = jnp.broadcast_to(thr[p][0:1], (N_META, TQ))
        tied_m = key_m[p] == thr_m
        rank_m = jnp.dot(mtril_ref[...], pad_rows(jnp.where(tied_m, 1.0, 0.0).astype(BF16)),
                         preferred_element_type=F32)
        tie_sc[p][...] = jnp.broadcast_to(rank_m[N_META - 1:N_META], (SUBLANES, TQ))
        sel_m = (key_m[p] > thr_m) | (tied_m & (rank_m <= jnp.broadcast_to(need[p][0:1], (N_META, TQ))))
        mask_m.append(jnp.where(sel_m, 0.0, -jnp.inf))
    heads = [(p, h) for p in seqs for h in range(B_HEADS)]
    s_m = [jnp.dot(mckv_ref[...], qabs_ref[p][0, h], preferred_element_type=F32) for p, h in heads]
    e_m = []
    for i, (p, h) in enumerate(heads):
        sh = s_m[i] + jnp.where(first, bmeta_ref[h], 0.0) + mask_m[p]
        m_new = jnp.maximum(jnp.max(sh, axis=0, keepdims=True), MAX_FLOOR)
        m_sc[p][h] = jnp.broadcast_to(m_new, (SUBLANES, TQ))
        e_m.append(pad_rows(jnp.exp2(sh - m_new).astype(BF16)))
    for i, (p, h) in enumerate(heads):
        acc_sc[p][h] = jnp.dot(mvext_ref[...], e_m[i], preferred_element_type=F32)

    def score_stage(c, buf):
        bias_row0 = pl.multiple_of(jnp.maximum(c - m_blk + 2, 0) * TQ, TQ)
        for p in seqs:
            s_buf, m_buf, al_buf = (b[p] for b in buf)
            ckv_c = ckv_ref[p][0, rows_of(c), :]
            kp = tiles(key_sc[p][rows_of(c), :])
            tied = kp == thr[p][None]
            tied_b = jnp.where(tied, 1.0, 0.0).reshape(TQ, TQ).astype(BF16)
            rank = tiles(jnp.dot(tril_ref[...], tied_b, preferred_element_type=F32)) + tie_sc[p][...][None]
            tie_sc[p][...] = jnp.broadcast_to(rank[groups - 1, SUBLANES - 1:SUBLANES, :], (SUBLANES, TQ))
            sel = ((kp > thr[p][None]) | (tied & (rank <= need[p][None]))) & (kp != INT_MIN)
            mask = jnp.where(sel, 0.0, -jnp.inf)
            for h in range(B_HEADS):
                sh = jnp.dot(ckv_c, qabs_ref[p][0, h], preferred_element_type=F32)
                sh = tiles(sh + bnear_ref[h, pl.ds(bias_row0, TQ), :]) + mask
                s_buf[h] = sh.reshape(TQ, TQ)
                m_prev = m_sc[p][h]
                mx = jnp.max(_tree_reduce(jnp.maximum, sh), axis=0, keepdims=True)
                m_new = jnp.maximum(m_prev, mx)
                al_buf[h] = jnp.exp2(m_prev - m_new)
                m_buf[h] = m_new
                m_sc[p][h] = m_new

    def value_stage(c, buf):
        for p in seqs:
            s_buf, m_buf, al_buf = (b[p] for b in buf)
            vt = ckv_ref[p][0, rows_of(c), :].astype(F32).T.astype(BF16)
            vext = jnp.concatenate([vt, jnp.ones((BF16_ROWS, TQ), BF16)], axis=0)
            for h in range(B_HEADS):
                e = jnp.exp2(tiles(s_buf[h]) - m_buf[h][None]).reshape(TQ, TQ).astype(BF16)
                pv = jnp.dot(vext, e, preferred_element_type=F32)
                acc = acc_sc[p][h].reshape(acc_rows // SUBLANES, SUBLANES, TQ)
                acc_sc[p][h] = (acc * al_buf[h][None] + pv.reshape(acc.shape)).reshape(acc_rows, TQ)

    buf0, buf1 = (s0_sc, mb0_sc, al0_sc), (s1_sc, mb1_sc, al1_sc)
    score_stage(0, buf0)

    def chunk_pair(i, carry):
        c = 2 * i + 1
        value_stage(c - 1, buf0)
        score_stage(c, buf1)
        value_stage(c, buf1)
        score_stage(c + 1, buf0)
        return carry

    npairs = (nchunks - 1) // 2
    lax.fori_loop(0, npairs, chunk_pair, 0)
    last = nchunks - 1

    @pl.when(last == 2 * npairs)
    def _():
        value_stage(last, buf0)

    @pl.when(last != 2 * npairs)
    def _():
        value_stage(last - 1, buf0)
        score_stage(last, buf1)
        value_stage(last, buf1)

    for p in seqs:
        outs = []
        for h in range(B_HEADS):
            acc = acc_sc[p][h]
            lat = (acc[0:B_KV_RANK] * (1.0 / acc[B_KV_RANK:B_KV_RANK + 1])).astype(BF16)
            outs.append(jnp.dot(wuvt_ref[h], lat, preferred_element_type=F32))
        o_ref[p][...] = jnp.concatenate(outs, axis=0).T.astype(BF16)


def _mla_call(iqs, iw, qabs, ik, ckv, mik, mckv, bnear, bmeta, wuvt, topk):
    b, s, _ = ik.shape
    nq = s // TQ
    acc_rows = B_KV_RANK + BF16_ROWS
    assert N_META == BF16_ROWS, "the meta keys are handled as one packed bf16 tile"
    tril = jnp.asarray(np.tril(np.ones((TQ, TQ), np.float32)), BF16)
    mtril = jnp.asarray(np.pad(np.tril(np.ones((N_META, N_META), np.float32)), ((0, 0), (0, LANES - N_META))), BF16)
    mvext = jnp.pad(jnp.concatenate([mckv.T, jnp.ones((BF16_ROWS, N_META), BF16)], axis=0),
                    ((0, 0), (0, LANES - N_META)))
    consts = (mik, mckv, mvext, bnear, bmeta, wuvt, tril, mtril)
    nseq = MLA_SEQS
    assert b % nseq == 0

    def per_seq(block, index_of):
        return [pl.BlockSpec(block, functools.partial(index_of, p)) for p in range(nseq)]

    blk4 = lambda p, i, m: ((i * nseq + p) * nq + m, 0, 0, 0)
    blk3 = lambda p, i, m: ((i * nseq + p) * nq + m, 0, 0)
    seq3 = lambda p, i, m: (i * nseq + p, 0, 0)
    out_dim = B_HEADS * B_HEAD_DIM
    vmem = lambda shape, dtype: pltpu.VMEM((nseq,) + shape, dtype)
    o = pl.pallas_call(
        functools.partial(_mla_body, topk=topk),
        grid=(b // nseq, nq),
        in_specs=per_seq((1, IDX_HEADS, IDX_DIM, TQ), blk4) + per_seq((1, SUBLANES, TQ), blk3)
                 + per_seq((1, B_HEADS, B_KV_RANK, TQ), blk4) + per_seq((1, s, IDX_DIM), seq3)
                 + per_seq((1, s, B_KV_RANK), seq3) + [_full_spec(c.shape) for c in consts],
        out_specs=pl.BlockSpec((1, nseq, TQ, out_dim), lambda i, m: (i, 0, m, 0)),
        out_shape=jax.ShapeDtypeStruct((b // nseq, nseq, s, out_dim), BF16),
        scratch_shapes=[vmem((s, TQ), jnp.int32), vmem((s, TQ), jnp.int16), vmem((s, TQ), jnp.int16),
                        vmem((B_HEADS, TQ, TQ), F32), vmem((B_HEADS, TQ, TQ), F32),
                        vmem((B_HEADS, SUBLANES, TQ), F32), vmem((B_HEADS, SUBLANES, TQ), F32),
                        vmem((B_HEADS, SUBLANES, TQ), F32), vmem((B_HEADS, SUBLANES, TQ), F32),
                        vmem((B_HEADS, SUBLANES, TQ), F32),
                        vmem((B_HEADS, acc_rows, TQ), F32),
                        vmem((SUBLANES, TQ), F32)],
        compiler_params=pltpu.CompilerParams(dimension_semantics=("arbitrary", "arbitrary"),
                                             vmem_limit_bytes=VMEM_LIMIT_BYTES),
        name="indexer_topk_mla",
    )(*([iqs] * nseq + [iw] * nseq + [qabs] * nseq + [ik] * nseq + [ckv] * nseq), *consts)
    return o.reshape(b, s, out_dim)


def _out_body(x_ref, oa_ref, ob_ref, ag_ref, wg_ref, bg_ref, wa_ref, wb_ref, wo_ref, fg_ref,
              wfg_ref, wfu_ref, wfd_ref, ng_ref, y_ref):
    d = x_ref.shape[1]
    x = x_ref[...]
    hn = _rms(x, ag_ref[...]).astype(BF16)

    def gated(o_ref, w_ref, lo):
        gate = jax.nn.sigmoid(jnp.dot(hn, wg_ref[:, lo:lo + d], preferred_element_type=F32) + bg_ref[:, lo:lo + d])
        return gate * jnp.dot(o_ref[...], w_ref[...], preferred_element_type=F32)

    mixed = (gated(oa_ref, wa_ref, 0) + gated(ob_ref, wb_ref, d)).astype(BF16)
    h = x + jnp.dot(mixed, wo_ref[...], preferred_element_type=F32)
    hn2 = _rms(h, fg_ref[...]).astype(BF16)
    tiles_ff = wfg_ref.shape[1] // MXU_DIM
    bounds = [MXU_DIM * ((tiles_ff * i + FFN_CHUNKS - 1) // FFN_CHUNKS) for i in range(FFN_CHUNKS + 1)]
    for lo, hi in zip(bounds[:-1], bounds[1:]):
        g = jnp.dot(hn2, wfg_ref[:, lo:hi], preferred_element_type=F32)
        u = jnp.dot(hn2, wfu_ref[:, lo:hi], preferred_element_type=F32)
        act = (g * jax.nn.sigmoid(g) * u).astype(BF16)
        h = h + jnp.dot(act, wfd_ref[lo:hi, :], preferred_element_type=F32)
    y_ref[...] = _rms(h, ng_ref[...])


def _out_call(x2, oa, ob, ag, wg, bg, wa, wb, wo, fg, wfg, wfu, wfd, ng):
    n, d = x2.shape
    tm = TOK_TILE
    row = lambda i: (i, 0)

    def const_spec(a):
        return pl.BlockSpec(a.shape, lambda i: (0,) * a.ndim, pipeline_mode=pl.Buffered(1))

    consts = [ag, wg, bg, wa, wb, wo, fg, wfg, wfu, wfd, ng]
    return pl.pallas_call(
        _out_body,
        grid=(n // tm,),
        in_specs=[pl.BlockSpec((tm, d), row), pl.BlockSpec((tm, oa.shape[1]), row),
                  pl.BlockSpec((tm, ob.shape[1]), row)] + [const_spec(a) for a in consts],
        out_specs=pl.BlockSpec((tm, d), row),
        out_shape=jax.ShapeDtypeStruct((n, d), F32),
        compiler_params=pltpu.CompilerParams(dimension_semantics=("arbitrary",),
                                             vmem_limit_bytes=VMEM_LIMIT_BYTES),
        name="merge_ffn_norm",
    )(x2, oa, ob, *consts)


def kernel(x, meta_tokens, attn_norm_g, w_in, b_gates, q_norm_g, kv_norm_g, w_uq, w_uk, w_uv, idx_k_ln_g,
           idx_k_ln_b, sinks, rel_bias, w_branch_a, w_branch_b, w_out, ffn_norm_g, w_ffn_gate, w_ffn_up,
           w_ffn_down, final_norm_g):
    b, s, d = x.shape
    assert attn_norm_g.shape[0] == 1, "single-layer block"
    assert s % TQ == 0 and (b * s) % TOK_TILE == 0 and (b * s) % PROJ_TILE == 0 and PROJ_TILE % TQ == 0
    assert w_ffn_gate.shape[2] % MXU_DIM == 0
    topk = min(TOPK_MAX, s // 4)
    far_bkts = np.unique(_t5_bucket_np(np.arange(BLOCK + 1, s + BLOCK + 1)))
    assert far_bkts.size == 1
    far_bkt = int(far_bkts[0])

    wi = w_in[0]
    widths = (A_HEADS * A_HEAD_DIM, A_KV_HEADS * A_HEAD_DIM, A_KV_HEADS * A_HEAD_DIM, B_Q_RANK, B_KV_RANK,
              IDX_HEADS * IDX_DIM, IDX_DIM, IDX_HEADS, 2 * d)
    starts = np.concatenate([[0], np.cumsum(widths)])
    w_aq, w_ak, w_av, w_bq, w_bkv, w_iq, w_ik, w_iw, w_gates = (wi[:, int(a):int(b_)]
                                                                 for a, b_ in zip(starts[:-1], starts[1:]))
    zpad = lambda k: jnp.zeros((d, k), wi.dtype)
    w1 = jnp.concatenate([w_ak, w_bkv, w_ik, zpad(LANES - IDX_DIM)], axis=1).astype(BF16)
    wt = jnp.concatenate([w_aq, w_av, w_bq, w_iq, w_iw, zpad(BF16_ROWS - IDX_HEADS)],
                         axis=1).T.astype(BF16)
    wg = w_gates.astype(BF16)
    row2 = lambda v: v.reshape(1, -1).astype(F32)
    col2 = lambda v: v.reshape(-1, 1).astype(F32)
    wuk = jnp.transpose(w_uk[0], (1, 0, 2)).astype(BF16)
    wuvt = jnp.transpose(w_uv[0], (1, 2, 0)).astype(BF16)
    proj_w = (row2(attn_norm_g[0]), w1, wt, col2(q_norm_g[0]), w_uq[0].T.astype(BF16), wuk,
              row2(kv_norm_g[0]), row2(idx_k_ln_g[0]), row2(idx_k_ln_b[0]))

    bias_a, bnear, bmeta = _bias_call(rel_bias, far_bkt)

    x2 = x.reshape(b * s, d)
    aq_t, ak, av_t, qabs, ckv, iqs, ik, iw = _proj_call(x2, PROJ_TILE, TQ, *proj_w)
    meta_blk = jnp.concatenate([jnp.zeros((PAD, d), x.dtype), meta_tokens.astype(x.dtype)], axis=0)
    _, mak, mav_t, _, mckv, _, mik, _ = _proj_call(meta_blk, BLOCK, BLOCK, *proj_w)

    o_a = _swa_call(sinks[0], aq_t, ak, av_t, mak, mav_t, bias_a, b)
    o_b = _mla_call(iqs, iw, qabs, ik.reshape(b, s, -1), ckv.reshape(b, s, -1), mik[PAD:], mckv[PAD:],
                    bnear, bmeta, wuvt, topk)

    y = _out_call(x2, o_a.reshape(b * s, -1), o_b.reshape(b * s, -1), row2(attn_norm_g[0]), wg,
                  row2(b_gates[0]), w_branch_a[0].astype(BF16), w_branch_b[0].astype(BF16),
                  w_out[0].astype(BF16), row2(ffn_norm_g[0]), w_ffn_gate[0].astype(BF16),
                  w_ffn_up[0].astype(BF16), w_ffn_down[0].astype(BF16), row2(final_norm_g))
    return y.reshape(b, s, d)
```

```python
import functools
import math

import numpy as np
import jax
import jax.numpy as jnp
from jax import lax
from jax.experimental import pallas as pl
from jax.experimental.pallas import tpu as pltpu

N_META = 16
BLOCK = 128
PAD = BLOCK - N_META
WINDOW = 128
A_HEADS = 8
A_KV_HEADS = 2
A_HEAD_DIM = 64
B_HEADS = 8
B_HEAD_DIM = 64
B_Q_RANK = 256
B_KV_RANK = 128
IDX_HEADS = 4
IDX_DIM = 64
TOPK_MAX = 256
N_BUCKETS = 32
MAX_DISTANCE = 128
EPS = 1e-6
MAX_FLOOR = -3.0e38
INT_MIN = -(2 ** 31)
LOG2E = math.log2(math.e)
I16_MIN = -(2 ** 15)
I16_MAX = 2 ** 15 - 1

LANES = 128
SUBLANES = 8
BF16_ROWS = 16
MXU_DIM = 256
VMEM_LIMIT_BYTES = 56 * 1024 * 1024

TOK_TILE = 512
PROJ_TILE = 2048
TQ = 256
FFN_CHUNKS = 1
SWA_BLOCKS = 8
MLA_SEQS = 2

C_AK = 0
C_BKV = C_AK + A_KV_HEADS * A_HEAD_DIM
C_IK = C_BKV + B_KV_RANK
C_END = C_IK + LANES
R_AQ = 0
R_AV = R_AQ + A_HEADS * A_HEAD_DIM
R_BQ = R_AV + A_KV_HEADS * A_HEAD_DIM
R_IQ = R_BQ + B_Q_RANK
R_IW = R_IQ + IDX_HEADS * IDX_DIM
R_END = R_IW + BF16_ROWS

F32 = jnp.float32
BF16 = jnp.bfloat16
NT_DIMS = (((1,), (1,)), ((), ()))


def _t5_bucket_np(dist):
    dist = np.asarray(dist, np.int64)
    max_exact = N_BUCKETS // 2
    d = np.maximum(dist, 1).astype(np.float32)
    large = max_exact + (np.log(d / np.float32(max_exact)) / np.float32(math.log(MAX_DISTANCE / max_exact))
                         * np.float32(N_BUCKETS - max_exact)).astype(np.int32)
    large = np.minimum(large, N_BUCKETS - 1)
    return np.where(dist < max_exact, dist, large).astype(np.int32)


def _rms(x, g):
    return x * lax.rsqrt(jnp.mean(x * x, axis=-1, keepdims=True) + EPS) * g


def _tree_reduce(op, x):
    while x.shape[0] > 1:
        half = x.shape[0] // 2
        x = op(x[:half], x[half:])
    return x[0]


def _full_spec(shape):
    nd = len(shape)
    return pl.BlockSpec(shape, lambda *_: (0,) * nd)


def _bias_body(tab_ref, bkt_a_ref, bkt_near_ref, bkt_meta_ref, ba_ref, bnear_ref, bmeta_ref, *, far_bkt):
    def lookup(bkt, col, fill):
        acc = jnp.full(bkt.shape, fill, F32)
        for b in range(N_BUCKETS):
            acc = jnp.where(bkt == b, tab_ref[b, col], acc)
        return acc

    bkt_a = bkt_a_ref[...]
    bkt_near = bkt_near_ref[...]
    grp = A_HEADS // A_KV_HEADS
    for h in range(A_HEADS):
        ba_ref[h // grp, :, (h % grp) * BLOCK:(h % grp + 1) * BLOCK] = lookup(bkt_a, h, -jnp.inf) * LOG2E
    for h in range(B_HEADS):
        col = A_HEADS + h
        bnear_ref[h, 0:TQ, :] = jnp.zeros((TQ, TQ), F32)
        bnear_ref[h, TQ:3 * TQ, :] = (lookup(bkt_near, col, 0.0) - tab_ref[far_bkt, col]) * LOG2E
        bmeta_ref[h] = (lookup(bkt_meta_ref[...], col, 0.0) - tab_ref[far_bkt, col]) * LOG2E


def _bias_call(rel_bias, far_bkt):
    k = np.arange(2 * BLOCK)[:, None]
    q = np.arange(BLOCK)[None, :]
    dist = q + BLOCK - k
    bkt_a = np.where((dist >= 0) & (dist < WINDOW), _t5_bucket_np(np.maximum(dist, 0)), -1).astype(np.int32)
    k = np.arange(2 * TQ)[:, None]
    q = np.arange(TQ)[None, :]
    bkt_near = _t5_bucket_np(np.maximum(q + TQ - k, 0))
    k = np.arange(N_META)[:, None]
    bkt_meta = _t5_bucket_np(q + N_META - k)
    vmem = pl.BlockSpec(memory_space=pltpu.VMEM)
    grp = A_HEADS // A_KV_HEADS
    return pl.pallas_call(
        functools.partial(_bias_body, far_bkt=far_bkt),
        out_shape=(jax.ShapeDtypeStruct((A_KV_HEADS, 2 * BLOCK, grp * BLOCK), F32),
                   jax.ShapeDtypeStruct((B_HEADS, 3 * TQ, TQ), F32),
                   jax.ShapeDtypeStruct((B_HEADS, N_META, TQ), F32)),
        in_specs=[pl.BlockSpec(memory_space=pltpu.SMEM), vmem, vmem, vmem],
        out_specs=(vmem, vmem, vmem),
        name="bias_tables",
    )(rel_bias, jnp.asarray(bkt_a), jnp.asarray(bkt_near), jnp.asarray(bkt_meta))


def _proj_body(x_ref, g_ref, w1_ref, wt_ref, qg_ref, wuq_ref, wuk_ref, kvg_ref, ikg_ref, ikb_ref,
               aq_ref, ak_ref, av_ref, qabs_ref, ckv_ref, iqs_ref, ik_ref, iw_ref, *, tq):
    tm = x_ref.shape[0]
    nblk = tm // tq
    nblk_a = tm // BLOCK
    grp = A_HEADS // A_KV_HEADS
    hn = _rms(x_ref[...], g_ref[...]).astype(BF16)

    def proj(lo, hi):
        return jnp.dot(hn, w1_ref[:, lo:hi], preferred_element_type=F32)

    ak_ref[...] = proj(C_AK, C_BKV).astype(BF16)

    feat_t = lax.dot_general(wt_ref[...], hn, NT_DIMS, preferred_element_type=F32)
    aq_t = (feat_t[R_AQ:R_AV] * (A_HEAD_DIM ** -0.5 * LOG2E)).astype(BF16)
    for j in range(nblk_a):
        tok = slice(j * BLOCK, (j + 1) * BLOCK)
        for h in range(A_HEADS):
            aq_ref[j, h // grp, :, (h % grp) * BLOCK:(h % grp + 1) * BLOCK] = \
                aq_t[h * A_HEAD_DIM:(h + 1) * A_HEAD_DIM, tok]
    av_t = feat_t[R_AV:R_BQ].astype(BF16)
    for j in range(nblk_a):
        for kvh in range(A_KV_HEADS):
            av_ref[j, kvh] = av_t[kvh * A_HEAD_DIM:(kvh + 1) * A_HEAD_DIM, j * BLOCK:(j + 1) * BLOCK]

    bq_t = feat_t[R_BQ:R_IQ]
    qn_t = bq_t * lax.rsqrt(jnp.mean(bq_t * bq_t, axis=0, keepdims=True) + EPS) * qg_ref[...]
    q_t = jnp.dot(wuq_ref[...], qn_t.astype(BF16), preferred_element_type=F32).astype(BF16)
    for h in range(B_HEADS):
        qa_t = jnp.dot(wuk_ref[h], q_t[h * B_HEAD_DIM:(h + 1) * B_HEAD_DIM], preferred_element_type=F32)
        qa_t = (qa_t * (B_HEAD_DIM ** -0.5 * LOG2E)).astype(BF16)
        for j in range(nblk):
            qabs_ref[j, h] = qa_t[:, j * tq:(j + 1) * tq]
    iq_t = feat_t[R_IQ:R_IW].astype(BF16)
    for h in range(IDX_HEADS):
        for j in range(nblk):
            iqs_ref[j, h] = iq_t[h * IDX_DIM:(h + 1) * IDX_DIM, j * tq:(j + 1) * tq]

    ckv_ref[...] = _rms(proj(C_BKV, C_IK), kvg_ref[...]).astype(BF16)

    iw_t = feat_t[R_IW:R_IW + SUBLANES] * ((IDX_HEADS * IDX_DIM) ** -0.5)
    for j in range(nblk):
        iw_ref[j] = iw_t[:, j * tq:(j + 1) * tq]

    ik = proj(C_IK, C_END)[:, :IDX_DIM]
    mu = jnp.mean(ik, axis=-1, keepdims=True)
    xc = ik - mu
    var = jnp.mean(xc * xc, axis=-1, keepdims=True)
    ik_ref[...] = (xc * lax.rsqrt(var + EPS) * ikg_ref[...] + ikb_ref[...]).astype(BF16)


def _proj_call(x2, tm, tq, *weights):
    n, d = x2.shape
    grid = (n // tm,)
    row = lambda i: (i, 0)
    blk4 = lambda i: (i, 0, 0, 0)
    grp = A_HEADS // A_KV_HEADS
    out_shape = (
        jax.ShapeDtypeStruct((n // BLOCK, A_KV_HEADS, A_HEAD_DIM, grp * BLOCK), BF16),
        jax.ShapeDtypeStruct((n, A_KV_HEADS * A_HEAD_DIM), BF16),
        jax.ShapeDtypeStruct((n // BLOCK, A_KV_HEADS, A_HEAD_DIM, BLOCK), BF16),
        jax.ShapeDtypeStruct((n // tq, B_HEADS, B_KV_RANK, tq), BF16),
        jax.ShapeDtypeStruct((n, B_KV_RANK), BF16),
        jax.ShapeDtypeStruct((n // tq, IDX_HEADS, IDX_DIM, tq), BF16),
        jax.ShapeDtypeStruct((n, IDX_DIM), BF16),
        jax.ShapeDtypeStruct((n // tq, SUBLANES, tq), F32),
    )
    out_specs = (
        pl.BlockSpec((tm // BLOCK, A_KV_HEADS, A_HEAD_DIM, grp * BLOCK), blk4),
        pl.BlockSpec((tm, A_KV_HEADS * A_HEAD_DIM), row),
        pl.BlockSpec((tm // BLOCK, A_KV_HEADS, A_HEAD_DIM, BLOCK), blk4),
        pl.BlockSpec((tm // tq, B_HEADS, B_KV_RANK, tq), blk4),
        pl.BlockSpec((tm, B_KV_RANK), row),
        pl.BlockSpec((tm // tq, IDX_HEADS, IDX_DIM, tq), blk4),
        pl.BlockSpec((tm, IDX_DIM), row),
        pl.BlockSpec((tm // tq, SUBLANES, tq), lambda i: (i, 0, 0)),
    )
    in_specs = [pl.BlockSpec((tm, d), row)] + [_full_spec(w.shape) for w in weights]
    return pl.pallas_call(
        functools.partial(_proj_body, tq=tq),
        grid=grid, in_specs=in_specs, out_specs=out_specs, out_shape=out_shape,
        compiler_params=pltpu.CompilerParams(dimension_semantics=("arbitrary",),
                                             vmem_limit_bytes=VMEM_LIMIT_BYTES),
        name="in_proj",
    )(x2, *weights)


def _swa_body(sinks_ref, aq_ref, kcur_ref, kprev_ref, kmeta_ref, vcur_ref, vprev_ref, vmeta_ref, bias_ref, o_ref):
    n = pl.program_id(1)
    first = n == 0
    grp = A_HEADS // A_KV_HEADS
    width = grp * BLOCK
    kall = jnp.concatenate([jnp.where(first, kmeta_ref[...], kprev_ref[0]), kcur_ref[0]], axis=0)
    vall = [jnp.concatenate([jnp.where(first, vmeta_ref[0, kvh], vprev_ref[0, kvh])]
                            + [vcur_ref[j, kvh] for j in range(SWA_BLOCKS)], axis=1)
            for kvh in range(A_KV_HEADS)]
    ntile = 2 * BLOCK // SUBLANES
    key_row = (lax.broadcasted_iota(jnp.int32, (ntile, SUBLANES, width), 0) * SUBLANES
               + lax.broadcasted_iota(jnp.int32, (ntile, SUBLANES, width), 1))
    pad_row = first & (key_row < PAD)
    lane_head = lax.broadcasted_iota(jnp.int32, (1, width), 1) // BLOCK
    ones = jnp.ones((BF16_ROWS, 2 * BLOCK), BF16)
    sinks = []
    for kvh in range(A_KV_HEADS):
        sink = jnp.zeros((1, width), F32)
        for g in range(grp):
            sink = jnp.where(lane_head == g, sinks_ref[kvh * grp + g] * LOG2E, sink)
        sinks.append(sink)
    probs_ids = [(j, kvh) for j in range(SWA_BLOCKS) for kvh in range(A_KV_HEADS)]
    scores = [jnp.dot(kall[j * BLOCK:(j + 2) * BLOCK, kvh * A_HEAD_DIM:(kvh + 1) * A_HEAD_DIM], aq_ref[j, kvh],
                      preferred_element_type=F32) for j, kvh in probs_ids]
    maxes, probs = [], []
    for i, (j, kvh) in enumerate(probs_ids):
        s = (scores[i] + bias_ref[kvh]).reshape(ntile, SUBLANES, width)
        if j == 0:
            s = jnp.where(pad_row, -jnp.inf, s)
        m = jnp.maximum(jnp.max(_tree_reduce(jnp.maximum, s), axis=0, keepdims=True), sinks[kvh])
        maxes.append(m)
        probs.append(jnp.exp2(s - m[None]).reshape(2 * BLOCK, width).astype(BF16))
    pvs = [jnp.dot(jnp.concatenate([vall[kvh][:, j * BLOCK:(j + 2) * BLOCK], ones], axis=0), probs[i],
                   preferred_element_type=F32) for i, (j, kvh) in enumerate(probs_ids)]
    for j in range(SWA_BLOCKS):
        heads_t = []
        for kvh in range(A_KV_HEADS):
            i = j * A_KV_HEADS + kvh
            den = pvs[i][A_HEAD_DIM:A_HEAD_DIM + 1] + jnp.exp2(sinks[kvh] - maxes[i])
            o_t = pvs[i][0:A_HEAD_DIM] * (1.0 / den)
            heads_t += [o_t[:, g * BLOCK:(g + 1) * BLOCK] for g in range(grp)]
        o_ref[0, j * BLOCK:(j + 1) * BLOCK, :] = jnp.concatenate(heads_t, axis=0).T.astype(BF16)


def _swa_call(sinks, aq_t, ak, av_t, mak, mav_t, bias_a, b):
    nblocks = aq_t.shape[0]
    nb = nblocks // b
    assert nb % SWA_BLOCKS == 0
    ns = nb // SWA_BLOCKS
    s = nb * BLOCK
    rows = SWA_BLOCKS * BLOCK
    cur4 = lambda i, n: (i * ns + n, 0, 0, 0)
    prev_blk = lambda n: jnp.maximum(n * SWA_BLOCKS - 1, 0)
    kdim = ak.shape[-1]
    ak3 = ak.reshape(b, s, kdim)
    return pl.pallas_call(
        _swa_body,
        grid=(b, ns),
        in_specs=[pl.BlockSpec(memory_space=pltpu.SMEM),
                  pl.BlockSpec((SWA_BLOCKS,) + aq_t.shape[1:], cur4),
                  pl.BlockSpec((1, rows, kdim), lambda i, n: (i, n, 0)),
                  pl.BlockSpec((1, BLOCK, kdim), lambda i, n: (i, prev_blk(n), 0)),
                  _full_spec(mak.shape),
                  pl.BlockSpec((SWA_BLOCKS,) + av_t.shape[1:], cur4),
                  pl.BlockSpec((1,) + av_t.shape[1:], lambda i, n: (i * nb + prev_blk(n), 0, 0, 0)),
                  _full_spec(mav_t.shape), _full_spec(bias_a.shape)],
        out_specs=pl.BlockSpec((1, rows, A_HEADS * A_HEAD_DIM), lambda i, n: (i, n, 0)),
        out_shape=jax.ShapeDtypeStruct((b, s, A_HEADS * A_HEAD_DIM), BF16),
        compiler_params=pltpu.CompilerParams(dimension_semantics=("arbitrary", "arbitrary"),
                                             vmem_limit_bytes=VMEM_LIMIT_BYTES),
        name="swa_sink_attention",
    )(sinks, aq_t, ak3, ak3, mak, av_t, av_t, mav_t, bias_a)


def _mla_body(*refs, topk):
    it = iter(refs)
    take = lambda n: [next(it) for _ in range(n)]
    iqs_refs, iw_refs, qabs_refs, ik_refs, ckv_refs = (take(MLA_SEQS) for _ in range(5))
    mik_ref, mckv_ref, mvext_ref, bnear_ref, bmeta_ref, wuvt_ref, tril_ref, mtril_ref = take(8)
    o_all, = take(1)
    o_refs = [o_all.at[0, p] for p in range(MLA_SEQS)]
    scratch = take(12)
    seqs = range(MLA_SEQS)
    m_blk = pl.program_id(1)
    first = m_blk == 0
    nchunks = m_blk + 1
    groups = TQ // SUBLANES
    acc_rows = B_KV_RANK + BF16_ROWS
    kf = float(topk)

    sub_pos = (lax.broadcasted_iota(jnp.int32, (groups, SUBLANES, TQ), 0) * SUBLANES
               + lax.broadcasted_iota(jnp.int32, (groups, SUBLANES, TQ), 1))
    qpos = m_blk * TQ + lax.broadcasted_iota(jnp.int32, (groups, SUBLANES, TQ), 2)

    def rows_of(c):
        return pl.ds(pl.multiple_of(c * TQ, TQ), TQ)

    def tiles(x):
        return x.reshape(groups, SUBLANES, x.shape[-1])

    def sort_key(sc):
        bits = lax.bitcast_convert_type(sc, jnp.int32)
        return jnp.where(bits < 0, INT_MIN - bits, bits)

    iqs_ref, iw_ref, qabs_ref, ik_ref, ckv_ref, o_ref = iqs_refs, iw_refs, qabs_refs, ik_refs, ckv_refs, o_refs
    (key_sc, khi_sc, klo_sc, s0_sc, s1_sc, mb0_sc, mb1_sc, al0_sc, al1_sc, m_sc, acc_sc, tie_sc) = (
        [r.at[p] for p in seqs] for r in scratch)
    wrow = [[iw_ref[p][0, h:h + 1, :] for h in range(IDX_HEADS)] for p in seqs]

    def index_score(p, ik_c):
        sc = None
        for h in range(IDX_HEADS):
            logits = jnp.dot(ik_c, iqs_ref[p][0, h], preferred_element_type=F32)
            term = jnp.maximum(logits, 0.0) * wrow[p][h]
            sc = term if sc is None else sc + term
        return sort_key(sc)

    def digits(key):
        return jnp.right_shift(key, 16).astype(jnp.int16), (key ^ 0x8000).astype(jnp.int16)

    def score_chunk(c):
        adm = c * TQ + sub_pos <= qpos
        for p in seqs:
            key = jnp.where(adm, tiles(index_score(p, ik_ref[p][0, rows_of(c), :])), INT_MIN).reshape(TQ, TQ)
            key_sc[p][rows_of(c), :] = key
            khi_sc[p][rows_of(c), :], klo_sc[p][rows_of(c), :] = digits(key)

    key_m = [index_score(p, mik_ref[...]) for p in seqs]
    khi_m, klo_m = zip(*[digits(k) for k in key_m])

    def score_pair(i, carry):
        score_chunk(2 * i)
        score_chunk(2 * i + 1)
        return carry

    lax.fori_loop(0, lax.shift_right_logical(nchunks, 1), score_pair, 0)

    @pl.when((nchunks & 1) == 1)
    def _():
        score_chunk(nchunks - 1)

    def count_gt(thrs):
        def hits(p, kp):
            return _tree_reduce(jnp.add, jnp.where(kp > thrs[p][None], 1.0, 0.0))

        def body(c, parts):
            return tuple(parts[p] + hits(p, tiles(key_sc[p][rows_of(c), :])) for p in seqs)
        parts = lax.fori_loop(0, nchunks, body,
                              tuple(hits(p, key_m[p].reshape(N_META // SUBLANES, SUBLANES, TQ)) for p in seqs))
        return [jnp.broadcast_to(jnp.sum(part, axis=0, keepdims=True), (SUBLANES, TQ)) for part in parts]

    def tiles16(x):
        return x.reshape(TQ // BF16_ROWS, BF16_ROWS, TQ)

    def count16_ge(refs16, meta_digits, cands):
        cand16 = [jnp.broadcast_to(cands[p][0:1], (BF16_ROWS, TQ)).astype(jnp.int16) for p in seqs]
        one, zero = jnp.int16(1), jnp.int16(0)

        def hits(p, c):
            hit = tiles16(refs16[p][rows_of(c), :]) >= cand16[p][None]
            return _tree_reduce(jnp.add, jnp.where(hit, one, zero))

        def two_chunks(i, parts):
            return tuple(parts[p] + hits(p, 2 * i) + hits(p, 2 * i + 1) for p in seqs)
        parts = lax.fori_loop(0, lax.shift_right_logical(nchunks, 1), two_chunks,
                              tuple(jnp.where(meta_digits[p] >= cand16[p], one, zero) for p in seqs))
        parts = lax.cond((nchunks & 1) == 1,
                         lambda ps: tuple(ps[p] + hits(p, nchunks - 1) for p in seqs), lambda ps: ps, parts)
        return [jnp.broadcast_to(jnp.sum(part.astype(jnp.int32), axis=0, keepdims=True), (SUBLANES, TQ))
                for part in parts]

    def digit_search(refs16, meta_digits):
        def step(i, ts):
            cands = [t + jnp.left_shift(jnp.int32(1), 15 - i) for t in ts]
            counts = count16_ge(refs16, meta_digits, cands)
            return tuple(jnp.where(counts[p] >= topk, cands[p], ts[p]) for p in seqs)
        return lax.fori_loop(0, 16, step, tuple(jnp.full((SUBLANES, TQ), I16_MIN, jnp.int32) for _ in seqs))

    t_hi = digit_search(khi_sc, khi_m)
    t_hi16 = [jnp.broadcast_to(t[0:1], (BF16_ROWS, TQ)).astype(jnp.int16) for t in t_hi]

    def pinned(hi, lo, t):
        return jnp.where(hi > t, jnp.int16(I16_MAX), jnp.where(hi < t, jnp.int16(I16_MIN), lo))

    def pin_low(c, carry):
        for p in seqs:
            lo = pinned(tiles16(khi_sc[p][rows_of(c), :]), tiles16(klo_sc[p][rows_of(c), :]), t_hi16[p][None])
            klo_sc[p][rows_of(c), :] = lo.reshape(TQ, TQ)
        return carry

    lax.fori_loop(0, nchunks, pin_low, 0)
    t_lo = digit_search(klo_sc, [pinned(khi_m[p], klo_m[p], t_hi16[p]) for p in seqs])
    thr = [t_hi[p] * 65536 + (t_lo[p] - I16_MIN) for p in seqs]
    need = [kf - cnt for cnt in count_gt(thr)]

    def pad_rows(x):
        return jnp.concatenate([x, jnp.zeros((LANES - N_META, TQ), x.dtype)], axis=0)

    mask_m = []
    for p in seqs:
        thr_m = jnp.broadcast_to(thr[p][0:1], (N_META, TQ))
        tied_m = key_m[p] == thr_m
        rank_m = jnp.dot(mtril_ref[...], pad_rows(jnp.where(tied_m, 1.0, 0.0).astype(BF16)),
                         preferred_element_type=F32)
        tie_sc[p][...] = jnp.broadcast_to(rank_m[N_META - 1:N_META], (SUBLANES, TQ))
        sel_m = (key_m[p] > thr_m) | (tied_m & (rank_m <= jnp.broadcast_to(need[p][0:1], (N_META, TQ))))
        mask_m.append(jnp.where(sel_m, 0.0, -jnp.inf))
    heads = [(p, h) for p in seqs for h in range(B_HEADS)]
    s_m = [jnp.dot(mckv_ref[...], qabs_ref[p][0, h], preferred_element_type=F32) for p, h in heads]
    e_m = []
    for i, (p, h) in enumerate(heads):
        sh = s_m[i] + jnp.where(first, bmeta_ref[h], 0.0) + mask_m[p]
        m_new = jnp.maximum(jnp.max(sh, axis=0, keepdims=True), MAX_FLOOR)
        m_sc[p][h] = jnp.broadcast_to(m_new, (SUBLANES, TQ))
        e_m.append(pad_rows(jnp.exp2(sh - m_new).astype(BF16)))
    for i, (p, h) in enumerate(heads):
        acc_sc[p][h] = jnp.dot(mvext_ref[...], e_m[i], preferred_element_type=F32)

    def score_stage(c, buf):
        bias_row0 = pl.multiple_of(jnp.maximum(c - m_blk + 2, 0) * TQ, TQ)
        for p in seqs:
            s_buf, m_buf, al_buf = (b[p] for b in buf)
            ckv_c = ckv_ref[p][0, rows_of(c), :]
            kp = tiles(key_sc[p][rows_of(c), :])
            tied = kp == thr[p][None]
            tied_b = jnp.where(tied, 1.0, 0.0).reshape(TQ, TQ).astype(BF16)
            rank = tiles(jnp.dot(tril_ref[...], tied_b, preferred_element_type=F32)) + tie_sc[p][...][None]
            tie_sc[p][...] = jnp.broadcast_to(rank[groups - 1, SUBLANES - 1:SUBLANES, :], (SUBLANES, TQ))
            sel = ((kp > thr[p][None]) | (tied & (rank <= need[p][None]))) & (kp != INT_MIN)
            mask = jnp.where(sel, 0.0, -jnp.inf)
            for h in range(B_HEADS):
                sh = jnp.dot(ckv_c, qabs_ref[p][0, h], preferred_element_type=F32)
                sh = tiles(sh + bnear_ref[h, pl.ds(bias_row0, TQ), :]) + mask
                s_buf[h] = sh.reshape(TQ, TQ)
                m_prev = m_sc[p][h]
                mx = jnp.max(_tree_reduce(jnp.maximum, sh), axis=0, keepdims=True)
                m_new = jnp.maximum(m_prev, mx)
                al_buf[h] = jnp.exp2(m_prev - m_new)
                m_buf[h] = m_new
                m_sc[p][h] = m_new

    def value_stage(c, buf):
        for p in seqs:
            s_buf, m_buf, al_buf = (b[p] for b in buf)
            vt = ckv_ref[p][0, rows_of(c), :].astype(F32).T.astype(BF16)
            vext = jnp.concatenate([vt, jnp.ones((BF16_ROWS, TQ), BF16)], axis=0)
            for h in range(B_HEADS):
                e = jnp.exp2(tiles(s_buf[h]) - m_buf[h][None]).reshape(TQ, TQ).astype(BF16)
                pv = jnp.dot(vext, e, preferred_element_type=F32)
                acc = acc_sc[p][h].reshape(acc_rows // SUBLANES, SUBLANES, TQ)
                acc_sc[p][h] = (acc * al_buf[h][None] + pv.reshape(acc.shape)).reshape(acc_rows, TQ)

    buf0, buf1 = (s0_sc, mb0_sc, al0_sc), (s1_sc, mb1_sc, al1_sc)
    score_stage(0, buf0)

    def chunk_pair(i, carry):
        c = 2 * i + 1
        value_stage(c - 1, buf0)
        score_stage(c, buf1)
        value_stage(c, buf1)
        score_stage(c + 1, buf0)
        return carry

    npairs = (nchunks - 1) // 2
    lax.fori_loop(0, npairs, chunk_pair, 0)
    last = nchunks - 1

    @pl.when(last == 2 * npairs)
    def _():
        value_stage(last, buf0)

    @pl.when(last != 2 * npairs)
    def _():
        value_stage(last - 1, buf0)
        score_stage(last, buf1)
        value_stage(last, buf1)

    for p in seqs:
        outs = []
        for h in range(B_HEADS):
            acc = acc_sc[p][h]
            lat = (acc[0:B_KV_RANK] * (1.0 / acc[B_KV_RANK:B_KV_RANK + 1])).astype(BF16)
            outs.append(jnp.dot(wuvt_ref[h], lat, preferred_element_type=F32))
        o_ref[p][...] = jnp.concatenate(outs, axis=0).T.astype(BF16)


def _mla_call(iqs, iw, qabs, ik, ckv, mik, mckv, bnear, bmeta, wuvt, topk):
    b, s, _ = ik.shape
    nq = s // TQ
    acc_rows = B_KV_RANK + BF16_ROWS
    assert N_META == BF16_ROWS, "the meta keys are handled as one packed bf16 tile"
    tril = jnp.asarray(np.tril(np.ones((TQ, TQ), np.float32)), BF16)
    mtril = jnp.asarray(np.pad(np.tril(np.ones((N_META, N_META), np.float32)), ((0, 0), (0, LANES - N_META))), BF16)
    mvext = jnp.pad(jnp.concatenate([mckv.T, jnp.ones((BF16_ROWS, N_META), BF16)], axis=0),
                    ((0, 0), (0, LANES - N_META)))
    consts = (mik, mckv, mvext, bnear, bmeta, wuvt, tril, mtril)
    nseq = MLA_SEQS
    assert b % nseq == 0

    def per_seq(block, index_of):
        return [pl.BlockSpec(block, functools.partial(index_of, p)) for p in range(nseq)]

    blk4 = lambda p, i, m: ((i * nseq + p) * nq + m, 0, 0, 0)
    blk3 = lambda p, i, m: ((i * nseq + p) * nq + m, 0, 0)
    seq3 = lambda p, i, m: (i * nseq + p, 0, 0)
    out_dim = B_HEADS * B_HEAD_DIM
    vmem = lambda shape, dtype: pltpu.VMEM((nseq,) + shape, dtype)
    o = pl.pallas_call(
        functools.partial(_mla_body, topk=topk),
        grid=(b // nseq, nq),
        in_specs=per_seq((1, IDX_HEADS, IDX_DIM, TQ), blk4) + per_seq((1, SUBLANES, TQ), blk3)
                 + per_seq((1, B_HEADS, B_KV_RANK, TQ), blk4) + per_seq((1, s, IDX_DIM), seq3)
                 + per_seq((1, s, B_KV_RANK), seq3) + [_full_spec(c.shape) for c in consts],
        out_specs=pl.BlockSpec((1, nseq, TQ, out_dim), lambda i, m: (i, 0, m, 0)),
        out_shape=jax.ShapeDtypeStruct((b // nseq, nseq, s, out_dim), BF16),
        scratch_shapes=[vmem((s, TQ), jnp.int32), vmem((s, TQ), jnp.int16), vmem((s, TQ), jnp.int16),
                        vmem((B_HEADS, TQ, TQ), F32), vmem((B_HEADS, TQ, TQ), F32),
                        vmem((B_HEADS, SUBLANES, TQ), F32), vmem((B_HEADS, SUBLANES, TQ), F32),
                        vmem((B_HEADS, SUBLANES, TQ), F32), vmem((B_HEADS, SUBLANES, TQ), F32),
                        vmem((B_HEADS, SUBLANES, TQ), F32),
                        vmem((B_HEADS, acc_rows, TQ), F32),
                        vmem((SUBLANES, TQ), F32)],
        compiler_params=pltpu.CompilerParams(dimension_semantics=("arbitrary", "arbitrary"),
                                             vmem_limit_bytes=VMEM_LIMIT_BYTES),
        name="indexer_topk_mla",
    )(*([iqs] * nseq + [iw] * nseq + [qabs] * nseq + [ik] * nseq + [ckv] * nseq), *consts)
    return o.reshape(b, s, out_dim)


def _out_body(x_ref, oa_ref, ob_ref, ag_ref, wg_ref, bg_ref, wa_ref, wb_ref, wo_ref, fg_ref,
              wfg_ref, wfu_ref, wfd_ref, ng_ref, y_ref):
    d = x_ref.shape[1]
    x = x_ref[...]
    hn = _rms(x, ag_ref[...]).astype(BF16)

    def gated(o_ref, w_ref, lo):
        gate = jax.nn.sigmoid(jnp.dot(hn, wg_ref[:, lo:lo + d], preferred_element_type=F32) + bg_ref[:, lo:lo + d])
        return gate * jnp.dot(o_ref[...], w_ref[...], preferred_element_type=F32)

    mixed = (gated(oa_ref, wa_ref, 0) + gated(ob_ref, wb_ref, d)).astype(BF16)
    h = x + jnp.dot(mixed, wo_ref[...], preferred_element_type=F32)
    hn2 = _rms(h, fg_ref[...]).astype(BF16)
    tiles_ff = wfg_ref.shape[1] // MXU_DIM
    bounds = [MXU_DIM * ((tiles_ff * i + FFN_CHUNKS - 1) // FFN_CHUNKS) for i in range(FFN_CHUNKS + 1)]
    for lo, hi in zip(bounds[:-1], bounds[1:]):
        g = jnp.dot(hn2, wfg_ref[:, lo:hi], preferred_element_type=F32)
        u = jnp.dot(hn2, wfu_ref[:, lo:hi], preferred_element_type=F32)
        act = (g * jax.nn.sigmoid(g) * u).astype(BF16)
        h = h + jnp.dot(act, wfd_ref[lo:hi, :], preferred_element_type=F32)
    y_ref[...] = _rms(h, ng_ref[...])


def _out_call(x2, oa, ob, ag, wg, bg, wa, wb, wo, fg, wfg, wfu, wfd, ng):
    n, d = x2.shape
    tm = TOK_TILE
    row = lambda i: (i, 0)

    def const_spec(a):
        return pl.BlockSpec(a.shape, lambda i: (0,) * a.ndim, pipeline_mode=pl.Buffered(1))

    consts = [ag, wg, bg, wa, wb, wo, fg, wfg, wfu, wfd, ng]
    return pl.pallas_call(
        _out_body,
        grid=(n // tm,),
        in_specs=[pl.BlockSpec((tm, d), row), pl.BlockSpec((tm, oa.shape[1]), row),
                  pl.BlockSpec((tm, ob.shape[1]), row)] + [const_spec(a) for a in consts],
        out_specs=pl.BlockSpec((tm, d), row),
        out_shape=jax.ShapeDtypeStruct((n, d), F32),
        compiler_params=pltpu.CompilerParams(dimension_semantics=("arbitrary",),
                                             vmem_limit_bytes=VMEM_LIMIT_BYTES),
        name="merge_ffn_norm",
    )(x2, oa, ob, *consts)


def kernel(x, meta_tokens, attn_norm_g, w_in, b_gates, q_norm_g, kv_norm_g, w_uq, w_uk, w_uv, idx_k_ln_g,
           idx_k_ln_b, sinks, rel_bias, w_branch_a, w_branch_b, w_out, ffn_norm_g, w_ffn_gate, w_ffn_up,
           w_ffn_down, final_norm_g):
    b, s, d = x.shape
    assert attn_norm_g.shape[0] == 1, "single-layer block"
    assert s % TQ == 0 and (b * s) % TOK_TILE == 0 and (b * s) % PROJ_TILE == 0 and PROJ_TILE % TQ == 0
    assert w_ffn_gate.shape[2] % MXU_DIM == 0
    topk = min(TOPK_MAX, s // 4)
    far_bkts = np.unique(_t5_bucket_np(np.arange(BLOCK + 1, s + BLOCK + 1)))
    assert far_bkts.size == 1
    far_bkt = int(far_bkts[0])

    wi = w_in[0]
    widths = (A_HEADS * A_HEAD_DIM, A_KV_HEADS * A_HEAD_DIM, A_KV_HEADS * A_HEAD_DIM, B_Q_RANK, B_KV_RANK,
              IDX_HEADS * IDX_DIM, IDX_DIM, IDX_HEADS, 2 * d)
    starts = np.concatenate([[0], np.cumsum(widths)])
    w_aq, w_ak, w_av, w_bq, w_bkv, w_iq, w_ik, w_iw, w_gates = (wi[:, int(a):int(b_)]
                                                                 for a, b_ in zip(starts[:-1], starts[1:]))
    zpad = lambda k: jnp.zeros((d, k), wi.dtype)
    w1 = jnp.concatenate([w_ak, w_bkv, w_ik, zpad(LANES - IDX_DIM)], axis=1).astype(BF16)
    wt = jnp.concatenate([w_aq, w_av, w_bq, w_iq, w_iw, zpad(BF16_ROWS - IDX_HEADS)],
                         axis=1).T.astype(BF16)
    wg = w_gates.astype(BF16)
    row2 = lambda v: v.reshape(1, -1).astype(F32)
    col2 = lambda v: v.reshape(-1, 1).astype(F32)
    wuk = jnp.transpose(w_uk[0], (1, 0, 2)).astype(BF16)
    wuvt = jnp.transpose(w_uv[0], (1, 2, 0)).astype(BF16)
    proj_w = (row2(attn_norm_g[0]), w1, wt, col2(q_norm_g[0]), w_uq[0].T.astype(BF16), wuk,
              row2(kv_norm_g[0]), row2(idx_k_ln_g[0]), row2(idx_k_ln_b[0]))

    bias_a, bnear, bmeta = _bias_call(rel_bias, far_bkt)

    x2 = x.reshape(b * s, d)
    aq_t, ak, av_t, qabs, ckv, iqs, ik, iw = _proj_call(x2, PROJ_TILE, TQ, *proj_w)
    meta_blk = jnp.concatenate([jnp.zeros((PAD, d), x.dtype), meta_tokens.astype(x.dtype)], axis=0)
    _, mak, mav_t, _, mckv, _, mik, _ = _proj_call(meta_blk, BLOCK, BLOCK, *proj_w)

    o_a = _swa_call(sinks[0], aq_t, ak, av_t, mak, mav_t, bias_a, b)
    o_b = _mla_call(iqs, iw, qabs, ik.reshape(b, s, -1), ckv.reshape(b, s, -1), mik[PAD:], mckv[PAD:],
                    bnear, bmeta, wuvt, topk)

    y = _out_call(x2, o_a.reshape(b * s, -1), o_b.reshape(b * s, -1), row2(attn_norm_g[0]), wg,
                  row2(b_gates[0]), w_branch_a[0].astype(BF16), w_branch_b[0].astype(BF16),
                  w_out[0].astype(BF16), row2(ffn_norm_g[0]), w_ffn_gate[0].astype(BF16),
                  w_ffn_up[0].astype(BF16), w_ffn_down[0].astype(BF16), row2(final_norm_g))
    return y.reshape(b, s, d)
```

```python
import functools
import math

import numpy as np
import jax
import jax.numpy as jnp
from jax import lax
from jax.experimental import pallas as pl
from jax.experimental.pallas import tpu as pltpu

N_META = 16
BLOCK = 128
PAD = BLOCK - N_META
WINDOW = 128
A_HEADS = 8
A_KV_HEADS = 2
A_HEAD_DIM = 64
B_HEADS = 8
B_HEAD_DIM = 64
B_Q_RANK = 256
B_KV_RANK = 128
IDX_HEADS = 4
IDX_DIM = 64
TOPK_MAX = 256
N_BUCKETS = 32
MAX_DISTANCE = 128
EPS = 1e-6
MAX_FLOOR = -3.0e38
INT_MIN = -(2 ** 31)
LOG2E = math.log2(math.e)
I16_MIN = -(2 ** 15)
I16_MAX = 2 ** 15 - 1

LANES = 128
SUBLANES = 8
BF16_ROWS = 16
MXU_DIM = 256
VMEM_LIMIT_BYTES = 56 * 1024 * 1024

TOK_TILE = 512
PROJ_TILE = 2048
TQ = 256
FFN_CHUNKS = 1
SWA_BLOCKS = 16
MLA_SEQS = 2

C_AK = 0
C_BKV = C_AK + A_KV_HEADS * A_HEAD_DIM
C_IK = C_BKV + B_KV_RANK
C_END = C_IK + LANES
R_AQ = 0
R_AV = R_AQ + A_HEADS * A_HEAD_DIM
R_BQ = R_AV + A_KV_HEADS * A_HEAD_DIM
R_IQ = R_BQ + B_Q_RANK
R_IW = R_IQ + IDX_HEADS * IDX_DIM
R_END = R_IW + BF16_ROWS

F32 = jnp.float32
BF16 = jnp.bfloat16
NT_DIMS = (((1,), (1,)), ((), ()))


def _t5_bucket_np(dist):
    dist = np.asarray(dist, np.int64)
    max_exact = N_BUCKETS // 2
    d = np.maximum(dist, 1).astype(np.float32)
    large = max_exact + (np.log(d / np.float32(max_exact)) / np.float32(math.log(MAX_DISTANCE / max_exact))
                         * np.float32(N_BUCKETS - max_exact)).astype(np.int32)
    large = np.minimum(large, N_BUCKETS - 1)
    return np.where(dist < max_exact, dist, large).astype(np.int32)


def _rms(x, g):
    return x * lax.rsqrt(jnp.mean(x * x, axis=-1, keepdims=True) + EPS) * g


def _tree_reduce(op, x):
    while x.shape[0] > 1:
        half = x.shape[0] // 2
        x = op(x[:half], x[half:])
    return x[0]


def _full_spec(shape):
    nd = len(shape)
    return pl.BlockSpec(shape, lambda *_: (0,) * nd)


def _bias_body(tab_ref, bkt_a_ref, bkt_near_ref, bkt_meta_ref, ba_ref, bnear_ref, bmeta_ref, *, far_bkt):
    def lookup(bkt, col, fill):
        acc = jnp.full(bkt.shape, fill, F32)
        for b in range(N_BUCKETS):
            acc = jnp.where(bkt == b, tab_ref[b, col], acc)
        return acc

    bkt_a = bkt_a_ref[...]
    bkt_near = bkt_near_ref[...]
    grp = A_HEADS // A_KV_HEADS
    for h in range(A_HEADS):
        ba_ref[h // grp, :, (h % grp) * BLOCK:(h % grp + 1) * BLOCK] = lookup(bkt_a, h, -jnp.inf) * LOG2E
    for h in range(B_HEADS):
        col = A_HEADS + h
        bnear_ref[h, 0:TQ, :] = jnp.zeros((TQ, TQ), F32)
        bnear_ref[h, TQ:3 * TQ, :] = (lookup(bkt_near, col, 0.0) - tab_ref[far_bkt, col]) * LOG2E
        bmeta_ref[h] = (lookup(bkt_meta_ref[...], col, 0.0) - tab_ref[far_bkt, col]) * LOG2E


def _bias_call(rel_bias, far_bkt):
    k = np.arange(2 * BLOCK)[:, None]
    q = np.arange(BLOCK)[None, :]
    dist = q + BLOCK - k
    bkt_a = np.where((dist >= 0) & (dist < WINDOW), _t5_bucket_np(np.maximum(dist, 0)), -1).astype(np.int32)
    k = np.arange(2 * TQ)[:, None]
    q = np.arange(TQ)[None, :]
    bkt_near = _t5_bucket_np(np.maximum(q + TQ - k, 0))
    k = np.arange(N_META)[:, None]
    bkt_meta = _t5_bucket_np(q + N_META - k)
    vmem = pl.BlockSpec(memory_space=pltpu.VMEM)
    grp = A_HEADS // A_KV_HEADS
    return pl.pallas_call(
        functools.partial(_bias_body, far_bkt=far_bkt),
        out_shape=(jax.ShapeDtypeStruct((A_KV_HEADS, 2 * BLOCK, grp * BLOCK), F32),
                   jax.ShapeDtypeStruct((B_HEADS, 3 * TQ, TQ), F32),
                   jax.ShapeDtypeStruct((B_HEADS, N_META, TQ), F32)),
        in_specs=[pl.BlockSpec(memory_space=pltpu.SMEM), vmem, vmem, vmem],
        out_specs=(vmem, vmem, vmem),
        name="bias_tables",
    )(rel_bias, jnp.asarray(bkt_a), jnp.asarray(bkt_near), jnp.asarray(bkt_meta))


def _proj_body(x_ref, g_ref, w1_ref, wt_ref, qg_ref, wuq_ref, wuk_ref, kvg_ref, ikg_ref, ikb_ref,
               aq_ref, ak_ref, av_ref, qabs_ref, ckv_ref, iqs_ref, ik_ref, iw_ref, *, tq):
    tm = x_ref.shape[0]
    nblk = tm // tq
    nblk_a = tm // BLOCK
    grp = A_HEADS // A_KV_HEADS
    hn = _rms(x_ref[...], g_ref[...]).astype(BF16)

    def proj(lo, hi):
        return jnp.dot(hn, w1_ref[:, lo:hi], preferred_element_type=F32)

    ak_ref[...] = proj(C_AK, C_BKV).astype(BF16)

    feat_t = lax.dot_general(wt_ref[...], hn, NT_DIMS, preferred_element_type=F32)
    aq_t = (feat_t[R_AQ:R_AV] * (A_HEAD_DIM ** -0.5 * LOG2E)).astype(BF16)
    for j in range(nblk_a):
        tok = slice(j * BLOCK, (j + 1) * BLOCK)
        for h in range(A_HEADS):
            aq_ref[j, h // grp, :, (h % grp) * BLOCK:(h % grp + 1) * BLOCK] = \
                aq_t[h * A_HEAD_DIM:(h + 1) * A_HEAD_DIM, tok]
    av_t = feat_t[R_AV:R_BQ].astype(BF16)
    for j in range(nblk_a):
        for kvh in range(A_KV_HEADS):
            av_ref[j, kvh] = av_t[kvh * A_HEAD_DIM:(kvh + 1) * A_HEAD_DIM, j * BLOCK:(j + 1) * BLOCK]

    bq_t = feat_t[R_BQ:R_IQ]
    qn_t = bq_t * lax.rsqrt(jnp.mean(bq_t * bq_t, axis=0, keepdims=True) + EPS) * qg_ref[...]
    q_t = jnp.dot(wuq_ref[...], qn_t.astype(BF16), preferred_element_type=F32).astype(BF16)
    for h in range(B_HEADS):
        qa_t = jnp.dot(wuk_ref[h], q_t[h * B_HEAD_DIM:(h + 1) * B_HEAD_DIM], preferred_element_type=F32)
        qa_t = (qa_t * (B_HEAD_DIM ** -0.5 * LOG2E)).astype(BF16)
        for j in range(nblk):
            qabs_ref[j, h] = qa_t[:, j * tq:(j + 1) * tq]
    iq_t = feat_t[R_IQ:R_IW].astype(BF16)
    for h in range(IDX_HEADS):
        for j in range(nblk):
            iqs_ref[j, h] = iq_t[h * IDX_DIM:(h + 1) * IDX_DIM, j * tq:(j + 1) * tq]

    ckv_ref[...] = _rms(proj(C_BKV, C_IK), kvg_ref[...]).astype(BF16)

    iw_t = feat_t[R_IW:R_IW + SUBLANES] * ((IDX_HEADS * IDX_DIM) ** -0.5)
    for j in range(nblk):
        iw_ref[j] = iw_t[:, j * tq:(j + 1) * tq]

    ik = proj(C_IK, C_END)[:, :IDX_DIM]
    mu = jnp.mean(ik, axis=-1, keepdims=True)
    xc = ik - mu
    var = jnp.mean(xc * xc, axis=-1, keepdims=True)
    ik_ref[...] = (xc * lax.rsqrt(var + EPS) * ikg_ref[...] + ikb_ref[...]).astype(BF16)


def _proj_call(x2, tm, tq, *weights):
    n, d = x2.shape
    grid = (n // tm,)
    row = lambda i: (i, 0)
    blk4 = lambda i: (i, 0, 0, 0)
    grp = A_HEADS // A_KV_HEADS
    out_shape = (
        jax.ShapeDtypeStruct((n // BLOCK, A_KV_HEADS, A_HEAD_DIM, grp * BLOCK), BF16),
        jax.ShapeDtypeStruct((n, A_KV_HEADS * A_HEAD_DIM), BF16),
        jax.ShapeDtypeStruct((n // BLOCK, A_KV_HEADS, A_HEAD_DIM, BLOCK), BF16),
        jax.ShapeDtypeStruct((n // tq, B_HEADS, B_KV_RANK, tq), BF16),
        jax.ShapeDtypeStruct((n, B_KV_RANK), BF16),
        jax.ShapeDtypeStruct((n // tq, IDX_HEADS, IDX_DIM, tq), BF16),
        jax.ShapeDtypeStruct((n, IDX_DIM), BF16),
        jax.ShapeDtypeStruct((n // tq, SUBLANES, tq), F32),
    )
    out_specs = (
        pl.BlockSpec((tm // BLOCK, A_KV_HEADS, A_HEAD_DIM, grp * BLOCK), blk4),
        pl.BlockSpec((tm, A_KV_HEADS * A_HEAD_DIM), row),
        pl.BlockSpec((tm // BLOCK, A_KV_HEADS, A_HEAD_DIM, BLOCK), blk4),
        pl.BlockSpec((tm // tq, B_HEADS, B_KV_RANK, tq), blk4),
        pl.BlockSpec((tm, B_KV_RANK), row),
        pl.BlockSpec((tm // tq, IDX_HEADS, IDX_DIM, tq), blk4),
        pl.BlockSpec((tm, IDX_DIM), row),
        pl.BlockSpec((tm // tq, SUBLANES, tq), lambda i: (i, 0, 0)),
    )
    in_specs = [pl.BlockSpec((tm, d), row)] + [_full_spec(w.shape) for w in weights]
    return pl.pallas_call(
        functools.partial(_proj_body, tq=tq),
        grid=grid, in_specs=in_specs, out_specs=out_specs, out_shape=out_shape,
        compiler_params=pltpu.CompilerParams(dimension_semantics=("arbitrary",),
                                             vmem_limit_bytes=VMEM_LIMIT_BYTES),
        name="in_proj",
    )(x2, *weights)


def _swa_body(sinks_ref, aq_ref, kcur_ref, kprev_ref, kmeta_ref, vcur_ref, vprev_ref, vmeta_ref, bias_ref, o_ref):
    n = pl.program_id(1)
    first = n == 0
    grp = A_HEADS // A_KV_HEADS
    width = grp * BLOCK
    kall = jnp.concatenate([jnp.where(first, kmeta_ref[...], kprev_ref[0]), kcur_ref[0]], axis=0)
    vall = [jnp.concatenate([jnp.where(first, vmeta_ref[0, kvh], vprev_ref[0, kvh])]
                            + [vcur_ref[j, kvh] for j in range(SWA_BLOCKS)], axis=1)
            for kvh in range(A_KV_HEADS)]
    ntile = 2 * BLOCK // SUBLANES
    key_row = (lax.broadcasted_iota(jnp.int32, (ntile, SUBLANES, width), 0) * SUBLANES
               + lax.broadcasted_iota(jnp.int32, (ntile, SUBLANES, width), 1))
    pad_row = first & (key_row < PAD)
    lane_head = lax.broadcasted_iota(jnp.int32, (1, width), 1) // BLOCK
    ones = jnp.ones((BF16_ROWS, 2 * BLOCK), BF16)
    sinks = []
    for kvh in range(A_KV_HEADS):
        sink = jnp.zeros((1, width), F32)
        for g in range(grp):
            sink = jnp.where(lane_head == g, sinks_ref[kvh * grp + g] * LOG2E, sink)
        sinks.append(sink)
    probs_ids = [(j, kvh) for j in range(SWA_BLOCKS) for kvh in range(A_KV_HEADS)]
    scores = [jnp.dot(kall[j * BLOCK:(j + 2) * BLOCK, kvh * A_HEAD_DIM:(kvh + 1) * A_HEAD_DIM], aq_ref[j, kvh],
                      preferred_element_type=F32) for j, kvh in probs_ids]
    maxes, probs = [], []
    for i, (j, kvh) in enumerate(probs_ids):
        s = (scores[i] + bias_ref[kvh]).reshape(ntile, SUBLANES, width)
        if j == 0:
            s = jnp.where(pad_row, -jnp.inf, s)
        m = jnp.maximum(jnp.max(_tree_reduce(jnp.maximum, s), axis=0, keepdims=True), sinks[kvh])
        maxes.append(m)
        probs.append(jnp.exp2(s - m[None]).reshape(2 * BLOCK, width).astype(BF16))
    pvs = [jnp.dot(jnp.concatenate([vall[kvh][:, j * BLOCK:(j + 2) * BLOCK], ones], axis=0), probs[i],
                   preferred_element_type=F32) for i, (j, kvh) in enumerate(probs_ids)]
    for j in range(SWA_BLOCKS):
        heads_t = []
        for kvh in range(A_KV_HEADS):
            i = j * A_KV_HEADS + kvh
            den = pvs[i][A_HEAD_DIM:A_HEAD_DIM + 1] + jnp.exp2(sinks[kvh] - maxes[i])
            o_t = pvs[i][0:A_HEAD_DIM] * (1.0 / den)
            heads_t += [o_t[:, g * BLOCK:(g + 1) * BLOCK] for g in range(grp)]
        o_ref[0, j * BLOCK:(j + 1) * BLOCK, :] = jnp.concatenate(heads_t, axis=0).T.astype(BF16)


def _swa_call(sinks, aq_t, ak, av_t, mak, mav_t, bias_a, b):
    nblocks = aq_t.shape[0]
    nb = nblocks // b
    assert nb % SWA_BLOCKS == 0
    ns = nb // SWA_BLOCKS
    s = nb * BLOCK
    rows = SWA_BLOCKS * BLOCK
    cur4 = lambda i, n: (i * ns + n, 0, 0, 0)
    prev_blk = lambda n: jnp.maximum(n * SWA_BLOCKS - 1, 0)
    kdim = ak.shape[-1]
    ak3 = ak.reshape(b, s, kdim)
    return pl.pallas_call(
        _swa_body,
        grid=(b, ns),
        in_specs=[pl.BlockSpec(memory_space=pltpu.SMEM),
                  pl.BlockSpec((SWA_BLOCKS,) + aq_t.shape[1:], cur4),
                  pl.BlockSpec((1, rows, kdim), lambda i, n: (i, n, 0)),
                  pl.BlockSpec((1, BLOCK, kdim), lambda i, n: (i, prev_blk(n), 0)),
                  _full_spec(mak.shape),
                  pl.BlockSpec((SWA_BLOCKS,) + av_t.shape[1:], cur4),
                  pl.BlockSpec((1,) + av_t.shape[1:], lambda i, n: (i * nb + prev_blk(n), 0, 0, 0)),
                  _full_spec(mav_t.shape), _full_spec(bias_a.shape)],
        out_specs=pl.BlockSpec((1, rows, A_HEADS * A_HEAD_DIM), lambda i, n: (i, n, 0)),
        out_shape=jax.ShapeDtypeStruct((b, s, A_HEADS * A_HEAD_DIM), BF16),
        compiler_params=pltpu.CompilerParams(dimension_semantics=("arbitrary", "arbitrary"),
                                             vmem_limit_bytes=VMEM_LIMIT_BYTES),
        name="swa_sink_attention",
    )(sinks, aq_t, ak3, ak3, mak, av_t, av_t, mav_t, bias_a)


def _mla_body(*refs, topk):
    it = iter(refs)
    take = lambda n: [next(it) for _ in range(n)]
    iqs_refs, iw_refs, qabs_refs, ik_refs, ckv_refs = (take(MLA_SEQS) for _ in range(5))
    mik_ref, mckv_ref, mvext_ref, bnear_ref, bmeta_ref, wuvt_ref, tril_ref, mtril_ref = take(8)
    o_all, = take(1)
    o_refs = [o_all.at[0, p] for p in range(MLA_SEQS)]
    scratch = take(12)
    seqs = range(MLA_SEQS)
    m_blk = pl.program_id(1)
    first = m_blk == 0
    nchunks = m_blk + 1
    groups = TQ // SUBLANES
    acc_rows = B_KV_RANK + BF16_ROWS
    kf = float(topk)

    sub_pos = (lax.broadcasted_iota(jnp.int32, (groups, SUBLANES, TQ), 0) * SUBLANES
               + lax.broadcasted_iota(jnp.int32, (groups, SUBLANES, TQ), 1))
    qpos = m_blk * TQ + lax.broadcasted_iota(jnp.int32, (groups, SUBLANES, TQ), 2)

    def rows_of(c):
        return pl.ds(pl.multiple_of(c * TQ, TQ), TQ)

    def tiles(x):
        return x.reshape(groups, SUBLANES, x.shape[-1])

    def sort_key(sc):
        bits = lax.bitcast_convert_type(sc, jnp.int32)
        return jnp.where(bits < 0, INT_MIN - bits, bits)

    iqs_ref, iw_ref, qabs_ref, ik_ref, ckv_ref, o_ref = iqs_refs, iw_refs, qabs_refs, ik_refs, ckv_refs, o_refs
    (key_sc, khi_sc, klo_sc, s0_sc, s1_sc, mb0_sc, mb1_sc, al0_sc, al1_sc, m_sc, acc_sc, tie_sc) = (
        [r.at[p] for p in seqs] for r in scratch)
    wrow = [[iw_ref[p][0, h:h + 1, :] for h in range(IDX_HEADS)] for p in seqs]

    def index_score(p, ik_c):
        sc = None
        for h in range(IDX_HEADS):
            logits = jnp.dot(ik_c, iqs_ref[p][0, h], preferred_element_type=F32)
            term = jnp.maximum(logits, 0.0) * wrow[p][h]
            sc = term if sc is None else sc + term
        return sort_key(sc)

    def digits(key):
        return jnp.right_shift(key, 16).astype(jnp.int16), (key ^ 0x8000).astype(jnp.int16)

    def score_chunk(c):
        adm = c * TQ + sub_pos <= qpos
        for p in seqs:
            key = jnp.where(adm, tiles(index_score(p, ik_ref[p][0, rows_of(c), :])), INT_MIN).reshape(TQ, TQ)
            key_sc[p][rows_of(c), :] = key
            khi_sc[p][rows_of(c), :], klo_sc[p][rows_of(c), :] = digits(key)

    key_m = [index_score(p, mik_ref[...]) for p in seqs]
    khi_m, klo_m = zip(*[digits(k) for k in key_m])

    def score_pair(i, carry):
        score_chunk(2 * i)
        score_chunk(2 * i + 1)
        return carry

    lax.fori_loop(0, lax.shift_right_logical(nchunks, 1), score_pair, 0)

    @pl.when((nchunks & 1) == 1)
    def _():
        score_chunk(nchunks - 1)

    def count_gt(thrs):
        def hits(p, kp):
            return _tree_reduce(jnp.add, jnp.where(kp > thrs[p][None], 1.0, 0.0))

        def body(c, parts):
            return tuple(parts[p] + hits(p, tiles(key_sc[p][rows_of(c), :])) for p in seqs)
        parts = lax.fori_loop(0, nchunks, body,
                              tuple(hits(p, key_m[p].reshape(N_META // SUBLANES, SUBLANES, TQ)) for p in seqs))
        return [jnp.broadcast_to(jnp.sum(part, axis=0, keepdims=True), (SUBLANES, TQ)) for part in parts]

    def tiles16(x):
        return x.reshape(TQ // BF16_ROWS, BF16_ROWS, TQ)

    def count16_ge(refs16, meta_digits, cands):
        cand16 = [jnp.broadcast_to(cands[p][0:1], (BF16_ROWS, TQ)).astype(jnp.int16) for p in seqs]
        one, zero = jnp.int16(1), jnp.int16(0)

        def hits(p, c):
            hit = tiles16(refs16[p][rows_of(c), :]) >= cand16[p][None]
            return _tree_reduce(jnp.add, jnp.where(hit, one, zero))

        def two_chunks(i, parts):
            return tuple(parts[p] + hits(p, 2 * i) + hits(p, 2 * i + 1) for p in seqs)
        parts = lax.fori_loop(0, lax.shift_right_logical(nchunks, 1), two_chunks,
                              tuple(jnp.where(meta_digits[p] >= cand16[p], one, zero) for p in seqs))
        parts = lax.cond((nchunks & 1) == 1,
                         lambda ps: tuple(ps[p] + hits(p, nchunks - 1) for p in seqs), lambda ps: ps, parts)
        return [jnp.broadcast_to(jnp.sum(part.astype(jnp.int32), axis=0, keepdims=True), (SUBLANES, TQ))
                for part in parts]

    def digit_search(refs16, meta_digits):
        def step(i, ts):
            cands = [t + jnp.left_shift(jnp.int32(1), 15 - i) for t in ts]
            counts = count16_ge(refs16, meta_digits, cands)
            return tuple(jnp.where(counts[p] >= topk, cands[p], ts[p]) for p in seqs)
        return lax.fori_loop(0, 16, step, tuple(jnp.full((SUBLANES, TQ), I16_MIN, jnp.int32) for _ in seqs))

    t_hi = digit_search(khi_sc, khi_m)
    t_hi16 = [jnp.broadcast_to(t[0:1], (BF16_ROWS, TQ)).astype(jnp.int16) for t in t_hi]

    def pinned(hi, lo, t):
        return jnp.where(hi > t, jnp.int16(I16_MAX), jnp.where(hi < t, jnp.int16(I16_MIN), lo))

    def pin_low(c, carry):
        for p in seqs:
            lo = pinned(tiles16(khi_sc[p][rows_of(c), :]), tiles16(klo_sc[p][rows_of(c), :]), t_hi16[p][None])
            klo_sc[p][rows_of(c), :] = lo.reshape(TQ, TQ)
        return carry

    lax.fori_loop(0, nchunks, pin_low, 0)
    t_lo = digit_search(klo_sc, [pinned(khi_m[p], klo_m[p], t_hi16[p]) for p in seqs])
    thr = [t_hi[p] * 65536 + (t_lo[p] - I16_MIN) for p in seqs]
    need = [kf - cnt for cnt in count_gt(thr)]

    def pad_rows(x):
        return jnp.concatenate([x, jnp.zeros((LANES - N_META, TQ), x.dtype)], axis=0)

    mask_m = []
    for p in seqs:
        thr_m = jnp.broadcast_to(thr[p][0:1], (N_META, TQ))
        tied_m = key_m[p] == thr_m
        rank_m = jnp.dot(mtril_ref[...], pad_rows(jnp.where(tied_m, 1.0, 0.0).astype(BF16)),
                         preferred_element_type=F32)
        tie_sc[p][...] = jnp.broadcast_to(rank_m[N_META - 1:N_META], (SUBLANES, TQ))
        sel_m = (key_m[p] > thr_m) | (tied_m & (rank_m <= jnp.broadcast_to(need[p][0:1], (N_META, TQ))))
        mask_m.append(jnp.where(sel_m, 0.0, -jnp.inf))
    heads = [(p, h) for p in seqs for h in range(B_HEADS)]
    s_m = [jnp.dot(mckv_ref[...], qabs_ref[p][0, h], preferred_element_type=F32) for p, h in heads]
    e_m = []
    for i, (p, h) in enumerate(heads):
        sh = s_m[i] + jnp.where(first, bmeta_ref[h], 0.0) + mask_m[p]
        m_new = jnp.maximum(jnp.max(sh, axis=0, keepdims=True), MAX_FLOOR)
        m_sc[p][h] = jnp.broadcast_to(m_new, (SUBLANES, TQ))
        e_m.append(pad_rows(jnp.exp2(sh - m_new).astype(BF16)))
    for i, (p, h) in enumerate(heads):
        acc_sc[p][h] = jnp.dot(mvext_ref[...], e_m[i], preferred_element_type=F32)

    def score_stage(c, buf):
        bias_row0 = pl.multiple_of(jnp.maximum(c - m_blk + 2, 0) * TQ, TQ)
        for p in seqs:
            s_buf, m_buf, al_buf = (b[p] for b in buf)
            ckv_c = ckv_ref[p][0, rows_of(c), :]
            kp = tiles(key_sc[p][rows_of(c), :])
            tied = kp == thr[p][None]
            tied_b = jnp.where(tied, 1.0, 0.0).reshape(TQ, TQ).astype(BF16)
            rank = tiles(jnp.dot(tril_ref[...], tied_b, preferred_element_type=F32)) + tie_sc[p][...][None]
            tie_sc[p][...] = jnp.broadcast_to(rank[groups - 1, SUBLANES - 1:SUBLANES, :], (SUBLANES, TQ))
            sel = ((kp > thr[p][None]) | (tied & (rank <= need[p][None]))) & (kp != INT_MIN)
            mask = jnp.where(sel, 0.0, -jnp.inf)
            for h in range(B_HEADS):
                sh = jnp.dot(ckv_c, qabs_ref[p][0, h], preferred_element_type=F32)
                sh = tiles(sh + bnear_ref[h, pl.ds(bias_row0, TQ), :]) + mask
                s_buf[h] = sh.reshape(TQ, TQ)
                m_prev = m_sc[p][h]
                mx = jnp.max(_tree_reduce(jnp.maximum, sh), axis=0, keepdims=True)
                m_new = jnp.maximum(m_prev, mx)
                al_buf[h] = jnp.exp2(m_prev - m_new)
                m_buf[h] = m_new
                m_sc[p][h] = m_new

    def value_stage(c, buf):
        for p in seqs:
            s_buf, m_buf, al_buf = (b[p] for b in buf)
            vt = ckv_ref[p][0, rows_of(c), :].astype(F32).T.astype(BF16)
            vext = jnp.concatenate([vt, jnp.ones((BF16_ROWS, TQ), BF16)], axis=0)
            for h in range(B_HEADS):
                e = jnp.exp2(tiles(s_buf[h]) - m_buf[h][None]).reshape(TQ, TQ).astype(BF16)
                pv = jnp.dot(vext, e, preferred_element_type=F32)
                acc = acc_sc[p][h].reshape(acc_rows // SUBLANES, SUBLANES, TQ)
                acc_sc[p][h] = (acc * al_buf[h][None] + pv.reshape(acc.shape)).reshape(acc_rows, TQ)

    buf0, buf1 = (s0_sc, mb0_sc, al0_sc), (s1_sc, mb1_sc, al1_sc)
    score_stage(0, buf0)

    def chunk_pair(i, carry):
        c = 2 * i + 1
        value_stage(c - 1, buf0)
        score_stage(c, buf1)
        value_stage(c, buf1)
        score_stage(c + 1, buf0)
        return carry

    npairs = (nchunks - 1) // 2
    lax.fori_loop(0, npairs, chunk_pair, 0)
    last = nchunks - 1

    @pl.when(last == 2 * npairs)
    def _():
        value_stage(last, buf0)

    @pl.when(last != 2 * npairs)
    def _():
        value_stage(last - 1, buf0)
        score_stage(last, buf1)
        value_stage(last, buf1)

    for p in seqs:
        outs = []
        for h in range(B_HEADS):
            acc = acc_sc[p][h]
            lat = (acc[0:B_KV_RANK] * (1.0 / acc[B_KV_RANK:B_KV_RANK + 1])).astype(BF16)
            outs.append(jnp.dot(wuvt_ref[h], lat, preferred_element_type=F32))
        o_ref[p][...] = jnp.concatenate(outs, axis=0).T.astype(BF16)


def _mla_call(iqs, iw, qabs, ik, ckv, mik, mckv, bnear, bmeta, wuvt, topk):
    b, s, _ = ik.shape
    nq = s // TQ
    acc_rows = B_KV_RANK + BF16_ROWS
    assert N_META == BF16_ROWS, "the meta keys are handled as one packed bf16 tile"
    tril = jnp.asarray(np.tril(np.ones((TQ, TQ), np.float32)), BF16)
    mtril = jnp.asarray(np.pad(np.tril(np.ones((N_META, N_META), np.float32)), ((0, 0), (0, LANES - N_META))), BF16)
    mvext = jnp.pad(jnp.concatenate([mckv.T, jnp.ones((BF16_ROWS, N_META), BF16)], axis=0),
                    ((0, 0), (0, LANES - N_META)))
    consts = (mik, mckv, mvext, bnear, bmeta, wuvt, tril, mtril)
    nseq = MLA_SEQS
    assert b % nseq == 0

    def per_seq(block, index_of):
        return [pl.BlockSpec(block, functools.partial(index_of, p)) for p in range(nseq)]

    blk4 = lambda p, i, m: ((i * nseq + p) * nq + m, 0, 0, 0)
    blk3 = lambda p, i, m: ((i * nseq + p) * nq + m, 0, 0)
    seq3 = lambda p, i, m: (i * nseq + p, 0, 0)
    out_dim = B_HEADS * B_HEAD_DIM
    vmem = lambda shape, dtype: pltpu.VMEM((nseq,) + shape, dtype)
    o = pl.pallas_call(
        functools.partial(_mla_body, topk=topk),
        grid=(b // nseq, nq),
        in_specs=per_seq((1, IDX_HEADS, IDX_DIM, TQ), blk4) + per_seq((1, SUBLANES, TQ), blk3)
                 + per_seq((1, B_HEADS, B_KV_RANK, TQ), blk4) + per_seq((1, s, IDX_DIM), seq3)
                 + per_seq((1, s, B_KV_RANK), seq3) + [_full_spec(c.shape) for c in consts],
        out_specs=pl.BlockSpec((1, nseq, TQ, out_dim), lambda i, m: (i, 0, m, 0)),
        out_shape=jax.ShapeDtypeStruct((b // nseq, nseq, s, out_dim), BF16),
        scratch_shapes=[vmem((s, TQ), jnp.int32), vmem((s, TQ), jnp.int16), vmem((s, TQ), jnp.int16),
                        vmem((B_HEADS, TQ, TQ), F32), vmem((B_HEADS, TQ, TQ), F32),
                        vmem((B_HEADS, SUBLANES, TQ), F32), vmem((B_HEADS, SUBLANES, TQ), F32),
                        vmem((B_HEADS, SUBLANES, TQ), F32), vmem((B_HEADS, SUBLANES, TQ), F32),
                        vmem((B_HEADS, SUBLANES, TQ), F32),
                        vmem((B_HEADS, acc_rows, TQ), F32),
                        vmem((SUBLANES, TQ), F32)],
        compiler_params=pltpu.CompilerParams(dimension_semantics=("arbitrary", "arbitrary"),
                                             vmem_limit_bytes=VMEM_LIMIT_BYTES),
        name="indexer_topk_mla",
    )(*([iqs] * nseq + [iw] * nseq + [qabs] * nseq + [ik] * nseq + [ckv] * nseq), *consts)
    return o.reshape(b, s, out_dim)


def _out_body(x_ref, oa_ref, ob_ref, ag_ref, wg_ref, bg_ref, wa_ref, wb_ref, wo_ref, fg_ref,
              wfg_ref, wfu_ref, wfd_ref, ng_ref, y_ref):
    d = x_ref.shape[1]
    x = x_ref[...]
    hn = _rms(x, ag_ref[...]).astype(BF16)

    def gated(o_ref, w_ref, lo):
        gate = jax.nn.sigmoid(jnp.dot(hn, wg_ref[:, lo:lo + d], preferred_element_type=F32) + bg_ref[:, lo:lo + d])
        return gate * jnp.dot(o_ref[...], w_ref[...], preferred_element_type=F32)

    mixed = (gated(oa_ref, wa_ref, 0) + gated(ob_ref, wb_ref, d)).astype(BF16)
    h = x + jnp.dot(mixed, wo_ref[...], preferred_element_type=F32)
    hn2 = _rms(h, fg_ref[...]).astype(BF16)
    tiles_ff = wfg_ref.shape[1] // MXU_DIM
    bounds = [MXU_DIM * ((tiles_ff * i + FFN_CHUNKS - 1) // FFN_CHUNKS) for i in range(FFN_CHUNKS + 1)]
    for lo, hi in zip(bounds[:-1], bounds[1:]):
        g = jnp.dot(hn2, wfg_ref[:, lo:hi], preferred_element_type=F32)
        u = jnp.dot(hn2, wfu_ref[:, lo:hi], preferred_element_type=F32)
        act = (g * jax.nn.sigmoid(g) * u).astype(BF16)
        h = h + jnp.dot(act, wfd_ref[lo:hi, :], preferred_element_type=F32)
    y_ref[...] = _rms(h, ng_ref[...])


def _out_call(x2, oa, ob, ag, wg, bg, wa, wb, wo, fg, wfg, wfu, wfd, ng):
    n, d = x2.shape
    tm = TOK_TILE
    row = lambda i: (i, 0)

    def const_spec(a):
        return pl.BlockSpec(a.shape, lambda i: (0,) * a.ndim, pipeline_mode=pl.Buffered(1))

    consts = [ag, wg, bg, wa, wb, wo, fg, wfg, wfu, wfd, ng]
    return pl.pallas_call(
        _out_body,
        grid=(n // tm,),
        in_specs=[pl.BlockSpec((tm, d), row), pl.BlockSpec((tm, oa.shape[1]), row),
                  pl.BlockSpec((tm, ob.shape[1]), row)] + [const_spec(a) for a in consts],
        out_specs=pl.BlockSpec((tm, d), row),
        out_shape=jax.ShapeDtypeStruct((n, d), F32),
        compiler_params=pltpu.CompilerParams(dimension_semantics=("arbitrary",),
                                             vmem_limit_bytes=VMEM_LIMIT_BYTES),
        name="merge_ffn_norm",
    )(x2, oa, ob, *consts)


def kernel(x, meta_tokens, attn_norm_g, w_in, b_gates, q_norm_g, kv_norm_g, w_uq, w_uk, w_uv, idx_k_ln_g,
           idx_k_ln_b, sinks, rel_bias, w_branch_a, w_branch_b, w_out, ffn_norm_g, w_ffn_gate, w_ffn_up,
           w_ffn_down, final_norm_g):
    b, s, d = x.shape
    assert attn_norm_g.shape[0] == 1, "single-layer block"
    assert s % TQ == 0 and (b * s) % TOK_TILE == 0 and (b * s) % PROJ_TILE == 0 and PROJ_TILE % TQ == 0
    assert w_ffn_gate.shape[2] % MXU_DIM == 0
    topk = min(TOPK_MAX, s // 4)
    far_bkts = np.unique(_t5_bucket_np(np.arange(BLOCK + 1, s + BLOCK + 1)))
    assert far_bkts.size == 1
    far_bkt = int(far_bkts[0])

    wi = w_in[0]
    widths = (A_HEADS * A_HEAD_DIM, A_KV_HEADS * A_HEAD_DIM, A_KV_HEADS * A_HEAD_DIM, B_Q_RANK, B_KV_RANK,
              IDX_HEADS * IDX_DIM, IDX_DIM, IDX_HEADS, 2 * d)
    starts = np.concatenate([[0], np.cumsum(widths)])
    w_aq, w_ak, w_av, w_bq, w_bkv, w_iq, w_ik, w_iw, w_gates = (wi[:, int(a):int(b_)]
                                                                 for a, b_ in zip(starts[:-1], starts[1:]))
    zpad = lambda k: jnp.zeros((d, k), wi.dtype)
    w1 = jnp.concatenate([w_ak, w_bkv, w_ik, zpad(LANES - IDX_DIM)], axis=1).astype(BF16)
    wt = jnp.concatenate([w_aq, w_av, w_bq, w_iq, w_iw, zpad(BF16_ROWS - IDX_HEADS)],
                         axis=1).T.astype(BF16)
    wg = w_gates.astype(BF16)
    row2 = lambda v: v.reshape(1, -1).astype(F32)
    col2 = lambda v: v.reshape(-1, 1).astype(F32)
    wuk = jnp.transpose(w_uk[0], (1, 0, 2)).astype(BF16)
    wuvt = jnp.transpose(w_uv[0], (1, 2, 0)).astype(BF16)
    proj_w = (row2(attn_norm_g[0]), w1, wt, col2(q_norm_g[0]), w_uq[0].T.astype(BF16), wuk,
              row2(kv_norm_g[0]), row2(idx_k_ln_g[0]), row2(idx_k_ln_b[0]))

    bias_a, bnear, bmeta = _bias_call(rel_bias, far_bkt)

    x2 = x.reshape(b * s, d)
    aq_t, ak, av_t, qabs, ckv, iqs, ik, iw = _proj_call(x2, PROJ_TILE, TQ, *proj_w)
    meta_blk = jnp.concatenate([jnp.zeros((PAD, d), x.dtype), meta_tokens.astype(x.dtype)], axis=0)
    _, mak, mav_t, _, mckv, _, mik, _ = _proj_call(meta_blk, BLOCK, BLOCK, *proj_w)

    o_a = _swa_call(sinks[0], aq_t, ak, av_t, mak, mav_t, bias_a, b)
    o_b = _mla_call(iqs, iw, qabs, ik.reshape(b, s, -1), ckv.reshape(b, s, -1), mik[PAD:], mckv[PAD:],
                    bnear, bmeta, wuvt, topk)

    y = _out_call(x2, o_a.reshape(b * s, -1), o_b.reshape(b * s, -1), row2(attn_norm_g[0]), wg,
                  row2(b_gates[0]), w_branch_a[0].astype(BF16), w_branch_b[0].astype(BF16),
                  w_out[0].astype(BF16), row2(ffn_norm_g[0]), w_ffn_gate[0].astype(BF16),
                  w_ffn_up[0].astype(BF16), w_ffn_down[0].astype(BF16), row2(final_norm_g))
    return y.reshape(b, s, d)
```
